```python
import math
import jax, jax.numpy as jnp
from jax import lax
import numpy as np

D_MODEL = 1024
BATCH = 8
SEQ = 2048
DEPTH = 1

N_META = 16
D_MIX = 1024
ATTN_HEADS = 8
QK_NOPE = 64
QK_ROPE = 32
V_HEAD = 64
Q_LORA = 384
KV_LORA = 256
ROPE_THETA = 10000.0
QBLK = 128
D_ATTN = ATTN_HEADS * V_HEAD
ATTN_SCALE = 1.0 / math.sqrt(QK_NOPE + QK_ROPE)
D_CONV = D_MIX - D_ATTN
CONV_W = 3
D_IN = Q_LORA + KV_LORA + QK_ROPE + 3 * D_CONV
N_EXPERTS = 256
TOP_K = 8
N_GROUPS = 8
TOPK_GROUPS = 4
D_EXPERT = 256
D_SHARED = 256
ROUTED_SCALE = 2.5
MOE_BLK = 128
ALPHA = (2.0 * DEPTH) ** 0.25
BETA = (8.0 * DEPTH) ** -0.25

kernel_name = "hymba_mla_shortconv_moe_deepnorm"


def _rms(x, g, eps=1e-6):
    xf = x.astype(jnp.float32)
    y = xf * lax.rsqrt(jnp.mean(xf * xf, axis=-1, keepdims=True) + eps) * g.astype(jnp.float32)
    return y.astype(x.dtype)


def _layernorm(x, g, b, eps=1e-5):
    xf = x.astype(jnp.float32)
    mu = jnp.mean(xf, axis=-1, keepdims=True)
    var = jnp.mean(jnp.square(xf - mu), axis=-1, keepdims=True)
    y = (xf - mu) * lax.rsqrt(var + eps) * g.astype(jnp.float32) + b.astype(jnp.float32)
    return y.astype(x.dtype)


def _rope_tables(L):
    pos = jnp.arange(L, dtype=jnp.float32)
    inv = ROPE_THETA ** (-jnp.arange(0, QK_ROPE, 2, dtype=jnp.float32) / QK_ROPE)
    ang = pos[:, None] * inv[None, :]
    return jnp.cos(ang), jnp.sin(ang)


def _apply_rope(x, cos, sin):
    half = x.shape[-1] // 2
    x1, x2 = x[..., :half], x[..., half:]
    c, s = cos.astype(x.dtype), sin.astype(x.dtype)
    return jnp.concatenate([x1 * c - x2 * s, x2 * c + x1 * s], axis=-1)


def _pad_seq(a, pad):
    return jnp.pad(a, ((0, 0), (0, pad)) + ((0, 0),) * (a.ndim - 2))


def _mla(c_q, c_kv, k_pe, q_norm, w_uq, kv_norm, w_ukv):
    B, L, _ = c_q.shape
    cos, sin = _rope_tables(L)
    q = (_rms(c_q, q_norm) @ w_uq).reshape(B, L, ATTN_HEADS, QK_NOPE + QK_ROPE)
    q_nope = q[..., :QK_NOPE]
    q_pe = _apply_rope(q[..., QK_NOPE:], cos[:, None, :], sin[:, None, :])
    k_pe = _apply_rope(k_pe, cos, sin)
    kv = (_rms(c_kv, kv_norm) @ w_ukv).reshape(B, L, ATTN_HEADS, QK_NOPE + V_HEAD)
    k_nope, v = kv[..., :QK_NOPE], kv[..., QK_NOPE:]

    n_blk = -(-L // QBLK)
    L_pad = n_blk * QBLK
    pad = L_pad - L
    q_nope, q_pe, k_nope, k_pe, v = (_pad_seq(a, pad) for a in (q_nope, q_pe, k_nope, k_pe, v))
    kpos = jnp.arange(L_pad)

    def one_block(i):
        s0 = i * QBLK
        qn = lax.dynamic_slice_in_dim(q_nope, s0, QBLK, axis=1)
        qp = lax.dynamic_slice_in_dim(q_pe, s0, QBLK, axis=1)
        sc = (jnp.einsum('bqhd,bkhd->bhqk', qn, k_nope)
              + jnp.einsum('bqhr,bkr->bhqk', qp, k_pe)).astype(jnp.float32) * ATTN_SCALE
        qpos = s0 + jnp.arange(QBLK)
        sc = jnp.where(kpos[None, :] <= qpos[:, None], sc, -jnp.inf)
        p = jax.nn.softmax(sc, axis=-1).astype(v.dtype)
        return jnp.einsum('bhqk,bkhd->bqhd', p, v)

    o = lax.map(one_block, jnp.arange(n_blk))
    o = jnp.moveaxis(o, 0, 1).reshape(B, L_pad, D_ATTN)
    return o[:, :L]


def _short_conv(b_g, c_g, u, conv_w):
    v = c_g * u
    y = lax.conv_general_dilated(v, conv_w.astype(v.dtype)[:, None, :], window_strides=(1,),
                                 padding=[(CONV_W - 1, 0)],
                                 dimension_numbers=('NWC', 'WIO', 'NWC'),
                                 feature_group_count=D_CONV)
    return b_g * y


def _swiglu(x, wg, wu, wd):
    return (jax.nn.silu(x @ wg) * (x @ wu)) @ wd


def _moe(h, w_router, router_bias, w_gate, w_up, w_down, ws_gate, ws_up, ws_down):
    Bn, L, D = h.shape
    xt = h.reshape(-1, D)
    N = xt.shape[0]
    s = jax.nn.sigmoid((xt @ w_router).astype(jnp.float32))
    sel = s + router_bias.astype(jnp.float32)
    gscore = lax.top_k(sel.reshape(N, N_GROUPS, N_EXPERTS // N_GROUPS), 2)[0].sum(-1)
    _, gidx = lax.top_k(gscore, TOPK_GROUPS)
    gmask = jnp.any(gidx[..., None] == jnp.arange(N_GROUPS), axis=1)
    emask = jnp.repeat(gmask, N_EXPERTS // N_GROUPS, axis=1)
    _, eidx = lax.top_k(jnp.where(emask, sel, -jnp.inf), TOP_K)
    gates = jnp.take_along_axis(s, eidx, axis=1)
    gates = gates / jnp.sum(gates, axis=-1, keepdims=True) * ROUTED_SCALE

    A = N * TOP_K
    flat_e = eidx.reshape(-1)
    flat_w = gates.reshape(-1).astype(xt.dtype)
    order = jnp.argsort(flat_e)
    sorted_e = flat_e[order]
    tok = order // TOP_K
    counts = jnp.bincount(flat_e, length=N_EXPERTS)
    padded = (counts + MOE_BLK - 1) // MOE_BLK * MOE_BLK
    pad_end = jnp.cumsum(padded)
    pad_start = pad_end - padded
    start = jnp.cumsum(counts) - counts
    dest = pad_start[sorted_e] + jnp.arange(A) - start[sorted_e]
    n_blocks = -(-A // MOE_BLK) + N_EXPERTS
    P = n_blocks * MOE_BLK
    buf_tok = jnp.full((P,), N, dtype=jnp.int32).at[dest].set(tok.astype(jnp.int32))
    buf_w = jnp.zeros((P,), dtype=xt.dtype).at[dest].set(flat_w[order])
    blk_e = jnp.minimum(jnp.searchsorted(pad_end, jnp.arange(n_blocks) * MOE_BLK, side='right'),
                        N_EXPERTS - 1)
    x_pad = jnp.concatenate([xt, jnp.zeros((1, D), xt.dtype)], axis=0)

    def run_block(args):
        t, g, e = args
        xb = x_pad[t]
        return _swiglu(xb, w_gate[e], w_up[e], w_down[e]) * g[:, None]

    out = lax.map(run_block, (buf_tok.reshape(n_blocks, MOE_BLK),
                              buf_w.reshape(n_blocks, MOE_BLK), blk_e))
    routed = jnp.zeros((N + 1, D), xt.dtype).at[buf_tok].add(out.reshape(P, D))[:N]
    shared = _swiglu(xt, ws_gate, ws_up, ws_down)
    return (routed + shared).reshape(Bn, L, D)


def setup_inputs(seed: int = 0) -> dict:
    key = jax.random.key(seed)
    ks = jax.random.split(key, 32)
    f32 = jnp.float32

    def nrm(k, shape, scale):
        return jax.random.normal(k, shape, f32) * scale

    def gain(k, shape):
        return 1.0 + 0.02 * jax.random.normal(k, shape, f32)

    Ld = DEPTH
    return {
        "x": nrm(ks[0], (BATCH, SEQ, D_MODEL), 1.0),
        "meta_tokens": nrm(ks[1], (N_META, D_MODEL), 1.0),
        "emb_ln_g": gain(ks[2], (D_MODEL,)),
        "emb_ln_b": nrm(ks[3], (D_MODEL,), 0.02),
        "w_in": nrm(ks[4], (Ld, D_MODEL, D_IN), D_MODEL ** -0.5),
        "q_norm": gain(ks[5], (Ld, Q_LORA)),
        "w_uq": nrm(ks[6], (Ld, Q_LORA, ATTN_HEADS * (QK_NOPE + QK_ROPE)), Q_LORA ** -0.5),
        "kv_norm": gain(ks[7], (Ld, KV_LORA)),
        "w_ukv": nrm(ks[8], (Ld, KV_LORA, ATTN_HEADS * (QK_NOPE + V_HEAD)), KV_LORA ** -0.5),
        "conv_w": nrm(ks[9], (Ld, CONV_W, D_CONV), CONV_W ** -0.5),
        "attn_out_norm": gain(ks[10], (Ld, D_ATTN)),
        "conv_out_norm": gain(ks[11], (Ld, D_CONV)),
        "w_o": nrm(ks[12], (Ld, D_MIX, D_MODEL), BETA * D_MIX ** -0.5),
        "ln1_g": gain(ks[13], (Ld, D_MODEL)),
        "ln1_b": nrm(ks[14], (Ld, D_MODEL), 0.02),
        "w_router": nrm(ks[15], (Ld, D_MODEL, N_EXPERTS), D_MODEL ** -0.5),
        "router_bias": nrm(ks[16], (Ld, N_EXPERTS), 0.01),
        "w_gate": nrm(ks[17], (Ld, N_EXPERTS, D_MODEL, D_EXPERT), D_MODEL ** -0.5),
        "w_up": nrm(ks[18], (Ld, N_EXPERTS, D_MODEL, D_EXPERT), D_MODEL ** -0.5),
        "w_down": nrm(ks[19], (Ld, N_EXPERTS, D_EXPERT, D_MODEL), BETA * D_EXPERT ** -0.5),
        "ws_gate": nrm(ks[20], (Ld, D_MODEL, D_SHARED), D_MODEL ** -0.5),
        "ws_up": nrm(ks[21], (Ld, D_MODEL, D_SHARED), D_MODEL ** -0.5),
        "ws_down": nrm(ks[22], (Ld, D_SHARED, D_MODEL), BETA * D_SHARED ** -0.5),
        "ln2_g": gain(ks[23], (Ld, D_MODEL)),
        "ln2_b": nrm(ks[24], (Ld, D_MODEL), 0.02),
    }


def reference(x, meta_tokens, emb_ln_g, emb_ln_b, w_in, q_norm, w_uq, kv_norm, w_ukv, conv_w,
              attn_out_norm, conv_out_norm, w_o, ln1_g, ln1_b, w_router, router_bias,
              w_gate, w_up, w_down, ws_gate, ws_up, ws_down, ln2_g, ln2_b):
    B = x.shape[0]
    meta = jnp.broadcast_to(meta_tokens.astype(x.dtype)[None], (B, N_META, D_MODEL))
    h = jnp.concatenate([meta, x], axis=1)
    h = _layernorm(h, emb_ln_g, emb_ln_b)
    o1 = Q_LORA
    o2 = o1 + KV_LORA
    o3 = o2 + QK_ROPE
    o4 = o3 + D_CONV
    o5 = o4 + D_CONV
    for l in range(DEPTH):
        z = h @ w_in[l]
        attn = _mla(z[..., :o1], z[..., o1:o2], z[..., o2:o3],
                    q_norm[l], w_uq[l], kv_norm[l], w_ukv[l])
        conv = _short_conv(z[..., o3:o4], z[..., o4:o5], z[..., o5:], conv_w[l])
        mix = jnp.concatenate([_rms(attn, attn_out_norm[l]), _rms(conv, conv_out_norm[l])],
                              axis=-1) @ w_o[l]
        h = _layernorm(ALPHA * h + mix, ln1_g[l], ln1_b[l])
        ff = _moe(h, w_router[l], router_bias[l], w_gate[l], w_up[l], w_down[l],
                  ws_gate[l], ws_up[l], ws_down[l])
        h = _layernorm(ALPHA * h + ff, ln2_g[l], ln2_b[l])
    return h[:, N_META:]
```

```python
import functools
import math

import jax
import jax.numpy as jnp
from jax import lax
from jax.experimental import pallas as pl
from jax.experimental.pallas import tpu as pltpu
from jax.experimental.pallas import tpu_sc as plsc

D_MODEL = 1024
N_META = 16
ATTN_HEADS = 8
QK_NOPE = 64
QK_ROPE = 32
V_HEAD = 64
Q_LORA = 384
KV_LORA = 256
ROPE_THETA = 10000.0
D_ATTN = ATTN_HEADS * V_HEAD
D_CONV = 512
CONV_W = 3
ATTN_SCALE = 1.0 / math.sqrt(QK_NOPE + QK_ROPE)
N_EXPERTS = 256
TOP_K = 8
N_GROUPS = 8
GROUP_SIZE = N_EXPERTS // N_GROUPS
TOPK_GROUPS = 4
D_EXPERT = 256
D_SHARED = 256
ROUTED_SCALE = 2.5
DEPTH = 1
ALPHA = (2.0 * DEPTH) ** 0.25

LANES = 128
SUBLANES = 8
HEAD_PAD = LANES
ROW_TILES = D_MODEL // LANES
SC_CORES = 2
SC_SUBCORES = 16

T_IN = 512
T_Q = 256
T_MIX = 256
BLK = 256
T_FIN = 128
SC_ROWS = 64

VMEM_LIMIT = 56 * 1024 * 1024

_NEG_INF = float("-inf")


def _bf(x):
    return x.astype(jnp.bfloat16)


def _dot(a, b):
    return jnp.dot(a, b, preferred_element_type=jnp.float32)


def _dot_t(a, b):
    return lax.dot_general(a, b, (((1,), (1,)), ((), ())), preferred_element_type=jnp.float32)


def _layernorm(x, g, b, eps=1e-5):
    mu = jnp.mean(x, axis=-1, keepdims=True)
    xc = x - mu
    var = jnp.mean(xc * xc, axis=-1, keepdims=True)
    return xc * lax.rsqrt(var + eps) * g + b


def _rms(x, g, n, eps=1e-6):
    ms = jnp.sum(x * x, axis=-1, keepdims=True) * (1.0 / n)
    return x * lax.rsqrt(ms + eps) * g


def _sigmoid(x):
    return 1.0 / (1.0 + jnp.exp(-x))


def _rope(x, c, s_up, s_dn):
    return x * c + pltpu.roll(x, QK_ROPE // 2, 1) * s_up + pltpu.roll(x, LANES - QK_ROPE // 2, 1) * s_dn


def _inproj_body(x_ref, g_ref, b_ref, win_ref, qn_ref, wuq_ref, kvn_ref, wukv_ref, cw_ref, con_ref,
                 tc_ref, tu_ref, td_ref, vcinit_ref,
                 h0_ref, q_ref, k_ref, v_ref, cn_ref, vct_ref, carry_ref):
    rows = x_ref.shape[1]

    @pl.when(pl.program_id(1) == 0)
    def _():
        carry_ref[...] = vcinit_ref[...]

    h0 = _layernorm(x_ref[0], g_ref[...], b_ref[...])
    h0_ref[0] = h0
    z = _dot(_bf(h0), win_ref[...])
    o1 = Q_LORA
    o2 = o1 + KV_LORA
    o3 = o2 + HEAD_PAD
    o4 = o3 + D_CONV
    o5 = o4 + D_CONV
    c, s_up, s_dn = tc_ref[...], tu_ref[...], td_ref[...]

    q = _dot(_bf(_rms(z[:, :o1], qn_ref[...], Q_LORA)), wuq_ref[...])
    for h in range(ATTN_HEADS):
        sl = slice(h * HEAD_PAD, (h + 1) * HEAD_PAD)
        q_ref[0, :, sl] = _bf(_rope(q[:, sl], c, s_up, s_dn) * ATTN_SCALE)

    kv = _dot(_bf(_rms(z[:, o1:o2], kvn_ref[...], KV_LORA)), wukv_ref[...])
    kpe = _rope(z[:, o2:o3], c, s_up, s_dn)
    lane = lax.broadcasted_iota(jnp.int32, (rows, HEAD_PAD), 1)
    for h in range(ATTN_HEADS):
        sl = slice(h * HEAD_PAD, (h + 1) * HEAD_PAD)
        k_ref[0, :, sl] = _bf(kv[:, sl] + kpe)
        vh = kv[:, ATTN_HEADS * HEAD_PAD + h * HEAD_PAD: ATTN_HEADS * HEAD_PAD + (h + 1) * HEAD_PAD]
        v_ref[0, :, sl] = _bf(jnp.where(lane < V_HEAD, vh, 1.0))

    vc = z[:, o4:o5] * z[:, o5:]
    carry = carry_ref[...]
    row = lax.broadcasted_iota(jnp.int32, (rows, D_CONV), 0)
    vc1 = jnp.where(row == 0, carry[SUBLANES - 1:SUBLANES, :], pltpu.roll(vc, 1, 0))
    vc2 = pltpu.roll(vc, 2, 0)
    vc2 = jnp.where(row == 0, carry[SUBLANES - 2:SUBLANES - 1, :], vc2)
    vc2 = jnp.where(row == 1, carry[SUBLANES - 1:SUBLANES, :], vc2)
    cw = cw_ref[...]
    conv = z[:, o3:o4] * (cw[0:1, :] * vc2 + cw[1:2, :] * vc1 + cw[2:3, :] * vc)
    cn_ref[0] = _bf(_rms(conv, con_ref[...], D_CONV))
    tail = vc[rows - SUBLANES:, :]
    carry_ref[...] = tail
    vct_ref[0] = tail


def _inproj(x, tabs, vcinit, w, rows):
    bsz, seq, _ = x.shape
    nj = seq // rows
    full = lambda a: pl.BlockSpec(a.shape, lambda b, j: (0,) * a.ndim)
    row_blk = lambda width: pl.BlockSpec((1, rows, width), lambda b, j: (b, j, 0))
    tab = pl.BlockSpec((rows, HEAD_PAD), lambda b, j: (j, 0))
    params = [w["emb_g"], w["emb_b"], w["w_in"], w["q_norm"], w["w_uq"], w["kv_norm"], w["w_ukv"],
              w["conv_w"], w["conv_norm"]]
    wide = ATTN_HEADS * HEAD_PAD
    out_shape = [
        jax.ShapeDtypeStruct((bsz, seq, D_MODEL), jnp.float32),
        jax.ShapeDtypeStruct((bsz, seq, wide), jnp.bfloat16),
        jax.ShapeDtypeStruct((bsz, seq, wide), jnp.bfloat16),
        jax.ShapeDtypeStruct((bsz, seq, wide), jnp.bfloat16),
        jax.ShapeDtypeStruct((bsz, seq, D_CONV), jnp.bfloat16),
        jax.ShapeDtypeStruct((bsz, nj * SUBLANES, D_CONV), jnp.float32),
    ]
    return pl.pallas_call(
        _inproj_body,
        grid=(bsz, nj),
        in_specs=[row_blk(D_MODEL)] + [full(p) for p in params] + [tab, tab, tab, full(vcinit)],
        out_specs=[row_blk(D_MODEL), row_blk(wide), row_blk(wide), row_blk(wide), row_blk(D_CONV),
                   pl.BlockSpec((1, SUBLANES, D_CONV), lambda b, j: (b, j, 0))],
        out_shape=out_shape,
        scratch_shapes=[pltpu.VMEM((SUBLANES, D_CONV), jnp.float32)],
        compiler_params=pltpu.CompilerParams(dimension_semantics=("arbitrary", "arbitrary"),
                                             vmem_limit_bytes=VMEM_LIMIT),
        name="inproj",
    )(x, *params, *tabs, vcinit)


def _attn_body(q_ref, k_ref, v_ref, km_ref, vm_ref, o_ref):
    i = pl.program_id(2)
    tq = q_ref.shape[1]
    rowi = lax.broadcasted_iota(jnp.int32, (tq, tq), 0)
    coli = lax.broadcasted_iota(jnp.int32, (tq, tq), 1)
    accs = []
    for hh in range(2):
        sl = slice(hh * HEAD_PAD, (hh + 1) * HEAD_PAD)
        q = q_ref[0, :, sl]
        s = _dot_t(q, km_ref[:, sl])
        m = jnp.max(s, axis=1, keepdims=True)
        acc = _dot(_bf(jnp.exp(s - m)), vm_ref[:, sl])

        def step(s, m, acc, vb):
            m_new = jnp.maximum(m, jnp.max(s, axis=1, keepdims=True))
            p = jnp.exp(s - m_new)
            return m_new, jnp.exp(m - m_new) * acc + _dot(_bf(p), vb)

        def body(j, carry, q=q, sl=sl):
            m, acc = carry
            r0 = pl.multiple_of(j * tq, tq)
            s = _dot_t(q, k_ref[0, pl.ds(r0, tq), sl])
            return step(s, m, acc, v_ref[0, pl.ds(r0, tq), sl])

        m, acc = lax.fori_loop(0, i, body, (m, acc))
        r0 = pl.multiple_of(i * tq, tq)
        s = _dot_t(q, k_ref[0, pl.ds(r0, tq), sl])
        s = jnp.where(coli <= rowi, s, _NEG_INF)
        m, acc = step(s, m, acc, v_ref[0, pl.ds(r0, tq), sl])
        accs.append(acc)
    lane = lax.broadcasted_iota(jnp.int32, (tq, HEAD_PAD), 1)
    a0, a1 = accs
    o0 = a0 / pltpu.roll(a0, V_HEAD, 1)
    o1 = pltpu.roll(a1, V_HEAD, 1) / a1
    o_ref[0] = _bf(jnp.where(lane < V_HEAD, o0, o1))


def _attention(q, k, v, km, vm):
    bsz, seq, _ = q.shape
    pairs = ATTN_HEADS // 2
    pw = 2 * HEAD_PAD
    return pl.pallas_call(
        _attn_body,
        grid=(bsz, pairs, seq // T_Q),
        in_specs=[pl.BlockSpec((1, T_Q, pw), lambda b, p, i: (b, i, p)),
                  pl.BlockSpec((1, seq, pw), lambda b, p, i: (b, 0, p)),
                  pl.BlockSpec((1, seq, pw), lambda b, p, i: (b, 0, p)),
                  pl.BlockSpec((N_META, pw), lambda b, p, i: (0, p)),
                  pl.BlockSpec((N_META, pw), lambda b, p, i: (0, p))],
        out_specs=pl.BlockSpec((1, T_Q, 2 * V_HEAD), lambda b, p, i: (b, i, p)),
        out_shape=jax.ShapeDtypeStruct((bsz, seq, D_ATTN), jnp.bfloat16),
        compiler_params=pltpu.CompilerParams(dimension_semantics=("arbitrary",) * 3,
                                             vmem_limit_bytes=VMEM_LIMIT),
        name="attention",
    )(q, k, v, km, vm)


def _mix_body(attn_ref, cn_ref, h0_ref, an_ref, woa_ref, woc_ref, g1_ref, b1_ref,
              wsg_ref, wsu_ref, wsd_ref, wrt_ref, rb_ref, tri_ref,
              hp_ref, base_ref, eidx_ref, gate_ref, pos_ref, cnt_ref, run_ref):
    t = attn_ref.shape[0]

    @pl.when(pl.program_id(0) == 0)
    def _():
        run_ref[...] = jnp.zeros_like(run_ref)

    attn_n = _rms(attn_ref[...].astype(jnp.float32), an_ref[...], D_ATTN)
    mix = _dot(_bf(attn_n), woa_ref[...]) + _dot(cn_ref[...], woc_ref[...])
    h1 = _layernorm(ALPHA * h0_ref[...] + mix, g1_ref[...], b1_ref[...])
    for s in range(ROW_TILES):
        hp_ref[:, s, :] = h1[:, s * LANES:(s + 1) * LANES]
    h1b = _bf(h1)
    g = _dot(h1b, wsg_ref[...])
    u = _dot(h1b, wsu_ref[...])
    base_ref[...] = ALPHA * h1 + _dot(_bf(g * _sigmoid(g) * u), wsd_ref[...])

    shp = (N_GROUPS, GROUP_SIZE, t)
    sc = _sigmoid(_dot_t(wrt_ref[...], h1b)).reshape(shp)
    sel = sc + rb_ref[...].reshape(N_GROUPS, GROUP_SIZE, 1)
    gio = lax.broadcasted_iota(jnp.int32, shp, 0)
    rio = lax.broadcasted_iota(jnp.int32, shp, 1)
    eio = gio * GROUP_SIZE + rio

    m1 = jnp.max(sel, axis=1, keepdims=True)
    i1 = jnp.min(jnp.where(sel == m1, rio, GROUP_SIZE), axis=1, keepdims=True)
    m2 = jnp.max(jnp.where(rio == i1, _NEG_INF, sel), axis=1, keepdims=True)
    gs = m1 + m2
    gio1 = lax.broadcasted_iota(jnp.int32, (N_GROUPS, 1, t), 0)
    gmask = jnp.zeros((N_GROUPS, 1, t), jnp.bool_)
    for _ in range(TOPK_GROUPS):
        mg = jnp.max(gs, axis=0, keepdims=True)
        ig = jnp.min(jnp.where(gs == mg, gio1, N_GROUPS), axis=0, keepdims=True)
        hit = gio1 == ig
        gmask = jnp.logical_or(gmask, hit)
        gs = jnp.where(hit, _NEG_INF, gs)

    cand = jnp.where(gmask, sel, _NEG_INF)

    def red(x, op):
        return op(op(x, axis=0, keepdims=True), axis=1, keepdims=True)

    picked = jnp.zeros(shp, jnp.bool_)
    e_rows, g_rows = [], []
    for _ in range(TOP_K):
        mk = red(cand, jnp.max)
        ek = red(jnp.where(cand == mk, eio, N_EXPERTS), jnp.min)
        hit = eio == ek
        g_rows.append(red(jnp.where(hit, sc, 0.0), jnp.sum))
        e_rows.append(ek)
        picked = jnp.logical_or(picked, hit)
        cand = jnp.where(hit, _NEG_INF, cand)
    gsum = g_rows[0]
    for gk in g_rows[1:]:
        gsum = gsum + gk

    pm = jnp.where(picked, 1.0, 0.0).reshape(N_EXPERTS, t)
    rank = _dot(_bf(pm), tri_ref[...])
    slot = (run_ref[...] + rank).reshape(shp)
    run_ref[...] = run_ref[...] + jnp.sum(pm, axis=1, keepdims=True)
    cnt_ref[...] = run_ref[...]
    for k in range(TOP_K):
        eidx_ref[0, k:k + 1, :] = e_rows[k].reshape(1, t)
        gate_ref[0, k:k + 1, :] = (g_rows[k] / gsum * ROUTED_SCALE).reshape(1, t)
        pk = red(jnp.where(eio == e_rows[k], slot, 0.0), jnp.sum)
        pos_ref[0, k:k + 1, :] = pk.reshape(1, t).astype(jnp.int32)


def _mix_route(attn, cn, h0, w):
    n = attn.shape[0]
    nt = n // T_MIX
    tri = _bf(jnp.triu(jnp.ones((T_MIX, T_MIX), jnp.float32), 1))
    params = [w["attn_norm"], w["wo_a"], w["wo_c"], w["ln1_g"], w["ln1_b"], w["ws_gate"], w["ws_up"],
              w["ws_down"], w["w_router_t"], w["router_bias"], tri]
    full = lambda a: pl.BlockSpec(a.shape, lambda i: (0,) * a.ndim)
    row_blk = lambda width: pl.BlockSpec((T_MIX, width), lambda i: (i, 0))
    k_blk = pl.BlockSpec((1, TOP_K, T_MIX), lambda i: (i, 0, 0))
    out_shape = [
        jax.ShapeDtypeStruct((n, ROW_TILES, LANES), jnp.float32),
        jax.ShapeDtypeStruct((n, D_MODEL), jnp.float32),
        jax.ShapeDtypeStruct((nt, TOP_K, T_MIX), jnp.int32),
        jax.ShapeDtypeStruct((nt, TOP_K, T_MIX), jnp.float32),
        jax.ShapeDtypeStruct((nt, TOP_K, T_MIX), jnp.int32),
        jax.ShapeDtypeStruct((N_EXPERTS, 1), jnp.float32),
    ]
    return pl.pallas_call(
        _mix_body,
        grid=(nt,),
        in_specs=[row_blk(D_ATTN), row_blk(D_CONV), row_blk(D_MODEL)] + [full(p) for p in params],
        out_specs=[pl.BlockSpec((T_MIX, ROW_TILES, LANES), lambda i: (i, 0, 0)), row_blk(D_MODEL),
                   k_blk, k_blk, k_blk, pl.BlockSpec((N_EXPERTS, 1), lambda i: (0, 0))],
        out_shape=out_shape,
        scratch_shapes=[pltpu.VMEM((N_EXPERTS, 1), jnp.float32)],
        compiler_params=pltpu.CompilerParams(dimension_semantics=("arbitrary",),
                                             vmem_limit_bytes=VMEM_LIMIT),
        name="mix_route",
    )(attn, cn, h0, *params)


def _sc_mesh():
    return plsc.VectorSubcoreMesh(core_axis_name="c", subcore_axis_name="s")


def _worker_id():
    return lax.axis_index("s") * SC_CORES + lax.axis_index("c")


def _dispatch(hp, dest, n_slots):
    n = hp.shape[0]
    per_worker = n // SC_ROWS // (SC_CORES * SC_SUBCORES)

    @functools.partial(
        pl.kernel, mesh=_sc_mesh(),
        out_type=jax.ShapeDtypeStruct((n_slots, ROW_TILES, LANES), jnp.float32),
        scratch_types=[pltpu.VMEM((TOP_K, SC_ROWS), jnp.int32),
                       pltpu.VMEM((SC_ROWS, ROW_TILES, LANES), jnp.float32),
                       pltpu.SemaphoreType.DMA],
        name="dispatch")
    def run(hp_hbm, dest_hbm, xs_hbm, idx_v, rows_v, sem):
        first = _worker_id() * per_worker

        @pl.loop(0, per_worker)
        def _(c):
            chunk = first + c
            pltpu.sync_copy(dest_hbm.at[chunk], idx_v)
            pltpu.sync_copy(hp_hbm.at[pl.ds(chunk * SC_ROWS, SC_ROWS)], rows_v)
            copies = [pltpu.async_copy(rows_v, xs_hbm.at[idx_v.at[k]], sem) for k in range(TOP_K)]
            for cp in copies:
                cp.wait()

    return run(hp, dest)


def _combine(ys, dest, n):
    per_worker = n // SC_ROWS // (SC_CORES * SC_SUBCORES)

    @functools.partial(
        pl.kernel, mesh=_sc_mesh(),
        out_type=jax.ShapeDtypeStruct((TOP_K, n, ROW_TILES, LANES), jnp.float32),
        scratch_types=[pltpu.VMEM((TOP_K, SC_ROWS), jnp.int32),
                       pltpu.VMEM((SC_ROWS, ROW_TILES, LANES), jnp.float32),
                       pltpu.SemaphoreType.DMA],
        name="combine")
    def run(ys_hbm, dest_hbm, out_hbm, idx_v, rows_v, sem):
        first = _worker_id() * per_worker

        @pl.loop(0, per_worker)
        def _(c):
            chunk = first + c
            pltpu.sync_copy(dest_hbm.at[chunk], idx_v)
            for k in range(TOP_K):
                pltpu.async_copy(ys_hbm.at[idx_v.at[k]], rows_v, sem).wait()
                pltpu.sync_copy(rows_v, out_hbm.at[k, pl.ds(chunk * SC_ROWS, SC_ROWS)])

    return run(ys, dest)


def _expert_body(blk_e_ref, nused_ref, xs_ref, wg_ref, wu_ref, wd_ref, ys_ref):
    @pl.when(pl.program_id(0) < nused_ref[0])
    def _():
        x = jnp.concatenate([_bf(xs_ref[:, s, :]) for s in range(ROW_TILES)], axis=1)
        g = _dot(x, _bf(wg_ref[0]))
        u = _dot(x, _bf(wu_ref[0]))
        y = _dot(_bf(g * _sigmoid(g) * u), _bf(wd_ref[0]))
        for s in range(ROW_TILES):
            ys_ref[:, s, :] = y[:, s * LANES:(s + 1) * LANES]


def _experts(xs, blk_e, nused, w_gate, w_up, w_down):
    n_slots = xs.shape[0]
    nb = n_slots // BLK
    row_map = lambda i, be, nu: (jnp.minimum(i, nu[0] - 1), 0, 0)
    w_map = lambda i, be, nu: (be[i], 0, 0)
    grid_spec = pltpu.PrefetchScalarGridSpec(
        num_scalar_prefetch=2,
        grid=(nb,),
        in_specs=[pl.BlockSpec((BLK, ROW_TILES, LANES), row_map),
                  pl.BlockSpec((1, D_MODEL, D_EXPERT), w_map),
                  pl.BlockSpec((1, D_MODEL, D_EXPERT), w_map),
                  pl.BlockSpec((1, D_EXPERT, D_MODEL), w_map)],
        out_specs=pl.BlockSpec((BLK, ROW_TILES, LANES), row_map),
    )
    return pl.pallas_call(
        _expert_body,
        grid_spec=grid_spec,
        out_shape=jax.ShapeDtypeStruct((n_slots, ROW_TILES, LANES), jnp.float32),
        compiler_params=pltpu.CompilerParams(dimension_semantics=("arbitrary",),
                                             vmem_limit_bytes=VMEM_LIMIT),
        name="experts",
    )(blk_e, nused, xs, w_gate, w_up, w_down)


def _final_body(yg_ref, gate_ref, base_ref, g2_ref, b2_ref, o_ref):
    gate = gate_ref[...]
    cols = []
    for s in range(ROW_TILES):
        acc = gate[:, 0:1] * yg_ref[0, :, s, :]
        for k in range(1, TOP_K):
            acc = acc + gate[:, k:k + 1] * yg_ref[k, :, s, :]
        cols.append(acc)
    ff = jnp.concatenate(cols, axis=1)
    o_ref[...] = _layernorm(base_ref[...] + ff, g2_ref[...], b2_ref[...])


def _final(yg, gates, base, g2, b2):
    n = base.shape[0]
    full = lambda a: pl.BlockSpec(a.shape, lambda i: (0,) * a.ndim)
    return pl.pallas_call(
        _final_body,
        grid=(n // T_FIN,),
        in_specs=[pl.BlockSpec((TOP_K, T_FIN, ROW_TILES, LANES), lambda i: (0, i, 0, 0)),
                  pl.BlockSpec((T_FIN, TOP_K), lambda i: (i, 0)),
                  pl.BlockSpec((T_FIN, D_MODEL), lambda i: (i, 0)),
                  full(g2), full(b2)],
        out_specs=pl.BlockSpec((T_FIN, D_MODEL), lambda i: (i, 0)),
        out_shape=jax.ShapeDtypeStruct((n, D_MODEL), jnp.float32),
        compiler_params=pltpu.CompilerParams(dimension_semantics=("arbitrary",),
                                             vmem_limit_bytes=VMEM_LIMIT),
        name="final",
    )(yg, gates, base, g2, b2)


def _rope_tables(pos):
    half = QK_ROPE // 2
    inv = ROPE_THETA ** (-jnp.arange(0, QK_ROPE, 2, dtype=jnp.float32) / QK_ROPE)
    ang = pos.astype(jnp.float32)[:, None] * inv[None, :]
    cos, sin = jnp.cos(ang), jnp.sin(ang)
    n = pos.shape[0]
    z = lambda w: jnp.zeros((n, w), jnp.float32)
    c = jnp.concatenate([jnp.ones((n, QK_NOPE), jnp.float32), cos, cos, z(LANES - QK_NOPE - QK_ROPE)], axis=1)
    s_up = jnp.concatenate([z(QK_NOPE + half), sin, z(LANES - QK_NOPE - QK_ROPE)], axis=1)
    s_dn = jnp.concatenate([z(QK_NOPE), -sin, z(half), z(LANES - QK_NOPE - QK_ROPE)], axis=1)
    return c, s_up, s_dn


def _pad_heads(w, per_head, keep):
    rows = w.shape[0]
    w = w.reshape(rows, ATTN_HEADS, per_head)[:, :, :keep]
    w = jnp.pad(w, ((0, 0), (0, 0), (0, HEAD_PAD - keep)))
    return w.reshape(rows, ATTN_HEADS * HEAD_PAD)


def _prep_weights(emb_ln_g, emb_ln_b, w_in, q_norm, w_uq, kv_norm, w_ukv, conv_w, attn_out_norm,
                  conv_out_norm, w_o, ln1_g, ln1_b, w_router, router_bias, ws_gate, ws_up, ws_down):
    o1 = Q_LORA
    o2 = o1 + KV_LORA
    o3 = o2 + QK_ROPE
    row = lambda a: a.reshape(1, -1)
    kpe_cols = jnp.pad(w_in[:, o2:o3], ((0, 0), (QK_NOPE, HEAD_PAD - QK_NOPE - QK_ROPE)))
    w_in_p = jnp.concatenate([w_in[:, :o2], kpe_cols, w_in[:, o3:]], axis=1)
    ukv = w_ukv.reshape(KV_LORA, ATTN_HEADS, QK_NOPE + V_HEAD)
    w_uk = jnp.pad(ukv[:, :, :QK_NOPE], ((0, 0), (0, 0), (0, HEAD_PAD - QK_NOPE)))
    w_uv = jnp.pad(ukv[:, :, QK_NOPE:], ((0, 0), (0, 0), (0, HEAD_PAD - V_HEAD)))
    w_ukv_p = jnp.concatenate([w_uk.reshape(KV_LORA, -1), w_uv.reshape(KV_LORA, -1)], axis=1)
    woa = w_o[:D_ATTN]
    return {
        "emb_g": row(emb_ln_g), "emb_b": row(emb_ln_b),
        "w_in": _bf(w_in_p),
        "q_norm": row(q_norm), "w_uq": _bf(_pad_heads(w_uq, QK_NOPE + QK_ROPE, QK_NOPE + QK_ROPE)),
        "kv_norm": row(kv_norm), "w_ukv": _bf(w_ukv_p),
        "conv_w": jnp.pad(conv_w, ((0, SUBLANES - CONV_W), (0, 0))), "conv_norm": row(conv_out_norm),
        "attn_norm": row(attn_out_norm), "wo_a": _bf(woa), "wo_c": _bf(w_o[D_ATTN:]),
        "ln1_g": row(ln1_g), "ln1_b": row(ln1_b),
        "ws_gate": _bf(ws_gate), "ws_up": _bf(ws_up), "ws_down": _bf(ws_down),
        "w_router_t": _bf(w_router.T), "router_bias": router_bias.reshape(-1, 1),
    }


def _slot_plan(eidx, pos, counts):
    counts = counts.reshape(-1).astype(jnp.int32)
    padded = (counts + BLK - 1) // BLK * BLK
    pad_end = jnp.cumsum(padded)
    pad_start = pad_end - padded
    dest = pad_start[eidx] + pos
    n_assign = eidx.size
    nb = n_assign // BLK + N_EXPERTS
    nused = pad_end[-1] // BLK
    blk = jnp.arange(nb, dtype=jnp.int32)
    blk_e = jnp.searchsorted(pad_end, jnp.minimum(blk, nused - 1) * BLK, side="right")
    blk_e = jnp.minimum(blk_e, N_EXPERTS - 1).astype(jnp.int32)
    return dest.astype(jnp.int32), blk_e, nused.reshape(1).astype(jnp.int32), nb * BLK


def _rechunk(a, rows):
    nt, k, t = a.shape
    return a.reshape(nt, k, t // rows, rows).transpose(0, 2, 1, 3).reshape(nt * (t // rows), k, rows)


def kernel(x, meta_tokens, emb_ln_g, emb_ln_b, w_in, q_norm, w_uq, kv_norm, w_ukv, conv_w, attn_out_norm, conv_out_norm, w_o, ln1_g, ln1_b, w_router, router_bias, w_gate, w_up, w_down, ws_gate, ws_up, ws_down, ln2_g, ln2_b):
    bsz, seq, _ = x.shape
    n = bsz * seq
    w = _prep_weights(emb_ln_g, emb_ln_b, w_in[0], q_norm[0], w_uq[0], kv_norm[0], w_ukv[0], conv_w[0],
                      attn_out_norm[0], conv_out_norm[0], w_o[0], ln1_g[0], ln1_b[0], w_router[0],
                      router_bias[0], ws_gate[0], ws_up[0], ws_down[0])

    meta_out = _inproj(meta_tokens[None], _rope_tables(jnp.arange(N_META)),
                       jnp.zeros((SUBLANES, D_CONV), jnp.float32), w, N_META)
    km, vm, vc_meta = meta_out[2][0], meta_out[3][0], meta_out[5][0]

    h0, q, k, v, cn, _ = _inproj(x, _rope_tables(N_META + jnp.arange(seq)), vc_meta, w, T_IN)
    attn = _attention(q, k, v, km, vm)

    hp, base, eidx, gates, pos, counts = _mix_route(
        attn.reshape(n, D_ATTN), cn.reshape(n, D_CONV), h0.reshape(n, D_MODEL), w)
    dest, blk_e, nused, n_slots = _slot_plan(eidx, pos, counts)
    dest = _rechunk(dest, SC_ROWS)

    xs = _dispatch(hp, dest, n_slots)
    ys = _experts(xs, blk_e, nused, w_gate[0], w_up[0], w_down[0])
    yg = _combine(ys, dest, n)
    gates_t = gates.transpose(0, 2, 1).reshape(n, TOP_K)
    out = _final(yg, gates_t, base, ln2_g[0].reshape(1, -1), ln2_b[0].reshape(1, -1))
    return out.reshape(bsz, seq, D_MODEL)
```

```python
import functools
import math

import jax
import jax.numpy as jnp
from jax import lax
from jax.experimental import pallas as pl
from jax.experimental.pallas import tpu as pltpu
from jax.experimental.pallas import tpu_sc as plsc

D_MODEL = 1024
N_META = 16
ATTN_HEADS = 8
QK_NOPE = 64
QK_ROPE = 32
V_HEAD = 64
Q_LORA = 384
KV_LORA = 256
ROPE_THETA = 10000.0
D_ATTN = ATTN_HEADS * V_HEAD
D_CONV = 512
CONV_W = 3
ATTN_SCALE = 1.0 / math.sqrt(QK_NOPE + QK_ROPE)
N_EXPERTS = 256
TOP_K = 8
N_GROUPS = 8
GROUP_SIZE = N_EXPERTS // N_GROUPS
TOPK_GROUPS = 4
D_EXPERT = 256
D_SHARED = 256
ROUTED_SCALE = 2.5
DEPTH = 1
ALPHA = (2.0 * DEPTH) ** 0.25

LANES = 128
SUBLANES = 8
HEAD_PAD = LANES
META_PAD = LANES
ROW_TILES = D_MODEL // LANES
SC_CORES = 2
SC_SUBCORES = 16

T_IN = 512
T_Q = 256
T_MIX = 256
BLK = 256
T_FIN = 128
SC_ROWS = 64

VMEM_LIMIT = 56 * 1024 * 1024

_NEG_INF = float("-inf")


def _bf(x):
    return x.astype(jnp.bfloat16)


def _dot(a, b):
    return jnp.dot(a, b, preferred_element_type=jnp.float32)


def _dot_t(a, b):
    return lax.dot_general(a, b, (((1,), (1,)), ((), ())), preferred_element_type=jnp.float32)


def _layernorm(x, g, b, eps=1e-5):
    mu = jnp.mean(x, axis=-1, keepdims=True)
    xc = x - mu
    var = jnp.mean(xc * xc, axis=-1, keepdims=True)
    return xc * lax.rsqrt(var + eps) * g + b


def _rms(x, g, n, eps=1e-6):
    ms = jnp.sum(x * x, axis=-1, keepdims=True) * (1.0 / n)
    return x * lax.rsqrt(ms + eps) * g


def _sigmoid(x):
    return 1.0 / (1.0 + jnp.exp(-x))


def _rope(x, c, s_up, s_dn):
    return x * c + pltpu.roll(x, QK_ROPE // 2, 1) * s_up + pltpu.roll(x, LANES - QK_ROPE // 2, 1) * s_dn


def _inproj_body(x_ref, g_ref, b_ref, win_ref, qn_ref, wuq_ref, kvn_ref, wukv_ref, cw_ref, con_ref,
                 tc_ref, tu_ref, td_ref, vcinit_ref,
                 h0_ref, q_ref, k_ref, v_ref, cn_ref, vct_ref, carry_ref):
    rows = x_ref.shape[1]

    @pl.when(pl.program_id(1) == 0)
    def _():
        carry_ref[...] = vcinit_ref[...]

    h0 = _layernorm(x_ref[0], g_ref[...], b_ref[...])
    h0_ref[0] = h0
    z = _dot(_bf(h0), win_ref[...])
    o1 = Q_LORA
    o2 = o1 + KV_LORA
    o3 = o2 + HEAD_PAD
    o4 = o3 + D_CONV
    o5 = o4 + D_CONV
    c, s_up, s_dn = tc_ref[...], tu_ref[...], td_ref[...]

    q = _dot(_bf(_rms(z[:, :o1], qn_ref[...], Q_LORA)), wuq_ref[...])
    for h in range(ATTN_HEADS):
        sl = slice(h * HEAD_PAD, (h + 1) * HEAD_PAD)
        q_ref[0, :, sl] = _bf(_rope(q[:, sl], c, s_up, s_dn) * ATTN_SCALE)

    kv = _dot(_bf(_rms(z[:, o1:o2], kvn_ref[...], KV_LORA)), wukv_ref[...])
    kpe = _rope(z[:, o2:o3], c, s_up, s_dn)
    lane = lax.broadcasted_iota(jnp.int32, (rows, HEAD_PAD), 1)
    for h in range(ATTN_HEADS):
        sl = slice(h * HEAD_PAD, (h + 1) * HEAD_PAD)
        k_ref[0, :, sl] = _bf(kv[:, sl] + kpe)
        vh = kv[:, ATTN_HEADS * HEAD_PAD + h * HEAD_PAD: ATTN_HEADS * HEAD_PAD + (h + 1) * HEAD_PAD]
        v_ref[0, :, sl] = _bf(jnp.where(lane < V_HEAD, vh, 1.0))

    vc = z[:, o4:o5] * z[:, o5:]
    carry = carry_ref[...]
    row = lax.broadcasted_iota(jnp.int32, (rows, D_CONV), 0)
    vc1 = jnp.where(row == 0, carry[SUBLANES - 1:SUBLANES, :], pltpu.roll(vc, 1, 0))
    vc2 = pltpu.roll(vc, 2, 0)
    vc2 = jnp.where(row == 0, carry[SUBLANES - 2:SUBLANES - 1, :], vc2)
    vc2 = jnp.where(row == 1, carry[SUBLANES - 1:SUBLANES, :], vc2)
    cw = cw_ref[...]
    conv = z[:, o3:o4] * (cw[0:1, :] * vc2 + cw[1:2, :] * vc1 + cw[2:3, :] * vc)
    cn_ref[0] = _bf(_rms(conv, con_ref[...], D_CONV))
    tail = vc[rows - SUBLANES:, :]
    carry_ref[...] = tail
    vct_ref[0] = tail


def _inproj(x, tabs, vcinit, w, rows):
    bsz, seq, _ = x.shape
    nj = seq // rows
    full = lambda a: pl.BlockSpec(a.shape, lambda b, j: (0,) * a.ndim)
    row_blk = lambda width: pl.BlockSpec((1, rows, width), lambda b, j: (b, j, 0))
    tab = pl.BlockSpec((rows, HEAD_PAD), lambda b, j: (j, 0))
    params = [w["emb_g"], w["emb_b"], w["w_in"], w["q_norm"], w["w_uq"], w["kv_norm"], w["w_ukv"],
              w["conv_w"], w["conv_norm"]]
    wide = ATTN_HEADS * HEAD_PAD
    out_shape = [
        jax.ShapeDtypeStruct((bsz, seq, D_MODEL), jnp.float32),
        jax.ShapeDtypeStruct((bsz, seq, wide), jnp.bfloat16),
        jax.ShapeDtypeStruct((bsz, seq, wide), jnp.bfloat16),
        jax.ShapeDtypeStruct((bsz, seq, wide), jnp.bfloat16),
        jax.ShapeDtypeStruct((bsz, seq, D_CONV), jnp.bfloat16),
        jax.ShapeDtypeStruct((bsz, nj * SUBLANES, D_CONV), jnp.float32),
    ]
    return pl.pallas_call(
        _inproj_body,
        grid=(bsz, nj),
        in_specs=[row_blk(D_MODEL)] + [full(p) for p in params] + [tab, tab, tab, full(vcinit)],
        out_specs=[row_blk(D_MODEL), row_blk(wide), row_blk(wide), row_blk(wide), row_blk(D_CONV),
                   pl.BlockSpec((1, SUBLANES, D_CONV), lambda b, j: (b, j, 0))],
        out_shape=out_shape,
        scratch_shapes=[pltpu.VMEM((SUBLANES, D_CONV), jnp.float32)],
        compiler_params=pltpu.CompilerParams(dimension_semantics=("arbitrary", "arbitrary"),
                                             vmem_limit_bytes=VMEM_LIMIT),
        name="inproj",
    )(x, *params, *tabs, vcinit)


def _attn_body(q_ref, k_ref, v_ref, km_ref, vm_ref, o_ref):
    i = pl.program_id(2)
    tq = q_ref.shape[1]
    heads = [slice(hh * HEAD_PAD, (hh + 1) * HEAD_PAD) for hh in range(2)]
    qs = [q_ref[0, :, sl] for sl in heads]

    causal = lax.broadcasted_iota(jnp.int32, (tq, tq), 1) <= lax.broadcasted_iota(jnp.int32, (tq, tq), 0)
    is_meta = lax.broadcasted_iota(jnp.int32, (tq, META_PAD), 1) < N_META
    r0 = pl.multiple_of(i * tq, tq)
    carry = []
    for q, sl in zip(qs, heads):
        sa = jnp.where(causal, _dot_t(q, k_ref[0, pl.ds(r0, tq), sl]), _NEG_INF)
        sb = jnp.where(is_meta, _dot_t(q, km_ref[:, sl]), _NEG_INF)
        m = jnp.maximum(jnp.max(sa, axis=1, keepdims=True), jnp.max(sb, axis=1, keepdims=True))
        acc = (_dot(_bf(jnp.exp(sa - m)), v_ref[0, pl.ds(r0, tq), sl])
               + _dot(_bf(jnp.exp(sb - m)), vm_ref[:, sl]))
        carry += [m, acc]

    def body(j, carry):
        rj = pl.multiple_of(j * tq, tq)
        out = []
        for h, (q, sl) in enumerate(zip(qs, heads)):
            m, acc = carry[2 * h], carry[2 * h + 1]
            s = _dot_t(q, k_ref[0, pl.ds(rj, tq), sl])
            m_new = jnp.maximum(m, jnp.max(s, axis=1, keepdims=True))
            p = jnp.exp(s - m_new)
            out += [m_new, jnp.exp(m - m_new) * acc + _dot(_bf(p), v_ref[0, pl.ds(rj, tq), sl])]
        return tuple(out)

    _, a0, _, a1 = lax.fori_loop(0, i, body, tuple(carry))
    lane = lax.broadcasted_iota(jnp.int32, (tq, HEAD_PAD), 1)
    o0 = a0 / pltpu.roll(a0, V_HEAD, 1)
    o1 = pltpu.roll(a1, V_HEAD, 1) / a1
    o_ref[0] = _bf(jnp.where(lane < V_HEAD, o0, o1))


def _attention(q, k, v, km, vm):
    bsz, seq, _ = q.shape
    pairs = ATTN_HEADS // 2
    pw = 2 * HEAD_PAD
    return pl.pallas_call(
        _attn_body,
        grid=(bsz, pairs, seq // T_Q),
        in_specs=[pl.BlockSpec((1, T_Q, pw), lambda b, p, i: (b, i, p)),
                  pl.BlockSpec((1, seq, pw), lambda b, p, i: (b, 0, p)),
                  pl.BlockSpec((1, seq, pw), lambda b, p, i: (b, 0, p)),
                  pl.BlockSpec((META_PAD, pw), lambda b, p, i: (0, p)),
                  pl.BlockSpec((META_PAD, pw), lambda b, p, i: (0, p))],
        out_specs=pl.BlockSpec((1, T_Q, 2 * V_HEAD), lambda b, p, i: (b, i, p)),
        out_shape=jax.ShapeDtypeStruct((bsz, seq, D_ATTN), jnp.bfloat16),
        compiler_params=pltpu.CompilerParams(dimension_semantics=("arbitrary",) * 3,
                                             vmem_limit_bytes=VMEM_LIMIT),
        name="attention",
    )(q, k, v, km, vm)


def _mix_body(attn_ref, cn_ref, h0_ref, an_ref, woa_ref, woc_ref, g1_ref, b1_ref,
              wsg_ref, wsu_ref, wsd_ref, wrt_ref, rb_ref, tri_ref,
              hp_ref, base_ref, eidx_ref, gate_ref, pos_ref, cnt_ref, run_ref):
    t = attn_ref.shape[0]

    @pl.when(pl.program_id(0) == 0)
    def _():
        run_ref[...] = jnp.zeros_like(run_ref)

    attn_n = _rms(attn_ref[...].astype(jnp.float32), an_ref[...], D_ATTN)
    mix = _dot(_bf(attn_n), woa_ref[...]) + _dot(cn_ref[...], woc_ref[...])
    h1 = _layernorm(ALPHA * h0_ref[...] + mix, g1_ref[...], b1_ref[...])
    for s in range(ROW_TILES):
        hp_ref[:, s, :] = h1[:, s * LANES:(s + 1) * LANES]
    h1b = _bf(h1)
    g = _dot(h1b, wsg_ref[...])
    u = _dot(h1b, wsu_ref[...])
    base_ref[...] = ALPHA * h1 + _dot(_bf(g * _sigmoid(g) * u), wsd_ref[...])

    shp = (N_GROUPS, GROUP_SIZE, t)
    sc = _sigmoid(_dot_t(wrt_ref[...], h1b)).reshape(shp)
    sel = sc + rb_ref[...].reshape(N_GROUPS, GROUP_SIZE, 1)
    gio = lax.broadcasted_iota(jnp.int32, shp, 0)
    rio = lax.broadcasted_iota(jnp.int32, shp, 1)
    eio = gio * GROUP_SIZE + rio

    m1 = jnp.max(sel, axis=1, keepdims=True)
    i1 = jnp.min(jnp.where(sel == m1, rio, GROUP_SIZE), axis=1, keepdims=True)
    m2 = jnp.max(jnp.where(rio == i1, _NEG_INF, sel), axis=1, keepdims=True)
    gs = m1 + m2
    gio1 = lax.broadcasted_iota(jnp.int32, (N_GROUPS, 1, t), 0)
    gmask = jnp.zeros((N_GROUPS, 1, t), jnp.bool_)
    for _ in range(TOPK_GROUPS):
        mg = jnp.max(gs, axis=0, keepdims=True)
        ig = jnp.min(jnp.where(gs == mg, gio1, N_GROUPS), axis=0, keepdims=True)
        hit = gio1 == ig
        gmask = jnp.logical_or(gmask, hit)
        gs = jnp.where(hit, _NEG_INF, gs)

    cand = jnp.where(gmask, sel, _NEG_INF)

    def red(x, op):
        return op(op(x, axis=0, keepdims=True), axis=1, keepdims=True)

    picked = jnp.zeros(shp, jnp.bool_)
    e_rows, g_rows = [], []
    for _ in range(TOP_K):
        mk = red(cand, jnp.max)
        ek = red(jnp.where(cand == mk, eio, N_EXPERTS), jnp.min)
        hit = eio == ek
        g_rows.append(red(jnp.where(hit, sc, 0.0), jnp.sum))
        e_rows.append(ek)
        picked = jnp.logical_or(picked, hit)
        cand = jnp.where(hit, _NEG_INF, cand)
    gsum = g_rows[0]
    for gk in g_rows[1:]:
        gsum = gsum + gk

    pm = jnp.where(picked, 1.0, 0.0).reshape(N_EXPERTS, t)
    rank = _dot(_bf(pm), tri_ref[...])
    slot = (run_ref[...] + rank).reshape(shp)
    run_ref[...] = run_ref[...] + jnp.sum(pm, axis=1, keepdims=True)
    cnt_ref[...] = run_ref[...]
    for k in range(TOP_K):
        eidx_ref[0, k:k + 1, :] = e_rows[k].reshape(1, t)
        gate_ref[0, k:k + 1, :] = (g_rows[k] / gsum * ROUTED_SCALE).reshape(1, t)
        pk = red(jnp.where(eio == e_rows[k], slot, 0.0), jnp.sum)
        pos_ref[0, k:k + 1, :] = pk.reshape(1, t).astype(jnp.int32)


def _mix_route(attn, cn, h0, w):
    n = attn.shape[0]
    nt = n // T_MIX
    tri = _bf(jnp.triu(jnp.ones((T_MIX, T_MIX), jnp.float32), 1))
    params = [w["attn_norm"], w["wo_a"], w["wo_c"], w["ln1_g"], w["ln1_b"], w["ws_gate"], w["ws_up"],
              w["ws_down"], w["w_router_t"], w["router_bias"], tri]
    full = lambda a: pl.BlockSpec(a.shape, lambda i: (0,) * a.ndim)
    row_blk = lambda width: pl.BlockSpec((T_MIX, width), lambda i: (i, 0))
    k_blk = pl.BlockSpec((1, TOP_K, T_MIX), lambda i: (i, 0, 0))
    out_shape = [
        jax.ShapeDtypeStruct((n, ROW_TILES, LANES), jnp.float32),
        jax.ShapeDtypeStruct((n, D_MODEL), jnp.float32),
        jax.ShapeDtypeStruct((nt, TOP_K, T_MIX), jnp.int32),
        jax.ShapeDtypeStruct((nt, TOP_K, T_MIX), jnp.float32),
        jax.ShapeDtypeStruct((nt, TOP_K, T_MIX), jnp.int32),
        jax.ShapeDtypeStruct((N_EXPERTS, 1), jnp.float32),
    ]
    return pl.pallas_call(
        _mix_body,
        grid=(nt,),
        in_specs=[row_blk(D_ATTN), row_blk(D_CONV), row_blk(D_MODEL)] + [full(p) for p in params],
        out_specs=[pl.BlockSpec((T_MIX, ROW_TILES, LANES), lambda i: (i, 0, 0)), row_blk(D_MODEL),
                   k_blk, k_blk, k_blk, pl.BlockSpec((N_EXPERTS, 1), lambda i: (0, 0))],
        out_shape=out_shape,
        scratch_shapes=[pltpu.VMEM((N_EXPERTS, 1), jnp.float32)],
        compiler_params=pltpu.CompilerParams(dimension_semantics=("arbitrary",),
                                             vmem_limit_bytes=VMEM_LIMIT),
        name="mix_route",
    )(attn, cn, h0, *params)


def _sc_mesh():
    return plsc.VectorSubcoreMesh(core_axis_name="c", subcore_axis_name="s")


def _worker_id():
    return lax.axis_index("s") * SC_CORES + lax.axis_index("c")


def _dispatch(hp, dest, n_slots):
    n = hp.shape[0]
    per_worker = n // SC_ROWS // (SC_CORES * SC_SUBCORES)

    @functools.partial(
        pl.kernel, mesh=_sc_mesh(),
        out_type=jax.ShapeDtypeStruct((n_slots, ROW_TILES, LANES), jnp.float32),
        scratch_types=[pltpu.VMEM((TOP_K, SC_ROWS), jnp.int32),
                       pltpu.VMEM((SC_ROWS, ROW_TILES, LANES), jnp.float32),
                       pltpu.SemaphoreType.DMA],
        name="dispatch")
    def run(hp_hbm, dest_hbm, xs_hbm, idx_v, rows_v, sem):
        first = _worker_id() * per_worker

        @pl.loop(0, per_worker)
        def _(c):
            chunk = first + c
            pltpu.sync_copy(dest_hbm.at[chunk], idx_v)
            pltpu.sync_copy(hp_hbm.at[pl.ds(chunk * SC_ROWS, SC_ROWS)], rows_v)
            copies = [pltpu.async_copy(rows_v, xs_hbm.at[idx_v.at[k]], sem) for k in range(TOP_K)]
            for cp in copies:
                cp.wait()

    return run(hp, dest)


def _combine(ys, dest, n):
    per_worker = n // SC_ROWS // (SC_CORES * SC_SUBCORES)

    @functools.partial(
        pl.kernel, mesh=_sc_mesh(),
        out_type=jax.ShapeDtypeStruct((TOP_K, n, ROW_TILES, LANES), jnp.float32),
        scratch_types=[pltpu.VMEM((TOP_K, SC_ROWS), jnp.int32),
                       pltpu.VMEM((SC_ROWS, ROW_TILES, LANES), jnp.float32),
                       pltpu.SemaphoreType.DMA],
        name="combine")
    def run(ys_hbm, dest_hbm, out_hbm, idx_v, rows_v, sem):
        first = _worker_id() * per_worker

        @pl.loop(0, per_worker)
        def _(c):
            chunk = first + c
            pltpu.sync_copy(dest_hbm.at[chunk], idx_v)
            for k in range(TOP_K):
                pltpu.async_copy(ys_hbm.at[idx_v.at[k]], rows_v, sem).wait()
                pltpu.sync_copy(rows_v, out_hbm.at[k, pl.ds(chunk * SC_ROWS, SC_ROWS)])

    return run(ys, dest)


def _expert_body(blk_e_ref, nused_ref, xs_ref, wg_ref, wu_ref, wd_ref, ys_ref):
    @pl.when(pl.program_id(0) < nused_ref[0])
    def _():
        x = jnp.concatenate([_bf(xs_ref[:, s, :]) for s in range(ROW_TILES)], axis=1)
        g = _dot(x, _bf(wg_ref[0]))
        u = _dot(x, _bf(wu_ref[0]))
        y = _dot(_bf(g * _sigmoid(g) * u), _bf(wd_ref[0]))
        for s in range(ROW_TILES):
            ys_ref[:, s, :] = y[:, s * LANES:(s + 1) * LANES]


def _experts(xs, blk_e, nused, w_gate, w_up, w_down):
    n_slots = xs.shape[0]
    nb = n_slots // BLK
    row_map = lambda i, be, nu: (jnp.minimum(i, nu[0] - 1), 0, 0)
    w_map = lambda i, be, nu: (be[i], 0, 0)
    grid_spec = pltpu.PrefetchScalarGridSpec(
        num_scalar_prefetch=2,
        grid=(nb,),
        in_specs=[pl.BlockSpec((BLK, ROW_TILES, LANES), row_map),
                  pl.BlockSpec((1, D_MODEL, D_EXPERT), w_map),
                  pl.BlockSpec((1, D_MODEL, D_EXPERT), w_map),
                  pl.BlockSpec((1, D_EXPERT, D_MODEL), w_map)],
        out_specs=pl.BlockSpec((BLK, ROW_TILES, LANES), row_map),
    )
    return pl.pallas_call(
        _expert_body,
        grid_spec=grid_spec,
        out_shape=jax.ShapeDtypeStruct((n_slots, ROW_TILES, LANES), jnp.float32),
        compiler_params=pltpu.CompilerParams(dimension_semantics=("arbitrary",),
                                             vmem_limit_bytes=VMEM_LIMIT),
        name="experts",
    )(blk_e, nused, xs, w_gate, w_up, w_down)


def _final_body(yg_ref, gate_ref, base_ref, g2_ref, b2_ref, o_ref):
    gate = gate_ref[...]
    cols = []
    for s in range(ROW_TILES):
        acc = gate[:, 0:1] * yg_ref[0, :, s, :]
        for k in range(1, TOP_K):
            acc = acc + gate[:, k:k + 1] * yg_ref[k, :, s, :]
        cols.append(acc)
    ff = jnp.concatenate(cols, axis=1)
    o_ref[...] = _layernorm(base_ref[...] + ff, g2_ref[...], b2_ref[...])


def _final(yg, gates, base, g2, b2):
    n = base.shape[0]
    full = lambda a: pl.BlockSpec(a.shape, lambda i: (0,) * a.ndim)
    return pl.pallas_call(
        _final_body,
        grid=(n // T_FIN,),
        in_specs=[pl.BlockSpec((TOP_K, T_FIN, ROW_TILES, LANES), lambda i: (0, i, 0, 0)),
                  pl.BlockSpec((T_FIN, TOP_K), lambda i: (i, 0)),
                  pl.BlockSpec((T_FIN, D_MODEL), lambda i: (i, 0)),
                  full(g2), full(b2)],
        out_specs=pl.BlockSpec((T_FIN, D_MODEL), lambda i: (i, 0)),
        out_shape=jax.ShapeDtypeStruct((n, D_MODEL), jnp.float32),
        compiler_params=pltpu.CompilerParams(dimension_semantics=("arbitrary",),
                                             vmem_limit_bytes=VMEM_LIMIT),
        name="final",
    )(yg, gates, base, g2, b2)


def _rope_tables(pos):
    half = QK_ROPE // 2
    inv = ROPE_THETA ** (-jnp.arange(0, QK_ROPE, 2, dtype=jnp.float32) / QK_ROPE)
    ang = pos.astype(jnp.float32)[:, None] * inv[None, :]
    cos, sin = jnp.cos(ang), jnp.sin(ang)
    n = pos.shape[0]
    z = lambda w: jnp.zeros((n, w), jnp.float32)
    c = jnp.concatenate([jnp.ones((n, QK_NOPE), jnp.float32), cos, cos, z(LANES - QK_NOPE - QK_ROPE)], axis=1)
    s_up = jnp.concatenate([z(QK_NOPE + half), sin, z(LANES - QK_NOPE - QK_ROPE)], axis=1)
    s_dn = jnp.concatenate([z(QK_NOPE), -sin, z(half), z(LANES - QK_NOPE - QK_ROPE)], axis=1)
    return c, s_up, s_dn


def _pad_heads(w, per_head, keep):
    rows = w.shape[0]
    w = w.reshape(rows, ATTN_HEADS, per_head)[:, :, :keep]
    w = jnp.pad(w, ((0, 0), (0, 0), (0, HEAD_PAD - keep)))
    return w.reshape(rows, ATTN_HEADS * HEAD_PAD)


def _prep_weights(emb_ln_g, emb_ln_b, w_in, q_norm, w_uq, kv_norm, w_ukv, conv_w, attn_out_norm,
                  conv_out_norm, w_o, ln1_g, ln1_b, w_router, router_bias, ws_gate, ws_up, ws_down):
    o1 = Q_LORA
    o2 = o1 + KV_LORA
    o3 = o2 + QK_ROPE
    row = lambda a: a.reshape(1, -1)
    kpe_cols = jnp.pad(w_in[:, o2:o3], ((0, 0), (QK_NOPE, HEAD_PAD - QK_NOPE - QK_ROPE)))
    w_in_p = jnp.concatenate([w_in[:, :o2], kpe_cols, w_in[:, o3:]], axis=1)
    ukv = w_ukv.reshape(KV_LORA, ATTN_HEADS, QK_NOPE + V_HEAD)
    w_uk = jnp.pad(ukv[:, :, :QK_NOPE], ((0, 0), (0, 0), (0, HEAD_PAD - QK_NOPE)))
    w_uv = jnp.pad(ukv[:, :, QK_NOPE:], ((0, 0), (0, 0), (0, HEAD_PAD - V_HEAD)))
    w_ukv_p = jnp.concatenate([w_uk.reshape(KV_LORA, -1), w_uv.reshape(KV_LORA, -1)], axis=1)
    woa = w_o[:D_ATTN]
    return {
        "emb_g": row(emb_ln_g), "emb_b": row(emb_ln_b),
        "w_in": _bf(w_in_p),
        "q_norm": row(q_norm), "w_uq": _bf(_pad_heads(w_uq, QK_NOPE + QK_ROPE, QK_NOPE + QK_ROPE)),
        "kv_norm": row(kv_norm), "w_ukv": _bf(w_ukv_p),
        "conv_w": jnp.pad(conv_w, ((0, SUBLANES - CONV_W), (0, 0))), "conv_norm": row(conv_out_norm),
        "attn_norm": row(attn_out_norm), "wo_a": _bf(woa), "wo_c": _bf(w_o[D_ATTN:]),
        "ln1_g": row(ln1_g), "ln1_b": row(ln1_b),
        "ws_gate": _bf(ws_gate), "ws_up": _bf(ws_up), "ws_down": _bf(ws_down),
        "w_router_t": _bf(w_router.T), "router_bias": router_bias.reshape(-1, 1),
    }


def _slot_body(eidx_ref, pos_ref, start_ref, dest_ref):
    t = eidx_ref.shape[2]
    shp = (N_GROUPS, GROUP_SIZE, t)
    eio = (lax.broadcasted_iota(jnp.int32, shp, 0) * GROUP_SIZE + lax.broadcasted_iota(jnp.int32, shp, 1))
    start = start_ref[...].reshape(N_GROUPS, GROUP_SIZE, 1)
    for k in range(TOP_K):
        hit = eio == eidx_ref[0, k:k + 1, :].reshape(1, 1, t)
        st = jnp.sum(jnp.sum(jnp.where(hit, start, 0.0), axis=0, keepdims=True), axis=1, keepdims=True)
        dest_ref[0, k:k + 1, :] = st.reshape(1, t).astype(jnp.int32) + pos_ref[0, k:k + 1, :]


def _slots(eidx, pos, start):
    nt, _, t = eidx.shape
    k_blk = pl.BlockSpec((1, TOP_K, t), lambda i: (i, 0, 0))
    return pl.pallas_call(
        _slot_body,
        grid=(nt,),
        in_specs=[k_blk, k_blk, pl.BlockSpec(start.shape, lambda i: (0, 0))],
        out_specs=k_blk,
        out_shape=jax.ShapeDtypeStruct(eidx.shape, jnp.int32),
        compiler_params=pltpu.CompilerParams(dimension_semantics=("arbitrary",)),
        name="slots",
    )(eidx, pos, start)


def _slot_plan(eidx, pos, counts):
    counts = counts.reshape(-1).astype(jnp.int32)
    padded = (counts + BLK - 1) // BLK * BLK
    pad_end = jnp.cumsum(padded)
    pad_start = pad_end - padded
    dest = _slots(eidx, pos, pad_start.astype(jnp.float32).reshape(-1, 1))
    n_assign = eidx.size
    nb = n_assign // BLK + N_EXPERTS
    nused = pad_end[-1] // BLK
    first_row = jnp.minimum(jnp.arange(nb, dtype=jnp.int32), nused - 1) * BLK
    blk_e = jnp.sum((pad_end[None, :] <= first_row[:, None]).astype(jnp.int32), axis=1)
    blk_e = jnp.minimum(blk_e, N_EXPERTS - 1).astype(jnp.int32)
    return dest, blk_e, nused.reshape(1).astype(jnp.int32), nb * BLK


def _rechunk(a, rows):
    nt, k, t = a.shape
    return a.reshape(nt, k, t // rows, rows).transpose(0, 2, 1, 3).reshape(nt * (t // rows), k, rows)


def kernel(x, meta_tokens, emb_ln_g, emb_ln_b, w_in, q_norm, w_uq, kv_norm, w_ukv, conv_w, attn_out_norm, conv_out_norm, w_o, ln1_g, ln1_b, w_router, router_bias, w_gate, w_up, w_down, ws_gate, ws_up, ws_down, ln2_g, ln2_b):
    bsz, seq, _ = x.shape
    n = bsz * seq
    w = _prep_weights(emb_ln_g, emb_ln_b, w_in[0], q_norm[0], w_uq[0], kv_norm[0], w_ukv[0], conv_w[0],
                      attn_out_norm[0], conv_out_norm[0], w_o[0], ln1_g[0], ln1_b[0], w_router[0],
                      router_bias[0], ws_gate[0], ws_up[0], ws_down[0])

    meta_out = _inproj(meta_tokens[None], _rope_tables(jnp.arange(N_META)),
                       jnp.zeros((SUBLANES, D_CONV), jnp.float32), w, N_META)
    pad_rows = lambda a: jnp.pad(a, ((0, META_PAD - N_META), (0, 0)))
    km, vm, vc_meta = pad_rows(meta_out[2][0]), pad_rows(meta_out[3][0]), meta_out[5][0]

    h0, q, k, v, cn, _ = _inproj(x, _rope_tables(N_META + jnp.arange(seq)), vc_meta, w, T_IN)
    attn = _attention(q, k, v, km, vm)

    hp, base, eidx, gates, pos, counts = _mix_route(
        attn.reshape(n, D_ATTN), cn.reshape(n, D_CONV), h0.reshape(n, D_MODEL), w)
    dest, blk_e, nused, n_slots = _slot_plan(eidx, pos, counts)
    dest = _rechunk(dest, SC_ROWS)

    xs = _dispatch(hp, dest, n_slots)
    ys = _experts(xs, blk_e, nused, w_gate[0], w_up[0], w_down[0])
    yg = _combine(ys, dest, n)
    gates_t = gates.transpose(0, 2, 1).reshape(n, TOP_K)
    out = _final(yg, gates_t, base, ln2_g[0].reshape(1, -1), ln2_b[0].reshape(1, -1))
    return out.reshape(bsz, seq, D_MODEL)
```

```python
import functools
import math

import jax
import jax.numpy as jnp
from jax import lax
from jax.experimental import pallas as pl
from jax.experimental.pallas import tpu as pltpu
from jax.experimental.pallas import tpu_sc as plsc

D_MODEL = 1024
N_META = 16
ATTN_HEADS = 8
QK_NOPE = 64
QK_ROPE = 32
V_HEAD = 64
Q_LORA = 384
KV_LORA = 256
ROPE_THETA = 10000.0
D_ATTN = ATTN_HEADS * V_HEAD
D_CONV = 512
CONV_W = 3
ATTN_SCALE = 1.0 / math.sqrt(QK_NOPE + QK_ROPE)
N_EXPERTS = 256
TOP_K = 8
N_GROUPS = 8
GROUP_SIZE = N_EXPERTS // N_GROUPS
TOPK_GROUPS = 4
D_EXPERT = 256
D_SHARED = 256
ROUTED_SCALE = 2.5
DEPTH = 1
ALPHA = (2.0 * DEPTH) ** 0.25

LANES = 128
SUBLANES = 8
HEAD_PAD = LANES
META_PAD = LANES
HALF = D_MODEL // 2
PACK_TILES = HALF // LANES
HI_MASK = -65536
SC_CORES = 2
SC_SUBCORES = 16

T_IN = 512
T_Q = 256
T_MIX = 256
BLK = 256
T_FIN = 256
SC_ROWS = 64
SC_PIECES = SC_ROWS * PACK_TILES
SC_LISTS = SC_PIECES // LANES

VMEM_LIMIT = 56 * 1024 * 1024

_NEG_INF = float("-inf")


def _bf(x):
    return x.astype(jnp.bfloat16)


def _dot(a, b):
    return jnp.dot(a, b, preferred_element_type=jnp.float32)


def _dot_t(a, b):
    return lax.dot_general(a, b, (((1,), (1,)), ((), ())), preferred_element_type=jnp.float32)


def _layernorm(x, g, b, eps=1e-5):
    mu = jnp.mean(x, axis=-1, keepdims=True)
    xc = x - mu
    var = jnp.mean(xc * xc, axis=-1, keepdims=True)
    return xc * lax.rsqrt(var + eps) * g + b


def _rms(x, g, n, eps=1e-6):
    ms = jnp.sum(x * x, axis=-1, keepdims=True) * (1.0 / n)
    return x * lax.rsqrt(ms + eps) * g


def _sigmoid(x):
    return 1.0 / (1.0 + jnp.exp(-x))


def _rope(x, c, s_up, s_dn):
    return x * c + pltpu.roll(x, QK_ROPE // 2, 1) * s_up + pltpu.roll(x, LANES - QK_ROPE // 2, 1) * s_dn


def _inproj_body(x_ref, g_ref, b_ref, win_ref, qn_ref, wuq_ref, kvn_ref, wukv_ref, cw_ref, con_ref,
                 tc_ref, tu_ref, td_ref, vcinit_ref,
                 h0_ref, q_ref, k_ref, v_ref, cn_ref, vct_ref, carry_ref):
    rows = x_ref.shape[1]

    @pl.when(pl.program_id(1) == 0)
    def _():
        carry_ref[...] = vcinit_ref[...]

    h0 = _layernorm(x_ref[0], g_ref[...], b_ref[...])
    h0_ref[0] = h0
    z = _dot(_bf(h0), win_ref[...])
    o1 = Q_LORA
    o2 = o1 + KV_LORA
    o3 = o2 + HEAD_PAD
    o4 = o3 + D_CONV
    o5 = o4 + D_CONV
    c, s_up, s_dn = tc_ref[...], tu_ref[...], td_ref[...]

    q = _dot(_bf(_rms(z[:, :o1], qn_ref[...], Q_LORA)), wuq_ref[...])
    for h in range(ATTN_HEADS):
        sl = slice(h * HEAD_PAD, (h + 1) * HEAD_PAD)
        q_ref[0, :, sl] = _bf(_rope(q[:, sl], c, s_up, s_dn) * ATTN_SCALE)

    kv = _dot(_bf(_rms(z[:, o1:o2], kvn_ref[...], KV_LORA)), wukv_ref[...])
    kpe = _rope(z[:, o2:o3], c, s_up, s_dn)
    lane = lax.broadcasted_iota(jnp.int32, (rows, HEAD_PAD), 1)
    for h in range(ATTN_HEADS):
        sl = slice(h * HEAD_PAD, (h + 1) * HEAD_PAD)
        k_ref[0, :, sl] = _bf(kv[:, sl] + kpe)
        vh = kv[:, ATTN_HEADS * HEAD_PAD + h * HEAD_PAD: ATTN_HEADS * HEAD_PAD + (h + 1) * HEAD_PAD]
        v_ref[0, :, sl] = _bf(jnp.where(lane < V_HEAD, vh, 1.0))

    vc = z[:, o4:o5] * z[:, o5:]
    carry = carry_ref[...]
    row = lax.broadcasted_iota(jnp.int32, (rows, D_CONV), 0)
    vc1 = jnp.where(row == 0, carry[SUBLANES - 1:SUBLANES, :], pltpu.roll(vc, 1, 0))
    vc2 = pltpu.roll(vc, 2, 0)
    vc2 = jnp.where(row == 0, carry[SUBLANES - 2:SUBLANES - 1, :], vc2)
    vc2 = jnp.where(row == 1, carry[SUBLANES - 1:SUBLANES, :], vc2)
    cw = cw_ref[...]
    conv = z[:, o3:o4] * (cw[0:1, :] * vc2 + cw[1:2, :] * vc1 + cw[2:3, :] * vc)
    cn_ref[0] = _bf(_rms(conv, con_ref[...], D_CONV))
    tail = vc[rows - SUBLANES:, :]
    carry_ref[...] = tail
    vct_ref[0] = tail


def _inproj(x, tabs, vcinit, w, rows):
    bsz, seq, _ = x.shape
    nj = seq // rows
    full = lambda a: pl.BlockSpec(a.shape, lambda b, j: (0,) * a.ndim)
    row_blk = lambda width: pl.BlockSpec((1, rows, width), lambda b, j: (b, j, 0))
    tab = pl.BlockSpec((rows, HEAD_PAD), lambda b, j: (j, 0))
    params = [w["emb_g"], w["emb_b"], w["w_in"], w["q_norm"], w["w_uq"], w["kv_norm"], w["w_ukv"],
              w["conv_w"], w["conv_norm"]]
    wide = ATTN_HEADS * HEAD_PAD
    out_shape = [
        jax.ShapeDtypeStruct((bsz, seq, D_MODEL), jnp.float32),
        jax.ShapeDtypeStruct((bsz, seq, wide), jnp.bfloat16),
        jax.ShapeDtypeStruct((bsz, seq, wide), jnp.bfloat16),
        jax.ShapeDtypeStruct((bsz, seq, wide), jnp.bfloat16),
        jax.ShapeDtypeStruct((bsz, seq, D_CONV), jnp.bfloat16),
        jax.ShapeDtypeStruct((bsz, nj * SUBLANES, D_CONV), jnp.float32),
    ]
    return pl.pallas_call(
        _inproj_body,
        grid=(bsz, nj),
        in_specs=[row_blk(D_MODEL)] + [full(p) for p in params] + [tab, tab, tab, full(vcinit)],
        out_specs=[row_blk(D_MODEL), row_blk(wide), row_blk(wide), row_blk(wide), row_blk(D_CONV),
                   pl.BlockSpec((1, SUBLANES, D_CONV), lambda b, j: (b, j, 0))],
        out_shape=out_shape,
        scratch_shapes=[pltpu.VMEM((SUBLANES, D_CONV), jnp.float32)],
        compiler_params=pltpu.CompilerParams(dimension_semantics=("arbitrary", "arbitrary"),
                                             vmem_limit_bytes=VMEM_LIMIT),
        name="inproj",
    )(x, *params, *tabs, vcinit)


def _attn_body(q_ref, k_ref, v_ref, km_ref, vm_ref, o_ref):
    i = pl.program_id(2)
    tq = q_ref.shape[1]
    heads = [slice(hh * HEAD_PAD, (hh + 1) * HEAD_PAD) for hh in range(2)]
    qs = [q_ref[0, :, sl] for sl in heads]

    causal = lax.broadcasted_iota(jnp.int32, (tq, tq), 1) <= lax.broadcasted_iota(jnp.int32, (tq, tq), 0)
    is_meta = lax.broadcasted_iota(jnp.int32, (tq, META_PAD), 1) < N_META
    r0 = pl.multiple_of(i * tq, tq)
    carry = []
    for q, sl in zip(qs, heads):
        sa = jnp.where(causal, _dot_t(q, k_ref[0, pl.ds(r0, tq), sl]), _NEG_INF)
        sb = jnp.where(is_meta, _dot_t(q, km_ref[:, sl]), _NEG_INF)
        m = jnp.maximum(jnp.max(sa, axis=1, keepdims=True), jnp.max(sb, axis=1, keepdims=True))
        acc = (_dot(_bf(jnp.exp(sa - m)), v_ref[0, pl.ds(r0, tq), sl])
               + _dot(_bf(jnp.exp(sb - m)), vm_ref[:, sl]))
        carry += [m, acc]

    def body(j, carry):
        rj = pl.multiple_of(j * tq, tq)
        out = []
        for h, (q, sl) in enumerate(zip(qs, heads)):
            m, acc = carry[2 * h], carry[2 * h + 1]
            s = _dot_t(q, k_ref[0, pl.ds(rj, tq), sl])
            m_new = jnp.maximum(m, jnp.max(s, axis=1, keepdims=True))
            p = jnp.exp(s - m_new)
            out += [m_new, jnp.exp(m - m_new) * acc + _dot(_bf(p), v_ref[0, pl.ds(rj, tq), sl])]
        return tuple(out)

    _, a0, _, a1 = lax.fori_loop(0, i, body, tuple(carry))
    lane = lax.broadcasted_iota(jnp.int32, (tq, HEAD_PAD), 1)
    o0 = a0 / pltpu.roll(a0, V_HEAD, 1)
    o1 = pltpu.roll(a1, V_HEAD, 1) / a1
    o_ref[0] = _bf(jnp.where(lane < V_HEAD, o0, o1))


def _attention(q, k, v, km, vm):
    bsz, seq, _ = q.shape
    pairs = ATTN_HEADS // 2
    pw = 2 * HEAD_PAD
    return pl.pallas_call(
        _attn_body,
        grid=(bsz, pairs, seq // T_Q),
        in_specs=[pl.BlockSpec((1, T_Q, pw), lambda b, p, i: (b, i, p)),
                  pl.BlockSpec((1, seq, pw), lambda b, p, i: (b, 0, p)),
                  pl.BlockSpec((1, seq, pw), lambda b, p, i: (b, 0, p)),
                  pl.BlockSpec((META_PAD, pw), lambda b, p, i: (0, p)),
                  pl.BlockSpec((META_PAD, pw), lambda b, p, i: (0, p))],
        out_specs=pl.BlockSpec((1, T_Q, 2 * V_HEAD), lambda b, p, i: (b, i, p)),
        out_shape=jax.ShapeDtypeStruct((bsz, seq, D_ATTN), jnp.bfloat16),
        compiler_params=pltpu.CompilerParams(dimension_semantics=("arbitrary",) * 3,
                                             vmem_limit_bytes=VMEM_LIMIT),
        name="attention",
    )(q, k, v, km, vm)


def _mix_body(attn_ref, cn_ref, h0_ref, an_ref, woa_ref, woc_ref, g1_ref, b1_ref,
              wsg_ref, wsu_ref, wsd_ref, wrt_ref, rb_ref, tri_ref,
              hp_ref, base_ref, eidx_ref, gate_ref, pos_ref, cnt_ref, run_ref):
    t = attn_ref.shape[0]

    @pl.when(pl.program_id(0) == 0)
    def _():
        run_ref[...] = jnp.zeros_like(run_ref)

    attn_n = _rms(attn_ref[...].astype(jnp.float32), an_ref[...], D_ATTN)
    mix = _dot(_bf(attn_n), woa_ref[...]) + _dot(cn_ref[...], woc_ref[...])
    h1 = _layernorm(ALPHA * h0_ref[...] + mix, g1_ref[...], b1_ref[...])
    _store_row_tiles(hp_ref, (), h1)
    h1b = _bf(h1)
    g = _dot(h1b, wsg_ref[...])
    u = _dot(h1b, wsu_ref[...])
    base_ref[...] = ALPHA * h1 + _dot(_bf(g * _sigmoid(g) * u), wsd_ref[...])

    shp = (N_GROUPS, GROUP_SIZE, t)
    sc = _sigmoid(_dot_t(wrt_ref[...], h1b)).reshape(shp)
    sel = sc + rb_ref[...].reshape(N_GROUPS, GROUP_SIZE, 1)
    gio = lax.broadcasted_iota(jnp.int32, shp, 0)
    rio = lax.broadcasted_iota(jnp.int32, shp, 1)
    eio = gio * GROUP_SIZE + rio

    m1 = jnp.max(sel, axis=1, keepdims=True)
    i1 = jnp.min(jnp.where(sel == m1, rio, GROUP_SIZE), axis=1, keepdims=True)
    m2 = jnp.max(jnp.where(rio == i1, _NEG_INF, sel), axis=1, keepdims=True)
    gs = m1 + m2
    gio1 = lax.broadcasted_iota(jnp.int32, (N_GROUPS, 1, t), 0)
    gmask = jnp.zeros((N_GROUPS, 1, t), jnp.bool_)
    for _ in range(TOPK_GROUPS):
        mg = jnp.max(gs, axis=0, keepdims=True)
        ig = jnp.min(jnp.where(gs == mg, gio1, N_GROUPS), axis=0, keepdims=True)
        hit = gio1 == ig
        gmask = jnp.logical_or(gmask, hit)
        gs = jnp.where(hit, _NEG_INF, gs)

    cand = jnp.where(gmask, sel, _NEG_INF)

    def red(x, op):
        return op(op(x, axis=0, keepdims=True), axis=1, keepdims=True)

    picked = jnp.zeros(shp, jnp.bool_)
    e_rows, g_rows = [], []
    for _ in range(TOP_K):
        mk = red(cand, jnp.max)
        ek = red(jnp.where(cand == mk, eio, N_EXPERTS), jnp.min)
        hit = eio == ek
        g_rows.append(red(jnp.where(hit, sc, 0.0), jnp.sum))
        e_rows.append(ek)
        picked = jnp.logical_or(picked, hit)
        cand = jnp.where(hit, _NEG_INF, cand)
    gsum = g_rows[0]
    for gk in g_rows[1:]:
        gsum = gsum + gk

    pm = jnp.where(picked, 1.0, 0.0).reshape(N_EXPERTS, t)
    rank = _dot(_bf(pm), tri_ref[...])
    slot = (run_ref[...] + rank).reshape(shp)
    run_ref[...] = run_ref[...] + jnp.sum(pm, axis=1, keepdims=True)
    cnt_ref[...] = run_ref[...]
    for k in range(TOP_K):
        eidx_ref[0, k:k + 1, :] = e_rows[k].reshape(1, t)
        gate_ref[0, k:k + 1, :] = (g_rows[k] / gsum * ROUTED_SCALE).reshape(1, t)
        pk = red(jnp.where(eio == e_rows[k], slot, 0.0), jnp.sum)
        pos_ref[0, k:k + 1, :] = pk.reshape(1, t).astype(jnp.int32)


def _mix_route(attn, cn, h0, w):
    n = attn.shape[0]
    nt = n // T_MIX
    tri = _bf(jnp.triu(jnp.ones((T_MIX, T_MIX), jnp.float32), 1))
    params = [w["attn_norm"], w["wo_a"], w["wo_c"], w["ln1_g"], w["ln1_b"], w["ws_gate"], w["ws_up"],
              w["ws_down"], w["w_router_t"], w["router_bias"], tri]
    full = lambda a: pl.BlockSpec(a.shape, lambda i: (0,) * a.ndim)
    row_blk = lambda width: pl.BlockSpec((T_MIX, width), lambda i: (i, 0))
    k_blk = pl.BlockSpec((1, TOP_K, T_MIX), lambda i: (i, 0, 0))
    out_shape = [
        jax.ShapeDtypeStruct((n // SUBLANES, PACK_TILES, SUBLANES, LANES), jnp.int32),
        jax.ShapeDtypeStruct((n, D_MODEL), jnp.float32),
        jax.ShapeDtypeStruct((nt, TOP_K, T_MIX), jnp.int32),
        jax.ShapeDtypeStruct((nt, TOP_K, T_MIX), jnp.float32),
        jax.ShapeDtypeStruct((nt, TOP_K, T_MIX), jnp.int32),
        jax.ShapeDtypeStruct((N_EXPERTS, 1), jnp.float32),
    ]
    return pl.pallas_call(
        _mix_body,
        grid=(nt,),
        in_specs=[row_blk(D_ATTN), row_blk(D_CONV), row_blk(D_MODEL)] + [full(p) for p in params],
        out_specs=[pl.BlockSpec((T_MIX // SUBLANES, PACK_TILES, SUBLANES, LANES), lambda i: (i, 0, 0, 0)),
                   row_blk(D_MODEL),
                   k_blk, k_blk, k_blk, pl.BlockSpec((N_EXPERTS, 1), lambda i: (0, 0))],
        out_shape=out_shape,
        scratch_shapes=[pltpu.VMEM((N_EXPERTS, 1), jnp.float32)],
        compiler_params=pltpu.CompilerParams(dimension_semantics=("arbitrary",),
                                             vmem_limit_bytes=VMEM_LIMIT),
        name="mix_route",
    )(attn, cn, h0, *params)


def _pack_row_tiles(x):
    bits = lambda a: lax.bitcast_convert_type(_bf(a).astype(jnp.float32), jnp.int32)
    out = []
    for lt in range(PACK_TILES):
        lo = bits(x[:, lt * LANES:(lt + 1) * LANES])
        hi = bits(x[:, HALF + lt * LANES:HALF + (lt + 1) * LANES])
        out.append(lax.shift_right_logical(lo, 16) | (hi & HI_MASK))
    return out


def _unpack_lo(w):
    return lax.bitcast_convert_type(lax.shift_left(w, 16), jnp.float32)


def _unpack_hi(w):
    return lax.bitcast_convert_type(w & HI_MASK, jnp.float32)


def _store_row_tiles(ref, lead, x):
    rows = x.shape[0]
    for lt, w in enumerate(_pack_row_tiles(x)):
        ref[lead + (slice(None), lt)] = w.reshape(rows // SUBLANES, SUBLANES, LANES)


def _load_row_tiles(ref, lead, rows):
    return [ref[lead + (slice(None), lt)].reshape(rows, LANES) for lt in range(PACK_TILES)]


def _sc_mesh():
    return plsc.VectorSubcoreMesh(core_axis_name="c", subcore_axis_name="s")


def _worker_id():
    return lax.axis_index("s") * SC_CORES + lax.axis_index("c")


def _dispatch(hp, idx, n_rows):
    per_worker = hp.shape[0] // SC_PIECES // (SC_CORES * SC_SUBCORES)
    lists = idx.shape[1]

    @functools.partial(
        pl.kernel, mesh=_sc_mesh(),
        out_type=jax.ShapeDtypeStruct((n_rows, LANES), jnp.int32),
        scratch_types=[pltpu.VMEM((lists, LANES), jnp.int32),
                       pltpu.VMEM((SC_PIECES, LANES), jnp.int32),
                       pltpu.SemaphoreType.DMA],
        name="dispatch")
    def run(hp_hbm, idx_hbm, xs_hbm, idx_v, rows_v, sem):
        first = _worker_id() * per_worker

        @pl.loop(0, per_worker)
        def _(c):
            chunk = first + c
            pltpu.sync_copy(idx_hbm.at[chunk], idx_v)
            pltpu.sync_copy(hp_hbm.at[pl.ds(chunk * SC_PIECES, SC_PIECES)], rows_v)
            copies = [pltpu.async_copy(rows_v.at[pl.ds((j % SC_LISTS) * LANES, LANES)],
                                       xs_hbm.at[idx_v.at[j]], sem) for j in range(lists)]
            for cp in copies:
                cp.wait()

    return run(hp, idx)


def _combine(ys, idx, n_rows):
    per_worker = n_rows // SC_PIECES // (SC_CORES * SC_SUBCORES)
    lists = idx.shape[1]

    @functools.partial(
        pl.kernel, mesh=_sc_mesh(),
        out_type=jax.ShapeDtypeStruct((TOP_K, n_rows, LANES), jnp.int32),
        scratch_types=[pltpu.VMEM((lists, LANES), jnp.int32),
                       pltpu.VMEM((2, SC_PIECES, LANES), jnp.int32),
                       pltpu.SemaphoreType.DMA, pltpu.SemaphoreType.DMA],
        name="combine")
    def run(ys_hbm, idx_hbm, out_hbm, idx_v, rows_v, sem0, sem1):
        first = _worker_id() * per_worker
        sems = (sem0, sem1)

        def gather(k):
            buf = rows_v.at[k % 2]
            return [pltpu.async_copy(ys_hbm.at[idx_v.at[k * SC_LISTS + g]],
                                     buf.at[pl.ds(g * LANES, LANES)], sems[k % 2]) for g in range(SC_LISTS)]

        @pl.loop(0, per_worker)
        def _(c):
            chunk = first + c
            pltpu.sync_copy(idx_hbm.at[chunk], idx_v)
            pending = gather(0)
            for k in range(TOP_K):
                for cp in pending:
                    cp.wait()
                if k + 1 < TOP_K:
                    pending = gather(k + 1)
                pltpu.sync_copy(rows_v.at[k % 2], out_hbm.at[k, pl.ds(chunk * SC_PIECES, SC_PIECES)])

    return run(ys, idx)


def _expert_body(blk_e_ref, nused_ref, xs_ref, wg_ref, wu_ref, wd_ref, ys_ref, wgu_ref, wdn_ref):
    i = pl.program_id(0)

    @pl.when(i < nused_ref[0])
    def _():
        @pl.when(jnp.logical_or(i == 0, blk_e_ref[i] != blk_e_ref[jnp.maximum(i - 1, 0)]))
        def _():
            wgu_ref[:, :D_EXPERT] = _bf(wg_ref[0])
            wgu_ref[:, D_EXPERT:] = _bf(wu_ref[0])
            wdn_ref[...] = _bf(wd_ref[0])

        words = _load_row_tiles(xs_ref, (), BLK)
        x = jnp.concatenate([_bf(_unpack_lo(w)) for w in words] + [_bf(_unpack_hi(w)) for w in words], axis=1)
        gu = _dot(x, wgu_ref[...])
        g, u = gu[:, :D_EXPERT], gu[:, D_EXPERT:]
        _store_row_tiles(ys_ref, (), _dot(_bf(g * _sigmoid(g) * u), wdn_ref[...]))


def _experts(xs, blk_e, nused, w_gate, w_up, w_down):
    nb = xs.shape[0] * SUBLANES // BLK
    blk = (BLK // SUBLANES, PACK_TILES, SUBLANES, LANES)
    row_map = lambda i, be, nu: (jnp.minimum(i, nu[0] - 1), 0, 0, 0)
    w_map = lambda i, be, nu: (be[i], 0, 0)
    grid_spec = pltpu.PrefetchScalarGridSpec(
        num_scalar_prefetch=2,
        grid=(nb,),
        in_specs=[pl.BlockSpec(blk, row_map),
                  pl.BlockSpec((1, D_MODEL, D_EXPERT), w_map),
                  pl.BlockSpec((1, D_MODEL, D_EXPERT), w_map),
                  pl.BlockSpec((1, D_EXPERT, D_MODEL), w_map)],
        out_specs=pl.BlockSpec(blk, row_map),
        scratch_shapes=[pltpu.VMEM((D_MODEL, 2 * D_EXPERT), jnp.bfloat16),
                        pltpu.VMEM((D_EXPERT, D_MODEL), jnp.bfloat16)],
    )
    return pl.pallas_call(
        _expert_body,
        grid_spec=grid_spec,
        out_shape=jax.ShapeDtypeStruct(xs.shape, jnp.int32),
        compiler_params=pltpu.CompilerParams(dimension_semantics=("arbitrary",),
                                             vmem_limit_bytes=VMEM_LIMIT),
        name="experts",
    )(blk_e, nused, xs, w_gate, w_up, w_down)


def _final_body(yg_ref, gate_ref, base_ref, g2_ref, b2_ref, o_ref):
    t = base_ref.shape[0]
    gate = gate_ref[...]
    lo = [None] * PACK_TILES
    hi = [None] * PACK_TILES
    for k in range(TOP_K):
        gk = gate[:, k:k + 1]
        for lt, w in enumerate(_load_row_tiles(yg_ref, (k,), t)):
            a, b = gk * _unpack_lo(w), gk * _unpack_hi(w)
            lo[lt] = a if k == 0 else lo[lt] + a
            hi[lt] = b if k == 0 else hi[lt] + b
    ff = jnp.concatenate(lo + hi, axis=1)
    o_ref[...] = _layernorm(base_ref[...] + ff, g2_ref[...], b2_ref[...])


def _final(yg, gates, base, g2, b2):
    n = base.shape[0]
    full = lambda a: pl.BlockSpec(a.shape, lambda i: (0,) * a.ndim)
    return pl.pallas_call(
        _final_body,
        grid=(n // T_FIN,),
        in_specs=[pl.BlockSpec((TOP_K, T_FIN // SUBLANES, PACK_TILES, SUBLANES, LANES),
                               lambda i: (0, i, 0, 0, 0)),
                  pl.BlockSpec((T_FIN, TOP_K), lambda i: (i, 0)),
                  pl.BlockSpec((T_FIN, D_MODEL), lambda i: (i, 0)),
                  full(g2), full(b2)],
        out_specs=pl.BlockSpec((T_FIN, D_MODEL), lambda i: (i, 0)),
        out_shape=jax.ShapeDtypeStruct((n, D_MODEL), jnp.float32),
        compiler_params=pltpu.CompilerParams(dimension_semantics=("arbitrary",),
                                             vmem_limit_bytes=VMEM_LIMIT),
        name="final",
    )(yg, gates, base, g2, b2)


def _rope_tables(pos):
    half = QK_ROPE // 2
    inv = ROPE_THETA ** (-jnp.arange(0, QK_ROPE, 2, dtype=jnp.float32) / QK_ROPE)
    ang = pos.astype(jnp.float32)[:, None] * inv[None, :]
    cos, sin = jnp.cos(ang), jnp.sin(ang)
    n = pos.shape[0]
    z = lambda w: jnp.zeros((n, w), jnp.float32)
    c = jnp.concatenate([jnp.ones((n, QK_NOPE), jnp.float32), cos, cos, z(LANES - QK_NOPE - QK_ROPE)], axis=1)
    s_up = jnp.concatenate([z(QK_NOPE + half), sin, z(LANES - QK_NOPE - QK_ROPE)], axis=1)
    s_dn = jnp.concatenate([z(QK_NOPE), -sin, z(half), z(LANES - QK_NOPE - QK_ROPE)], axis=1)
    return c, s_up, s_dn


def _pad_heads(w, per_head, keep):
    rows = w.shape[0]
    w = w.reshape(rows, ATTN_HEADS, per_head)[:, :, :keep]
    w = jnp.pad(w, ((0, 0), (0, 0), (0, HEAD_PAD - keep)))
    return w.reshape(rows, ATTN_HEADS * HEAD_PAD)


def _prep_weights(emb_ln_g, emb_ln_b, w_in, q_norm, w_uq, kv_norm, w_ukv, conv_w, attn_out_norm,
                  conv_out_norm, w_o, ln1_g, ln1_b, w_router, router_bias, ws_gate, ws_up, ws_down):
    o1 = Q_LORA
    o2 = o1 + KV_LORA
    o3 = o2 + QK_ROPE
    row = lambda a: a.reshape(1, -1)
    kpe_cols = jnp.pad(w_in[:, o2:o3], ((0, 0), (QK_NOPE, HEAD_PAD - QK_NOPE - QK_ROPE)))
    w_in_p = jnp.concatenate([w_in[:, :o2], kpe_cols, w_in[:, o3:]], axis=1)
    ukv = w_ukv.reshape(KV_LORA, ATTN_HEADS, QK_NOPE + V_HEAD)
    w_uk = jnp.pad(ukv[:, :, :QK_NOPE], ((0, 0), (0, 0), (0, HEAD_PAD - QK_NOPE)))
    w_uv = jnp.pad(ukv[:, :, QK_NOPE:], ((0, 0), (0, 0), (0, HEAD_PAD - V_HEAD)))
    w_ukv_p = jnp.concatenate([w_uk.reshape(KV_LORA, -1), w_uv.reshape(KV_LORA, -1)], axis=1)
    woa = w_o[:D_ATTN]
    return {
        "emb_g": row(emb_ln_g), "emb_b": row(emb_ln_b),
        "w_in": _bf(w_in_p),
        "q_norm": row(q_norm), "w_uq": _bf(_pad_heads(w_uq, QK_NOPE + QK_ROPE, QK_NOPE + QK_ROPE)),
        "kv_norm": row(kv_norm), "w_ukv": _bf(w_ukv_p),
        "conv_w": jnp.pad(conv_w, ((0, SUBLANES - CONV_W), (0, 0))), "conv_norm": row(conv_out_norm),
        "attn_norm": row(attn_out_norm), "wo_a": _bf(woa), "wo_c": _bf(w_o[D_ATTN:]),
        "ln1_g": row(ln1_g), "ln1_b": row(ln1_b),
        "ws_gate": _bf(ws_gate), "ws_up": _bf(ws_up), "ws_down": _bf(ws_down),
        "w_router_t": _bf(w_router.T), "router_bias": router_bias.reshape(-1, 1),
    }


def _slot_body(eidx_ref, pos_ref, start_ref, dest_ref):
    t = eidx_ref.shape[2]
    shp = (N_GROUPS, GROUP_SIZE, t)
    eio = (lax.broadcasted_iota(jnp.int32, shp, 0) * GROUP_SIZE + lax.broadcasted_iota(jnp.int32, shp, 1))
    start = start_ref[...].reshape(N_GROUPS, GROUP_SIZE, 1)
    for k in range(TOP_K):
        hit = eio == eidx_ref[0, k:k + 1, :].reshape(1, 1, t)
        st = jnp.sum(jnp.sum(jnp.where(hit, start, 0.0), axis=0, keepdims=True), axis=1, keepdims=True)
        dest_ref[0, k:k + 1, :] = st.reshape(1, t).astype(jnp.int32) + pos_ref[0, k:k + 1, :]


def _slots(eidx, pos, start):
    nt, _, t = eidx.shape
    k_blk = pl.BlockSpec((1, TOP_K, t), lambda i: (i, 0, 0))
    return pl.pallas_call(
        _slot_body,
        grid=(nt,),
        in_specs=[k_blk, k_blk, pl.BlockSpec(start.shape, lambda i: (0, 0))],
        out_specs=k_blk,
        out_shape=jax.ShapeDtypeStruct(eidx.shape, jnp.int32),
        compiler_params=pltpu.CompilerParams(dimension_semantics=("arbitrary",)),
        name="slots",
    )(eidx, pos, start)


def _slot_plan(eidx, pos, counts):
    counts = counts.reshape(-1).astype(jnp.int32)
    padded = (counts + BLK - 1) // BLK * BLK
    pad_end = jnp.cumsum(padded)
    pad_start = pad_end - padded
    dest = _slots(eidx, pos, pad_start.astype(jnp.float32).reshape(-1, 1))
    n_assign = eidx.size
    nb = n_assign // BLK + N_EXPERTS
    nused = pad_end[-1] // BLK
    first_row = jnp.minimum(jnp.arange(nb, dtype=jnp.int32), nused - 1) * BLK
    blk_e = jnp.sum((pad_end[None, :] <= first_row[:, None]).astype(jnp.int32), axis=1)
    blk_e = jnp.minimum(blk_e, N_EXPERTS - 1).astype(jnp.int32)
    return dest, blk_e, nused.reshape(1).astype(jnp.int32), nb * BLK


def _piece_lists(dest):
    nt, k, t = dest.shape
    d = dest.reshape(nt, k, t // SC_ROWS, SC_ROWS // SUBLANES, SUBLANES).transpose(0, 2, 1, 3, 4)
    base = (d // SUBLANES) * (PACK_TILES * SUBLANES) + d % SUBLANES
    lt = jnp.arange(PACK_TILES, dtype=jnp.int32) * SUBLANES
    pieces = base[:, :, :, :, None, :] + lt[:, None]
    return pieces.reshape(nt * (t // SC_ROWS), k * SC_LISTS, LANES)


def kernel(x, meta_tokens, emb_ln_g, emb_ln_b, w_in, q_norm, w_uq, kv_norm, w_ukv, conv_w, attn_out_norm, conv_out_norm, w_o, ln1_g, ln1_b, w_router, router_bias, w_gate, w_up, w_down, ws_gate, ws_up, ws_down, ln2_g, ln2_b):
    bsz, seq, _ = x.shape
    n = bsz * seq
    w = _prep_weights(emb_ln_g, emb_ln_b, w_in[0], q_norm[0], w_uq[0], kv_norm[0], w_ukv[0], conv_w[0],
                      attn_out_norm[0], conv_out_norm[0], w_o[0], ln1_g[0], ln1_b[0], w_router[0],
                      router_bias[0], ws_gate[0], ws_up[0], ws_down[0])

    meta_out = _inproj(meta_tokens[None], _rope_tables(jnp.arange(N_META)),
                       jnp.zeros((SUBLANES, D_CONV), jnp.float32), w, N_META)
    pad_rows = lambda a: jnp.pad(a, ((0, META_PAD - N_META), (0, 0)))
    km, vm, vc_meta = pad_rows(meta_out[2][0]), pad_rows(meta_out[3][0]), meta_out[5][0]

    h0, q, k, v, cn, _ = _inproj(x, _rope_tables(N_META + jnp.arange(seq)), vc_meta, w, T_IN)
    attn = _attention(q, k, v, km, vm)

    hp, base, eidx, gates, pos, counts = _mix_route(
        attn.reshape(n, D_ATTN), cn.reshape(n, D_CONV), h0.reshape(n, D_MODEL), w)
    dest, blk_e, nused, n_slots = _slot_plan(eidx, pos, counts)
    idx = _piece_lists(dest)
    tiled = lambda a, lead: a.reshape(lead + (-1, PACK_TILES, SUBLANES, LANES))

    xs = _dispatch(hp.reshape(-1, LANES), idx, n_slots * PACK_TILES)
    ys = _experts(tiled(xs, ()), blk_e, nused, w_gate[0], w_up[0], w_down[0])
    yg = tiled(_combine(ys.reshape(-1, LANES), idx, n * PACK_TILES), (TOP_K,))
    gates_t = gates.transpose(0, 2, 1).reshape(n, TOP_K)
    out = _final(yg, gates_t, base, ln2_g[0].reshape(1, -1), ln2_b[0].reshape(1, -1))
    return out.reshape(bsz, seq, D_MODEL)
```

```python
import functools
import math

import jax
import jax.numpy as jnp
from jax import lax
from jax.experimental import pallas as pl
from jax.experimental.pallas import tpu as pltpu
from jax.experimental.pallas import tpu_sc as plsc

D_MODEL = 1024
N_META = 16
ATTN_HEADS = 8
QK_NOPE = 64
QK_ROPE = 32
V_HEAD = 64
Q_LORA = 384
KV_LORA = 256
ROPE_THETA = 10000.0
D_ATTN = ATTN_HEADS * V_HEAD
D_CONV = 512
CONV_W = 3
ATTN_SCALE = 1.0 / math.sqrt(QK_NOPE + QK_ROPE)
N_EXPERTS = 256
TOP_K = 8
N_GROUPS = 8
GROUP_SIZE = N_EXPERTS // N_GROUPS
TOPK_GROUPS = 4
D_EXPERT = 256
D_SHARED = 256
ROUTED_SCALE = 2.5
DEPTH = 1
ALPHA = (2.0 * DEPTH) ** 0.25

LANES = 128
SUBLANES = 8
HEAD_PAD = LANES
META_PAD = LANES
HALF = D_MODEL // 2
PACK_TILES = HALF // LANES
HI_MASK = -65536
SC_CORES = 2
SC_SUBCORES = 16

T_IN = 512
T_Q = 512
T_MIX = 256
BLK = 256
T_FIN = 256
SC_ROWS = 64
SC_PIECES = SC_ROWS * PACK_TILES
SC_LISTS = SC_PIECES // LANES

VMEM_LIMIT = 56 * 1024 * 1024

_NEG_INF = float("-inf")


def _bf(x):
    return x.astype(jnp.bfloat16)


def _dot(a, b):
    return jnp.dot(a, b, preferred_element_type=jnp.float32)


def _dot_t(a, b):
    return lax.dot_general(a, b, (((1,), (1,)), ((), ())), preferred_element_type=jnp.float32)


def _layernorm(x, g, b, eps=1e-5):
    mu = jnp.mean(x, axis=-1, keepdims=True)
    xc = x - mu
    var = jnp.mean(xc * xc, axis=-1, keepdims=True)
    return xc * lax.rsqrt(var + eps) * g + b


def _rms(x, g, n, eps=1e-6):
    ms = jnp.sum(x * x, axis=-1, keepdims=True) * (1.0 / n)
    return x * lax.rsqrt(ms + eps) * g


def _sigmoid(x):
    return 1.0 / (1.0 + jnp.exp(-x))


def _rope(x, c, s_up, s_dn):
    return x * c + pltpu.roll(x, QK_ROPE // 2, 1) * s_up + pltpu.roll(x, LANES - QK_ROPE // 2, 1) * s_dn


def _inproj_body(x_ref, g_ref, b_ref, win_ref, qn_ref, wuq_ref, kvn_ref, wukv_ref, cw_ref, con_ref,
                 tc_ref, tu_ref, td_ref, vcinit_ref,
                 h0_ref, q_ref, k_ref, v_ref, cn_ref, vct_ref, carry_ref):
    rows = x_ref.shape[1]

    @pl.when(pl.program_id(1) == 0)
    def _():
        carry_ref[...] = vcinit_ref[...]

    h0 = _layernorm(x_ref[0], g_ref[...], b_ref[...])
    h0_ref[0] = h0
    z = _dot(_bf(h0), win_ref[...])
    o1 = Q_LORA
    o2 = o1 + KV_LORA
    o3 = o2 + HEAD_PAD
    o4 = o3 + D_CONV
    o5 = o4 + D_CONV
    c, s_up, s_dn = tc_ref[...], tu_ref[...], td_ref[...]

    q = _dot(_bf(_rms(z[:, :o1], qn_ref[...], Q_LORA)), wuq_ref[...])
    for h in range(ATTN_HEADS):
        sl = slice(h * HEAD_PAD, (h + 1) * HEAD_PAD)
        q_ref[0, :, sl] = _bf(_rope(q[:, sl], c, s_up, s_dn) * ATTN_SCALE)

    kv = _dot(_bf(_rms(z[:, o1:o2], kvn_ref[...], KV_LORA)), wukv_ref[...])
    kpe = _rope(z[:, o2:o3], c, s_up, s_dn)
    lane = lax.broadcasted_iota(jnp.int32, (rows, HEAD_PAD), 1)
    for h in range(ATTN_HEADS):
        sl = slice(h * HEAD_PAD, (h + 1) * HEAD_PAD)
        k_ref[0, :, sl] = _bf(kv[:, sl] + kpe)
        vh = kv[:, ATTN_HEADS * HEAD_PAD + h * HEAD_PAD: ATTN_HEADS * HEAD_PAD + (h + 1) * HEAD_PAD]
        v_ref[0, :, sl] = _bf(jnp.where(lane < V_HEAD, vh, 1.0))

    vc = z[:, o4:o5] * z[:, o5:]
    carry = carry_ref[...]
    row = lax.broadcasted_iota(jnp.int32, (rows, D_CONV), 0)
    vc1 = jnp.where(row == 0, carry[SUBLANES - 1:SUBLANES, :], pltpu.roll(vc, 1, 0))
    vc2 = pltpu.roll(vc, 2, 0)
    vc2 = jnp.where(row == 0, carry[SUBLANES - 2:SUBLANES - 1, :], vc2)
    vc2 = jnp.where(row == 1, carry[SUBLANES - 1:SUBLANES, :], vc2)
    cw = cw_ref[...]
    conv = z[:, o3:o4] * (cw[0:1, :] * vc2 + cw[1:2, :] * vc1 + cw[2:3, :] * vc)
    cn_ref[0] = _bf(_rms(conv, con_ref[...], D_CONV))
    tail = vc[rows - SUBLANES:, :]
    carry_ref[...] = tail
    vct_ref[0] = tail


def _inproj(x, tabs, vcinit, w, rows):
    bsz, seq, _ = x.shape
    nj = seq // rows
    full = lambda a: pl.BlockSpec(a.shape, lambda b, j: (0,) * a.ndim)
    row_blk = lambda width: pl.BlockSpec((1, rows, width), lambda b, j: (b, j, 0))
    tab = pl.BlockSpec((rows, HEAD_PAD), lambda b, j: (j, 0))
    params = [w["emb_g"], w["emb_b"], w["w_in"], w["q_norm"], w["w_uq"], w["kv_norm"], w["w_ukv"],
              w["conv_w"], w["conv_norm"]]
    wide = ATTN_HEADS * HEAD_PAD
    out_shape = [
        jax.ShapeDtypeStruct((bsz, seq, D_MODEL), jnp.float32),
        jax.ShapeDtypeStruct((bsz, seq, wide), jnp.bfloat16),
        jax.ShapeDtypeStruct((bsz, seq, wide), jnp.bfloat16),
        jax.ShapeDtypeStruct((bsz, seq, wide), jnp.bfloat16),
        jax.ShapeDtypeStruct((bsz, seq, D_CONV), jnp.bfloat16),
        jax.ShapeDtypeStruct((bsz, nj * SUBLANES, D_CONV), jnp.float32),
    ]
    return pl.pallas_call(
        _inproj_body,
        grid=(bsz, nj),
        in_specs=[row_blk(D_MODEL)] + [full(p) for p in params] + [tab, tab, tab, full(vcinit)],
        out_specs=[row_blk(D_MODEL), row_blk(wide), row_blk(wide), row_blk(wide), row_blk(D_CONV),
                   pl.BlockSpec((1, SUBLANES, D_CONV), lambda b, j: (b, j, 0))],
        out_shape=out_shape,
        scratch_shapes=[pltpu.VMEM((SUBLANES, D_CONV), jnp.float32)],
        compiler_params=pltpu.CompilerParams(dimension_semantics=("arbitrary", "arbitrary"),
                                             vmem_limit_bytes=VMEM_LIMIT),
        name="inproj",
    )(x, *params, *tabs, vcinit)


def _attn_body(q_ref, k_ref, v_ref, km_ref, vm_ref, o_ref):
    i = pl.program_id(2)
    tq = q_ref.shape[1]
    heads = [slice(hh * HEAD_PAD, (hh + 1) * HEAD_PAD) for hh in range(2)]
    qs = [q_ref[0, :, sl] for sl in heads]

    causal = lax.broadcasted_iota(jnp.int32, (tq, tq), 1) <= lax.broadcasted_iota(jnp.int32, (tq, tq), 0)
    is_meta = lax.broadcasted_iota(jnp.int32, (tq, META_PAD), 1) < N_META
    r0 = pl.multiple_of(i * tq, tq)
    carry = []
    for q, sl in zip(qs, heads):
        sa = jnp.where(causal, _dot_t(q, k_ref[0, pl.ds(r0, tq), sl]), _NEG_INF)
        sb = jnp.where(is_meta, _dot_t(q, km_ref[:, sl]), _NEG_INF)
        m = jnp.maximum(jnp.max(sa, axis=1, keepdims=True), jnp.max(sb, axis=1, keepdims=True))
        acc = (_dot(_bf(jnp.exp(sa - m)), v_ref[0, pl.ds(r0, tq), sl])
               + _dot(_bf(jnp.exp(sb - m)), vm_ref[:, sl]))
        carry += [m, acc]

    def body(j, carry):
        rj = pl.multiple_of(j * tq, tq)
        out = []
        for h, (q, sl) in enumerate(zip(qs, heads)):
            m, acc = carry[2 * h], carry[2 * h + 1]
            s = _dot_t(q, k_ref[0, pl.ds(rj, tq), sl])
            m_new = jnp.maximum(m, jnp.max(s, axis=1, keepdims=True))
            p = jnp.exp(s - m_new)
            out += [m_new, jnp.exp(m - m_new) * acc + _dot(_bf(p), v_ref[0, pl.ds(rj, tq), sl])]
        return tuple(out)

    _, a0, _, a1 = lax.fori_loop(0, i, body, tuple(carry))
    lane = lax.broadcasted_iota(jnp.int32, (tq, HEAD_PAD), 1)
    o0 = a0 / pltpu.roll(a0, V_HEAD, 1)
    o1 = pltpu.roll(a1, V_HEAD, 1) / a1
    o_ref[0] = _bf(jnp.where(lane < V_HEAD, o0, o1))


def _attention(q, k, v, km, vm):
    bsz, seq, _ = q.shape
    pairs = ATTN_HEADS // 2
    pw = 2 * HEAD_PAD
    return pl.pallas_call(
        _attn_body,
        grid=(bsz, pairs, seq // T_Q),
        in_specs=[pl.BlockSpec((1, T_Q, pw), lambda b, p, i: (b, i, p)),
                  pl.BlockSpec((1, seq, pw), lambda b, p, i: (b, 0, p)),
                  pl.BlockSpec((1, seq, pw), lambda b, p, i: (b, 0, p)),
                  pl.BlockSpec((META_PAD, pw), lambda b, p, i: (0, p)),
                  pl.BlockSpec((META_PAD, pw), lambda b, p, i: (0, p))],
        out_specs=pl.BlockSpec((1, T_Q, 2 * V_HEAD), lambda b, p, i: (b, i, p)),
        out_shape=jax.ShapeDtypeStruct((bsz, seq, D_ATTN), jnp.bfloat16),
        compiler_params=pltpu.CompilerParams(dimension_semantics=("arbitrary",) * 3,
                                             vmem_limit_bytes=VMEM_LIMIT),
        name="attention",
    )(q, k, v, km, vm)


def _mix_body(attn_ref, cn_ref, h0_ref, an_ref, woa_ref, woc_ref, g1_ref, b1_ref,
              wsg_ref, wsu_ref, wsd_ref, wrt_ref, rb_ref, tri_ref,
              hp_ref, base_ref, eidx_ref, gate_ref, pos_ref, cnt_ref, run_ref):
    t = attn_ref.shape[0]

    @pl.when(pl.program_id(0) == 0)
    def _():
        run_ref[...] = jnp.zeros_like(run_ref)

    attn_n = _rms(attn_ref[...].astype(jnp.float32), an_ref[...], D_ATTN)
    mix = _dot(_bf(attn_n), woa_ref[...]) + _dot(cn_ref[...], woc_ref[...])
    h1 = _layernorm(ALPHA * h0_ref[...] + mix, g1_ref[...], b1_ref[...])
    _store_row_tiles(hp_ref, (), h1)
    h1b = _bf(h1)
    g = _dot(h1b, wsg_ref[...])
    u = _dot(h1b, wsu_ref[...])
    base_ref[...] = ALPHA * h1 + _dot(_bf(g * _sigmoid(g) * u), wsd_ref[...])

    shp = (N_GROUPS, GROUP_SIZE, t)
    sc = _sigmoid(_dot_t(wrt_ref[...], h1b)).reshape(shp)
    sel = sc + rb_ref[...].reshape(N_GROUPS, GROUP_SIZE, 1)
    gio = lax.broadcasted_iota(jnp.int32, shp, 0)
    rio = lax.broadcasted_iota(jnp.int32, shp, 1)
    eio = gio * GROUP_SIZE + rio

    m1 = jnp.max(sel, axis=1, keepdims=True)
    i1 = jnp.min(jnp.where(sel == m1, rio, GROUP_SIZE), axis=1, keepdims=True)
    m2 = jnp.max(jnp.where(rio == i1, _NEG_INF, sel), axis=1, keepdims=True)
    gs = m1 + m2
    gio1 = lax.broadcasted_iota(jnp.int32, (N_GROUPS, 1, t), 0)
    gmask = jnp.zeros((N_GROUPS, 1, t), jnp.bool_)
    for _ in range(TOPK_GROUPS):
        mg = jnp.max(gs, axis=0, keepdims=True)
        ig = jnp.min(jnp.where(gs == mg, gio1, N_GROUPS), axis=0, keepdims=True)
        hit = gio1 == ig
        gmask = jnp.logical_or(gmask, hit)
        gs = jnp.where(hit, _NEG_INF, gs)

    cand = jnp.where(gmask, sel, _NEG_INF)

    def red(x, op):
        return op(op(x, axis=0, keepdims=True), axis=1, keepdims=True)

    picked = jnp.zeros(shp, jnp.bool_)
    e_rows, g_rows = [], []
    for _ in range(TOP_K):
        mk = red(cand, jnp.max)
        ek = red(jnp.where(cand == mk, eio, N_EXPERTS), jnp.min)
        hit = eio == ek
        g_rows.append(red(jnp.where(hit, sc, 0.0), jnp.sum))
        e_rows.append(ek)
        picked = jnp.logical_or(picked, hit)
        cand = jnp.where(hit, _NEG_INF, cand)
    gsum = g_rows[0]
    for gk in g_rows[1:]:
        gsum = gsum + gk

    pm = jnp.where(picked, 1.0, 0.0).reshape(N_EXPERTS, t)
    rank = _dot(_bf(pm), tri_ref[...])
    slot = (run_ref[...] + rank).reshape(shp)
    run_ref[...] = run_ref[...] + jnp.sum(pm, axis=1, keepdims=True)
    cnt_ref[...] = run_ref[...]
    for k in range(TOP_K):
        eidx_ref[0, k:k + 1, :] = e_rows[k].reshape(1, t)
        gate_ref[0, k:k + 1, :] = (g_rows[k] / gsum * ROUTED_SCALE).reshape(1, t)
        pk = red(jnp.where(eio == e_rows[k], slot, 0.0), jnp.sum)
        pos_ref[0, k:k + 1, :] = pk.reshape(1, t).astype(jnp.int32)


def _mix_route(attn, cn, h0, w):
    n = attn.shape[0]
    nt = n // T_MIX
    tri = _bf(jnp.triu(jnp.ones((T_MIX, T_MIX), jnp.float32), 1))
    params = [w["attn_norm"], w["wo_a"], w["wo_c"], w["ln1_g"], w["ln1_b"], w["ws_gate"], w["ws_up"],
              w["ws_down"], w["w_router_t"], w["router_bias"], tri]
    full = lambda a: pl.BlockSpec(a.shape, lambda i: (0,) * a.ndim)
    row_blk = lambda width: pl.BlockSpec((T_MIX, width), lambda i: (i, 0))
    k_blk = pl.BlockSpec((1, TOP_K, T_MIX), lambda i: (i, 0, 0))
    out_shape = [
        jax.ShapeDtypeStruct((n // SUBLANES, PACK_TILES, SUBLANES, LANES), jnp.int32),
        jax.ShapeDtypeStruct((n, D_MODEL), jnp.float32),
        jax.ShapeDtypeStruct((nt, TOP_K, T_MIX), jnp.int32),
        jax.ShapeDtypeStruct((nt, TOP_K, T_MIX), jnp.float32),
        jax.ShapeDtypeStruct((nt, TOP_K, T_MIX), jnp.int32),
        jax.ShapeDtypeStruct((N_EXPERTS, 1), jnp.float32),
    ]
    return pl.pallas_call(
        _mix_body,
        grid=(nt,),
        in_specs=[row_blk(D_ATTN), row_blk(D_CONV), row_blk(D_MODEL)] + [full(p) for p in params],
        out_specs=[pl.BlockSpec((T_MIX // SUBLANES, PACK_TILES, SUBLANES, LANES), lambda i: (i, 0, 0, 0)),
                   row_blk(D_MODEL),
                   k_blk, k_blk, k_blk, pl.BlockSpec((N_EXPERTS, 1), lambda i: (0, 0))],
        out_shape=out_shape,
        scratch_shapes=[pltpu.VMEM((N_EXPERTS, 1), jnp.float32)],
        compiler_params=pltpu.CompilerParams(dimension_semantics=("arbitrary",),
                                             vmem_limit_bytes=VMEM_LIMIT),
        name="mix_route",
    )(attn, cn, h0, *params)


def _pack_row_tiles(x):
    bits = lambda a: lax.bitcast_convert_type(_bf(a).astype(jnp.float32), jnp.int32)
    out = []
    for lt in range(PACK_TILES):
        lo = bits(x[:, lt * LANES:(lt + 1) * LANES])
        hi = bits(x[:, HALF + lt * LANES:HALF + (lt + 1) * LANES])
        out.append(lax.shift_right_logical(lo, 16) | (hi & HI_MASK))
    return out


def _unpack_lo(w):
    return lax.bitcast_convert_type(lax.shift_left(w, 16), jnp.float32)


def _unpack_hi(w):
    return lax.bitcast_convert_type(w & HI_MASK, jnp.float32)


def _store_row_tiles(ref, lead, x):
    rows = x.shape[0]
    for lt, w in enumerate(_pack_row_tiles(x)):
        ref[lead + (slice(None), lt)] = w.reshape(rows // SUBLANES, SUBLANES, LANES)


def _load_row_tiles(ref, lead, rows):
    return [ref[lead + (slice(None), lt)].reshape(rows, LANES) for lt in range(PACK_TILES)]


def _sc_mesh():
    return plsc.VectorSubcoreMesh(core_axis_name="c", subcore_axis_name="s")


def _worker_id():
    return lax.axis_index("s") * SC_CORES + lax.axis_index("c")


def _dispatch(hp, idx, n_rows):
    per_worker = hp.shape[0] // SC_PIECES // (SC_CORES * SC_SUBCORES)
    lists = idx.shape[1]

    @functools.partial(
        pl.kernel, mesh=_sc_mesh(),
        out_type=jax.ShapeDtypeStruct((n_rows, LANES), jnp.int32),
        scratch_types=[pltpu.VMEM((lists, LANES), jnp.int32),
                       pltpu.VMEM((SC_PIECES, LANES), jnp.int32),
                       pltpu.SemaphoreType.DMA],
        name="dispatch")
    def run(hp_hbm, idx_hbm, xs_hbm, idx_v, rows_v, sem):
        first = _worker_id() * per_worker

        @pl.loop(0, per_worker)
        def _(c):
            chunk = first + c
            pltpu.sync_copy(idx_hbm.at[chunk], idx_v)
            pltpu.sync_copy(hp_hbm.at[pl.ds(chunk * SC_PIECES, SC_PIECES)], rows_v)
            copies = [pltpu.async_copy(rows_v.at[pl.ds((j % SC_LISTS) * LANES, LANES)],
                                       xs_hbm.at[idx_v.at[j]], sem) for j in range(lists)]
            for cp in copies:
                cp.wait()

    return run(hp, idx)


def _combine(ys, idx, n_rows):
    per_worker = n_rows // SC_PIECES // (SC_CORES * SC_SUBCORES)
    lists = idx.shape[1]

    @functools.partial(
        pl.kernel, mesh=_sc_mesh(),
        out_type=jax.ShapeDtypeStruct((TOP_K, n_rows, LANES), jnp.int32),
        scratch_types=[pltpu.VMEM((lists, LANES), jnp.int32),
                       pltpu.VMEM((2, SC_PIECES, LANES), jnp.int32),
                       pltpu.SemaphoreType.DMA, pltpu.SemaphoreType.DMA],
        name="combine")
    def run(ys_hbm, idx_hbm, out_hbm, idx_v, rows_v, sem0, sem1):
        first = _worker_id() * per_worker
        sems = (sem0, sem1)

        def gather(k):
            buf = rows_v.at[k % 2]
            return [pltpu.async_copy(ys_hbm.at[idx_v.at[k * SC_LISTS + g]],
                                     buf.at[pl.ds(g * LANES, LANES)], sems[k % 2]) for g in range(SC_LISTS)]

        @pl.loop(0, per_worker)
        def _(c):
            chunk = first + c
            pltpu.sync_copy(idx_hbm.at[chunk], idx_v)
            pending = gather(0)
            for k in range(TOP_K):
                for cp in pending:
                    cp.wait()
                if k + 1 < TOP_K:
                    pending = gather(k + 1)
                pltpu.sync_copy(rows_v.at[k % 2], out_hbm.at[k, pl.ds(chunk * SC_PIECES, SC_PIECES)])

    return run(ys, idx)


def _expert_body(blk_e_ref, nused_ref, xs_ref, wg_ref, wu_ref, wd_ref, ys_ref, wgu_ref, wdn_ref):
    i = pl.program_id(0)

    @pl.when(i < nused_ref[0])
    def _():
        @pl.when(jnp.logical_or(i == 0, blk_e_ref[i] != blk_e_ref[jnp.maximum(i - 1, 0)]))
        def _():
            wgu_ref[:, :D_EXPERT] = _bf(wg_ref[0])
            wgu_ref[:, D_EXPERT:] = _bf(wu_ref[0])
            wdn_ref[...] = _bf(wd_ref[0])

        words = _load_row_tiles(xs_ref, (), BLK)
        x = jnp.concatenate([_bf(_unpack_lo(w)) for w in words] + [_bf(_unpack_hi(w)) for w in words], axis=1)
        gu = _dot(x, wgu_ref[...])
        g, u = gu[:, :D_EXPERT], gu[:, D_EXPERT:]
        _store_row_tiles(ys_ref, (), _dot(_bf(g * _sigmoid(g) * u), wdn_ref[...]))


def _experts(xs, blk_e, nused, w_gate, w_up, w_down):
    nb = xs.shape[0] * SUBLANES // BLK
    blk = (BLK // SUBLANES, PACK_TILES, SUBLANES, LANES)
    row_map = lambda i, be, nu: (jnp.minimum(i, nu[0] - 1), 0, 0, 0)
    w_map = lambda i, be, nu: (be[i], 0, 0)
    grid_spec = pltpu.PrefetchScalarGridSpec(
        num_scalar_prefetch=2,
        grid=(nb,),
        in_specs=[pl.BlockSpec(blk, row_map),
                  pl.BlockSpec((1, D_MODEL, D_EXPERT), w_map),
                  pl.BlockSpec((1, D_MODEL, D_EXPERT), w_map),
                  pl.BlockSpec((1, D_EXPERT, D_MODEL), w_map)],
        out_specs=pl.BlockSpec(blk, row_map),
        scratch_shapes=[pltpu.VMEM((D_MODEL, 2 * D_EXPERT), jnp.bfloat16),
                        pltpu.VMEM((D_EXPERT, D_MODEL), jnp.bfloat16)],
    )
    return pl.pallas_call(
        _expert_body,
        grid_spec=grid_spec,
        out_shape=jax.ShapeDtypeStruct(xs.shape, jnp.int32),
        compiler_params=pltpu.CompilerParams(dimension_semantics=("arbitrary",),
                                             vmem_limit_bytes=VMEM_LIMIT),
        name="experts",
    )(blk_e, nused, xs, w_gate, w_up, w_down)


def _final_body(yg_ref, gate_ref, base_ref, g2_ref, b2_ref, o_ref):
    t = base_ref.shape[0]
    gate = gate_ref[...]
    lo = [None] * PACK_TILES
    hi = [None] * PACK_TILES
    for k in range(TOP_K):
        gk = gate[:, k:k + 1]
        for lt, w in enumerate(_load_row_tiles(yg_ref, (k,), t)):
            a, b = gk * _unpack_lo(w), gk * _unpack_hi(w)
            lo[lt] = a if k == 0 else lo[lt] + a
            hi[lt] = b if k == 0 else hi[lt] + b
    ff = jnp.concatenate(lo + hi, axis=1)
    o_ref[...] = _layernorm(base_ref[...] + ff, g2_ref[...], b2_ref[...])


def _final(yg, gates, base, g2, b2):
    n = base.shape[0]
    full = lambda a: pl.BlockSpec(a.shape, lambda i: (0,) * a.ndim)
    return pl.pallas_call(
        _final_body,
        grid=(n // T_FIN,),
        in_specs=[pl.BlockSpec((TOP_K, T_FIN // SUBLANES, PACK_TILES, SUBLANES, LANES),
                               lambda i: (0, i, 0, 0, 0)),
                  pl.BlockSpec((T_FIN, TOP_K), lambda i: (i, 0)),
                  pl.BlockSpec((T_FIN, D_MODEL), lambda i: (i, 0)),
                  full(g2), full(b2)],
        out_specs=pl.BlockSpec((T_FIN, D_MODEL), lambda i: (i, 0)),
        out_shape=jax.ShapeDtypeStruct((n, D_MODEL), jnp.float32),
        compiler_params=pltpu.CompilerParams(dimension_semantics=("arbitrary",),
                                             vmem_limit_bytes=VMEM_LIMIT),
        name="final",
    )(yg, gates, base, g2, b2)


def _rope_tables(pos):
    half = QK_ROPE // 2
    inv = ROPE_THETA ** (-jnp.arange(0, QK_ROPE, 2, dtype=jnp.float32) / QK_ROPE)
    ang = pos.astype(jnp.float32)[:, None] * inv[None, :]
    cos, sin = jnp.cos(ang), jnp.sin(ang)
    n = pos.shape[0]
    z = lambda w: jnp.zeros((n, w), jnp.float32)
    c = jnp.concatenate([jnp.ones((n, QK_NOPE), jnp.float32), cos, cos, z(LANES - QK_NOPE - QK_ROPE)], axis=1)
    s_up = jnp.concatenate([z(QK_NOPE + half), sin, z(LANES - QK_NOPE - QK_ROPE)], axis=1)
    s_dn = jnp.concatenate([z(QK_NOPE), -sin, z(half), z(LANES - QK_NOPE - QK_ROPE)], axis=1)
    return c, s_up, s_dn


def _pad_heads(w, per_head, keep):
    rows = w.shape[0]
    w = w.reshape(rows, ATTN_HEADS, per_head)[:, :, :keep]
    w = jnp.pad(w, ((0, 0), (0, 0), (0, HEAD_PAD - keep)))
    return w.reshape(rows, ATTN_HEADS * HEAD_PAD)


def _prep_weights(emb_ln_g, emb_ln_b, w_in, q_norm, w_uq, kv_norm, w_ukv, conv_w, attn_out_norm,
                  conv_out_norm, w_o, ln1_g, ln1_b, w_router, router_bias, ws_gate, ws_up, ws_down):
    o1 = Q_LORA
    o2 = o1 + KV_LORA
    o3 = o2 + QK_ROPE
    row = lambda a: a.reshape(1, -1)
    kpe_cols = jnp.pad(w_in[:, o2:o3], ((0, 0), (QK_NOPE, HEAD_PAD - QK_NOPE - QK_ROPE)))
    w_in_p = jnp.concatenate([w_in[:, :o2], kpe_cols, w_in[:, o3:]], axis=1)
    ukv = w_ukv.reshape(KV_LORA, ATTN_HEADS, QK_NOPE + V_HEAD)
    w_uk = jnp.pad(ukv[:, :, :QK_NOPE], ((0, 0), (0, 0), (0, HEAD_PAD - QK_NOPE)))
    w_uv = jnp.pad(ukv[:, :, QK_NOPE:], ((0, 0), (0, 0), (0, HEAD_PAD - V_HEAD)))
    w_ukv_p = jnp.concatenate([w_uk.reshape(KV_LORA, -1), w_uv.reshape(KV_LORA, -1)], axis=1)
    woa = w_o[:D_ATTN]
    return {
        "emb_g": row(emb_ln_g), "emb_b": row(emb_ln_b),
        "w_in": _bf(w_in_p),
        "q_norm": row(q_norm), "w_uq": _bf(_pad_heads(w_uq, QK_NOPE + QK_ROPE, QK_NOPE + QK_ROPE)),
        "kv_norm": row(kv_norm), "w_ukv": _bf(w_ukv_p),
        "conv_w": jnp.pad(conv_w, ((0, SUBLANES - CONV_W), (0, 0))), "conv_norm": row(conv_out_norm),
        "attn_norm": row(attn_out_norm), "wo_a": _bf(woa), "wo_c": _bf(w_o[D_ATTN:]),
        "ln1_g": row(ln1_g), "ln1_b": row(ln1_b),
        "ws_gate": _bf(ws_gate), "ws_up": _bf(ws_up), "ws_down": _bf(ws_down),
        "w_router_t": _bf(w_router.T), "router_bias": router_bias.reshape(-1, 1),
    }


def _slot_body(eidx_ref, pos_ref, start_ref, dest_ref):
    t = eidx_ref.shape[2]
    shp = (N_GROUPS, GROUP_SIZE, t)
    eio = (lax.broadcasted_iota(jnp.int32, shp, 0) * GROUP_SIZE + lax.broadcasted_iota(jnp.int32, shp, 1))
    start = start_ref[...].reshape(N_GROUPS, GROUP_SIZE, 1)
    for k in range(TOP_K):
        hit = eio == eidx_ref[0, k:k + 1, :].reshape(1, 1, t)
        st = jnp.sum(jnp.sum(jnp.where(hit, start, 0.0), axis=0, keepdims=True), axis=1, keepdims=True)
        dest_ref[0, k:k + 1, :] = st.reshape(1, t).astype(jnp.int32) + pos_ref[0, k:k + 1, :]


def _slots(eidx, pos, start):
    nt, _, t = eidx.shape
    k_blk = pl.BlockSpec((1, TOP_K, t), lambda i: (i, 0, 0))
    return pl.pallas_call(
        _slot_body,
        grid=(nt,),
        in_specs=[k_blk, k_blk, pl.BlockSpec(start.shape, lambda i: (0, 0))],
        out_specs=k_blk,
        out_shape=jax.ShapeDtypeStruct(eidx.shape, jnp.int32),
        compiler_params=pltpu.CompilerParams(dimension_semantics=("arbitrary",)),
        name="slots",
    )(eidx, pos, start)


def _slot_plan(eidx, pos, counts):
    counts = counts.reshape(-1).astype(jnp.int32)
    padded = (counts + BLK - 1) // BLK * BLK
    pad_end = jnp.cumsum(padded)
    pad_start = pad_end - padded
    dest = _slots(eidx, pos, pad_start.astype(jnp.float32).reshape(-1, 1))
    n_assign = eidx.size
    nb = n_assign // BLK + N_EXPERTS
    nused = pad_end[-1] // BLK
    first_row = jnp.minimum(jnp.arange(nb, dtype=jnp.int32), nused - 1) * BLK
    blk_e = jnp.sum((pad_end[None, :] <= first_row[:, None]).astype(jnp.int32), axis=1)
    blk_e = jnp.minimum(blk_e, N_EXPERTS - 1).astype(jnp.int32)
    return dest, blk_e, nused.reshape(1).astype(jnp.int32), nb * BLK


def _piece_lists(dest):
    nt, k, t = dest.shape
    d = dest.reshape(nt, k, t // SC_ROWS, SC_ROWS // SUBLANES, SUBLANES).transpose(0, 2, 1, 3, 4)
    base = (d // SUBLANES) * (PACK_TILES * SUBLANES) + d % SUBLANES
    lt = jnp.arange(PACK_TILES, dtype=jnp.int32) * SUBLANES
    pieces = base[:, :, :, :, None, :] + lt[:, None]
    return pieces.reshape(nt * (t // SC_ROWS), k * SC_LISTS, LANES)


def kernel(x, meta_tokens, emb_ln_g, emb_ln_b, w_in, q_norm, w_uq, kv_norm, w_ukv, conv_w, attn_out_norm, conv_out_norm, w_o, ln1_g, ln1_b, w_router, router_bias, w_gate, w_up, w_down, ws_gate, ws_up, ws_down, ln2_g, ln2_b):
    bsz, seq, _ = x.shape
    n = bsz * seq
    w = _prep_weights(emb_ln_g, emb_ln_b, w_in[0], q_norm[0], w_uq[0], kv_norm[0], w_ukv[0], conv_w[0],
                      attn_out_norm[0], conv_out_norm[0], w_o[0], ln1_g[0], ln1_b[0], w_router[0],
                      router_bias[0], ws_gate[0], ws_up[0], ws_down[0])

    meta_out = _inproj(meta_tokens[None], _rope_tables(jnp.arange(N_META)),
                       jnp.zeros((SUBLANES, D_CONV), jnp.float32), w, N_META)
    pad_rows = lambda a: jnp.pad(a, ((0, META_PAD - N_META), (0, 0)))
    km, vm, vc_meta = pad_rows(meta_out[2][0]), pad_rows(meta_out[3][0]), meta_out[5][0]

    h0, q, k, v, cn, _ = _inproj(x, _rope_tables(N_META + jnp.arange(seq)), vc_meta, w, T_IN)
    attn = _attention(q, k, v, km, vm)

    hp, base, eidx, gates, pos, counts = _mix_route(
        attn.reshape(n, D_ATTN), cn.reshape(n, D_CONV), h0.reshape(n, D_MODEL), w)
    dest, blk_e, nused, n_slots = _slot_plan(eidx, pos, counts)
    idx = _piece_lists(dest)
    tiled = lambda a, lead: a.reshape(lead + (-1, PACK_TILES, SUBLANES, LANES))

    xs = _dispatch(hp.reshape(-1, LANES), idx, n_slots * PACK_TILES)
    ys = _experts(tiled(xs, ()), blk_e, nused, w_gate[0], w_up[0], w_down[0])
    yg = tiled(_combine(ys.reshape(-1, LANES), idx, n * PACK_TILES), (TOP_K,))
    gates_t = gates.transpose(0, 2, 1).reshape(n, TOP_K)
    out = _final(yg, gates_t, base, ln2_g[0].reshape(1, -1), ln2_b[0].reshape(1, -1))
    return out.reshape(bsz, seq, D_MODEL)
```

```python
import functools
import math

import jax
import jax.numpy as jnp
from jax import lax
from jax.experimental import pallas as pl
from jax.experimental.pallas import tpu as pltpu
from jax.experimental.pallas import tpu_sc as plsc

D_MODEL = 1024
N_META = 16
ATTN_HEADS = 8
QK_NOPE = 64
QK_ROPE = 32
V_HEAD = 64
Q_LORA = 384
KV_LORA = 256
ROPE_THETA = 10000.0
D_ATTN = ATTN_HEADS * V_HEAD
D_CONV = 512
CONV_W = 3
ATTN_SCALE = 1.0 / math.sqrt(QK_NOPE + QK_ROPE)
N_EXPERTS = 256
TOP_K = 8
N_GROUPS = 8
GROUP_SIZE = N_EXPERTS // N_GROUPS
TOPK_GROUPS = 4
D_EXPERT = 256
D_SHARED = 256
ROUTED_SCALE = 2.5
DEPTH = 1
ALPHA = (2.0 * DEPTH) ** 0.25

LANES = 128
SUBLANES = 8
HEAD_PAD = LANES
META_PAD = LANES
HALF = D_MODEL // 2
PACK_TILES = HALF // LANES
HI_MASK = -65536
SC_CORES = 2
SC_SUBCORES = 16

T_IN = 512
T_Q = 512
T_MIX = 256
BLK = 256
T_FIN = 256
SC_ROWS = 64
SC_PIECES = SC_ROWS * PACK_TILES
SC_LISTS = SC_PIECES // LANES

VMEM_LIMIT = 56 * 1024 * 1024

_NEG_INF = float("-inf")


def _bf(x):
    return x.astype(jnp.bfloat16)


def _dot(a, b):
    return jnp.dot(a, b, preferred_element_type=jnp.float32)


def _dot_t(a, b):
    return lax.dot_general(a, b, (((1,), (1,)), ((), ())), preferred_element_type=jnp.float32)


def _layernorm(x, g, b, eps=1e-5):
    mu = jnp.mean(x, axis=-1, keepdims=True)
    xc = x - mu
    var = jnp.mean(xc * xc, axis=-1, keepdims=True)
    return xc * lax.rsqrt(var + eps) * g + b


def _rms(x, g, n, eps=1e-6):
    ms = jnp.sum(x * x, axis=-1, keepdims=True) * (1.0 / n)
    return x * lax.rsqrt(ms + eps) * g


def _sigmoid(x):
    return 1.0 / (1.0 + jnp.exp(-x))


def _rope(x, c, s_up, s_dn):
    return x * c + pltpu.roll(x, QK_ROPE // 2, 1) * s_up + pltpu.roll(x, LANES - QK_ROPE // 2, 1) * s_dn


def _inproj_body(x_ref, g_ref, b_ref, win_ref, qn_ref, wuq_ref, kvn_ref, wukv_ref, cw_ref, con_ref,
                 tc_ref, tu_ref, td_ref, vcinit_ref,
                 h0_ref, q_ref, k_ref, v_ref, cn_ref, vct_ref, carry_ref):
    rows = x_ref.shape[1]

    @pl.when(pl.program_id(1) == 0)
    def _():
        carry_ref[...] = vcinit_ref[...]

    h0 = _layernorm(x_ref[0], g_ref[...], b_ref[...])
    h0_ref[0] = h0
    z = _dot(_bf(h0), win_ref[...])
    o1 = Q_LORA
    o2 = o1 + KV_LORA
    o3 = o2 + HEAD_PAD
    o4 = o3 + D_CONV
    o5 = o4 + D_CONV
    c, s_up, s_dn = tc_ref[...], tu_ref[...], td_ref[...]

    q = _dot(_bf(_rms(z[:, :o1], qn_ref[...], Q_LORA)), wuq_ref[...])
    for h in range(ATTN_HEADS):
        sl = slice(h * HEAD_PAD, (h + 1) * HEAD_PAD)
        q_ref[0, :, sl] = _bf(_rope(q[:, sl], c, s_up, s_dn) * ATTN_SCALE)

    kv = _dot(_bf(_rms(z[:, o1:o2], kvn_ref[...], KV_LORA)), wukv_ref[...])
    kpe = _rope(z[:, o2:o3], c, s_up, s_dn)
    lane = lax.broadcasted_iota(jnp.int32, (rows, HEAD_PAD), 1)
    for h in range(ATTN_HEADS):
        sl = slice(h * HEAD_PAD, (h + 1) * HEAD_PAD)
        k_ref[0, :, sl] = _bf(kv[:, sl] + kpe)
        vh = kv[:, ATTN_HEADS * HEAD_PAD + h * HEAD_PAD: ATTN_HEADS * HEAD_PAD + (h + 1) * HEAD_PAD]
        v_ref[0, :, sl] = _bf(jnp.where(lane < V_HEAD, vh, 1.0))

    vc = z[:, o4:o5] * z[:, o5:]
    carry = carry_ref[...]
    row = lax.broadcasted_iota(jnp.int32, (rows, D_CONV), 0)
    vc1 = jnp.where(row == 0, carry[SUBLANES - 1:SUBLANES, :], pltpu.roll(vc, 1, 0))
    vc2 = pltpu.roll(vc, 2, 0)
    vc2 = jnp.where(row == 0, carry[SUBLANES - 2:SUBLANES - 1, :], vc2)
    vc2 = jnp.where(row == 1, carry[SUBLANES - 1:SUBLANES, :], vc2)
    cw = cw_ref[...]
    conv = z[:, o3:o4] * (cw[0:1, :] * vc2 + cw[1:2, :] * vc1 + cw[2:3, :] * vc)
    cn_ref[0] = _bf(_rms(conv, con_ref[...], D_CONV))
    tail = vc[rows - SUBLANES:, :]
    carry_ref[...] = tail
    vct_ref[0] = tail


def _inproj(x, tabs, vcinit, w, rows):
    bsz, seq, _ = x.shape
    nj = seq // rows
    full = lambda a: pl.BlockSpec(a.shape, lambda b, j: (0,) * a.ndim)
    row_blk = lambda width: pl.BlockSpec((1, rows, width), lambda b, j: (b, j, 0))
    tab = pl.BlockSpec((rows, HEAD_PAD), lambda b, j: (j, 0))
    params = [w["emb_g"], w["emb_b"], w["w_in"], w["q_norm"], w["w_uq"], w["kv_norm"], w["w_ukv"],
              w["conv_w"], w["conv_norm"]]
    wide = ATTN_HEADS * HEAD_PAD
    out_shape = [
        jax.ShapeDtypeStruct((bsz, seq, D_MODEL), jnp.float32),
        jax.ShapeDtypeStruct((bsz, seq, wide), jnp.bfloat16),
        jax.ShapeDtypeStruct((bsz, seq, wide), jnp.bfloat16),
        jax.ShapeDtypeStruct((bsz, seq, wide), jnp.bfloat16),
        jax.ShapeDtypeStruct((bsz, seq, D_CONV), jnp.bfloat16),
        jax.ShapeDtypeStruct((bsz, nj * SUBLANES, D_CONV), jnp.float32),
    ]
    return pl.pallas_call(
        _inproj_body,
        grid=(bsz, nj),
        in_specs=[row_blk(D_MODEL)] + [full(p) for p in params] + [tab, tab, tab, full(vcinit)],
        out_specs=[row_blk(D_MODEL), row_blk(wide), row_blk(wide), row_blk(wide), row_blk(D_CONV),
                   pl.BlockSpec((1, SUBLANES, D_CONV), lambda b, j: (b, j, 0))],
        out_shape=out_shape,
        scratch_shapes=[pltpu.VMEM((SUBLANES, D_CONV), jnp.float32)],
        compiler_params=pltpu.CompilerParams(dimension_semantics=("arbitrary", "arbitrary"),
                                             vmem_limit_bytes=VMEM_LIMIT),
        name="inproj",
    )(x, *params, *tabs, vcinit)


def _attn_body(q_ref, k_ref, v_ref, km_ref, vm_ref, o_ref):
    i = pl.program_id(2)
    tq = q_ref.shape[1]
    heads = [slice(hh * HEAD_PAD, (hh + 1) * HEAD_PAD) for hh in range(2)]
    qs = [q_ref[0, :, sl] for sl in heads]

    causal = lax.broadcasted_iota(jnp.int32, (tq, tq), 1) <= lax.broadcasted_iota(jnp.int32, (tq, tq), 0)
    is_meta = lax.broadcasted_iota(jnp.int32, (tq, META_PAD), 1) < N_META
    r0 = pl.multiple_of(i * tq, tq)
    carry = []
    for q, sl in zip(qs, heads):
        sa = jnp.where(causal, _dot_t(q, k_ref[0, pl.ds(r0, tq), sl]), _NEG_INF)
        sb = jnp.where(is_meta, _dot_t(q, km_ref[:, sl]), _NEG_INF)
        m = jnp.maximum(jnp.max(sa, axis=1, keepdims=True), jnp.max(sb, axis=1, keepdims=True))
        acc = (_dot(_bf(jnp.exp(sa - m)), v_ref[0, pl.ds(r0, tq), sl])
               + _dot(_bf(jnp.exp(sb - m)), vm_ref[:, sl]))
        carry += [m, acc]

    def body(j, carry):
        rj = pl.multiple_of(j * tq, tq)
        out = []
        for h, (q, sl) in enumerate(zip(qs, heads)):
            m, acc = carry[2 * h], carry[2 * h + 1]
            s = _dot_t(q, k_ref[0, pl.ds(rj, tq), sl])
            m_new = jnp.maximum(m, jnp.max(s, axis=1, keepdims=True))
            p = jnp.exp(s - m_new)
            out += [m_new, jnp.exp(m - m_new) * acc + _dot(_bf(p), v_ref[0, pl.ds(rj, tq), sl])]
        return tuple(out)

    _, a0, _, a1 = lax.fori_loop(0, i, body, tuple(carry))
    lane = lax.broadcasted_iota(jnp.int32, (tq, HEAD_PAD), 1)
    o0 = a0 / pltpu.roll(a0, V_HEAD, 1)
    o1 = pltpu.roll(a1, V_HEAD, 1) / a1
    o_ref[0] = _bf(jnp.where(lane < V_HEAD, o0, o1))


def _attention(q, k, v, km, vm):
    bsz, seq, _ = q.shape
    pairs = ATTN_HEADS // 2
    pw = 2 * HEAD_PAD
    return pl.pallas_call(
        _attn_body,
        grid=(bsz, pairs, seq // T_Q),
        in_specs=[pl.BlockSpec((1, T_Q, pw), lambda b, p, i: (b, i, p)),
                  pl.BlockSpec((1, seq, pw), lambda b, p, i: (b, 0, p)),
                  pl.BlockSpec((1, seq, pw), lambda b, p, i: (b, 0, p)),
                  pl.BlockSpec((META_PAD, pw), lambda b, p, i: (0, p)),
                  pl.BlockSpec((META_PAD, pw), lambda b, p, i: (0, p))],
        out_specs=pl.BlockSpec((1, T_Q, 2 * V_HEAD), lambda b, p, i: (b, i, p)),
        out_shape=jax.ShapeDtypeStruct((bsz, seq, D_ATTN), jnp.bfloat16),
        compiler_params=pltpu.CompilerParams(dimension_semantics=("arbitrary",) * 3,
                                             vmem_limit_bytes=VMEM_LIMIT),
        name="attention",
    )(q, k, v, km, vm)


def _mix_body(attn_ref, cn_ref, h0_ref, an_ref, woa_ref, woc_ref, g1_ref, b1_ref,
              wsg_ref, wsu_ref, wsd_ref, wrt_ref, rb_ref, tri_ref,
              hp_ref, base_ref, eidx_ref, gate_ref, pos_ref, cnt_ref, run_ref):
    t = attn_ref.shape[0]

    @pl.when(pl.program_id(0) == 0)
    def _():
        run_ref[...] = jnp.zeros_like(run_ref)

    attn_n = _rms(attn_ref[...].astype(jnp.float32), an_ref[...], D_ATTN)
    mix = _dot(_bf(attn_n), woa_ref[...]) + _dot(cn_ref[...], woc_ref[...])
    h1 = _layernorm(ALPHA * h0_ref[...] + mix, g1_ref[...], b1_ref[...])
    _store_row_tiles(hp_ref, (), h1)
    h1b = _bf(h1)
    g = _dot(h1b, wsg_ref[...])
    u = _dot(h1b, wsu_ref[...])
    base_ref[...] = ALPHA * h1 + _dot(_bf(g * _sigmoid(g) * u), wsd_ref[...])

    shp = (N_GROUPS, GROUP_SIZE, t)
    sc = _sigmoid(_dot_t(wrt_ref[...], h1b)).reshape(shp)
    sel = sc + rb_ref[...].reshape(N_GROUPS, GROUP_SIZE, 1)
    gio = lax.broadcasted_iota(jnp.int32, shp, 0)
    rio = lax.broadcasted_iota(jnp.int32, shp, 1)
    eio = gio * GROUP_SIZE + rio

    m1 = jnp.max(sel, axis=1, keepdims=True)
    i1 = jnp.min(jnp.where(sel == m1, rio, GROUP_SIZE), axis=1, keepdims=True)
    m2 = jnp.max(jnp.where(rio == i1, _NEG_INF, sel), axis=1, keepdims=True)
    gs = m1 + m2
    gio1 = lax.broadcasted_iota(jnp.int32, (N_GROUPS, 1, t), 0)
    gmask = jnp.zeros((N_GROUPS, 1, t), jnp.bool_)
    for _ in range(TOPK_GROUPS):
        mg = jnp.max(gs, axis=0, keepdims=True)
        ig = jnp.min(jnp.where(gs == mg, gio1, N_GROUPS), axis=0, keepdims=True)
        hit = gio1 == ig
        gmask = jnp.logical_or(gmask, hit)
        gs = jnp.where(hit, _NEG_INF, gs)

    cand = jnp.where(gmask, sel, _NEG_INF)

    def red(x, op):
        return op(op(x, axis=0, keepdims=True), axis=1, keepdims=True)

    picked = jnp.zeros(shp, jnp.bool_)
    e_rows, g_rows = [], []
    for _ in range(TOP_K):
        mk = red(cand, jnp.max)
        ek = red(jnp.where(cand == mk, eio, N_EXPERTS), jnp.min)
        hit = eio == ek
        g_rows.append(red(jnp.where(hit, sc, 0.0), jnp.sum))
        e_rows.append(ek)
        picked = jnp.logical_or(picked, hit)
        cand = jnp.where(hit, _NEG_INF, cand)
    gsum = g_rows[0]
    for gk in g_rows[1:]:
        gsum = gsum + gk

    pm = jnp.where(picked, 1.0, 0.0).reshape(N_EXPERTS, t)
    rank = _dot(_bf(pm), tri_ref[...])
    slot = (run_ref[...] + rank).reshape(shp)
    run_ref[...] = run_ref[...] + jnp.sum(pm, axis=1, keepdims=True)
    cnt_ref[...] = run_ref[...]
    for k in range(TOP_K):
        eidx_ref[0, k:k + 1, :] = e_rows[k].reshape(1, t)
        gate_ref[0, k:k + 1, :] = (g_rows[k] / gsum * ROUTED_SCALE).reshape(1, t)
        pk = red(jnp.where(eio == e_rows[k], slot, 0.0), jnp.sum)
        pos_ref[0, k:k + 1, :] = pk.reshape(1, t).astype(jnp.int32)


def _mix_route(attn, cn, h0, w):
    n = attn.shape[0]
    nt = n // T_MIX
    tri = _bf(jnp.triu(jnp.ones((T_MIX, T_MIX), jnp.float32), 1))
    params = [w["attn_norm"], w["wo_a"], w["wo_c"], w["ln1_g"], w["ln1_b"], w["ws_gate"], w["ws_up"],
              w["ws_down"], w["w_router_t"], w["router_bias"], tri]
    full = lambda a: pl.BlockSpec(a.shape, lambda i: (0,) * a.ndim)
    row_blk = lambda width: pl.BlockSpec((T_MIX, width), lambda i: (i, 0))
    k_blk = pl.BlockSpec((1, TOP_K, T_MIX), lambda i: (i, 0, 0))
    out_shape = [
        jax.ShapeDtypeStruct((n // SUBLANES, PACK_TILES, SUBLANES, LANES), jnp.int32),
        jax.ShapeDtypeStruct((n, D_MODEL), jnp.float32),
        jax.ShapeDtypeStruct((nt, TOP_K, T_MIX), jnp.int32),
        jax.ShapeDtypeStruct((nt, TOP_K, T_MIX), jnp.float32),
        jax.ShapeDtypeStruct((nt, TOP_K, T_MIX), jnp.int32),
        jax.ShapeDtypeStruct((N_EXPERTS, 1), jnp.float32),
    ]
    return pl.pallas_call(
        _mix_body,
        grid=(nt,),
        in_specs=[row_blk(D_ATTN), row_blk(D_CONV), row_blk(D_MODEL)] + [full(p) for p in params],
        out_specs=[pl.BlockSpec((T_MIX // SUBLANES, PACK_TILES, SUBLANES, LANES), lambda i: (i, 0, 0, 0)),
                   row_blk(D_MODEL),
                   k_blk, k_blk, k_blk, pl.BlockSpec((N_EXPERTS, 1), lambda i: (0, 0))],
        out_shape=out_shape,
        scratch_shapes=[pltpu.VMEM((N_EXPERTS, 1), jnp.float32)],
        compiler_params=pltpu.CompilerParams(dimension_semantics=("arbitrary",),
                                             vmem_limit_bytes=VMEM_LIMIT),
        name="mix_route",
    )(attn, cn, h0, *params)


def _pack_row_tiles(x):
    bits = lambda a: lax.bitcast_convert_type(_bf(a).astype(jnp.float32), jnp.int32)
    out = []
    for lt in range(PACK_TILES):
        lo = bits(x[:, lt * LANES:(lt + 1) * LANES])
        hi = bits(x[:, HALF + lt * LANES:HALF + (lt + 1) * LANES])
        out.append(lax.shift_right_logical(lo, 16) | (hi & HI_MASK))
    return out


def _unpack_lo(w):
    return lax.bitcast_convert_type(lax.shift_left(w, 16), jnp.float32)


def _unpack_hi(w):
    return lax.bitcast_convert_type(w & HI_MASK, jnp.float32)


def _store_row_tiles(ref, lead, x):
    rows = x.shape[0]
    for lt, w in enumerate(_pack_row_tiles(x)):
        ref[lead + (slice(None), lt)] = w.reshape(rows // SUBLANES, SUBLANES, LANES)


def _load_row_tiles(ref, lead, rows):
    return [ref[lead + (slice(None), lt)].reshape(rows, LANES) for lt in range(PACK_TILES)]


def _sc_mesh():
    return plsc.VectorSubcoreMesh(core_axis_name="c", subcore_axis_name="s")


def _worker_id():
    return lax.axis_index("s") * SC_CORES + lax.axis_index("c")


def _dispatch(hp, idx, n_rows):
    per_worker = hp.shape[0] // SC_PIECES // (SC_CORES * SC_SUBCORES)
    lists = idx.shape[1]

    @functools.partial(
        pl.kernel, mesh=_sc_mesh(),
        out_type=jax.ShapeDtypeStruct((n_rows, LANES), jnp.int32),
        scratch_types=[pltpu.VMEM((lists, LANES), jnp.int32),
                       pltpu.VMEM((SC_PIECES, LANES), jnp.int32),
                       pltpu.SemaphoreType.DMA],
        name="dispatch")
    def run(hp_hbm, idx_hbm, xs_hbm, idx_v, rows_v, sem):
        first = _worker_id() * per_worker

        @pl.loop(0, per_worker)
        def _(c):
            chunk = first + c
            pltpu.sync_copy(idx_hbm.at[chunk], idx_v)
            pltpu.sync_copy(hp_hbm.at[pl.ds(chunk * SC_PIECES, SC_PIECES)], rows_v)
            copies = [pltpu.async_copy(rows_v.at[pl.ds((j % SC_LISTS) * LANES, LANES)],
                                       xs_hbm.at[idx_v.at[j]], sem) for j in range(lists)]
            for cp in copies:
                cp.wait()

    return run(hp, idx)


def _combine(ys, idx, n_rows):
    per_worker = n_rows // SC_PIECES // (SC_CORES * SC_SUBCORES)
    lists = idx.shape[1]

    @functools.partial(
        pl.kernel, mesh=_sc_mesh(),
        out_type=jax.ShapeDtypeStruct((TOP_K, n_rows, LANES), jnp.int32),
        scratch_types=[pltpu.VMEM((lists, LANES), jnp.int32),
                       pltpu.VMEM((2, SC_PIECES, LANES), jnp.int32),
                       pltpu.SemaphoreType.DMA, pltpu.SemaphoreType.DMA],
        name="combine")
    def run(ys_hbm, idx_hbm, out_hbm, idx_v, rows_v, sem0, sem1):
        first = _worker_id() * per_worker
        sems = (sem0, sem1)

        def gather(k):
            buf = rows_v.at[k % 2]
            return [pltpu.async_copy(ys_hbm.at[idx_v.at[k * SC_LISTS + g]],
                                     buf.at[pl.ds(g * LANES, LANES)], sems[k % 2]) for g in range(SC_LISTS)]

        @pl.loop(0, per_worker)
        def _(c):
            chunk = first + c
            pltpu.sync_copy(idx_hbm.at[chunk], idx_v)
            pending = gather(0)
            for k in range(TOP_K):
                for cp in pending:
                    cp.wait()
                if k + 1 < TOP_K:
                    pending = gather(k + 1)
                pltpu.sync_copy(rows_v.at[k % 2], out_hbm.at[k, pl.ds(chunk * SC_PIECES, SC_PIECES)])

    return run(ys, idx)


def _expert_body(first_ref, count_ref, nused_ref, xs_hbm, wg_ref, wu_ref, wd_ref, ys_hbm,
                 xbuf, ybuf, wgu_ref, wdn_ref, sem_in, sem_out):
    e = pl.program_id(0)
    nused = nused_ref[0]
    tiles = BLK // SUBLANES

    def fetch(b, slot):
        return pltpu.make_async_copy(xs_hbm.at[pl.ds(b * tiles, tiles)], xbuf.at[slot], sem_in.at[slot])

    def writeback(b, slot):
        return pltpu.make_async_copy(ybuf.at[slot], ys_hbm.at[pl.ds(b * tiles, tiles)], sem_out.at[slot])

    @pl.when(e == 0)
    def _():
        fetch(0, 0).start()

    @pl.when(count_ref[e] > 0)
    def _():
        wgu_ref[:, :D_EXPERT] = _bf(wg_ref[0])
        wgu_ref[:, D_EXPERT:] = _bf(wu_ref[0])
        wdn_ref[...] = _bf(wd_ref[0])

    def block(t, carry):
        b = first_ref[e] + t
        slot = b % 2
        fetch(b, slot).wait()

        @pl.when(b + 1 < nused)
        def _():
            fetch(b + 1, 1 - slot).start()

        words = _load_row_tiles(xbuf, (slot,), BLK)
        x = jnp.concatenate([_bf(_unpack_lo(w)) for w in words] + [_bf(_unpack_hi(w)) for w in words], axis=1)
        gu = _dot(x, wgu_ref[...])
        g, u = gu[:, :D_EXPERT], gu[:, D_EXPERT:]
        y = _dot(_bf(g * _sigmoid(g) * u), wdn_ref[...])

        @pl.when(b >= 2)
        def _():
            writeback(b - 2, slot).wait()

        _store_row_tiles(ybuf, (slot,), y)
        writeback(b, slot).start()
        return carry

    lax.fori_loop(0, count_ref[e], block, 0)

    @pl.when(e == pl.num_programs(0) - 1)
    def _():
        for back in (1, 2):
            @pl.when(nused >= back)
            def _():
                b = nused - back
                writeback(b, b % 2).wait()


def _experts(xs, first_blk, n_blk, nused, w_gate, w_up, w_down):
    blk = (BLK // SUBLANES, PACK_TILES, SUBLANES, LANES)
    w_map = lambda e, fb, nb, nu: (e, 0, 0)
    grid_spec = pltpu.PrefetchScalarGridSpec(
        num_scalar_prefetch=3,
        grid=(N_EXPERTS,),
        in_specs=[pl.BlockSpec(memory_space=pl.ANY),
                  pl.BlockSpec((1, D_MODEL, D_EXPERT), w_map),
                  pl.BlockSpec((1, D_MODEL, D_EXPERT), w_map),
                  pl.BlockSpec((1, D_EXPERT, D_MODEL), w_map)],
        out_specs=pl.BlockSpec(memory_space=pl.ANY),
        scratch_shapes=[pltpu.VMEM((2,) + blk, jnp.int32),
                        pltpu.VMEM((2,) + blk, jnp.int32),
                        pltpu.VMEM((D_MODEL, 2 * D_EXPERT), jnp.bfloat16),
                        pltpu.VMEM((D_EXPERT, D_MODEL), jnp.bfloat16),
                        pltpu.SemaphoreType.DMA((2,)),
                        pltpu.SemaphoreType.DMA((2,))],
    )
    return pl.pallas_call(
        _expert_body,
        grid_spec=grid_spec,
        out_shape=jax.ShapeDtypeStruct(xs.shape, jnp.int32),
        compiler_params=pltpu.CompilerParams(dimension_semantics=("arbitrary",),
                                             vmem_limit_bytes=VMEM_LIMIT),
        name="experts",
    )(first_blk, n_blk, nused, xs, w_gate, w_up, w_down)


def _final_body(yg_ref, gate_ref, base_ref, g2_ref, b2_ref, o_ref):
    t = base_ref.shape[0]
    gate = gate_ref[...]
    lo = [None] * PACK_TILES
    hi = [None] * PACK_TILES
    for k in range(TOP_K):
        gk = gate[:, k:k + 1]
        for lt, w in enumerate(_load_row_tiles(yg_ref, (k,), t)):
            a, b = gk * _unpack_lo(w), gk * _unpack_hi(w)
            lo[lt] = a if k == 0 else lo[lt] + a
            hi[lt] = b if k == 0 else hi[lt] + b
    ff = jnp.concatenate(lo + hi, axis=1)
    o_ref[...] = _layernorm(base_ref[...] + ff, g2_ref[...], b2_ref[...])


def _final(yg, gates, base, g2, b2):
    n = base.shape[0]
    full = lambda a: pl.BlockSpec(a.shape, lambda i: (0,) * a.ndim)
    return pl.pallas_call(
        _final_body,
        grid=(n // T_FIN,),
        in_specs=[pl.BlockSpec((TOP_K, T_FIN // SUBLANES, PACK_TILES, SUBLANES, LANES),
                               lambda i: (0, i, 0, 0, 0)),
                  pl.BlockSpec((T_FIN, TOP_K), lambda i: (i, 0)),
                  pl.BlockSpec((T_FIN, D_MODEL), lambda i: (i, 0)),
                  full(g2), full(b2)],
        out_specs=pl.BlockSpec((T_FIN, D_MODEL), lambda i: (i, 0)),
        out_shape=jax.ShapeDtypeStruct((n, D_MODEL), jnp.float32),
        compiler_params=pltpu.CompilerParams(dimension_semantics=("arbitrary",),
                                             vmem_limit_bytes=VMEM_LIMIT),
        name="final",
    )(yg, gates, base, g2, b2)


def _rope_tables(pos):
    half = QK_ROPE // 2
    inv = ROPE_THETA ** (-jnp.arange(0, QK_ROPE, 2, dtype=jnp.float32) / QK_ROPE)
    ang = pos.astype(jnp.float32)[:, None] * inv[None, :]
    cos, sin = jnp.cos(ang), jnp.sin(ang)
    n = pos.shape[0]
    z = lambda w: jnp.zeros((n, w), jnp.float32)
    c = jnp.concatenate([jnp.ones((n, QK_NOPE), jnp.float32), cos, cos, z(LANES - QK_NOPE - QK_ROPE)], axis=1)
    s_up = jnp.concatenate([z(QK_NOPE + half), sin, z(LANES - QK_NOPE - QK_ROPE)], axis=1)
    s_dn = jnp.concatenate([z(QK_NOPE), -sin, z(half), z(LANES - QK_NOPE - QK_ROPE)], axis=1)
    return c, s_up, s_dn


def _pad_heads(w, per_head, keep):
    rows = w.shape[0]
    w = w.reshape(rows, ATTN_HEADS, per_head)[:, :, :keep]
    w = jnp.pad(w, ((0, 0), (0, 0), (0, HEAD_PAD - keep)))
    return w.reshape(rows, ATTN_HEADS * HEAD_PAD)


def _prep_weights(emb_ln_g, emb_ln_b, w_in, q_norm, w_uq, kv_norm, w_ukv, conv_w, attn_out_norm,
                  conv_out_norm, w_o, ln1_g, ln1_b, w_router, router_bias, ws_gate, ws_up, ws_down):
    o1 = Q_LORA
    o2 = o1 + KV_LORA
    o3 = o2 + QK_ROPE
    row = lambda a: a.reshape(1, -1)
    kpe_cols = jnp.pad(w_in[:, o2:o3], ((0, 0), (QK_NOPE, HEAD_PAD - QK_NOPE - QK_ROPE)))
    w_in_p = jnp.concatenate([w_in[:, :o2], kpe_cols, w_in[:, o3:]], axis=1)
    ukv = w_ukv.reshape(KV_LORA, ATTN_HEADS, QK_NOPE + V_HEAD)
    w_uk = jnp.pad(ukv[:, :, :QK_NOPE], ((0, 0), (0, 0), (0, HEAD_PAD - QK_NOPE)))
    w_uv = jnp.pad(ukv[:, :, QK_NOPE:], ((0, 0), (0, 0), (0, HEAD_PAD - V_HEAD)))
    w_ukv_p = jnp.concatenate([w_uk.reshape(KV_LORA, -1), w_uv.reshape(KV_LORA, -1)], axis=1)
    woa = w_o[:D_ATTN]
    return {
        "emb_g": row(emb_ln_g), "emb_b": row(emb_ln_b),
        "w_in": _bf(w_in_p),
        "q_norm": row(q_norm), "w_uq": _bf(_pad_heads(w_uq, QK_NOPE + QK_ROPE, QK_NOPE + QK_ROPE)),
        "kv_norm": row(kv_norm), "w_ukv": _bf(w_ukv_p),
        "conv_w": jnp.pad(conv_w, ((0, SUBLANES - CONV_W), (0, 0))), "conv_norm": row(conv_out_norm),
        "attn_norm": row(attn_out_norm), "wo_a": _bf(woa), "wo_c": _bf(w_o[D_ATTN:]),
        "ln1_g": row(ln1_g), "ln1_b": row(ln1_b),
        "ws_gate": _bf(ws_gate), "ws_up": _bf(ws_up), "ws_down": _bf(ws_down),
        "w_router_t": _bf(w_router.T), "router_bias": router_bias.reshape(-1, 1),
    }


def _slot_body(eidx_ref, pos_ref, start_ref, dest_ref):
    t = eidx_ref.shape[2]
    shp = (N_GROUPS, GROUP_SIZE, t)
    eio = (lax.broadcasted_iota(jnp.int32, shp, 0) * GROUP_SIZE + lax.broadcasted_iota(jnp.int32, shp, 1))
    start = start_ref[...].reshape(N_GROUPS, GROUP_SIZE, 1)
    for k in range(TOP_K):
        hit = eio == eidx_ref[0, k:k + 1, :].reshape(1, 1, t)
        st = jnp.sum(jnp.sum(jnp.where(hit, start, 0.0), axis=0, keepdims=True), axis=1, keepdims=True)
        dest_ref[0, k:k + 1, :] = st.reshape(1, t).astype(jnp.int32) + pos_ref[0, k:k + 1, :]


def _slots(eidx, pos, start):
    nt, _, t = eidx.shape
    k_blk = pl.BlockSpec((1, TOP_K, t), lambda i: (i, 0, 0))
    return pl.pallas_call(
        _slot_body,
        grid=(nt,),
        in_specs=[k_blk, k_blk, pl.BlockSpec(start.shape, lambda i: (0, 0))],
        out_specs=k_blk,
        out_shape=jax.ShapeDtypeStruct(eidx.shape, jnp.int32),
        compiler_params=pltpu.CompilerParams(dimension_semantics=("arbitrary",)),
        name="slots",
    )(eidx, pos, start)


def _slot_plan(eidx, pos, counts):
    counts = counts.reshape(-1).astype(jnp.int32)
    padded = (counts + BLK - 1) // BLK * BLK
    pad_end = jnp.cumsum(padded)
    pad_start = pad_end - padded
    dest = _slots(eidx, pos, pad_start.astype(jnp.float32).reshape(-1, 1))
    nb = eidx.size // BLK + N_EXPERTS
    nused = (pad_end[-1:] // BLK).astype(jnp.int32)
    return dest, (pad_start // BLK).astype(jnp.int32), (padded // BLK).astype(jnp.int32), nused, nb * BLK


def _piece_lists(dest):
    nt, k, t = dest.shape
    d = dest.reshape(nt, k, t // SC_ROWS, SC_ROWS // SUBLANES, SUBLANES).transpose(0, 2, 1, 3, 4)
    base = (d // SUBLANES) * (PACK_TILES * SUBLANES) + d % SUBLANES
    lt = jnp.arange(PACK_TILES, dtype=jnp.int32) * SUBLANES
    pieces = base[:, :, :, :, None, :] + lt[:, None]
    return pieces.reshape(nt * (t // SC_ROWS), k * SC_LISTS, LANES)


def kernel(x, meta_tokens, emb_ln_g, emb_ln_b, w_in, q_norm, w_uq, kv_norm, w_ukv, conv_w, attn_out_norm, conv_out_norm, w_o, ln1_g, ln1_b, w_router, router_bias, w_gate, w_up, w_down, ws_gate, ws_up, ws_down, ln2_g, ln2_b):
    bsz, seq, _ = x.shape
    n = bsz * seq
    w = _prep_weights(emb_ln_g, emb_ln_b, w_in[0], q_norm[0], w_uq[0], kv_norm[0], w_ukv[0], conv_w[0],
                      attn_out_norm[0], conv_out_norm[0], w_o[0], ln1_g[0], ln1_b[0], w_router[0],
                      router_bias[0], ws_gate[0], ws_up[0], ws_down[0])

    meta_out = _inproj(meta_tokens[None], _rope_tables(jnp.arange(N_META)),
                       jnp.zeros((SUBLANES, D_CONV), jnp.float32), w, N_META)
    pad_rows = lambda a: jnp.pad(a, ((0, META_PAD - N_META), (0, 0)))
    km, vm, vc_meta = pad_rows(meta_out[2][0]), pad_rows(meta_out[3][0]), meta_out[5][0]

    h0, q, k, v, cn, _ = _inproj(x, _rope_tables(N_META + jnp.arange(seq)), vc_meta, w, T_IN)
    attn = _attention(q, k, v, km, vm)

    hp, base, eidx, gates, pos, counts = _mix_route(
        attn.reshape(n, D_ATTN), cn.reshape(n, D_CONV), h0.reshape(n, D_MODEL), w)
    dest, first_blk, n_blk, nused, n_slots = _slot_plan(eidx, pos, counts)
    idx = _piece_lists(dest)
    tiled = lambda a, lead: a.reshape(lead + (-1, PACK_TILES, SUBLANES, LANES))

    xs = _dispatch(hp.reshape(-1, LANES), idx, n_slots * PACK_TILES)
    ys = _experts(tiled(xs, ()), first_blk, n_blk, nused, w_gate[0], w_up[0], w_down[0])
    yg = tiled(_combine(ys.reshape(-1, LANES), idx, n * PACK_TILES), (TOP_K,))
    gates_t = gates.transpose(0, 2, 1).reshape(n, TOP_K)
    out = _final(yg, gates_t, base, ln2_g[0].reshape(1, -1), ln2_b[0].reshape(1, -1))
    return out.reshape(bsz, seq, D_MODEL)
```

```python
import functools
import math

import jax
import jax.numpy as jnp
from jax import lax
from jax.experimental import pallas as pl
from jax.experimental.pallas import tpu as pltpu
from jax.experimental.pallas import tpu_sc as plsc

D_MODEL = 1024
N_META = 16
ATTN_HEADS = 8
QK_NOPE = 64
QK_ROPE = 32
V_HEAD = 64
Q_LORA = 384
KV_LORA = 256
ROPE_THETA = 10000.0
D_ATTN = ATTN_HEADS * V_HEAD
D_CONV = 512
CONV_W = 3
ATTN_SCALE = 1.0 / math.sqrt(QK_NOPE + QK_ROPE)
N_EXPERTS = 256
TOP_K = 8
N_GROUPS = 8
GROUP_SIZE = N_EXPERTS // N_GROUPS
TOPK_GROUPS = 4
D_EXPERT = 256
D_SHARED = 256
ROUTED_SCALE = 2.5
DEPTH = 1
ALPHA = (2.0 * DEPTH) ** 0.25

LANES = 128
SUBLANES = 8
HEAD_PAD = LANES
META_PAD = LANES
HALF = D_MODEL // 2
PACK_TILES = HALF // LANES
HI_MASK = -65536
SC_CORES = 2
SC_SUBCORES = 16

T_IN = 512
T_Q = 512
T_MIX = 256
BLK = 256
ROW_SLOTS = 4
T_FIN = 256
SC_ROWS = 64
SC_PIECES = SC_ROWS * PACK_TILES
SC_LISTS = SC_PIECES // LANES

VMEM_LIMIT = 56 * 1024 * 1024

_NEG_INF = float("-inf")


def _bf(x):
    return x.astype(jnp.bfloat16)


def _dot(a, b):
    return jnp.dot(a, b, preferred_element_type=jnp.float32)


def _dot_t(a, b):
    return lax.dot_general(a, b, (((1,), (1,)), ((), ())), preferred_element_type=jnp.float32)


def _layernorm(x, g, b, eps=1e-5):
    mu = jnp.mean(x, axis=-1, keepdims=True)
    xc = x - mu
    var = jnp.mean(xc * xc, axis=-1, keepdims=True)
    return xc * lax.rsqrt(var + eps) * g + b


def _rms(x, g, n, eps=1e-6):
    ms = jnp.sum(x * x, axis=-1, keepdims=True) * (1.0 / n)
    return x * lax.rsqrt(ms + eps) * g


def _sigmoid(x):
    return 1.0 / (1.0 + jnp.exp(-x))


def _rope(x, c, s_up, s_dn):
    return x * c + pltpu.roll(x, QK_ROPE // 2, 1) * s_up + pltpu.roll(x, LANES - QK_ROPE // 2, 1) * s_dn


def _inproj_body(x_ref, g_ref, b_ref, win_ref, qn_ref, wuq_ref, kvn_ref, wukv_ref, cw_ref, con_ref,
                 tc_ref, tu_ref, td_ref, vcinit_ref,
                 h0_ref, q_ref, k_ref, v_ref, cn_ref, vct_ref, carry_ref):
    rows = x_ref.shape[1]

    @pl.when(pl.program_id(1) == 0)
    def _():
        carry_ref[...] = vcinit_ref[...]

    h0 = _layernorm(x_ref[0], g_ref[...], b_ref[...])
    h0_ref[0] = h0
    z = _dot(_bf(h0), win_ref[...])
    o1 = Q_LORA
    o2 = o1 + KV_LORA
    o3 = o2 + HEAD_PAD
    o4 = o3 + D_CONV
    o5 = o4 + D_CONV
    c, s_up, s_dn = tc_ref[...], tu_ref[...], td_ref[...]

    q = _dot(_bf(_rms(z[:, :o1], qn_ref[...], Q_LORA)), wuq_ref[...])
    for h in range(ATTN_HEADS):
        sl = slice(h * HEAD_PAD, (h + 1) * HEAD_PAD)
        q_ref[0, :, sl] = _bf(_rope(q[:, sl], c, s_up, s_dn) * ATTN_SCALE)

    kv = _dot(_bf(_rms(z[:, o1:o2], kvn_ref[...], KV_LORA)), wukv_ref[...])
    kpe = _rope(z[:, o2:o3], c, s_up, s_dn)
    lane = lax.broadcasted_iota(jnp.int32, (rows, HEAD_PAD), 1)
    for h in range(ATTN_HEADS):
        sl = slice(h * HEAD_PAD, (h + 1) * HEAD_PAD)
        k_ref[0, :, sl] = _bf(kv[:, sl] + kpe)
        vh = kv[:, ATTN_HEADS * HEAD_PAD + h * HEAD_PAD: ATTN_HEADS * HEAD_PAD + (h + 1) * HEAD_PAD]
        v_ref[0, :, sl] = _bf(jnp.where(lane < V_HEAD, vh, 1.0))

    vc = z[:, o4:o5] * z[:, o5:]
    carry = carry_ref[...]
    row = lax.broadcasted_iota(jnp.int32, (rows, D_CONV), 0)
    vc1 = jnp.where(row == 0, carry[SUBLANES - 1:SUBLANES, :], pltpu.roll(vc, 1, 0))
    vc2 = pltpu.roll(vc, 2, 0)
    vc2 = jnp.where(row == 0, carry[SUBLANES - 2:SUBLANES - 1, :], vc2)
    vc2 = jnp.where(row == 1, carry[SUBLANES - 1:SUBLANES, :], vc2)
    cw = cw_ref[...]
    conv = z[:, o3:o4] * (cw[0:1, :] * vc2 + cw[1:2, :] * vc1 + cw[2:3, :] * vc)
    cn_ref[0] = _bf(_rms(conv, con_ref[...], D_CONV))
    tail = vc[rows - SUBLANES:, :]
    carry_ref[...] = tail
    vct_ref[0] = tail


def _inproj(x, tabs, vcinit, w, rows):
    bsz, seq, _ = x.shape
    nj = seq // rows
    full = lambda a: pl.BlockSpec(a.shape, lambda b, j: (0,) * a.ndim)
    row_blk = lambda width: pl.BlockSpec((1, rows, width), lambda b, j: (b, j, 0))
    tab = pl.BlockSpec((rows, HEAD_PAD), lambda b, j: (j, 0))
    params = [w["emb_g"], w["emb_b"], w["w_in"], w["q_norm"], w["w_uq"], w["kv_norm"], w["w_ukv"],
              w["conv_w"], w["conv_norm"]]
    wide = ATTN_HEADS * HEAD_PAD
    out_shape = [
        jax.ShapeDtypeStruct((bsz, seq, D_MODEL), jnp.float32),
        jax.ShapeDtypeStruct((bsz, seq, wide), jnp.bfloat16),
        jax.ShapeDtypeStruct((bsz, seq, wide), jnp.bfloat16),
        jax.ShapeDtypeStruct((bsz, seq, wide), jnp.bfloat16),
        jax.ShapeDtypeStruct((bsz, seq, D_CONV), jnp.bfloat16),
        jax.ShapeDtypeStruct((bsz, nj * SUBLANES, D_CONV), jnp.float32),
    ]
    return pl.pallas_call(
        _inproj_body,
        grid=(bsz, nj),
        in_specs=[row_blk(D_MODEL)] + [full(p) for p in params] + [tab, tab, tab, full(vcinit)],
        out_specs=[row_blk(D_MODEL), row_blk(wide), row_blk(wide), row_blk(wide), row_blk(D_CONV),
                   pl.BlockSpec((1, SUBLANES, D_CONV), lambda b, j: (b, j, 0))],
        out_shape=out_shape,
        scratch_shapes=[pltpu.VMEM((SUBLANES, D_CONV), jnp.float32)],
        compiler_params=pltpu.CompilerParams(dimension_semantics=("arbitrary", "arbitrary"),
                                             vmem_limit_bytes=VMEM_LIMIT),
        name="inproj",
    )(x, *params, *tabs, vcinit)


def _attn_body(q_ref, k_ref, v_ref, km_ref, vm_ref, o_ref):
    i = pl.program_id(2)
    tq = q_ref.shape[1]
    heads = [slice(hh * HEAD_PAD, (hh + 1) * HEAD_PAD) for hh in range(2)]
    qs = [q_ref[0, :, sl] for sl in heads]

    causal = lax.broadcasted_iota(jnp.int32, (tq, tq), 1) <= lax.broadcasted_iota(jnp.int32, (tq, tq), 0)
    is_meta = lax.broadcasted_iota(jnp.int32, (tq, META_PAD), 1) < N_META
    r0 = pl.multiple_of(i * tq, tq)
    carry = []
    for q, sl in zip(qs, heads):
        sa = jnp.where(causal, _dot_t(q, k_ref[0, pl.ds(r0, tq), sl]), _NEG_INF)
        sb = jnp.where(is_meta, _dot_t(q, km_ref[:, sl]), _NEG_INF)
        m = jnp.maximum(jnp.max(sa, axis=1, keepdims=True), jnp.max(sb, axis=1, keepdims=True))
        acc = (_dot(_bf(jnp.exp(sa - m)), v_ref[0, pl.ds(r0, tq), sl])
               + _dot(_bf(jnp.exp(sb - m)), vm_ref[:, sl]))
        carry += [m, acc]

    def body(j, carry):
        rj = pl.multiple_of(j * tq, tq)
        out = []
        for h, (q, sl) in enumerate(zip(qs, heads)):
            m, acc = carry[2 * h], carry[2 * h + 1]
            s = _dot_t(q, k_ref[0, pl.ds(rj, tq), sl])
            m_new = jnp.maximum(m, jnp.max(s, axis=1, keepdims=True))
            p = jnp.exp(s - m_new)
            out += [m_new, jnp.exp(m - m_new) * acc + _dot(_bf(p), v_ref[0, pl.ds(rj, tq), sl])]
        return tuple(out)

    _, a0, _, a1 = lax.fori_loop(0, i, body, tuple(carry))
    lane = lax.broadcasted_iota(jnp.int32, (tq, HEAD_PAD), 1)
    o0 = a0 / pltpu.roll(a0, V_HEAD, 1)
    o1 = pltpu.roll(a1, V_HEAD, 1) / a1
    o_ref[0] = _bf(jnp.where(lane < V_HEAD, o0, o1))


def _attention(q, k, v, km, vm):
    bsz, seq, _ = q.shape
    pairs = ATTN_HEADS // 2
    pw = 2 * HEAD_PAD
    return pl.pallas_call(
        _attn_body,
        grid=(bsz, pairs, seq // T_Q),
        in_specs=[pl.BlockSpec((1, T_Q, pw), lambda b, p, i: (b, i, p)),
                  pl.BlockSpec((1, seq, pw), lambda b, p, i: (b, 0, p)),
                  pl.BlockSpec((1, seq, pw), lambda b, p, i: (b, 0, p)),
                  pl.BlockSpec((META_PAD, pw), lambda b, p, i: (0, p)),
                  pl.BlockSpec((META_PAD, pw), lambda b, p, i: (0, p))],
        out_specs=pl.BlockSpec((1, T_Q, 2 * V_HEAD), lambda b, p, i: (b, i, p)),
        out_shape=jax.ShapeDtypeStruct((bsz, seq, D_ATTN), jnp.bfloat16),
        compiler_params=pltpu.CompilerParams(dimension_semantics=("arbitrary",) * 3,
                                             vmem_limit_bytes=VMEM_LIMIT),
        name="attention",
    )(q, k, v, km, vm)


def _mix_body(attn_ref, cn_ref, h0_ref, an_ref, woa_ref, woc_ref, g1_ref, b1_ref,
              wsg_ref, wsu_ref, wsd_ref, wrt_ref, rb_ref, tri_ref,
              hp_ref, base_ref, eidx_ref, gate_ref, pos_ref, cnt_ref, run_ref):
    t = attn_ref.shape[0]

    @pl.when(pl.program_id(0) == 0)
    def _():
        run_ref[...] = jnp.zeros_like(run_ref)

    attn_n = _rms(attn_ref[...].astype(jnp.float32), an_ref[...], D_ATTN)
    mix = _dot(_bf(attn_n), woa_ref[...]) + _dot(cn_ref[...], woc_ref[...])
    h1 = _layernorm(ALPHA * h0_ref[...] + mix, g1_ref[...], b1_ref[...])
    _store_row_tiles(hp_ref, (), h1)
    h1b = _bf(h1)
    g = _dot(h1b, wsg_ref[...])
    u = _dot(h1b, wsu_ref[...])
    base_ref[...] = ALPHA * h1 + _dot(_bf(g * _sigmoid(g) * u), wsd_ref[...])

    shp = (N_GROUPS, GROUP_SIZE, t)
    sc = _sigmoid(_dot_t(wrt_ref[...], h1b)).reshape(shp)
    sel = sc + rb_ref[...].reshape(N_GROUPS, GROUP_SIZE, 1)
    gio = lax.broadcasted_iota(jnp.int32, shp, 0)
    rio = lax.broadcasted_iota(jnp.int32, shp, 1)
    eio = gio * GROUP_SIZE + rio

    m1 = jnp.max(sel, axis=1, keepdims=True)
    i1 = jnp.min(jnp.where(sel == m1, rio, GROUP_SIZE), axis=1, keepdims=True)
    m2 = jnp.max(jnp.where(rio == i1, _NEG_INF, sel), axis=1, keepdims=True)
    gs = m1 + m2
    gio1 = lax.broadcasted_iota(jnp.int32, (N_GROUPS, 1, t), 0)
    gmask = jnp.zeros((N_GROUPS, 1, t), jnp.bool_)
    for _ in range(TOPK_GROUPS):
        mg = jnp.max(gs, axis=0, keepdims=True)
        ig = jnp.min(jnp.where(gs == mg, gio1, N_GROUPS), axis=0, keepdims=True)
        hit = gio1 == ig
        gmask = jnp.logical_or(gmask, hit)
        gs = jnp.where(hit, _NEG_INF, gs)

    cand = jnp.where(gmask, sel, _NEG_INF)

    def red(x, op):
        return op(op(x, axis=0, keepdims=True), axis=1, keepdims=True)

    picked = jnp.zeros(shp, jnp.bool_)
    e_rows, g_rows = [], []
    for _ in range(TOP_K):
        mk = red(cand, jnp.max)
        ek = red(jnp.where(cand == mk, eio, N_EXPERTS), jnp.min)
        hit = eio == ek
        g_rows.append(red(jnp.where(hit, sc, 0.0), jnp.sum))
        e_rows.append(ek)
        picked = jnp.logical_or(picked, hit)
        cand = jnp.where(hit, _NEG_INF, cand)
    gsum = g_rows[0]
    for gk in g_rows[1:]:
        gsum = gsum + gk

    pm = jnp.where(picked, 1.0, 0.0).reshape(N_EXPERTS, t)
    rank = _dot(_bf(pm), tri_ref[...])
    slot = (run_ref[...] + rank).reshape(shp)
    run_ref[...] = run_ref[...] + jnp.sum(pm, axis=1, keepdims=True)
    cnt_ref[...] = run_ref[...]
    for k in range(TOP_K):
        eidx_ref[0, k:k + 1, :] = e_rows[k].reshape(1, t)
        gate_ref[0, k:k + 1, :] = (g_rows[k] / gsum * ROUTED_SCALE).reshape(1, t)
        pk = red(jnp.where(eio == e_rows[k], slot, 0.0), jnp.sum)
        pos_ref[0, k:k + 1, :] = pk.reshape(1, t).astype(jnp.int32)


def _mix_route(attn, cn, h0, w):
    n = attn.shape[0]
    nt = n // T_MIX
    tri = _bf(jnp.triu(jnp.ones((T_MIX, T_MIX), jnp.float32), 1))
    params = [w["attn_norm"], w["wo_a"], w["wo_c"], w["ln1_g"], w["ln1_b"], w["ws_gate"], w["ws_up"],
              w["ws_down"], w["w_router_t"], w["router_bias"], tri]
    full = lambda a: pl.BlockSpec(a.shape, lambda i: (0,) * a.ndim)
    row_blk = lambda width: pl.BlockSpec((T_MIX, width), lambda i: (i, 0))
    k_blk = pl.BlockSpec((1, TOP_K, T_MIX), lambda i: (i, 0, 0))
    out_shape = [
        jax.ShapeDtypeStruct((n // SUBLANES, PACK_TILES, SUBLANES, LANES), jnp.int32),
        jax.ShapeDtypeStruct((n, D_MODEL), jnp.float32),
        jax.ShapeDtypeStruct((nt, TOP_K, T_MIX), jnp.int32),
        jax.ShapeDtypeStruct((nt, TOP_K, T_MIX), jnp.float32),
        jax.ShapeDtypeStruct((nt, TOP_K, T_MIX), jnp.int32),
        jax.ShapeDtypeStruct((N_EXPERTS, 1), jnp.float32),
    ]
    return pl.pallas_call(
        _mix_body,
        grid=(nt,),
        in_specs=[row_blk(D_ATTN), row_blk(D_CONV), row_blk(D_MODEL)] + [full(p) for p in params],
        out_specs=[pl.BlockSpec((T_MIX // SUBLANES, PACK_TILES, SUBLANES, LANES), lambda i: (i, 0, 0, 0)),
                   row_blk(D_MODEL),
                   k_blk, k_blk, k_blk, pl.BlockSpec((N_EXPERTS, 1), lambda i: (0, 0))],
        out_shape=out_shape,
        scratch_shapes=[pltpu.VMEM((N_EXPERTS, 1), jnp.float32)],
        compiler_params=pltpu.CompilerParams(dimension_semantics=("arbitrary",),
                                             vmem_limit_bytes=VMEM_LIMIT),
        name="mix_route",
    )(attn, cn, h0, *params)


def _pack_row_tiles(x):
    bits = lambda a: lax.bitcast_convert_type(_bf(a).astype(jnp.float32), jnp.int32)
    out = []
    for lt in range(PACK_TILES):
        lo = bits(x[:, lt * LANES:(lt + 1) * LANES])
        hi = bits(x[:, HALF + lt * LANES:HALF + (lt + 1) * LANES])
        out.append(lax.shift_right_logical(lo, 16) | (hi & HI_MASK))
    return out


def _unpack_lo(w):
    return lax.bitcast_convert_type(lax.shift_left(w, 16), jnp.float32)


def _unpack_hi(w):
    return lax.bitcast_convert_type(w & HI_MASK, jnp.float32)


def _store_row_tiles(ref, lead, x):
    rows = x.shape[0]
    for lt, w in enumerate(_pack_row_tiles(x)):
        ref[lead + (slice(None), lt)] = w.reshape(rows // SUBLANES, SUBLANES, LANES)


def _load_row_tiles(ref, lead, rows):
    return [ref[lead + (slice(None), lt)].reshape(rows, LANES) for lt in range(PACK_TILES)]


def _sc_mesh():
    return plsc.VectorSubcoreMesh(core_axis_name="c", subcore_axis_name="s")


def _worker_id():
    return lax.axis_index("s") * SC_CORES + lax.axis_index("c")


def _dispatch(hp, idx, n_rows):
    per_worker = hp.shape[0] // SC_PIECES // (SC_CORES * SC_SUBCORES)
    lists = idx.shape[1]

    @functools.partial(
        pl.kernel, mesh=_sc_mesh(),
        out_type=jax.ShapeDtypeStruct((n_rows, LANES), jnp.int32),
        scratch_types=[pltpu.VMEM((lists, LANES), jnp.int32),
                       pltpu.VMEM((SC_PIECES, LANES), jnp.int32),
                       pltpu.SemaphoreType.DMA],
        name="dispatch")
    def run(hp_hbm, idx_hbm, xs_hbm, idx_v, rows_v, sem):
        first = _worker_id() * per_worker

        @pl.loop(0, per_worker)
        def _(c):
            chunk = first + c
            pltpu.sync_copy(idx_hbm.at[chunk], idx_v)
            pltpu.sync_copy(hp_hbm.at[pl.ds(chunk * SC_PIECES, SC_PIECES)], rows_v)
            copies = [pltpu.async_copy(rows_v.at[pl.ds((j % SC_LISTS) * LANES, LANES)],
                                       xs_hbm.at[idx_v.at[j]], sem) for j in range(lists)]
            for cp in copies:
                cp.wait()

    return run(hp, idx)


def _combine(ys, idx, n_rows):
    per_worker = n_rows // SC_PIECES // (SC_CORES * SC_SUBCORES)
    lists = idx.shape[1]

    @functools.partial(
        pl.kernel, mesh=_sc_mesh(),
        out_type=jax.ShapeDtypeStruct((TOP_K, n_rows, LANES), jnp.int32),
        scratch_types=[pltpu.VMEM((lists, LANES), jnp.int32),
                       pltpu.VMEM((2, SC_PIECES, LANES), jnp.int32),
                       pltpu.SemaphoreType.DMA, pltpu.SemaphoreType.DMA],
        name="combine")
    def run(ys_hbm, idx_hbm, out_hbm, idx_v, rows_v, sem0, sem1):
        first = _worker_id() * per_worker
        sems = (sem0, sem1)

        def gather(k):
            buf = rows_v.at[k % 2]
            return [pltpu.async_copy(ys_hbm.at[idx_v.at[k * SC_LISTS + g]],
                                     buf.at[pl.ds(g * LANES, LANES)], sems[k % 2]) for g in range(SC_LISTS)]

        @pl.loop(0, per_worker)
        def _(c):
            chunk = first + c
            pltpu.sync_copy(idx_hbm.at[chunk], idx_v)
            pending = gather(0)
            for k in range(TOP_K):
                for cp in pending:
                    cp.wait()
                if k + 1 < TOP_K:
                    pending = gather(k + 1)
                pltpu.sync_copy(rows_v.at[k % 2], out_hbm.at[k, pl.ds(chunk * SC_PIECES, SC_PIECES)])

    return run(ys, idx)


def _expert_body(first_ref, count_ref, nused_ref, xs_hbm, wg_ref, wu_ref, wd_ref, ys_hbm,
                 xbuf, ybuf, wgu_ref, wdn_ref, sem_in, sem_out):
    e = pl.program_id(0)
    nused = nused_ref[0]
    tiles = BLK // SUBLANES
    ahead = ROW_SLOTS - 1

    def fetch(b):
        slot = b % ROW_SLOTS
        return pltpu.make_async_copy(xs_hbm.at[pl.ds(b * tiles, tiles)], xbuf.at[slot], sem_in.at[slot])

    def writeback(b):
        slot = b % ROW_SLOTS
        return pltpu.make_async_copy(ybuf.at[slot], ys_hbm.at[pl.ds(b * tiles, tiles)], sem_out.at[slot])

    @pl.when(e == 0)
    def _():
        for b in range(ahead):
            @pl.when(b < nused)
            def _():
                fetch(b).start()

    @pl.when(count_ref[e] > 0)
    def _():
        wgu_ref[:, :D_EXPERT] = _bf(wg_ref[0])
        wgu_ref[:, D_EXPERT:] = _bf(wu_ref[0])
        wdn_ref[...] = _bf(wd_ref[0])

    def block(t, carry):
        b = first_ref[e] + t
        slot = b % ROW_SLOTS
        fetch(b).wait()

        @pl.when(b + ahead < nused)
        def _():
            fetch(b + ahead).start()

        words = _load_row_tiles(xbuf, (slot,), BLK)
        x = jnp.concatenate([_bf(_unpack_lo(w)) for w in words] + [_bf(_unpack_hi(w)) for w in words], axis=1)
        gu = _dot(x, wgu_ref[...])
        g, u = gu[:, :D_EXPERT], gu[:, D_EXPERT:]
        y = _dot(_bf(g * _sigmoid(g) * u), wdn_ref[...])

        @pl.when(b >= ROW_SLOTS)
        def _():
            writeback(b - ROW_SLOTS).wait()

        _store_row_tiles(ybuf, (slot,), y)
        writeback(b).start()
        return carry

    lax.fori_loop(0, count_ref[e], block, 0)

    @pl.when(e == pl.num_programs(0) - 1)
    def _():
        for back in range(1, ROW_SLOTS + 1):
            @pl.when(nused >= back)
            def _():
                writeback(nused - back).wait()


def _experts(xs, first_blk, n_blk, nused, w_gate, w_up, w_down):
    blk = (BLK // SUBLANES, PACK_TILES, SUBLANES, LANES)
    w_map = lambda e, fb, nb, nu: (e, 0, 0)
    grid_spec = pltpu.PrefetchScalarGridSpec(
        num_scalar_prefetch=3,
        grid=(N_EXPERTS,),
        in_specs=[pl.BlockSpec(memory_space=pl.ANY),
                  pl.BlockSpec((1, D_MODEL, D_EXPERT), w_map),
                  pl.BlockSpec((1, D_MODEL, D_EXPERT), w_map),
                  pl.BlockSpec((1, D_EXPERT, D_MODEL), w_map)],
        out_specs=pl.BlockSpec(memory_space=pl.ANY),
        scratch_shapes=[pltpu.VMEM((ROW_SLOTS,) + blk, jnp.int32),
                        pltpu.VMEM((ROW_SLOTS,) + blk, jnp.int32),
                        pltpu.VMEM((D_MODEL, 2 * D_EXPERT), jnp.bfloat16),
                        pltpu.VMEM((D_EXPERT, D_MODEL), jnp.bfloat16),
                        pltpu.SemaphoreType.DMA((ROW_SLOTS,)),
                        pltpu.SemaphoreType.DMA((ROW_SLOTS,))],
    )
    return pl.pallas_call(
        _expert_body,
        grid_spec=grid_spec,
        out_shape=jax.ShapeDtypeStruct(xs.shape, jnp.int32),
        compiler_params=pltpu.CompilerParams(dimension_semantics=("arbitrary",),
                                             vmem_limit_bytes=VMEM_LIMIT),
        name="experts",
    )(first_blk, n_blk, nused, xs, w_gate, w_up, w_down)


def _final_body(yg_ref, gate_ref, base_ref, g2_ref, b2_ref, o_ref):
    t = base_ref.shape[0]
    gate = gate_ref[...]
    lo = [None] * PACK_TILES
    hi = [None] * PACK_TILES
    for k in range(TOP_K):
        gk = gate[:, k:k + 1]
        for lt, w in enumerate(_load_row_tiles(yg_ref, (k,), t)):
            a, b = gk * _unpack_lo(w), gk * _unpack_hi(w)
            lo[lt] = a if k == 0 else lo[lt] + a
            hi[lt] = b if k == 0 else hi[lt] + b
    ff = jnp.concatenate(lo + hi, axis=1)
    o_ref[...] = _layernorm(base_ref[...] + ff, g2_ref[...], b2_ref[...])


def _final(yg, gates, base, g2, b2):
    n = base.shape[0]
    full = lambda a: pl.BlockSpec(a.shape, lambda i: (0,) * a.ndim)
    return pl.pallas_call(
        _final_body,
        grid=(n // T_FIN,),
        in_specs=[pl.BlockSpec((TOP_K, T_FIN // SUBLANES, PACK_TILES, SUBLANES, LANES),
                               lambda i: (0, i, 0, 0, 0)),
                  pl.BlockSpec((T_FIN, TOP_K), lambda i: (i, 0)),
                  pl.BlockSpec((T_FIN, D_MODEL), lambda i: (i, 0)),
                  full(g2), full(b2)],
        out_specs=pl.BlockSpec((T_FIN, D_MODEL), lambda i: (i, 0)),
        out_shape=jax.ShapeDtypeStruct((n, D_MODEL), jnp.float32),
        compiler_params=pltpu.CompilerParams(dimension_semantics=("arbitrary",),
                                             vmem_limit_bytes=VMEM_LIMIT),
        name="final",
    )(yg, gates, base, g2, b2)


def _rope_tables(pos):
    half = QK_ROPE // 2
    inv = ROPE_THETA ** (-jnp.arange(0, QK_ROPE, 2, dtype=jnp.float32) / QK_ROPE)
    ang = pos.astype(jnp.float32)[:, None] * inv[None, :]
    cos, sin = jnp.cos(ang), jnp.sin(ang)
    n = pos.shape[0]
    z = lambda w: jnp.zeros((n, w), jnp.float32)
    c = jnp.concatenate([jnp.ones((n, QK_NOPE), jnp.float32), cos, cos, z(LANES - QK_NOPE - QK_ROPE)], axis=1)
    s_up = jnp.concatenate([z(QK_NOPE + half), sin, z(LANES - QK_NOPE - QK_ROPE)], axis=1)
    s_dn = jnp.concatenate([z(QK_NOPE), -sin, z(half), z(LANES - QK_NOPE - QK_ROPE)], axis=1)
    return c, s_up, s_dn


def _pad_heads(w, per_head, keep):
    rows = w.shape[0]
    w = w.reshape(rows, ATTN_HEADS, per_head)[:, :, :keep]
    w = jnp.pad(w, ((0, 0), (0, 0), (0, HEAD_PAD - keep)))
    return w.reshape(rows, ATTN_HEADS * HEAD_PAD)


def _prep_weights(emb_ln_g, emb_ln_b, w_in, q_norm, w_uq, kv_norm, w_ukv, conv_w, attn_out_norm,
                  conv_out_norm, w_o, ln1_g, ln1_b, w_router, router_bias, ws_gate, ws_up, ws_down):
    o1 = Q_LORA
    o2 = o1 + KV_LORA
    o3 = o2 + QK_ROPE
    row = lambda a: a.reshape(1, -1)
    kpe_cols = jnp.pad(w_in[:, o2:o3], ((0, 0), (QK_NOPE, HEAD_PAD - QK_NOPE - QK_ROPE)))
    w_in_p = jnp.concatenate([w_in[:, :o2], kpe_cols, w_in[:, o3:]], axis=1)
    ukv = w_ukv.reshape(KV_LORA, ATTN_HEADS, QK_NOPE + V_HEAD)
    w_uk = jnp.pad(ukv[:, :, :QK_NOPE], ((0, 0), (0, 0), (0, HEAD_PAD - QK_NOPE)))
    w_uv = jnp.pad(ukv[:, :, QK_NOPE:], ((0, 0), (0, 0), (0, HEAD_PAD - V_HEAD)))
    w_ukv_p = jnp.concatenate([w_uk.reshape(KV_LORA, -1), w_uv.reshape(KV_LORA, -1)], axis=1)
    woa = w_o[:D_ATTN]
    return {
        "emb_g": row(emb_ln_g), "emb_b": row(emb_ln_b),
        "w_in": _bf(w_in_p),
        "q_norm": row(q_norm), "w_uq": _bf(_pad_heads(w_uq, QK_NOPE + QK_ROPE, QK_NOPE + QK_ROPE)),
        "kv_norm": row(kv_norm), "w_ukv": _bf(w_ukv_p),
        "conv_w": jnp.pad(conv_w, ((0, SUBLANES - CONV_W), (0, 0))), "conv_norm": row(conv_out_norm),
        "attn_norm": row(attn_out_norm), "wo_a": _bf(woa), "wo_c": _bf(w_o[D_ATTN:]),
        "ln1_g": row(ln1_g), "ln1_b": row(ln1_b),
        "ws_gate": _bf(ws_gate), "ws_up": _bf(ws_up), "ws_down": _bf(ws_down),
        "w_router_t": _bf(w_router.T), "router_bias": router_bias.reshape(-1, 1),
    }


def _slot_body(eidx_ref, pos_ref, start_ref, dest_ref):
    t = eidx_ref.shape[2]
    shp = (N_GROUPS, GROUP_SIZE, t)
    eio = (lax.broadcasted_iota(jnp.int32, shp, 0) * GROUP_SIZE + lax.broadcasted_iota(jnp.int32, shp, 1))
    start = start_ref[...].reshape(N_GROUPS, GROUP_SIZE, 1)
    for k in range(TOP_K):
        hit = eio == eidx_ref[0, k:k + 1, :].reshape(1, 1, t)
        st = jnp.sum(jnp.sum(jnp.where(hit, start, 0.0), axis=0, keepdims=True), axis=1, keepdims=True)
        dest_ref[0, k:k + 1, :] = st.reshape(1, t).astype(jnp.int32) + pos_ref[0, k:k + 1, :]


def _slots(eidx, pos, start):
    nt, _, t = eidx.shape
    k_blk = pl.BlockSpec((1, TOP_K, t), lambda i: (i, 0, 0))
    return pl.pallas_call(
        _slot_body,
        grid=(nt,),
        in_specs=[k_blk, k_blk, pl.BlockSpec(start.shape, lambda i: (0, 0))],
        out_specs=k_blk,
        out_shape=jax.ShapeDtypeStruct(eidx.shape, jnp.int32),
        compiler_params=pltpu.CompilerParams(dimension_semantics=("arbitrary",)),
        name="slots",
    )(eidx, pos, start)


def _slot_plan(eidx, pos, counts):
    counts = counts.reshape(-1).astype(jnp.int32)
    padded = (counts + BLK - 1) // BLK * BLK
    pad_end = jnp.cumsum(padded)
    pad_start = pad_end - padded
    dest = _slots(eidx, pos, pad_start.astype(jnp.float32).reshape(-1, 1))
    nb = eidx.size // BLK + N_EXPERTS
    nused = (pad_end[-1:] // BLK).astype(jnp.int32)
    return dest, (pad_start // BLK).astype(jnp.int32), (padded // BLK).astype(jnp.int32), nused, nb * BLK


def _piece_lists(dest):
    nt, k, t = dest.shape
    d = dest.reshape(nt, k, t // SC_ROWS, SC_ROWS // SUBLANES, SUBLANES).transpose(0, 2, 1, 3, 4)
    base = (d // SUBLANES) * (PACK_TILES * SUBLANES) + d % SUBLANES
    lt = jnp.arange(PACK_TILES, dtype=jnp.int32) * SUBLANES
    pieces = base[:, :, :, :, None, :] + lt[:, None]
    return pieces.reshape(nt * (t // SC_ROWS), k * SC_LISTS, LANES)


def kernel(x, meta_tokens, emb_ln_g, emb_ln_b, w_in, q_norm, w_uq, kv_norm, w_ukv, conv_w, attn_out_norm, conv_out_norm, w_o, ln1_g, ln1_b, w_router, router_bias, w_gate, w_up, w_down, ws_gate, ws_up, ws_down, ln2_g, ln2_b):
    bsz, seq, _ = x.shape
    n = bsz * seq
    w = _prep_weights(emb_ln_g, emb_ln_b, w_in[0], q_norm[0], w_uq[0], kv_norm[0], w_ukv[0], conv_w[0],
                      attn_out_norm[0], conv_out_norm[0], w_o[0], ln1_g[0], ln1_b[0], w_router[0],
                      router_bias[0], ws_gate[0], ws_up[0], ws_down[0])

    meta_out = _inproj(meta_tokens[None], _rope_tables(jnp.arange(N_META)),
                       jnp.zeros((SUBLANES, D_CONV), jnp.float32), w, N_META)
    pad_rows = lambda a: jnp.pad(a, ((0, META_PAD - N_META), (0, 0)))
    km, vm, vc_meta = pad_rows(meta_out[2][0]), pad_rows(meta_out[3][0]), meta_out[5][0]

    h0, q, k, v, cn, _ = _inproj(x, _rope_tables(N_META + jnp.arange(seq)), vc_meta, w, T_IN)
    attn = _attention(q, k, v, km, vm)

    hp, base, eidx, gates, pos, counts = _mix_route(
        attn.reshape(n, D_ATTN), cn.reshape(n, D_CONV), h0.reshape(n, D_MODEL), w)
    dest, first_blk, n_blk, nused, n_slots = _slot_plan(eidx, pos, counts)
    idx = _piece_lists(dest)
    tiled = lambda a, lead: a.reshape(lead + (-1, PACK_TILES, SUBLANES, LANES))

    xs = _dispatch(hp.reshape(-1, LANES), idx, n_slots * PACK_TILES)
    ys = _experts(tiled(xs, ()), first_blk, n_blk, nused, w_gate[0], w_up[0], w_down[0])
    yg = tiled(_combine(ys.reshape(-1, LANES), idx, n * PACK_TILES), (TOP_K,))
    gates_t = gates.transpose(0, 2, 1).reshape(n, TOP_K)
    out = _final(yg, gates_t, base, ln2_g[0].reshape(1, -1), ln2_b[0].reshape(1, -1))
    return out.reshape(bsz, seq, D_MODEL)
```

```python
import functools
import math

import jax
import jax.numpy as jnp
from jax import lax
from jax.experimental import pallas as pl
from jax.experimental.pallas import tpu as pltpu
from jax.experimental.pallas import tpu_sc as plsc

D_MODEL = 1024
N_META = 16
ATTN_HEADS = 8
QK_NOPE = 64
QK_ROPE = 32
V_HEAD = 64
Q_LORA = 384
KV_LORA = 256
ROPE_THETA = 10000.0
D_ATTN = ATTN_HEADS * V_HEAD
D_CONV = 512
CONV_W = 3
ATTN_SCALE = 1.0 / math.sqrt(QK_NOPE + QK_ROPE)
N_EXPERTS = 256
TOP_K = 8
N_GROUPS = 8
GROUP_SIZE = N_EXPERTS // N_GROUPS
TOPK_GROUPS = 4
D_EXPERT = 256
D_SHARED = 256
ROUTED_SCALE = 2.5
DEPTH = 1
ALPHA = (2.0 * DEPTH) ** 0.25

LANES = 128
SUBLANES = 8
HEAD_PAD = LANES
META_PAD = LANES
HALF = D_MODEL // 2
PACK_TILES = HALF // LANES
HI_MASK = -65536
SC_CORES = 2
SC_SUBCORES = 16

T_IN = 512
T_Q = 512
T_MIX = 256
BLK = 256
ROW_SLOTS = 4
W_SLOTS = 3
W_PARTS = 2
T_FIN = 256
OUT_CHUNKS = 4
SC_ROWS = 64
SC_PIECES = SC_ROWS * PACK_TILES
SC_LISTS = SC_PIECES // LANES

VMEM_LIMIT = 56 * 1024 * 1024

_NEG_INF = float("-inf")


def _bf(x):
    return x.astype(jnp.bfloat16)


def _dot(a, b):
    return jnp.dot(a, b, preferred_element_type=jnp.float32)


def _dot_t(a, b):
    return lax.dot_general(a, b, (((1,), (1,)), ((), ())), preferred_element_type=jnp.float32)


def _layernorm(x, g, b, eps=1e-5):
    mu = jnp.mean(x, axis=-1, keepdims=True)
    xc = x - mu
    var = jnp.mean(xc * xc, axis=-1, keepdims=True)
    return xc * lax.rsqrt(var + eps) * g + b


def _rms(x, g, n, eps=1e-6):
    ms = jnp.sum(x * x, axis=-1, keepdims=True) * (1.0 / n)
    return x * lax.rsqrt(ms + eps) * g


def _sigmoid(x):
    return 1.0 / (1.0 + jnp.exp(-x))


def _rope(x, c, s_up, s_dn):
    return x * c + pltpu.roll(x, QK_ROPE // 2, 1) * s_up + pltpu.roll(x, LANES - QK_ROPE // 2, 1) * s_dn


def _inproj_body(x_ref, g_ref, b_ref, win_ref, qn_ref, wuq_ref, kvn_ref, wukv_ref, cw_ref, con_ref,
                 tc_ref, tu_ref, td_ref, vcinit_ref,
                 h0_ref, q_ref, k_ref, v_ref, cn_ref, vct_ref, carry_ref):
    rows = x_ref.shape[1]

    @pl.when(pl.program_id(1) == 0)
    def _():
        carry_ref[...] = vcinit_ref[...]

    h0 = _layernorm(x_ref[0], g_ref[...], b_ref[...])
    h0_ref[0] = h0
    z = _dot(_bf(h0), win_ref[...])
    o1 = Q_LORA
    o2 = o1 + KV_LORA
    o3 = o2 + HEAD_PAD
    o4 = o3 + D_CONV
    o5 = o4 + D_CONV
    c, s_up, s_dn = tc_ref[...], tu_ref[...], td_ref[...]

    q = _dot(_bf(_rms(z[:, :o1], qn_ref[...], Q_LORA)), wuq_ref[...])
    for h in range(ATTN_HEADS):
        sl = slice(h * HEAD_PAD, (h + 1) * HEAD_PAD)
        q_ref[0, :, sl] = _bf(_rope(q[:, sl], c, s_up, s_dn) * ATTN_SCALE)

    kv = _dot(_bf(_rms(z[:, o1:o2], kvn_ref[...], KV_LORA)), wukv_ref[...])
    kpe = _rope(z[:, o2:o3], c, s_up, s_dn)
    lane = lax.broadcasted_iota(jnp.int32, (rows, HEAD_PAD), 1)
    for h in range(ATTN_HEADS):
        sl = slice(h * HEAD_PAD, (h + 1) * HEAD_PAD)
        k_ref[0, :, sl] = _bf(kv[:, sl] + kpe)
        vh = kv[:, ATTN_HEADS * HEAD_PAD + h * HEAD_PAD: ATTN_HEADS * HEAD_PAD + (h + 1) * HEAD_PAD]
        v_ref[0, :, sl] = _bf(jnp.where(lane < V_HEAD, vh, 1.0))

    vc = z[:, o4:o5] * z[:, o5:]
    carry = carry_ref[...]
    row = lax.broadcasted_iota(jnp.int32, (rows, D_CONV), 0)
    vc1 = jnp.where(row == 0, carry[SUBLANES - 1:SUBLANES, :], pltpu.roll(vc, 1, 0))
    vc2 = pltpu.roll(vc, 2, 0)
    vc2 = jnp.where(row == 0, carry[SUBLANES - 2:SUBLANES - 1, :], vc2)
    vc2 = jnp.where(row == 1, carry[SUBLANES - 1:SUBLANES, :], vc2)
    cw = cw_ref[...]
    conv = z[:, o3:o4] * (cw[0:1, :] * vc2 + cw[1:2, :] * vc1 + cw[2:3, :] * vc)
    cn_ref[0] = _bf(_rms(conv, con_ref[...], D_CONV))
    tail = vc[rows - SUBLANES:, :]
    carry_ref[...] = tail
    vct_ref[0] = tail


def _inproj(x, tabs, vcinit, w, rows):
    bsz, seq, _ = x.shape
    nj = seq // rows
    full = lambda a: pl.BlockSpec(a.shape, lambda b, j: (0,) * a.ndim)
    row_blk = lambda width: pl.BlockSpec((1, rows, width), lambda b, j: (b, j, 0))
    tab = pl.BlockSpec((rows, HEAD_PAD), lambda b, j: (j, 0))
    params = [w["emb_g"], w["emb_b"], w["w_in"], w["q_norm"], w["w_uq"], w["kv_norm"], w["w_ukv"],
              w["conv_w"], w["conv_norm"]]
    wide = ATTN_HEADS * HEAD_PAD
    out_shape = [
        jax.ShapeDtypeStruct((bsz, seq, D_MODEL), jnp.float32),
        jax.ShapeDtypeStruct((bsz, seq, wide), jnp.bfloat16),
        jax.ShapeDtypeStruct((bsz, seq, wide), jnp.bfloat16),
        jax.ShapeDtypeStruct((bsz, seq, wide), jnp.bfloat16),
        jax.ShapeDtypeStruct((bsz, seq, D_CONV), jnp.bfloat16),
        jax.ShapeDtypeStruct((bsz, nj * SUBLANES, D_CONV), jnp.float32),
    ]
    return pl.pallas_call(
        _inproj_body,
        grid=(bsz, nj),
        in_specs=[row_blk(D_MODEL)] + [full(p) for p in params] + [tab, tab, tab, full(vcinit)],
        out_specs=[row_blk(D_MODEL), row_blk(wide), row_blk(wide), row_blk(wide), row_blk(D_CONV),
                   pl.BlockSpec((1, SUBLANES, D_CONV), lambda b, j: (b, j, 0))],
        out_shape=out_shape,
        scratch_shapes=[pltpu.VMEM((SUBLANES, D_CONV), jnp.float32)],
        compiler_params=pltpu.CompilerParams(dimension_semantics=("arbitrary", "arbitrary"),
                                             vmem_limit_bytes=VMEM_LIMIT),
        name="inproj",
    )(x, *params, *tabs, vcinit)


def _attn_body(q_ref, k_ref, v_ref, km_ref, vm_ref, o_ref):
    i = pl.program_id(2)
    tq = q_ref.shape[1]
    heads = [slice(hh * HEAD_PAD, (hh + 1) * HEAD_PAD) for hh in range(2)]
    qs = [q_ref[0, :, sl] for sl in heads]

    causal = lax.broadcasted_iota(jnp.int32, (tq, tq), 1) <= lax.broadcasted_iota(jnp.int32, (tq, tq), 0)
    is_meta = lax.broadcasted_iota(jnp.int32, (tq, META_PAD), 1) < N_META
    r0 = pl.multiple_of(i * tq, tq)
    carry = []
    for q, sl in zip(qs, heads):
        sa = jnp.where(causal, _dot_t(q, k_ref[0, pl.ds(r0, tq), sl]), _NEG_INF)
        sb = jnp.where(is_meta, _dot_t(q, km_ref[:, sl]), _NEG_INF)
        m = jnp.maximum(jnp.max(sa, axis=1, keepdims=True), jnp.max(sb, axis=1, keepdims=True))
        acc = (_dot(_bf(jnp.exp(sa - m)), v_ref[0, pl.ds(r0, tq), sl])
               + _dot(_bf(jnp.exp(sb - m)), vm_ref[:, sl]))
        carry += [m, acc]

    def body(j, carry):
        rj = pl.multiple_of(j * tq, tq)
        out = []
        for h, (q, sl) in enumerate(zip(qs, heads)):
            m, acc = carry[2 * h], carry[2 * h + 1]
            s = _dot_t(q, k_ref[0, pl.ds(rj, tq), sl])
            m_new = jnp.maximum(m, jnp.max(s, axis=1, keepdims=True))
            p = jnp.exp(s - m_new)
            out += [m_new, jnp.exp(m - m_new) * acc + _dot(_bf(p), v_ref[0, pl.ds(rj, tq), sl])]
        return tuple(out)

    _, a0, _, a1 = lax.fori_loop(0, i, body, tuple(carry))
    lane = lax.broadcasted_iota(jnp.int32, (tq, HEAD_PAD), 1)
    o0 = a0 / pltpu.roll(a0, V_HEAD, 1)
    o1 = pltpu.roll(a1, V_HEAD, 1) / a1
    o_ref[0] = _bf(jnp.where(lane < V_HEAD, o0, o1))


def _attention(q, k, v, km, vm):
    bsz, seq, _ = q.shape
    pairs = ATTN_HEADS // 2
    pw = 2 * HEAD_PAD
    return pl.pallas_call(
        _attn_body,
        grid=(bsz, pairs, seq // T_Q),
        in_specs=[pl.BlockSpec((1, T_Q, pw), lambda b, p, i: (b, i, p)),
                  pl.BlockSpec((1, seq, pw), lambda b, p, i: (b, 0, p)),
                  pl.BlockSpec((1, seq, pw), lambda b, p, i: (b, 0, p)),
                  pl.BlockSpec((META_PAD, pw), lambda b, p, i: (0, p)),
                  pl.BlockSpec((META_PAD, pw), lambda b, p, i: (0, p))],
        out_specs=pl.BlockSpec((1, T_Q, 2 * V_HEAD), lambda b, p, i: (b, i, p)),
        out_shape=jax.ShapeDtypeStruct((bsz, seq, D_ATTN), jnp.bfloat16),
        compiler_params=pltpu.CompilerParams(dimension_semantics=("arbitrary",) * 3,
                                             vmem_limit_bytes=VMEM_LIMIT),
        name="attention",
    )(q, k, v, km, vm)


def _mix_body(attn_ref, cn_ref, h0_ref, an_ref, woa_ref, woc_ref, g1_ref, b1_ref,
              wsg_ref, wsu_ref, wsd_ref, wrt_ref, rb_ref, tri_ref,
              hp_ref, base_ref, eidx_ref, gate_ref, pos_ref, cnt_ref, run_ref):
    t = attn_ref.shape[0]

    @pl.when(pl.program_id(0) == 0)
    def _():
        run_ref[...] = jnp.zeros_like(run_ref)

    attn_n = _rms(attn_ref[...].astype(jnp.float32), an_ref[...], D_ATTN)
    mix = _dot(_bf(attn_n), woa_ref[...]) + _dot(cn_ref[...], woc_ref[...])
    h1 = _layernorm(ALPHA * h0_ref[...] + mix, g1_ref[...], b1_ref[...])
    _store_row_tiles(hp_ref, (), h1)
    h1b = _bf(h1)
    g = _dot(h1b, wsg_ref[...])
    u = _dot(h1b, wsu_ref[...])
    base_ref[...] = ALPHA * h1 + _dot(_bf(g * _sigmoid(g) * u), wsd_ref[...])

    shp = (N_GROUPS, GROUP_SIZE, t)
    sc = _sigmoid(_dot_t(wrt_ref[...], h1b)).reshape(shp)
    sel = sc + rb_ref[...].reshape(N_GROUPS, GROUP_SIZE, 1)
    gio = lax.broadcasted_iota(jnp.int32, shp, 0)
    rio = lax.broadcasted_iota(jnp.int32, shp, 1)
    eio = gio * GROUP_SIZE + rio

    m1 = jnp.max(sel, axis=1, keepdims=True)
    i1 = jnp.min(jnp.where(sel == m1, rio, GROUP_SIZE), axis=1, keepdims=True)
    m2 = jnp.max(jnp.where(rio == i1, _NEG_INF, sel), axis=1, keepdims=True)
    gs = m1 + m2
    gio1 = lax.broadcasted_iota(jnp.int32, (N_GROUPS, 1, t), 0)
    gmask = jnp.zeros((N_GROUPS, 1, t), jnp.bool_)
    for _ in range(TOPK_GROUPS):
        mg = jnp.max(gs, axis=0, keepdims=True)
        ig = jnp.min(jnp.where(gs == mg, gio1, N_GROUPS), axis=0, keepdims=True)
        hit = gio1 == ig
        gmask = jnp.logical_or(gmask, hit)
        gs = jnp.where(hit, _NEG_INF, gs)

    cand = jnp.where(gmask, sel, _NEG_INF)

    def red(x, op):
        return op(op(x, axis=0, keepdims=True), axis=1, keepdims=True)

    picked = jnp.zeros(shp, jnp.bool_)
    e_rows, g_rows = [], []
    for _ in range(TOP_K):
        mk = red(cand, jnp.max)
        ek = red(jnp.where(cand == mk, eio, N_EXPERTS), jnp.min)
        hit = eio == ek
        g_rows.append(red(jnp.where(hit, sc, 0.0), jnp.sum))
        e_rows.append(ek)
        picked = jnp.logical_or(picked, hit)
        cand = jnp.where(hit, _NEG_INF, cand)
    gsum = g_rows[0]
    for gk in g_rows[1:]:
        gsum = gsum + gk

    pm = jnp.where(picked, 1.0, 0.0).reshape(N_EXPERTS, t)
    rank = _dot(_bf(pm), tri_ref[...])
    slot = (run_ref[...] + rank).reshape(shp)
    run_ref[...] = run_ref[...] + jnp.sum(pm, axis=1, keepdims=True)
    cnt_ref[...] = run_ref[...]
    for k in range(TOP_K):
        eidx_ref[0, k:k + 1, :] = e_rows[k].reshape(1, t)
        gate_ref[0, k:k + 1, :] = (g_rows[k] / gsum * ROUTED_SCALE).reshape(1, t)
        pk = red(jnp.where(eio == e_rows[k], slot, 0.0), jnp.sum)
        pos_ref[0, k:k + 1, :] = pk.reshape(1, t).astype(jnp.int32)


def _mix_route(attn, cn, h0, w):
    n = attn.shape[0]
    nt = n // T_MIX
    tri = _bf(jnp.triu(jnp.ones((T_MIX, T_MIX), jnp.float32), 1))
    params = [w["attn_norm"], w["wo_a"], w["wo_c"], w["ln1_g"], w["ln1_b"], w["ws_gate"], w["ws_up"],
              w["ws_down"], w["w_router_t"], w["router_bias"], tri]
    full = lambda a: pl.BlockSpec(a.shape, lambda i: (0,) * a.ndim)
    row_blk = lambda width: pl.BlockSpec((T_MIX, width), lambda i: (i, 0))
    k_blk = pl.BlockSpec((1, TOP_K, T_MIX), lambda i: (i, 0, 0))
    out_shape = [
        jax.ShapeDtypeStruct((n // SUBLANES, PACK_TILES, SUBLANES, LANES), jnp.int32),
        jax.ShapeDtypeStruct((n, D_MODEL), jnp.float32),
        jax.ShapeDtypeStruct((nt, TOP_K, T_MIX), jnp.int32),
        jax.ShapeDtypeStruct((nt, TOP_K, T_MIX), jnp.float32),
        jax.ShapeDtypeStruct((nt, TOP_K, T_MIX), jnp.int32),
        jax.ShapeDtypeStruct((N_EXPERTS, 1), jnp.float32),
    ]
    return pl.pallas_call(
        _mix_body,
        grid=(nt,),
        in_specs=[row_blk(D_ATTN), row_blk(D_CONV), row_blk(D_MODEL)] + [full(p) for p in params],
        out_specs=[pl.BlockSpec((T_MIX // SUBLANES, PACK_TILES, SUBLANES, LANES), lambda i: (i, 0, 0, 0)),
                   row_blk(D_MODEL),
                   k_blk, k_blk, k_blk, pl.BlockSpec((N_EXPERTS, 1), lambda i: (0, 0))],
        out_shape=out_shape,
        scratch_shapes=[pltpu.VMEM((N_EXPERTS, 1), jnp.float32)],
        compiler_params=pltpu.CompilerParams(dimension_semantics=("arbitrary",),
                                             vmem_limit_bytes=VMEM_LIMIT),
        name="mix_route",
    )(attn, cn, h0, *params)


def _pack_row_tiles(x):
    bits = lambda a: lax.bitcast_convert_type(_bf(a).astype(jnp.float32), jnp.int32)
    out = []
    for lt in range(PACK_TILES):
        lo = bits(x[:, lt * LANES:(lt + 1) * LANES])
        hi = bits(x[:, HALF + lt * LANES:HALF + (lt + 1) * LANES])
        out.append(lax.shift_right_logical(lo, 16) | (hi & HI_MASK))
    return out


def _unpack_lo(w):
    return lax.bitcast_convert_type(lax.shift_left(w, 16), jnp.float32)


def _unpack_hi(w):
    return lax.bitcast_convert_type(w & HI_MASK, jnp.float32)


def _store_row_tiles(ref, lead, x):
    rows = x.shape[0]
    for lt, w in enumerate(_pack_row_tiles(x)):
        ref[lead + (slice(None), lt)] = w.reshape(rows // SUBLANES, SUBLANES, LANES)


def _load_row_tiles(ref, lead, rows):
    return [ref[lead + (slice(None), lt)].reshape(rows, LANES) for lt in range(PACK_TILES)]


def _sc_mesh():
    return plsc.VectorSubcoreMesh(core_axis_name="c", subcore_axis_name="s")


def _worker_id():
    return lax.axis_index("s") * SC_CORES + lax.axis_index("c")


def _dispatch(hp, idx, n_rows):
    per_worker = hp.shape[0] // SC_PIECES // (SC_CORES * SC_SUBCORES)
    lists = idx.shape[1]

    @functools.partial(
        pl.kernel, mesh=_sc_mesh(),
        out_type=jax.ShapeDtypeStruct((n_rows, LANES), jnp.int32),
        scratch_types=[pltpu.VMEM((lists, LANES), jnp.int32),
                       pltpu.VMEM((SC_PIECES, LANES), jnp.int32),
                       pltpu.SemaphoreType.DMA],
        name="dispatch")
    def run(hp_hbm, idx_hbm, xs_hbm, idx_v, rows_v, sem):
        first = _worker_id() * per_worker

        @pl.loop(0, per_worker)
        def _(c):
            chunk = first + c
            pltpu.sync_copy(idx_hbm.at[chunk], idx_v)
            pltpu.sync_copy(hp_hbm.at[pl.ds(chunk * SC_PIECES, SC_PIECES)], rows_v)
            copies = [pltpu.async_copy(rows_v.at[pl.ds((j // TOP_K) * LANES, LANES)],
                                       xs_hbm.at[idx_v.at[j]], sem) for j in range(lists)]
            for cp in copies:
                cp.wait()

    return run(hp, idx)


def _combine(ys, idx, n_rows):
    per_worker = n_rows // SC_PIECES // (SC_CORES * SC_SUBCORES)
    lists = idx.shape[1]

    @functools.partial(
        pl.kernel, mesh=_sc_mesh(),
        out_type=jax.ShapeDtypeStruct((TOP_K, n_rows, LANES), jnp.int32),
        scratch_types=[pltpu.VMEM((lists, LANES), jnp.int32),
                       pltpu.VMEM((2, SC_PIECES, LANES), jnp.int32),
                       pltpu.SemaphoreType.DMA, pltpu.SemaphoreType.DMA],
        name="combine")
    def run(ys_hbm, idx_hbm, out_hbm, idx_v, rows_v, sem0, sem1):
        first = _worker_id() * per_worker
        sems = (sem0, sem1)

        def gather(k):
            buf = rows_v.at[k % 2]
            return [pltpu.async_copy(ys_hbm.at[idx_v.at[g * TOP_K + k]],
                                     buf.at[pl.ds(g * LANES, LANES)], sems[k % 2]) for g in range(SC_LISTS)]

        @pl.loop(0, per_worker)
        def _(c):
            chunk = first + c
            pltpu.sync_copy(idx_hbm.at[chunk], idx_v)
            pending = gather(0)
            for k in range(TOP_K):
                for cp in pending:
                    cp.wait()
                if k + 1 < TOP_K:
                    pending = gather(k + 1)
                pltpu.sync_copy(rows_v.at[k % 2], out_hbm.at[k, pl.ds(chunk * SC_PIECES, SC_PIECES)])

    return run(ys, idx)


def _expert_body(first_ref, count_ref, nused_ref, xs_hbm, wg_hbm, wu_hbm, wd_hbm, ys_hbm,
                 xbuf, ybuf, wgbuf, wubuf, wdbuf, wgu_ref, wdn_ref, sem_in, sem_out, sem_w):
    e = pl.program_id(0)
    nused = nused_ref[0]
    tiles = BLK // SUBLANES
    ahead = ROW_SLOTS - 1

    def fetch(b):
        slot = b % ROW_SLOTS
        return pltpu.make_async_copy(xs_hbm.at[pl.ds(b * tiles, tiles)], xbuf.at[slot], sem_in.at[slot])

    def writeback(b):
        slot = b % ROW_SLOTS
        return pltpu.make_async_copy(ybuf.at[slot], ys_hbm.at[pl.ds(b * tiles, tiles)], sem_out.at[slot])

    def weights(ex):
        slot = ex % W_SLOTS
        copies = []
        for src, dst in ((wg_hbm, wgbuf), (wu_hbm, wubuf), (wd_hbm, wdbuf)):
            part = src.shape[1] // W_PARTS
            for p in range(W_PARTS):
                rows = pl.ds(p * part, part)
                copies.append(pltpu.make_async_copy(src.at[ex, rows], dst.at[slot, rows], sem_w.at[slot]))
        return copies

    @pl.when(e == 0)
    def _():
        for b in range(ahead):
            @pl.when(b < nused)
            def _():
                fetch(b).start()
        for ex in range(W_SLOTS - 1):
            for cp in weights(ex):
                cp.start()

    for cp in weights(e):
        cp.wait()

    @pl.when(e + W_SLOTS - 1 < pl.num_programs(0))
    def _():
        for cp in weights(e + W_SLOTS - 1):
            cp.start()

    @pl.when(count_ref[e] > 0)
    def _():
        slot = e % W_SLOTS
        wgu_ref[:, :D_EXPERT] = _bf(wgbuf[slot])
        wgu_ref[:, D_EXPERT:] = _bf(wubuf[slot])
        wdn_ref[...] = _bf(wdbuf[slot])

    def block(t, carry):
        b = first_ref[e] + t
        slot = b % ROW_SLOTS
        fetch(b).wait()

        @pl.when(b + ahead < nused)
        def _():
            fetch(b + ahead).start()

        words = _load_row_tiles(xbuf, (slot,), BLK)
        x = jnp.concatenate([_bf(_unpack_lo(w)) for w in words] + [_bf(_unpack_hi(w)) for w in words], axis=1)
        gu = _dot(x, wgu_ref[...])
        g, u = gu[:, :D_EXPERT], gu[:, D_EXPERT:]
        y = _dot(_bf(g * _sigmoid(g) * u), wdn_ref[...])

        @pl.when(b >= ROW_SLOTS)
        def _():
            writeback(b - ROW_SLOTS).wait()

        _store_row_tiles(ybuf, (slot,), y)
        writeback(b).start()
        return carry

    lax.fori_loop(0, count_ref[e], block, 0)

    @pl.when(e == pl.num_programs(0) - 1)
    def _():
        for back in range(1, ROW_SLOTS + 1):
            @pl.when(nused >= back)
            def _():
                writeback(nused - back).wait()


def _experts(xs, first_blk, n_blk, nused, w_gate, w_up, w_down):
    blk = (BLK // SUBLANES, PACK_TILES, SUBLANES, LANES)
    grid_spec = pltpu.PrefetchScalarGridSpec(
        num_scalar_prefetch=3,
        grid=(N_EXPERTS,),
        in_specs=[pl.BlockSpec(memory_space=pl.ANY)] * 4,
        out_specs=pl.BlockSpec(memory_space=pl.ANY),
        scratch_shapes=[pltpu.VMEM((ROW_SLOTS,) + blk, jnp.int32),
                        pltpu.VMEM((ROW_SLOTS,) + blk, jnp.int32),
                        pltpu.VMEM((W_SLOTS, D_MODEL, D_EXPERT), jnp.float32),
                        pltpu.VMEM((W_SLOTS, D_MODEL, D_EXPERT), jnp.float32),
                        pltpu.VMEM((W_SLOTS, D_EXPERT, D_MODEL), jnp.float32),
                        pltpu.VMEM((D_MODEL, 2 * D_EXPERT), jnp.bfloat16),
                        pltpu.VMEM((D_EXPERT, D_MODEL), jnp.bfloat16),
                        pltpu.SemaphoreType.DMA((ROW_SLOTS,)),
                        pltpu.SemaphoreType.DMA((ROW_SLOTS,)),
                        pltpu.SemaphoreType.DMA((W_SLOTS,))],
    )
    return pl.pallas_call(
        _expert_body,
        grid_spec=grid_spec,
        out_shape=jax.ShapeDtypeStruct(xs.shape, jnp.int32),
        compiler_params=pltpu.CompilerParams(dimension_semantics=("arbitrary",),
                                             vmem_limit_bytes=VMEM_LIMIT),
        name="experts",
    )(first_blk, n_blk, nused, xs, w_gate, w_up, w_down)


def _final_body(yg_ref, gate_ref, base_ref, g2_ref, b2_ref, *rest):
    o_ref = rest[-1]
    t = base_ref.shape[0]
    gate = gate_ref[...]
    lo = [None] * PACK_TILES
    hi = [None] * PACK_TILES
    for k in range(TOP_K):
        gk = gate[:, k:k + 1]
        for lt, w in enumerate(_load_row_tiles(yg_ref, (k,), t)):
            a, b = gk * _unpack_lo(w), gk * _unpack_hi(w)
            lo[lt] = a if k == 0 else lo[lt] + a
            hi[lt] = b if k == 0 else hi[lt] + b
    ff = jnp.concatenate(lo + hi, axis=1)
    o_ref[...] = _layernorm(base_ref[...] + ff, g2_ref[...], b2_ref[...])


def _final(yg, gates, base, g2, b2, chunk, prev_out):
    n = base.shape[0]
    steps = yg.shape[1] * SUBLANES // T_FIN
    off = chunk * steps
    full = lambda a: pl.BlockSpec(a.shape, lambda i: (0,) * a.ndim)
    in_specs = [pl.BlockSpec((TOP_K, T_FIN // SUBLANES, PACK_TILES, SUBLANES, LANES),
                             lambda i: (0, i, 0, 0, 0)),
                pl.BlockSpec((T_FIN, TOP_K), lambda i: (off + i, 0)),
                pl.BlockSpec((T_FIN, D_MODEL), lambda i: (off + i, 0)),
                full(g2), full(b2)]
    args = [yg, gates, base, g2, b2]
    aliases = {}
    if prev_out is not None:
        in_specs.append(pl.BlockSpec(memory_space=pl.ANY))
        aliases = {len(args): 0}
        args.append(prev_out)
    return pl.pallas_call(
        _final_body,
        grid=(steps,),
        in_specs=in_specs,
        out_specs=pl.BlockSpec((T_FIN, D_MODEL), lambda i: (off + i, 0)),
        out_shape=jax.ShapeDtypeStruct((n, D_MODEL), jnp.float32),
        input_output_aliases=aliases,
        compiler_params=pltpu.CompilerParams(dimension_semantics=("arbitrary",),
                                             vmem_limit_bytes=VMEM_LIMIT),
        name="final",
    )(*args)


def _rope_tables(pos):
    half = QK_ROPE // 2
    inv = ROPE_THETA ** (-jnp.arange(0, QK_ROPE, 2, dtype=jnp.float32) / QK_ROPE)
    ang = pos.astype(jnp.float32)[:, None] * inv[None, :]
    cos, sin = jnp.cos(ang), jnp.sin(ang)
    n = pos.shape[0]
    z = lambda w: jnp.zeros((n, w), jnp.float32)
    c = jnp.concatenate([jnp.ones((n, QK_NOPE), jnp.float32), cos, cos, z(LANES - QK_NOPE - QK_ROPE)], axis=1)
    s_up = jnp.concatenate([z(QK_NOPE + half), sin, z(LANES - QK_NOPE - QK_ROPE)], axis=1)
    s_dn = jnp.concatenate([z(QK_NOPE), -sin, z(half), z(LANES - QK_NOPE - QK_ROPE)], axis=1)
    return c, s_up, s_dn


def _pad_heads(w, per_head, keep):
    rows = w.shape[0]
    w = w.reshape(rows, ATTN_HEADS, per_head)[:, :, :keep]
    w = jnp.pad(w, ((0, 0), (0, 0), (0, HEAD_PAD - keep)))
    return w.reshape(rows, ATTN_HEADS * HEAD_PAD)


def _prep_weights(emb_ln_g, emb_ln_b, w_in, q_norm, w_uq, kv_norm, w_ukv, conv_w, attn_out_norm,
                  conv_out_norm, w_o, ln1_g, ln1_b, w_router, router_bias, ws_gate, ws_up, ws_down):
    o1 = Q_LORA
    o2 = o1 + KV_LORA
    o3 = o2 + QK_ROPE
    row = lambda a: a.reshape(1, -1)
    kpe_cols = jnp.pad(w_in[:, o2:o3], ((0, 0), (QK_NOPE, HEAD_PAD - QK_NOPE - QK_ROPE)))
    w_in_p = jnp.concatenate([w_in[:, :o2], kpe_cols, w_in[:, o3:]], axis=1)
    ukv = w_ukv.reshape(KV_LORA, ATTN_HEADS, QK_NOPE + V_HEAD)
    w_uk = jnp.pad(ukv[:, :, :QK_NOPE], ((0, 0), (0, 0), (0, HEAD_PAD - QK_NOPE)))
    w_uv = jnp.pad(ukv[:, :, QK_NOPE:], ((0, 0), (0, 0), (0, HEAD_PAD - V_HEAD)))
    w_ukv_p = jnp.concatenate([w_uk.reshape(KV_LORA, -1), w_uv.reshape(KV_LORA, -1)], axis=1)
    woa = w_o[:D_ATTN]
    return {
        "emb_g": row(emb_ln_g), "emb_b": row(emb_ln_b),
        "w_in": _bf(w_in_p),
        "q_norm": row(q_norm), "w_uq": _bf(_pad_heads(w_uq, QK_NOPE + QK_ROPE, QK_NOPE + QK_ROPE)),
        "kv_norm": row(kv_norm), "w_ukv": _bf(w_ukv_p),
        "conv_w": jnp.pad(conv_w, ((0, SUBLANES - CONV_W), (0, 0))), "conv_norm": row(conv_out_norm),
        "attn_norm": row(attn_out_norm), "wo_a": _bf(woa), "wo_c": _bf(w_o[D_ATTN:]),
        "ln1_g": row(ln1_g), "ln1_b": row(ln1_b),
        "ws_gate": _bf(ws_gate), "ws_up": _bf(ws_up), "ws_down": _bf(ws_down),
        "w_router_t": _bf(w_router.T), "router_bias": router_bias.reshape(-1, 1),
    }


def _slot_body(eidx_ref, pos_ref, start_ref, idx_ref, base_ref):
    t = eidx_ref.shape[2]
    shp = (N_GROUPS, GROUP_SIZE, t)
    eio = (lax.broadcasted_iota(jnp.int32, shp, 0) * GROUP_SIZE + lax.broadcasted_iota(jnp.int32, shp, 1))
    start = start_ref[...].reshape(N_GROUPS, GROUP_SIZE, 1)
    for k in range(TOP_K):
        hit = eio == eidx_ref[0, k:k + 1, :].reshape(1, 1, t)
        st = jnp.sum(jnp.sum(jnp.where(hit, start, 0.0), axis=0, keepdims=True), axis=1, keepdims=True)
        d = st.reshape(1, t).astype(jnp.int32) + pos_ref[0, k:k + 1, :]
        base_ref[k:k + 1, :] = (d // SUBLANES) * (PACK_TILES * SUBLANES) + d % SUBLANES

    j = lax.broadcasted_iota(jnp.int32, (TOP_K, LANES), 1)
    per_list = LANES // (PACK_TILES * SUBLANES)
    lane_tile = (j // SUBLANES) % PACK_TILES * SUBLANES
    for c in range(t // SC_ROWS):
        lane0 = c * SC_ROWS
        src = base_ref[:, lane0 // LANES * LANES:(lane0 // LANES + 1) * LANES]
        for g in range(SC_LISTS):
            tok = lane0 % LANES + (g * per_list + j // (PACK_TILES * SUBLANES)) * SUBLANES + j % SUBLANES
            idx_ref[c, g * TOP_K:(g + 1) * TOP_K, :] = jnp.take_along_axis(src, tok, axis=1) + lane_tile


def _slots(eidx, pos, start):
    nt, _, t = eidx.shape
    per_tile = t // SC_ROWS
    k_blk = pl.BlockSpec((1, TOP_K, t), lambda i: (i, 0, 0))
    return pl.pallas_call(
        _slot_body,
        grid=(nt,),
        in_specs=[k_blk, k_blk, pl.BlockSpec(start.shape, lambda i: (0, 0))],
        out_specs=pl.BlockSpec((per_tile, SC_LISTS * TOP_K, LANES), lambda i: (i, 0, 0)),
        out_shape=jax.ShapeDtypeStruct((nt * per_tile, SC_LISTS * TOP_K, LANES), jnp.int32),
        scratch_shapes=[pltpu.VMEM((TOP_K, t), jnp.int32)],
        compiler_params=pltpu.CompilerParams(dimension_semantics=("arbitrary",)),
        name="slots",
    )(eidx, pos, start)


def _slot_plan(eidx, pos, counts):
    counts = counts.reshape(-1).astype(jnp.int32)
    padded = (counts + BLK - 1) // BLK * BLK
    pad_end = jnp.cumsum(padded)
    pad_start = pad_end - padded
    idx = _slots(eidx, pos, pad_start.astype(jnp.float32).reshape(-1, 1))
    nb = eidx.size // BLK + N_EXPERTS
    nused = (pad_end[-1:] // BLK).astype(jnp.int32)
    return idx, (pad_start // BLK).astype(jnp.int32), (padded // BLK).astype(jnp.int32), nused, nb * BLK


def kernel(x, meta_tokens, emb_ln_g, emb_ln_b, w_in, q_norm, w_uq, kv_norm, w_ukv, conv_w, attn_out_norm, conv_out_norm, w_o, ln1_g, ln1_b, w_router, router_bias, w_gate, w_up, w_down, ws_gate, ws_up, ws_down, ln2_g, ln2_b):
    bsz, seq, _ = x.shape
    n = bsz * seq
    w = _prep_weights(emb_ln_g, emb_ln_b, w_in[0], q_norm[0], w_uq[0], kv_norm[0], w_ukv[0], conv_w[0],
                      attn_out_norm[0], conv_out_norm[0], w_o[0], ln1_g[0], ln1_b[0], w_router[0],
                      router_bias[0], ws_gate[0], ws_up[0], ws_down[0])

    meta_out = _inproj(meta_tokens[None], _rope_tables(jnp.arange(N_META)),
                       jnp.zeros((SUBLANES, D_CONV), jnp.float32), w, N_META)
    pad_rows = lambda a: jnp.pad(a, ((0, META_PAD - N_META), (0, 0)))
    km, vm, vc_meta = pad_rows(meta_out[2][0]), pad_rows(meta_out[3][0]), meta_out[5][0]

    h0, q, k, v, cn, _ = _inproj(x, _rope_tables(N_META + jnp.arange(seq)), vc_meta, w, T_IN)
    attn = _attention(q, k, v, km, vm)

    hp, base, eidx, gates, pos, counts = _mix_route(
        attn.reshape(n, D_ATTN), cn.reshape(n, D_CONV), h0.reshape(n, D_MODEL), w)
    idx, first_blk, n_blk, nused, n_slots = _slot_plan(eidx, pos, counts)
    tiled = lambda a, lead: a.reshape(lead + (-1, PACK_TILES, SUBLANES, LANES))

    xs = _dispatch(hp.reshape(-1, LANES), idx, n_slots * PACK_TILES)
    ys = _experts(tiled(xs, ()), first_blk, n_blk, nused, w_gate[0], w_up[0], w_down[0])
    gates_t = gates.transpose(0, 2, 1).reshape(n, TOP_K)
    g2, b2 = ln2_g[0].reshape(1, -1), ln2_b[0].reshape(1, -1)
    ys_flat = ys.reshape(-1, LANES)
    per = idx.shape[0] // OUT_CHUNKS
    out = None
    for c in range(OUT_CHUNKS):
        yg = tiled(_combine(ys_flat, idx[c * per:(c + 1) * per], n // OUT_CHUNKS * PACK_TILES), (TOP_K,))
        out = _final(yg, gates_t, base, g2, b2, c, out)
    return out.reshape(bsz, seq, D_MODEL)
```

```python
import functools
import math

import jax
import jax.numpy as jnp
from jax import lax
from jax.experimental import pallas as pl
from jax.experimental.pallas import tpu as pltpu
from jax.experimental.pallas import tpu_sc as plsc

D_MODEL = 1024
N_META = 16
ATTN_HEADS = 8
QK_NOPE = 64
QK_ROPE = 32
V_HEAD = 64
Q_LORA = 384
KV_LORA = 256
ROPE_THETA = 10000.0
D_ATTN = ATTN_HEADS * V_HEAD
D_CONV = 512
CONV_W = 3
ATTN_SCALE = 1.0 / math.sqrt(QK_NOPE + QK_ROPE)
N_EXPERTS = 256
TOP_K = 8
N_GROUPS = 8
GROUP_SIZE = N_EXPERTS // N_GROUPS
TOPK_GROUPS = 4
D_EXPERT = 256
D_SHARED = 256
ROUTED_SCALE = 2.5
DEPTH = 1
ALPHA = (2.0 * DEPTH) ** 0.25

LANES = 128
SUBLANES = 8
HEAD_PAD = LANES
META_PAD = LANES
HALF = D_MODEL // 2
PACK_TILES = HALF // LANES
HI_MASK = -65536
SC_CORES = 2
SC_SUBCORES = 16

T_IN = 512
T_Q = 512
SHIFT_MARGIN = 40.0
T_MIX = 256
BLK = 256
ROW_SLOTS = 4
W_SLOTS = 3
W_PARTS = 2
T_FIN = 256
OUT_CHUNKS = 4
SC_ROWS = 64
SC_PIECES = SC_ROWS * PACK_TILES
SC_LISTS = SC_PIECES // LANES

VMEM_LIMIT = 56 * 1024 * 1024

_NEG_INF = float("-inf")


def _bf(x):
    return x.astype(jnp.bfloat16)


def _dot(a, b):
    return jnp.dot(a, b, preferred_element_type=jnp.float32)


def _dot_t(a, b):
    return lax.dot_general(a, b, (((1,), (1,)), ((), ())), preferred_element_type=jnp.float32)


def _layernorm(x, g, b, eps=1e-5):
    mu = jnp.mean(x, axis=-1, keepdims=True)
    xc = x - mu
    var = jnp.mean(xc * xc, axis=-1, keepdims=True)
    return xc * lax.rsqrt(var + eps) * g + b


def _rms(x, g, n, eps=1e-6):
    ms = jnp.sum(x * x, axis=-1, keepdims=True) * (1.0 / n)
    return x * lax.rsqrt(ms + eps) * g


def _sigmoid(x):
    return 1.0 / (1.0 + jnp.exp(-x))


def _rope(x, c, s_up, s_dn):
    return x * c + pltpu.roll(x, QK_ROPE // 2, 1) * s_up + pltpu.roll(x, LANES - QK_ROPE // 2, 1) * s_dn


def _rope_t(x, c, s_up, s_dn):
    return x * c + pltpu.roll(x, QK_ROPE // 2, 0) * s_up + pltpu.roll(x, LANES - QK_ROPE // 2, 0) * s_dn


def _inproj_body(x_ref, g_ref, b_ref, win_ref, qn_ref, wuq_ref, kvn_ref, wuk_ref, wuv_ref, cw_ref, con_ref,
                 tc_ref, tu_ref, td_ref, tct_ref, tut_ref, tdt_ref, vcinit_ref,
                 h0_ref, q_ref, k_ref, v_ref, cn_ref, vct_ref, carry_ref):
    rows = x_ref.shape[1]

    @pl.when(pl.program_id(1) == 0)
    def _():
        carry_ref[...] = vcinit_ref[...]

    h0 = _layernorm(x_ref[0], g_ref[...], b_ref[...])
    h0_ref[0] = h0
    z = _dot(_bf(h0), win_ref[...])
    o1 = Q_LORA
    o2 = o1 + KV_LORA
    o3 = o2 + HEAD_PAD
    o4 = o3 + D_CONV
    o5 = o4 + D_CONV
    c, s_up, s_dn = tc_ref[...], tu_ref[...], td_ref[...]

    ct, st_up, st_dn = tct_ref[...], tut_ref[...], tdt_ref[...]
    q_t = _dot_t(wuq_ref[...], _bf(_rms(z[:, :o1], qn_ref[...], Q_LORA)))
    for h in range(ATTN_HEADS):
        sl = slice(h * HEAD_PAD, (h + 1) * HEAD_PAD)
        q_ref[0, sl, :] = _bf(_rope_t(q_t[sl, :], ct, st_up, st_dn) * ATTN_SCALE)

    ckv = _bf(_rms(z[:, o1:o2], kvn_ref[...], KV_LORA))
    kk = _dot(ckv, wuk_ref[...])
    kpe = _rope(z[:, o2:o3], c, s_up, s_dn)
    for h in range(ATTN_HEADS):
        sl = slice(h * HEAD_PAD, (h + 1) * HEAD_PAD)
        k_ref[0, :, sl] = _bf(kk[:, sl] + kpe)
    v_t = _dot_t(wuv_ref[...], ckv)
    dim = lax.broadcasted_iota(jnp.int32, (HEAD_PAD, rows), 0)
    for h in range(ATTN_HEADS):
        sl = slice(h * HEAD_PAD, (h + 1) * HEAD_PAD)
        v_ref[0, sl, :] = _bf(jnp.where(dim < V_HEAD, v_t[sl, :], 1.0))

    vc = z[:, o4:o5] * z[:, o5:]
    carry = carry_ref[...]
    row = lax.broadcasted_iota(jnp.int32, (rows, D_CONV), 0)
    vc1 = jnp.where(row == 0, carry[SUBLANES - 1:SUBLANES, :], pltpu.roll(vc, 1, 0))
    vc2 = pltpu.roll(vc, 2, 0)
    vc2 = jnp.where(row == 0, carry[SUBLANES - 2:SUBLANES - 1, :], vc2)
    vc2 = jnp.where(row == 1, carry[SUBLANES - 1:SUBLANES, :], vc2)
    cw = cw_ref[...]
    conv = z[:, o3:o4] * (cw[0:1, :] * vc2 + cw[1:2, :] * vc1 + cw[2:3, :] * vc)
    cn_ref[0] = _bf(_rms(conv, con_ref[...], D_CONV))
    tail = vc[rows - SUBLANES:, :]
    carry_ref[...] = tail
    vct_ref[0] = tail


def _inproj(x, tabs, vcinit, w, rows):
    bsz, seq, _ = x.shape
    nj = seq // rows
    full = lambda a: pl.BlockSpec(a.shape, lambda b, j: (0,) * a.ndim)
    row_blk = lambda width: pl.BlockSpec((1, rows, width), lambda b, j: (b, j, 0))
    tab = pl.BlockSpec((rows, HEAD_PAD), lambda b, j: (j, 0))
    tab_t = pl.BlockSpec((HEAD_PAD, rows), lambda b, j: (0, j))
    col_blk = pl.BlockSpec((1, ATTN_HEADS * HEAD_PAD, rows), lambda b, j: (b, 0, j))
    params = [w["emb_g"], w["emb_b"], w["w_in"], w["q_norm"], w["w_uq_t"], w["kv_norm"], w["w_uk"], w["w_uv_t"],
              w["conv_w"], w["conv_norm"]]
    wide = ATTN_HEADS * HEAD_PAD
    out_shape = [
        jax.ShapeDtypeStruct((bsz, seq, D_MODEL), jnp.float32),
        jax.ShapeDtypeStruct((bsz, wide, seq), jnp.bfloat16),
        jax.ShapeDtypeStruct((bsz, seq, wide), jnp.bfloat16),
        jax.ShapeDtypeStruct((bsz, wide, seq), jnp.bfloat16),
        jax.ShapeDtypeStruct((bsz, seq, D_CONV), jnp.bfloat16),
        jax.ShapeDtypeStruct((bsz, nj * SUBLANES, D_CONV), jnp.float32),
    ]
    tabs_t = [t.T for t in tabs]
    return pl.pallas_call(
        _inproj_body,
        grid=(bsz, nj),
        in_specs=([row_blk(D_MODEL)] + [full(p) for p in params] + [tab, tab, tab, tab_t, tab_t, tab_t]
                  + [full(vcinit)]),
        out_specs=[row_blk(D_MODEL), col_blk, row_blk(wide), col_blk, row_blk(D_CONV),
                   pl.BlockSpec((1, SUBLANES, D_CONV), lambda b, j: (b, j, 0))],
        out_shape=out_shape,
        scratch_shapes=[pltpu.VMEM((SUBLANES, D_CONV), jnp.float32)],
        compiler_params=pltpu.CompilerParams(dimension_semantics=("arbitrary", "arbitrary"),
                                             vmem_limit_bytes=VMEM_LIMIT),
        name="inproj",
    )(x, *params, *tabs, *tabs_t, vcinit)


def _attn_body(q_ref, k_ref, v_ref, km_ref, vm_ref, o_ref):
    i = pl.program_id(2)
    tq = q_ref.shape[2]
    heads = [slice(hh * HEAD_PAD, (hh + 1) * HEAD_PAD) for hh in range(2)]
    qs = [q_ref[0, sl, :] for sl in heads]

    causal = lax.broadcasted_iota(jnp.int32, (tq, tq), 0) <= lax.broadcasted_iota(jnp.int32, (tq, tq), 1)
    is_meta = lax.broadcasted_iota(jnp.int32, (META_PAD, tq), 0) < N_META
    r0 = pl.multiple_of(i * tq, tq)
    carry = []
    for q, sl in zip(qs, heads):
        sa = jnp.where(causal, _dot(k_ref[0, pl.ds(r0, tq), sl], q), _NEG_INF)
        sb = jnp.where(is_meta, _dot(km_ref[:, sl], q), _NEG_INF)
        m = jnp.maximum(jnp.max(sa, axis=0, keepdims=True), jnp.max(sb, axis=0, keepdims=True))
        acc = (_dot(v_ref[0, sl, pl.ds(r0, tq)], _bf(jnp.exp(sa - m)))
               + _dot(vm_ref[sl, :], _bf(jnp.exp(sb - m))))
        carry += [m, acc]

    def body(j, carry):
        rj = pl.multiple_of(j * tq, tq)
        ms, accs = (carry[0], carry[2]), (carry[1], carry[3])
        fast, tops = [], []
        for q, sl, m, acc in zip(qs, heads, ms, accs):
            s = _dot(k_ref[0, pl.ds(rj, tq), sl], q)
            tops.append(jnp.max(s, axis=0, keepdims=True))
            fast.append(acc + _dot(v_ref[0, sl, pl.ds(rj, tq)], _bf(jnp.exp(s - m))))
        excess = jnp.maximum(tops[0] - ms[0], tops[1] - ms[1])
        over = jnp.max(excess, axis=1, keepdims=True)[0, 0] > SHIFT_MARGIN

        def rescale(_):
            out = []
            for q, sl, m, acc, top in zip(qs, heads, ms, accs, tops):
                s = _dot(k_ref[0, pl.ds(rj, tq), sl], q)
                m_new = jnp.maximum(m, top)
                out += [m_new, jnp.exp(m - m_new) * acc + _dot(v_ref[0, sl, pl.ds(rj, tq)], _bf(jnp.exp(s - m_new)))]
            return tuple(out)

        return lax.cond(over, rescale, lambda _: (ms[0], fast[0], ms[1], fast[1]), 0)

    _, a0, _, a1 = lax.fori_loop(0, i, body, tuple(carry))
    o_t = jnp.concatenate([a[:V_HEAD, :] / a[V_HEAD:V_HEAD + 1, :] for a in (a0, a1)], axis=0)
    o_ref[0] = _bf(o_t.T)


def _attention(q_t, k, v_t, km, vm_t):
    bsz, seq, _ = k.shape
    pairs = ATTN_HEADS // 2
    pw = 2 * HEAD_PAD
    return pl.pallas_call(
        _attn_body,
        grid=(bsz, pairs, seq // T_Q),
        in_specs=[pl.BlockSpec((1, pw, T_Q), lambda b, p, i: (b, p, i)),
                  pl.BlockSpec((1, seq, pw), lambda b, p, i: (b, 0, p)),
                  pl.BlockSpec((1, pw, seq), lambda b, p, i: (b, p, 0)),
                  pl.BlockSpec((META_PAD, pw), lambda b, p, i: (0, p)),
                  pl.BlockSpec((pw, META_PAD), lambda b, p, i: (p, 0))],
        out_specs=pl.BlockSpec((1, T_Q, 2 * V_HEAD), lambda b, p, i: (b, i, p)),
        out_shape=jax.ShapeDtypeStruct((bsz, seq, D_ATTN), jnp.bfloat16),
        compiler_params=pltpu.CompilerParams(dimension_semantics=("arbitrary",) * 3,
                                             vmem_limit_bytes=VMEM_LIMIT),
        name="attention",
    )(q_t, k, v_t, km, vm_t)


def _mix_body(attn_ref, cn_ref, h0_ref, an_ref, woa_ref, woc_ref, g1_ref, b1_ref,
              wsg_ref, wsu_ref, wsd_ref, wrt_ref, rb_ref, tri_ref,
              hp_ref, base_ref, eidx_ref, gate_ref, pos_ref, cnt_ref, run_ref):
    t = attn_ref.shape[0]

    @pl.when(pl.program_id(0) == 0)
    def _():
        run_ref[...] = jnp.zeros_like(run_ref)

    attn_n = _rms(attn_ref[...].astype(jnp.float32), an_ref[...], D_ATTN)
    mix = _dot(_bf(attn_n), woa_ref[...]) + _dot(cn_ref[...], woc_ref[...])
    h1 = _layernorm(ALPHA * h0_ref[...] + mix, g1_ref[...], b1_ref[...])
    _store_row_tiles(hp_ref, (), h1)
    h1b = _bf(h1)
    g = _dot(h1b, wsg_ref[...])
    u = _dot(h1b, wsu_ref[...])
    base_ref[...] = ALPHA * h1 + _dot(_bf(g * _sigmoid(g) * u), wsd_ref[...])

    shp = (N_GROUPS, GROUP_SIZE, t)
    sc = _sigmoid(_dot_t(wrt_ref[...], h1b)).reshape(shp)
    sel = sc + rb_ref[...].reshape(N_GROUPS, GROUP_SIZE, 1)
    gio = lax.broadcasted_iota(jnp.int32, shp, 0)
    rio = lax.broadcasted_iota(jnp.int32, shp, 1)
    eio = gio * GROUP_SIZE + rio

    m1 = jnp.max(sel, axis=1, keepdims=True)
    i1 = jnp.min(jnp.where(sel == m1, rio, GROUP_SIZE), axis=1, keepdims=True)
    m2 = jnp.max(jnp.where(rio == i1, _NEG_INF, sel), axis=1, keepdims=True)
    gs = m1 + m2
    gio1 = lax.broadcasted_iota(jnp.int32, (N_GROUPS, 1, t), 0)
    gmask = jnp.zeros((N_GROUPS, 1, t), jnp.bool_)
    for _ in range(TOPK_GROUPS):
        mg = jnp.max(gs, axis=0, keepdims=True)
        ig = jnp.min(jnp.where(gs == mg, gio1, N_GROUPS), axis=0, keepdims=True)
        hit = gio1 == ig
        gmask = jnp.logical_or(gmask, hit)
        gs = jnp.where(hit, _NEG_INF, gs)

    cand = jnp.where(gmask, sel, _NEG_INF)

    def red(x, op):
        return op(op(x, axis=0, keepdims=True), axis=1, keepdims=True)

    picked = jnp.zeros(shp, jnp.bool_)
    e_rows, g_rows = [], []
    for _ in range(TOP_K):
        mk = red(cand, jnp.max)
        ek = red(jnp.where(cand == mk, eio, N_EXPERTS), jnp.min)
        hit = eio == ek
        g_rows.append(red(jnp.where(hit, sc, 0.0), jnp.sum))
        e_rows.append(ek)
        picked = jnp.logical_or(picked, hit)
        cand = jnp.where(hit, _NEG_INF, cand)
    gsum = g_rows[0]
    for gk in g_rows[1:]:
        gsum = gsum + gk

    pm = jnp.where(picked, 1.0, 0.0).reshape(N_EXPERTS, t)
    rank = _dot(_bf(pm), tri_ref[...])
    slot = (run_ref[...] + rank).reshape(shp)
    run_ref[...] = run_ref[...] + jnp.sum(pm, axis=1, keepdims=True)
    cnt_ref[...] = run_ref[...]
    for k in range(TOP_K):
        eidx_ref[0, k:k + 1, :] = e_rows[k].reshape(1, t)
        gate_ref[0, k:k + 1, :] = (g_rows[k] / gsum * ROUTED_SCALE).reshape(1, t)
        pk = red(jnp.where(eio == e_rows[k], slot, 0.0), jnp.sum)
        pos_ref[0, k:k + 1, :] = pk.reshape(1, t).astype(jnp.int32)


def _mix_route(attn, cn, h0, w):
    n = attn.shape[0]
    nt = n // T_MIX
    tri = _bf(jnp.triu(jnp.ones((T_MIX, T_MIX), jnp.float32), 1))
    params = [w["attn_norm"], w["wo_a"], w["wo_c"], w["ln1_g"], w["ln1_b"], w["ws_gate"], w["ws_up"],
              w["ws_down"], w["w_router_t"], w["router_bias"], tri]
    full = lambda a: pl.BlockSpec(a.shape, lambda i: (0,) * a.ndim)
    row_blk = lambda width: pl.BlockSpec((T_MIX, width), lambda i: (i, 0))
    k_blk = pl.BlockSpec((1, TOP_K, T_MIX), lambda i: (i, 0, 0))
    out_shape = [
        jax.ShapeDtypeStruct((n // SUBLANES, PACK_TILES, SUBLANES, LANES), jnp.int32),
        jax.ShapeDtypeStruct((n, D_MODEL), jnp.float32),
        jax.ShapeDtypeStruct((nt, TOP_K, T_MIX), jnp.int32),
        jax.ShapeDtypeStruct((nt, TOP_K, T_MIX), jnp.float32),
        jax.ShapeDtypeStruct((nt, TOP_K, T_MIX), jnp.int32),
        jax.ShapeDtypeStruct((N_EXPERTS, 1), jnp.float32),
    ]
    return pl.pallas_call(
        _mix_body,
        grid=(nt,),
        in_specs=[row_blk(D_ATTN), row_blk(D_CONV), row_blk(D_MODEL)] + [full(p) for p in params],
        out_specs=[pl.BlockSpec((T_MIX // SUBLANES, PACK_TILES, SUBLANES, LANES), lambda i: (i, 0, 0, 0)),
                   row_blk(D_MODEL),
                   k_blk, k_blk, k_blk, pl.BlockSpec((N_EXPERTS, 1), lambda i: (0, 0))],
        out_shape=out_shape,
        scratch_shapes=[pltpu.VMEM((N_EXPERTS, 1), jnp.float32)],
        compiler_params=pltpu.CompilerParams(dimension_semantics=("arbitrary",),
                                             vmem_limit_bytes=VMEM_LIMIT),
        name="mix_route",
    )(attn, cn, h0, *params)


def _pack_row_tiles(x):
    bits = lambda a: lax.bitcast_convert_type(_bf(a).astype(jnp.float32), jnp.int32)
    out = []
    for lt in range(PACK_TILES):
        lo = bits(x[:, lt * LANES:(lt + 1) * LANES])
        hi = bits(x[:, HALF + lt * LANES:HALF + (lt + 1) * LANES])
        out.append(lax.shift_right_logical(lo, 16) | (hi & HI_MASK))
    return out


def _unpack_lo(w):
    return lax.bitcast_convert_type(lax.shift_left(w, 16), jnp.float32)


def _unpack_hi(w):
    return lax.bitcast_convert_type(w & HI_MASK, jnp.float32)


def _store_row_tiles(ref, lead, x):
    rows = x.shape[0]
    for lt, w in enumerate(_pack_row_tiles(x)):
        ref[lead + (slice(None), lt)] = w.reshape(rows // SUBLANES, SUBLANES, LANES)


def _load_row_tiles(ref, lead, rows):
    return [ref[lead + (slice(None), lt)].reshape(rows, LANES) for lt in range(PACK_TILES)]


def _sc_mesh():
    return plsc.VectorSubcoreMesh(core_axis_name="c", subcore_axis_name="s")


def _worker_id():
    return lax.axis_index("s") * SC_CORES + lax.axis_index("c")


def _dispatch(hp, idx, n_rows):
    per_worker = hp.shape[0] // SC_PIECES // (SC_CORES * SC_SUBCORES)
    lists = idx.shape[1]

    @functools.partial(
        pl.kernel, mesh=_sc_mesh(),
        out_type=jax.ShapeDtypeStruct((n_rows, LANES), jnp.int32),
        scratch_types=[pltpu.VMEM((lists, LANES), jnp.int32),
                       pltpu.VMEM((SC_PIECES, LANES), jnp.int32),
                       pltpu.SemaphoreType.DMA],
        name="dispatch")
    def run(hp_hbm, idx_hbm, xs_hbm, idx_v, rows_v, sem):
        first = _worker_id() * per_worker

        @pl.loop(0, per_worker)
        def _(c):
            chunk = first + c
            pltpu.sync_copy(idx_hbm.at[chunk], idx_v)
            pltpu.sync_copy(hp_hbm.at[pl.ds(chunk * SC_PIECES, SC_PIECES)], rows_v)
            copies = [pltpu.async_copy(rows_v.at[pl.ds((j // TOP_K) * LANES, LANES)],
                                       xs_hbm.at[idx_v.at[j]], sem) for j in range(lists)]
            for cp in copies:
                cp.wait()

    return run(hp, idx)


def _combine(ys, idx, n_rows):
    per_worker = n_rows // SC_PIECES // (SC_CORES * SC_SUBCORES)
    lists = idx.shape[1]

    @functools.partial(
        pl.kernel, mesh=_sc_mesh(),
        out_type=jax.ShapeDtypeStruct((TOP_K, n_rows, LANES), jnp.int32),
        scratch_types=[pltpu.VMEM((lists, LANES), jnp.int32),
                       pltpu.VMEM((2, SC_PIECES, LANES), jnp.int32),
                       pltpu.SemaphoreType.DMA, pltpu.SemaphoreType.DMA],
        name="combine")
    def run(ys_hbm, idx_hbm, out_hbm, idx_v, rows_v, sem0, sem1):
        first = _worker_id() * per_worker
        sems = (sem0, sem1)

        def gather(k):
            buf = rows_v.at[k % 2]
            return [pltpu.async_copy(ys_hbm.at[idx_v.at[g * TOP_K + k]],
                                     buf.at[pl.ds(g * LANES, LANES)], sems[k % 2]) for g in range(SC_LISTS)]

        @pl.loop(0, per_worker)
        def _(c):
            chunk = first + c
            pltpu.sync_copy(idx_hbm.at[chunk], idx_v)
            pending = gather(0)
            for k in range(TOP_K):
                for cp in pending:
                    cp.wait()
                if k + 1 < TOP_K:
                    pending = gather(k + 1)
                pltpu.sync_copy(rows_v.at[k % 2], out_hbm.at[k, pl.ds(chunk * SC_PIECES, SC_PIECES)])

    return run(ys, idx)


def _expert_body(first_ref, count_ref, nused_ref, xs_hbm, wg_hbm, wu_hbm, wd_hbm, ys_hbm,
                 xbuf, ybuf, wgbuf, wubuf, wdbuf, wgu_ref, wdn_ref, sem_in, sem_out, sem_w):
    e = pl.program_id(0)
    nused = nused_ref[0]
    tiles = BLK // SUBLANES
    ahead = ROW_SLOTS - 1

    def fetch(b):
        slot = b % ROW_SLOTS
        return pltpu.make_async_copy(xs_hbm.at[pl.ds(b * tiles, tiles)], xbuf.at[slot], sem_in.at[slot])

    def writeback(b):
        slot = b % ROW_SLOTS
        return pltpu.make_async_copy(ybuf.at[slot], ys_hbm.at[pl.ds(b * tiles, tiles)], sem_out.at[slot])

    def weights(ex):
        slot = ex % W_SLOTS
        copies = []
        for src, dst in ((wg_hbm, wgbuf), (wu_hbm, wubuf), (wd_hbm, wdbuf)):
            part = src.shape[1] // W_PARTS
            for p in range(W_PARTS):
                rows = pl.ds(p * part, part)
                copies.append(pltpu.make_async_copy(src.at[ex, rows], dst.at[slot, rows], sem_w.at[slot]))
        return copies

    @pl.when(e == 0)
    def _():
        for b in range(ahead):
            @pl.when(b < nused)
            def _():
                fetch(b).start()
        for ex in range(W_SLOTS - 1):
            for cp in weights(ex):
                cp.start()

    for cp in weights(e):
        cp.wait()

    @pl.when(e + W_SLOTS - 1 < pl.num_programs(0))
    def _():
        for cp in weights(e + W_SLOTS - 1):
            cp.start()

    @pl.when(count_ref[e] > 0)
    def _():
        slot = e % W_SLOTS
        wgu_ref[:, :D_EXPERT] = _bf(wgbuf[slot])
        wgu_ref[:, D_EXPERT:] = _bf(wubuf[slot])
        wdn_ref[...] = _bf(wdbuf[slot])

    def block(t, carry):
        b = first_ref[e] + t
        slot = b % ROW_SLOTS
        fetch(b).wait()

        @pl.when(b + ahead < nused)
        def _():
            fetch(b + ahead).start()

        words = _load_row_tiles(xbuf, (slot,), BLK)
        x = jnp.concatenate([_bf(_unpack_lo(w)) for w in words] + [_bf(_unpack_hi(w)) for w in words], axis=1)
        gu = _dot(x, wgu_ref[...])
        g, u = gu[:, :D_EXPERT], gu[:, D_EXPERT:]
        y = _dot(_bf(g * _sigmoid(g) * u), wdn_ref[...])

        @pl.when(b >= ROW_SLOTS)
        def _():
            writeback(b - ROW_SLOTS).wait()

        _store_row_tiles(ybuf, (slot,), y)
        writeback(b).start()
        return carry

    lax.fori_loop(0, count_ref[e], block, 0)

    @pl.when(e == pl.num_programs(0) - 1)
    def _():
        for back in range(1, ROW_SLOTS + 1):
            @pl.when(nused >= back)
            def _():
                writeback(nused - back).wait()


def _experts(xs, first_blk, n_blk, nused, w_gate, w_up, w_down):
    blk = (BLK // SUBLANES, PACK_TILES, SUBLANES, LANES)
    grid_spec = pltpu.PrefetchScalarGridSpec(
        num_scalar_prefetch=3,
        grid=(N_EXPERTS,),
        in_specs=[pl.BlockSpec(memory_space=pl.ANY)] * 4,
        out_specs=pl.BlockSpec(memory_space=pl.ANY),
        scratch_shapes=[pltpu.VMEM((ROW_SLOTS,) + blk, jnp.int32),
                        pltpu.VMEM((ROW_SLOTS,) + blk, jnp.int32),
                        pltpu.VMEM((W_SLOTS, D_MODEL, D_EXPERT), jnp.float32),
                        pltpu.VMEM((W_SLOTS, D_MODEL, D_EXPERT), jnp.float32),
                        pltpu.VMEM((W_SLOTS, D_EXPERT, D_MODEL), jnp.float32),
                        pltpu.VMEM((D_MODEL, 2 * D_EXPERT), jnp.bfloat16),
                        pltpu.VMEM((D_EXPERT, D_MODEL), jnp.bfloat16),
                        pltpu.SemaphoreType.DMA((ROW_SLOTS,)),
                        pltpu.SemaphoreType.DMA((ROW_SLOTS,)),
                        pltpu.SemaphoreType.DMA((W_SLOTS,))],
    )
    return pl.pallas_call(
        _expert_body,
        grid_spec=grid_spec,
        out_shape=jax.ShapeDtypeStruct(xs.shape, jnp.int32),
        compiler_params=pltpu.CompilerParams(dimension_semantics=("arbitrary",),
                                             vmem_limit_bytes=VMEM_LIMIT),
        name="experts",
    )(first_blk, n_blk, nused, xs, w_gate, w_up, w_down)


def _final_body(yg_ref, gate_ref, base_ref, g2_ref, b2_ref, *rest):
    o_ref = rest[-1]
    t = base_ref.shape[0]
    gate = gate_ref[...]
    lo = [None] * PACK_TILES
    hi = [None] * PACK_TILES
    for k in range(TOP_K):
        gk = gate[:, k:k + 1]
        for lt, w in enumerate(_load_row_tiles(yg_ref, (k,), t)):
            a, b = gk * _unpack_lo(w), gk * _unpack_hi(w)
            lo[lt] = a if k == 0 else lo[lt] + a
            hi[lt] = b if k == 0 else hi[lt] + b
    ff = jnp.concatenate(lo + hi, axis=1)
    o_ref[...] = _layernorm(base_ref[...] + ff, g2_ref[...], b2_ref[...])


def _final(yg, gates, base, g2, b2, chunk, prev_out):
    n = base.shape[0]
    steps = yg.shape[1] * SUBLANES // T_FIN
    off = chunk * steps
    full = lambda a: pl.BlockSpec(a.shape, lambda i: (0,) * a.ndim)
    in_specs = [pl.BlockSpec((TOP_K, T_FIN // SUBLANES, PACK_TILES, SUBLANES, LANES),
                             lambda i: (0, i, 0, 0, 0)),
                pl.BlockSpec((T_FIN, TOP_K), lambda i: (off + i, 0)),
                pl.BlockSpec((T_FIN, D_MODEL), lambda i: (off + i, 0)),
                full(g2), full(b2)]
    args = [yg, gates, base, g2, b2]
    aliases = {}
    if prev_out is not None:
        in_specs.append(pl.BlockSpec(memory_space=pl.ANY))
        aliases = {len(args): 0}
        args.append(prev_out)
    return pl.pallas_call(
        _final_body,
        grid=(steps,),
        in_specs=in_specs,
        out_specs=pl.BlockSpec((T_FIN, D_MODEL), lambda i: (off + i, 0)),
        out_shape=jax.ShapeDtypeStruct((n, D_MODEL), jnp.float32),
        input_output_aliases=aliases,
        compiler_params=pltpu.CompilerParams(dimension_semantics=("arbitrary",),
                                             vmem_limit_bytes=VMEM_LIMIT),
        name="final",
    )(*args)


def _rope_tables(pos):
    half = QK_ROPE // 2
    inv = ROPE_THETA ** (-jnp.arange(0, QK_ROPE, 2, dtype=jnp.float32) / QK_ROPE)
    ang = pos.astype(jnp.float32)[:, None] * inv[None, :]
    cos, sin = jnp.cos(ang), jnp.sin(ang)
    n = pos.shape[0]
    z = lambda w: jnp.zeros((n, w), jnp.float32)
    c = jnp.concatenate([jnp.ones((n, QK_NOPE), jnp.float32), cos, cos, z(LANES - QK_NOPE - QK_ROPE)], axis=1)
    s_up = jnp.concatenate([z(QK_NOPE + half), sin, z(LANES - QK_NOPE - QK_ROPE)], axis=1)
    s_dn = jnp.concatenate([z(QK_NOPE), -sin, z(half), z(LANES - QK_NOPE - QK_ROPE)], axis=1)
    return c, s_up, s_dn


def _pad_heads(w, per_head, keep):
    rows = w.shape[0]
    w = w.reshape(rows, ATTN_HEADS, per_head)[:, :, :keep]
    w = jnp.pad(w, ((0, 0), (0, 0), (0, HEAD_PAD - keep)))
    return w.reshape(rows, ATTN_HEADS * HEAD_PAD)


def _prep_weights(emb_ln_g, emb_ln_b, w_in, q_norm, w_uq, kv_norm, w_ukv, conv_w, attn_out_norm,
                  conv_out_norm, w_o, ln1_g, ln1_b, w_router, router_bias, ws_gate, ws_up, ws_down):
    o1 = Q_LORA
    o2 = o1 + KV_LORA
    o3 = o2 + QK_ROPE
    row = lambda a: a.reshape(1, -1)
    kpe_cols = jnp.pad(w_in[:, o2:o3], ((0, 0), (QK_NOPE, HEAD_PAD - QK_NOPE - QK_ROPE)))
    w_in_p = jnp.concatenate([w_in[:, :o2], kpe_cols, w_in[:, o3:]], axis=1)
    ukv = w_ukv.reshape(KV_LORA, ATTN_HEADS, QK_NOPE + V_HEAD)
    w_uk = jnp.pad(ukv[:, :, :QK_NOPE], ((0, 0), (0, 0), (0, HEAD_PAD - QK_NOPE)))
    w_uv = jnp.pad(ukv[:, :, QK_NOPE:], ((0, 0), (0, 0), (0, HEAD_PAD - V_HEAD)))
    woa = w_o[:D_ATTN]
    return {
        "emb_g": row(emb_ln_g), "emb_b": row(emb_ln_b),
        "w_in": _bf(w_in_p),
        "q_norm": row(q_norm), "w_uq_t": _bf(_pad_heads(w_uq, QK_NOPE + QK_ROPE, QK_NOPE + QK_ROPE).T),
        "kv_norm": row(kv_norm), "w_uk": _bf(w_uk.reshape(KV_LORA, -1)), "w_uv_t": _bf(w_uv.reshape(KV_LORA, -1).T),
        "conv_w": jnp.pad(conv_w, ((0, SUBLANES - CONV_W), (0, 0))), "conv_norm": row(conv_out_norm),
        "attn_norm": row(attn_out_norm), "wo_a": _bf(woa), "wo_c": _bf(w_o[D_ATTN:]),
        "ln1_g": row(ln1_g), "ln1_b": row(ln1_b),
        "ws_gate": _bf(ws_gate), "ws_up": _bf(ws_up), "ws_down": _bf(ws_down),
        "w_router_t": _bf(w_router.T), "router_bias": router_bias.reshape(-1, 1),
    }


def _slot_body(eidx_ref, pos_ref, start_ref, idx_ref, base_ref):
    t = eidx_ref.shape[2]
    shp = (N_GROUPS, GROUP_SIZE, t)
    eio = (lax.broadcasted_iota(jnp.int32, shp, 0) * GROUP_SIZE + lax.broadcasted_iota(jnp.int32, shp, 1))
    start = start_ref[...].reshape(N_GROUPS, GROUP_SIZE, 1)
    for k in range(TOP_K):
        hit = eio == eidx_ref[0, k:k + 1, :].reshape(1, 1, t)
        st = jnp.sum(jnp.sum(jnp.where(hit, start, 0.0), axis=0, keepdims=True), axis=1, keepdims=True)
        d = st.reshape(1, t).astype(jnp.int32) + pos_ref[0, k:k + 1, :]
        base_ref[k:k + 1, :] = (d // SUBLANES) * (PACK_TILES * SUBLANES) + d % SUBLANES

    j = lax.broadcasted_iota(jnp.int32, (TOP_K, LANES), 1)
    per_list = LANES // (PACK_TILES * SUBLANES)
    lane_tile = (j // SUBLANES) % PACK_TILES * SUBLANES
    for c in range(t // SC_ROWS):
        lane0 = c * SC_ROWS
        src = base_ref[:, lane0 // LANES * LANES:(lane0 // LANES + 1) * LANES]
        for g in range(SC_LISTS):
            tok = lane0 % LANES + (g * per_list + j // (PACK_TILES * SUBLANES)) * SUBLANES + j % SUBLANES
            idx_ref[c, g * TOP_K:(g + 1) * TOP_K, :] = jnp.take_along_axis(src, tok, axis=1) + lane_tile


def _slots(eidx, pos, start):
    nt, _, t = eidx.shape
    per_tile = t // SC_ROWS
    k_blk = pl.BlockSpec((1, TOP_K, t), lambda i: (i, 0, 0))
    return pl.pallas_call(
        _slot_body,
        grid=(nt,),
        in_specs=[k_blk, k_blk, pl.BlockSpec(start.shape, lambda i: (0, 0))],
        out_specs=pl.BlockSpec((per_tile, SC_LISTS * TOP_K, LANES), lambda i: (i, 0, 0)),
        out_shape=jax.ShapeDtypeStruct((nt * per_tile, SC_LISTS * TOP_K, LANES), jnp.int32),
        scratch_shapes=[pltpu.VMEM((TOP_K, t), jnp.int32)],
        compiler_params=pltpu.CompilerParams(dimension_semantics=("arbitrary",)),
        name="slots",
    )(eidx, pos, start)


def _slot_plan(eidx, pos, counts):
    counts = counts.reshape(-1).astype(jnp.int32)
    padded = (counts + BLK - 1) // BLK * BLK
    pad_end = jnp.cumsum(padded)
    pad_start = pad_end - padded
    idx = _slots(eidx, pos, pad_start.astype(jnp.float32).reshape(-1, 1))
    nb = eidx.size // BLK + N_EXPERTS
    nused = (pad_end[-1:] // BLK).astype(jnp.int32)
    return idx, (pad_start // BLK).astype(jnp.int32), (padded // BLK).astype(jnp.int32), nused, nb * BLK


def kernel(x, meta_tokens, emb_ln_g, emb_ln_b, w_in, q_norm, w_uq, kv_norm, w_ukv, conv_w, attn_out_norm, conv_out_norm, w_o, ln1_g, ln1_b, w_router, router_bias, w_gate, w_up, w_down, ws_gate, ws_up, ws_down, ln2_g, ln2_b):
    bsz, seq, _ = x.shape
    n = bsz * seq
    w = _prep_weights(emb_ln_g, emb_ln_b, w_in[0], q_norm[0], w_uq[0], kv_norm[0], w_ukv[0], conv_w[0],
                      attn_out_norm[0], conv_out_norm[0], w_o[0], ln1_g[0], ln1_b[0], w_router[0],
                      router_bias[0], ws_gate[0], ws_up[0], ws_down[0])

    meta_out = _inproj(meta_tokens[None], _rope_tables(jnp.arange(N_META)),
                       jnp.zeros((SUBLANES, D_CONV), jnp.float32), w, N_META)
    km = jnp.pad(meta_out[2][0], ((0, META_PAD - N_META), (0, 0)))
    vm = jnp.pad(meta_out[3][0], ((0, 0), (0, META_PAD - N_META)))
    vc_meta = meta_out[5][0]

    h0, q, k, v, cn, _ = _inproj(x, _rope_tables(N_META + jnp.arange(seq)), vc_meta, w, T_IN)
    attn = _attention(q, k, v, km, vm)

    hp, base, eidx, gates, pos, counts = _mix_route(
        attn.reshape(n, D_ATTN), cn.reshape(n, D_CONV), h0.reshape(n, D_MODEL), w)
    idx, first_blk, n_blk, nused, n_slots = _slot_plan(eidx, pos, counts)
    tiled = lambda a, lead: a.reshape(lead + (-1, PACK_TILES, SUBLANES, LANES))

    xs = _dispatch(hp.reshape(-1, LANES), idx, n_slots * PACK_TILES)
    ys = _experts(tiled(xs, ()), first_blk, n_blk, nused, w_gate[0], w_up[0], w_down[0])
    gates_t = gates.transpose(0, 2, 1).reshape(n, TOP_K)
    g2, b2 = ln2_g[0].reshape(1, -1), ln2_b[0].reshape(1, -1)
    ys_flat = ys.reshape(-1, LANES)
    per = idx.shape[0] // OUT_CHUNKS
    out = None
    for c in range(OUT_CHUNKS):
        yg = tiled(_combine(ys_flat, idx[c * per:(c + 1) * per], n // OUT_CHUNKS * PACK_TILES), (TOP_K,))
        out = _final(yg, gates_t, base, g2, b2, c, out)
    return out.reshape(bsz, seq, D_MODEL)
```

```python
import functools
import math

import jax
import jax.numpy as jnp
from jax import lax
from jax.experimental import pallas as pl
from jax.experimental.pallas import tpu as pltpu
from jax.experimental.pallas import tpu_sc as plsc

D_MODEL = 1024
N_META = 16
ATTN_HEADS = 8
QK_NOPE = 64
QK_ROPE = 32
V_HEAD = 64
Q_LORA = 384
KV_LORA = 256
ROPE_THETA = 10000.0
D_ATTN = ATTN_HEADS * V_HEAD
D_CONV = 512
CONV_W = 3
ATTN_SCALE = 1.0 / math.sqrt(QK_NOPE + QK_ROPE)
N_EXPERTS = 256
TOP_K = 8
N_GROUPS = 8
GROUP_SIZE = N_EXPERTS // N_GROUPS
TOPK_GROUPS = 4
D_EXPERT = 256
D_SHARED = 256
ROUTED_SCALE = 2.5
DEPTH = 1
ALPHA = (2.0 * DEPTH) ** 0.25

LANES = 128
SUBLANES = 8
HEAD_PAD = LANES
META_PAD = LANES
HALF = D_MODEL // 2
PACK_TILES = HALF // LANES
HI_MASK = -65536
SC_CORES = 2
SC_SUBCORES = 16

T_IN = 512
T_Q = 512
SHIFT_MARGIN = 40.0
T_MIX = 512
BLK = 256
ROW_SLOTS = 4
W_SLOTS = 3
W_PARTS = 2
T_FIN = 256
OUT_CHUNKS = 4
SC_ROWS = 64
SC_PIECES = SC_ROWS * PACK_TILES
SC_LISTS = SC_PIECES // LANES

VMEM_LIMIT = 56 * 1024 * 1024

_NEG_INF = float("-inf")


def _bf(x):
    return x.astype(jnp.bfloat16)


def _dot(a, b):
    return jnp.dot(a, b, preferred_element_type=jnp.float32)


def _dot_t(a, b):
    return lax.dot_general(a, b, (((1,), (1,)), ((), ())), preferred_element_type=jnp.float32)


def _layernorm(x, g, b, eps=1e-5):
    mu = jnp.mean(x, axis=-1, keepdims=True)
    xc = x - mu
    var = jnp.mean(xc * xc, axis=-1, keepdims=True)
    return xc * lax.rsqrt(var + eps) * g + b


def _rms(x, g, n, eps=1e-6):
    ms = jnp.sum(x * x, axis=-1, keepdims=True) * (1.0 / n)
    return x * lax.rsqrt(ms + eps) * g


def _sigmoid(x):
    return 1.0 / (1.0 + jnp.exp(-x))


def _rope(x, c, s_up, s_dn):
    return x * c + pltpu.roll(x, QK_ROPE // 2, 1) * s_up + pltpu.roll(x, LANES - QK_ROPE // 2, 1) * s_dn


def _rope_t(x, c, s_up, s_dn):
    return x * c + pltpu.roll(x, QK_ROPE // 2, 0) * s_up + pltpu.roll(x, LANES - QK_ROPE // 2, 0) * s_dn


def _inproj_body(x_ref, g_ref, b_ref, win_ref, qn_ref, wuq_ref, kvn_ref, wuk_ref, wuv_ref, cw_ref, con_ref,
                 tc_ref, tu_ref, td_ref, tct_ref, tut_ref, tdt_ref, vcinit_ref,
                 q_ref, k_ref, v_ref, cn_ref, vct_ref, carry_ref):
    rows = x_ref.shape[1]

    @pl.when(pl.program_id(1) == 0)
    def _():
        carry_ref[...] = vcinit_ref[...]

    h0 = _layernorm(x_ref[0], g_ref[...], b_ref[...])
    z = _dot(_bf(h0), win_ref[...])
    o1 = Q_LORA
    o2 = o1 + KV_LORA
    o3 = o2 + HEAD_PAD
    o4 = o3 + D_CONV
    o5 = o4 + D_CONV
    c, s_up, s_dn = tc_ref[...], tu_ref[...], td_ref[...]

    ct, st_up, st_dn = tct_ref[...], tut_ref[...], tdt_ref[...]
    q_t = _dot_t(wuq_ref[...], _bf(_rms(z[:, :o1], qn_ref[...], Q_LORA)))
    for h in range(ATTN_HEADS):
        sl = slice(h * HEAD_PAD, (h + 1) * HEAD_PAD)
        q_ref[0, sl, :] = _bf(_rope_t(q_t[sl, :], ct, st_up, st_dn) * ATTN_SCALE)

    ckv = _bf(_rms(z[:, o1:o2], kvn_ref[...], KV_LORA))
    kk = _dot(ckv, wuk_ref[...])
    kpe = _rope(z[:, o2:o3], c, s_up, s_dn)
    for h in range(ATTN_HEADS):
        sl = slice(h * HEAD_PAD, (h + 1) * HEAD_PAD)
        k_ref[0, :, sl] = _bf(kk[:, sl] + kpe)
    v_t = _dot_t(wuv_ref[...], ckv)
    dim = lax.broadcasted_iota(jnp.int32, (HEAD_PAD, rows), 0)
    for h in range(ATTN_HEADS):
        sl = slice(h * HEAD_PAD, (h + 1) * HEAD_PAD)
        v_ref[0, sl, :] = _bf(jnp.where(dim < V_HEAD, v_t[sl, :], 1.0))

    vc = z[:, o4:o5] * z[:, o5:]
    carry = carry_ref[...]
    row = lax.broadcasted_iota(jnp.int32, (rows, D_CONV), 0)
    vc1 = jnp.where(row == 0, carry[SUBLANES - 1:SUBLANES, :], pltpu.roll(vc, 1, 0))
    vc2 = pltpu.roll(vc, 2, 0)
    vc2 = jnp.where(row == 0, carry[SUBLANES - 2:SUBLANES - 1, :], vc2)
    vc2 = jnp.where(row == 1, carry[SUBLANES - 1:SUBLANES, :], vc2)
    cw = cw_ref[...]
    conv = z[:, o3:o4] * (cw[0:1, :] * vc2 + cw[1:2, :] * vc1 + cw[2:3, :] * vc)
    cn_ref[0] = _bf(_rms(conv, con_ref[...], D_CONV))
    tail = vc[rows - SUBLANES:, :]
    carry_ref[...] = tail
    vct_ref[0] = tail


def _inproj(x, tabs, vcinit, w, rows):
    bsz, seq, _ = x.shape
    nj = seq // rows
    full = lambda a: pl.BlockSpec(a.shape, lambda b, j: (0,) * a.ndim)
    row_blk = lambda width: pl.BlockSpec((1, rows, width), lambda b, j: (b, j, 0))
    tab = pl.BlockSpec((rows, HEAD_PAD), lambda b, j: (j, 0))
    tab_t = pl.BlockSpec((HEAD_PAD, rows), lambda b, j: (0, j))
    col_blk = pl.BlockSpec((1, ATTN_HEADS * HEAD_PAD, rows), lambda b, j: (b, 0, j))
    params = [w["emb_g"], w["emb_b"], w["w_in"], w["q_norm"], w["w_uq_t"], w["kv_norm"], w["w_uk"], w["w_uv_t"],
              w["conv_w"], w["conv_norm"]]
    wide = ATTN_HEADS * HEAD_PAD
    out_shape = [
        jax.ShapeDtypeStruct((bsz, wide, seq), jnp.bfloat16),
        jax.ShapeDtypeStruct((bsz, seq, wide), jnp.bfloat16),
        jax.ShapeDtypeStruct((bsz, wide, seq), jnp.bfloat16),
        jax.ShapeDtypeStruct((bsz, seq, D_CONV), jnp.bfloat16),
        jax.ShapeDtypeStruct((bsz, nj * SUBLANES, D_CONV), jnp.float32),
    ]
    tabs_t = [t.T for t in tabs]
    return pl.pallas_call(
        _inproj_body,
        grid=(bsz, nj),
        in_specs=([row_blk(D_MODEL)] + [full(p) for p in params] + [tab, tab, tab, tab_t, tab_t, tab_t]
                  + [full(vcinit)]),
        out_specs=[col_blk, row_blk(wide), col_blk, row_blk(D_CONV),
                   pl.BlockSpec((1, SUBLANES, D_CONV), lambda b, j: (b, j, 0))],
        out_shape=out_shape,
        scratch_shapes=[pltpu.VMEM((SUBLANES, D_CONV), jnp.float32)],
        compiler_params=pltpu.CompilerParams(dimension_semantics=("arbitrary", "arbitrary"),
                                             vmem_limit_bytes=VMEM_LIMIT),
        name="inproj",
    )(x, *params, *tabs, *tabs_t, vcinit)


def _attn_body(q_ref, k_ref, v_ref, km_ref, vm_ref, o_ref):
    i = pl.program_id(2)
    tq = q_ref.shape[2]
    heads = [slice(hh * HEAD_PAD, (hh + 1) * HEAD_PAD) for hh in range(2)]
    qs = [q_ref[0, sl, :] for sl in heads]

    causal = lax.broadcasted_iota(jnp.int32, (tq, tq), 0) <= lax.broadcasted_iota(jnp.int32, (tq, tq), 1)
    is_meta = lax.broadcasted_iota(jnp.int32, (META_PAD, tq), 0) < N_META
    r0 = pl.multiple_of(i * tq, tq)
    carry = []
    for q, sl in zip(qs, heads):
        sa = jnp.where(causal, _dot(k_ref[0, pl.ds(r0, tq), sl], q), _NEG_INF)
        sb = jnp.where(is_meta, _dot(km_ref[:, sl], q), _NEG_INF)
        m = jnp.maximum(jnp.max(sa, axis=0, keepdims=True), jnp.max(sb, axis=0, keepdims=True))
        acc = (_dot(v_ref[0, sl, pl.ds(r0, tq)], _bf(jnp.exp(sa - m)))
               + _dot(vm_ref[sl, :], _bf(jnp.exp(sb - m))))
        carry += [m, acc]

    def body(j, carry):
        rj = pl.multiple_of(j * tq, tq)
        ms, accs = (carry[0], carry[2]), (carry[1], carry[3])
        fast, tops = [], []
        for q, sl, m, acc in zip(qs, heads, ms, accs):
            s = _dot(k_ref[0, pl.ds(rj, tq), sl], q)
            tops.append(jnp.max(s, axis=0, keepdims=True))
            fast.append(acc + _dot(v_ref[0, sl, pl.ds(rj, tq)], _bf(jnp.exp(s - m))))
        excess = jnp.maximum(tops[0] - ms[0], tops[1] - ms[1])
        over = jnp.max(excess, axis=1, keepdims=True)[0, 0] > SHIFT_MARGIN

        def rescale(_):
            out = []
            for q, sl, m, acc, top in zip(qs, heads, ms, accs, tops):
                s = _dot(k_ref[0, pl.ds(rj, tq), sl], q)
                m_new = jnp.maximum(m, top)
                out += [m_new, jnp.exp(m - m_new) * acc + _dot(v_ref[0, sl, pl.ds(rj, tq)], _bf(jnp.exp(s - m_new)))]
            return tuple(out)

        return lax.cond(over, rescale, lambda _: (ms[0], fast[0], ms[1], fast[1]), 0)

    _, a0, _, a1 = lax.fori_loop(0, i, body, tuple(carry))
    o_t = jnp.concatenate([a[:V_HEAD, :] / a[V_HEAD:V_HEAD + 1, :] for a in (a0, a1)], axis=0)
    o_ref[0] = _bf(o_t.T)


def _attention(q_t, k, v_t, km, vm_t):
    bsz, seq, _ = k.shape
    pairs = ATTN_HEADS // 2
    pw = 2 * HEAD_PAD
    return pl.pallas_call(
        _attn_body,
        grid=(bsz, pairs, seq // T_Q),
        in_specs=[pl.BlockSpec((1, pw, T_Q), lambda b, p, i: (b, p, i)),
                  pl.BlockSpec((1, seq, pw), lambda b, p, i: (b, 0, p)),
                  pl.BlockSpec((1, pw, seq), lambda b, p, i: (b, p, 0)),
                  pl.BlockSpec((META_PAD, pw), lambda b, p, i: (0, p)),
                  pl.BlockSpec((pw, META_PAD), lambda b, p, i: (p, 0))],
        out_specs=pl.BlockSpec((1, T_Q, 2 * V_HEAD), lambda b, p, i: (b, i, p)),
        out_shape=jax.ShapeDtypeStruct((bsz, seq, D_ATTN), jnp.bfloat16),
        compiler_params=pltpu.CompilerParams(dimension_semantics=("arbitrary",) * 3,
                                             vmem_limit_bytes=VMEM_LIMIT),
        name="attention",
    )(q_t, k, v_t, km, vm_t)


def _mix_body(attn_ref, cn_ref, x_ref, eg_ref, eb_ref, an_ref, woa_ref, woc_ref, g1_ref, b1_ref,
              wsg_ref, wsu_ref, wsd_ref, wrt_ref, rb_ref, tri_ref,
              hp_ref, base_ref, eidx_ref, gate_ref, pos_ref, cnt_ref, run_ref):
    t = attn_ref.shape[0]

    @pl.when(pl.program_id(0) == 0)
    def _():
        run_ref[...] = jnp.zeros_like(run_ref)

    attn_n = _rms(attn_ref[...].astype(jnp.float32), an_ref[...], D_ATTN)
    mix = _dot(_bf(attn_n), woa_ref[...]) + _dot(cn_ref[...], woc_ref[...])
    h0 = _layernorm(x_ref[...], eg_ref[...], eb_ref[...])
    h1 = _layernorm(ALPHA * h0 + mix, g1_ref[...], b1_ref[...])
    _store_row_tiles(hp_ref, (), h1)
    h1b = _bf(h1)
    g = _dot(h1b, wsg_ref[...])
    u = _dot(h1b, wsu_ref[...])
    base_ref[...] = ALPHA * h1 + _dot(_bf(g * _sigmoid(g) * u), wsd_ref[...])

    shp = (N_GROUPS, GROUP_SIZE, t)
    sc = _sigmoid(_dot_t(wrt_ref[...], h1b)).reshape(shp)
    sel = sc + rb_ref[...].reshape(N_GROUPS, GROUP_SIZE, 1)
    gio = lax.broadcasted_iota(jnp.int32, shp, 0)
    rio = lax.broadcasted_iota(jnp.int32, shp, 1)
    eio = gio * GROUP_SIZE + rio

    m1 = jnp.max(sel, axis=1, keepdims=True)
    i1 = jnp.min(jnp.where(sel == m1, rio, GROUP_SIZE), axis=1, keepdims=True)
    m2 = jnp.max(jnp.where(rio == i1, _NEG_INF, sel), axis=1, keepdims=True)
    gs = m1 + m2
    gio1 = lax.broadcasted_iota(jnp.int32, (N_GROUPS, 1, t), 0)
    gmask = jnp.zeros((N_GROUPS, 1, t), jnp.bool_)
    for _ in range(TOPK_GROUPS):
        mg = jnp.max(gs, axis=0, keepdims=True)
        ig = jnp.min(jnp.where(gs == mg, gio1, N_GROUPS), axis=0, keepdims=True)
        hit = gio1 == ig
        gmask = jnp.logical_or(gmask, hit)
        gs = jnp.where(hit, _NEG_INF, gs)

    cand = jnp.where(gmask, sel, _NEG_INF)

    def red(x, op):
        return op(op(x, axis=0, keepdims=True), axis=1, keepdims=True)

    pm = jnp.zeros(shp, jnp.float32)
    e_rows, g_rows = [], []
    for _ in range(TOP_K):
        mk = red(cand, jnp.max)
        ek = red(jnp.where(cand == mk, eio, N_EXPERTS), jnp.min)
        hit = eio == ek
        g_rows.append(red(jnp.where(hit, sc, 0.0), jnp.sum))
        e_rows.append(ek)
        pm = jnp.where(hit, 1.0, pm)
        cand = jnp.where(hit, _NEG_INF, cand)
    gsum = g_rows[0]
    for gk in g_rows[1:]:
        gsum = gsum + gk

    pm = pm.reshape(N_EXPERTS, t)
    rank = _dot(_bf(pm), tri_ref[...])
    slot = (run_ref[...] + rank).reshape(shp)
    run_ref[...] = run_ref[...] + jnp.sum(pm, axis=1, keepdims=True)
    cnt_ref[...] = run_ref[...]
    for k in range(TOP_K):
        eidx_ref[0, k:k + 1, :] = e_rows[k].reshape(1, t)
        gate_ref[0, k:k + 1, :] = (g_rows[k] / gsum * ROUTED_SCALE).reshape(1, t)
        pk = red(jnp.where(eio == e_rows[k], slot, 0.0), jnp.sum)
        pos_ref[0, k:k + 1, :] = pk.reshape(1, t).astype(jnp.int32)


def _mix_route(attn, cn, x, w):
    n = attn.shape[0]
    nt = n // T_MIX
    tri = _bf(jnp.triu(jnp.ones((T_MIX, T_MIX), jnp.float32), 1))
    params = [w["emb_g"], w["emb_b"], w["attn_norm"], w["wo_a"], w["wo_c"], w["ln1_g"], w["ln1_b"], w["ws_gate"], w["ws_up"],
              w["ws_down"], w["w_router_t"], w["router_bias"], tri]
    full = lambda a: pl.BlockSpec(a.shape, lambda i: (0,) * a.ndim)
    row_blk = lambda width: pl.BlockSpec((T_MIX, width), lambda i: (i, 0))
    k_blk = pl.BlockSpec((1, TOP_K, T_MIX), lambda i: (i, 0, 0))
    out_shape = [
        jax.ShapeDtypeStruct((n // SUBLANES, PACK_TILES, SUBLANES, LANES), jnp.int32),
        jax.ShapeDtypeStruct((n, D_MODEL), jnp.float32),
        jax.ShapeDtypeStruct((nt, TOP_K, T_MIX), jnp.int32),
        jax.ShapeDtypeStruct((nt, TOP_K, T_MIX), jnp.float32),
        jax.ShapeDtypeStruct((nt, TOP_K, T_MIX), jnp.int32),
        jax.ShapeDtypeStruct((N_EXPERTS, 1), jnp.float32),
    ]
    return pl.pallas_call(
        _mix_body,
        grid=(nt,),
        in_specs=[row_blk(D_ATTN), row_blk(D_CONV), row_blk(D_MODEL)] + [full(p) for p in params],
        out_specs=[pl.BlockSpec((T_MIX // SUBLANES, PACK_TILES, SUBLANES, LANES), lambda i: (i, 0, 0, 0)),
                   row_blk(D_MODEL),
                   k_blk, k_blk, k_blk, pl.BlockSpec((N_EXPERTS, 1), lambda i: (0, 0))],
        out_shape=out_shape,
        scratch_shapes=[pltpu.VMEM((N_EXPERTS, 1), jnp.float32)],
        compiler_params=pltpu.CompilerParams(dimension_semantics=("arbitrary",),
                                             vmem_limit_bytes=VMEM_LIMIT),
        name="mix_route",
    )(attn, cn, x, *params)


def _pack_row_tiles(x):
    bits = lambda a: lax.bitcast_convert_type(_bf(a).astype(jnp.float32), jnp.int32)
    out = []
    for lt in range(PACK_TILES):
        lo = bits(x[:, lt * LANES:(lt + 1) * LANES])
        hi = bits(x[:, HALF + lt * LANES:HALF + (lt + 1) * LANES])
        out.append(lax.shift_right_logical(lo, 16) | (hi & HI_MASK))
    return out


def _unpack_lo(w):
    return lax.bitcast_convert_type(lax.shift_left(w, 16), jnp.float32)


def _unpack_hi(w):
    return lax.bitcast_convert_type(w & HI_MASK, jnp.float32)


def _store_row_tiles(ref, lead, x):
    rows = x.shape[0]
    for lt, w in enumerate(_pack_row_tiles(x)):
        ref[lead + (slice(None), lt)] = w.reshape(rows // SUBLANES, SUBLANES, LANES)


def _load_row_tiles(ref, lead, rows):
    return [ref[lead + (slice(None), lt)].reshape(rows, LANES) for lt in range(PACK_TILES)]


def _sc_mesh():
    return plsc.VectorSubcoreMesh(core_axis_name="c", subcore_axis_name="s")


def _worker_id():
    return lax.axis_index("s") * SC_CORES + lax.axis_index("c")


def _dispatch(hp, idx, n_rows):
    per_worker = hp.shape[0] // SC_PIECES // (SC_CORES * SC_SUBCORES)
    lists = idx.shape[1]

    @functools.partial(
        pl.kernel, mesh=_sc_mesh(),
        out_type=jax.ShapeDtypeStruct((n_rows, LANES), jnp.int32),
        scratch_types=[pltpu.VMEM((lists, LANES), jnp.int32),
                       pltpu.VMEM((SC_PIECES, LANES), jnp.int32),
                       pltpu.SemaphoreType.DMA],
        name="dispatch")
    def run(hp_hbm, idx_hbm, xs_hbm, idx_v, rows_v, sem):
        first = _worker_id() * per_worker

        @pl.loop(0, per_worker)
        def _(c):
            chunk = first + c
            pltpu.sync_copy(idx_hbm.at[chunk], idx_v)
            pltpu.sync_copy(hp_hbm.at[pl.ds(chunk * SC_PIECES, SC_PIECES)], rows_v)
            copies = [pltpu.async_copy(rows_v.at[pl.ds((j // TOP_K) * LANES, LANES)],
                                       xs_hbm.at[idx_v.at[j]], sem) for j in range(lists)]
            for cp in copies:
                cp.wait()

    return run(hp, idx)


def _combine(ys, idx, n_rows):
    per_worker = n_rows // SC_PIECES // (SC_CORES * SC_SUBCORES)
    lists = idx.shape[1]

    @functools.partial(
        pl.kernel, mesh=_sc_mesh(),
        out_type=jax.ShapeDtypeStruct((TOP_K, n_rows, LANES), jnp.int32),
        scratch_types=[pltpu.VMEM((lists, LANES), jnp.int32),
                       pltpu.VMEM((2, SC_PIECES, LANES), jnp.int32),
                       pltpu.SemaphoreType.DMA, pltpu.SemaphoreType.DMA],
        name="combine")
    def run(ys_hbm, idx_hbm, out_hbm, idx_v, rows_v, sem0, sem1):
        first = _worker_id() * per_worker
        sems = (sem0, sem1)

        def gather(k):
            buf = rows_v.at[k % 2]
            return [pltpu.async_copy(ys_hbm.at[idx_v.at[g * TOP_K + k]],
                                     buf.at[pl.ds(g * LANES, LANES)], sems[k % 2]) for g in range(SC_LISTS)]

        @pl.loop(0, per_worker)
        def _(c):
            chunk = first + c
            pltpu.sync_copy(idx_hbm.at[chunk], idx_v)
            pending = gather(0)
            for k in range(TOP_K):
                for cp in pending:
                    cp.wait()
                if k + 1 < TOP_K:
                    pending = gather(k + 1)
                pltpu.sync_copy(rows_v.at[k % 2], out_hbm.at[k, pl.ds(chunk * SC_PIECES, SC_PIECES)])

    return run(ys, idx)


def _expert_body(first_ref, count_ref, nused_ref, xs_hbm, wg_hbm, wu_hbm, wd_hbm, ys_hbm,
                 xbuf, ybuf, wgbuf, wubuf, wdbuf, wgu_ref, wdn_ref, sem_in, sem_out, sem_w):
    e = pl.program_id(0)
    nused = nused_ref[0]
    tiles = BLK // SUBLANES
    ahead = ROW_SLOTS - 1

    def fetch(b):
        slot = b % ROW_SLOTS
        return pltpu.make_async_copy(xs_hbm.at[pl.ds(b * tiles, tiles)], xbuf.at[slot], sem_in.at[slot])

    def writeback(b):
        slot = b % ROW_SLOTS
        return pltpu.make_async_copy(ybuf.at[slot], ys_hbm.at[pl.ds(b * tiles, tiles)], sem_out.at[slot])

    def weights(ex):
        slot = ex % W_SLOTS
        copies = []
        for src, dst in ((wg_hbm, wgbuf), (wu_hbm, wubuf), (wd_hbm, wdbuf)):
            part = src.shape[1] // W_PARTS
            for p in range(W_PARTS):
                rows = pl.ds(p * part, part)
                copies.append(pltpu.make_async_copy(src.at[ex, rows], dst.at[slot, rows], sem_w.at[slot]))
        return copies

    @pl.when(e == 0)
    def _():
        for b in range(ahead):
            @pl.when(b < nused)
            def _():
                fetch(b).start()
        for ex in range(W_SLOTS - 1):
            for cp in weights(ex):
                cp.start()

    for cp in weights(e):
        cp.wait()

    @pl.when(e + W_SLOTS - 1 < pl.num_programs(0))
    def _():
        for cp in weights(e + W_SLOTS - 1):
            cp.start()

    @pl.when(count_ref[e] > 0)
    def _():
        slot = e % W_SLOTS
        wgu_ref[:, :D_EXPERT] = _bf(wgbuf[slot])
        wgu_ref[:, D_EXPERT:] = _bf(wubuf[slot])
        wdn_ref[...] = _bf(wdbuf[slot])

    def block(t, carry):
        b = first_ref[e] + t
        slot = b % ROW_SLOTS
        fetch(b).wait()

        @pl.when(b + ahead < nused)
        def _():
            fetch(b + ahead).start()

        words = _load_row_tiles(xbuf, (slot,), BLK)
        x = jnp.concatenate([_bf(_unpack_lo(w)) for w in words] + [_bf(_unpack_hi(w)) for w in words], axis=1)
        gu = _dot(x, wgu_ref[...])
        g, u = gu[:, :D_EXPERT], gu[:, D_EXPERT:]
        y = _dot(_bf(g * _sigmoid(g) * u), wdn_ref[...])

        @pl.when(b >= ROW_SLOTS)
        def _():
            writeback(b - ROW_SLOTS).wait()

        _store_row_tiles(ybuf, (slot,), y)
        writeback(b).start()
        return carry

    lax.fori_loop(0, count_ref[e], block, 0)

    @pl.when(e == pl.num_programs(0) - 1)
    def _():
        for back in range(1, ROW_SLOTS + 1):
            @pl.when(nused >= back)
            def _():
                writeback(nused - back).wait()


def _experts(xs, first_blk, n_blk, nused, w_gate, w_up, w_down):
    blk = (BLK // SUBLANES, PACK_TILES, SUBLANES, LANES)
    grid_spec = pltpu.PrefetchScalarGridSpec(
        num_scalar_prefetch=3,
        grid=(N_EXPERTS,),
        in_specs=[pl.BlockSpec(memory_space=pl.ANY)] * 4,
        out_specs=pl.BlockSpec(memory_space=pl.ANY),
        scratch_shapes=[pltpu.VMEM((ROW_SLOTS,) + blk, jnp.int32),
                        pltpu.VMEM((ROW_SLOTS,) + blk, jnp.int32),
                        pltpu.VMEM((W_SLOTS, D_MODEL, D_EXPERT), jnp.float32),
                        pltpu.VMEM((W_SLOTS, D_MODEL, D_EXPERT), jnp.float32),
                        pltpu.VMEM((W_SLOTS, D_EXPERT, D_MODEL), jnp.float32),
                        pltpu.VMEM((D_MODEL, 2 * D_EXPERT), jnp.bfloat16),
                        pltpu.VMEM((D_EXPERT, D_MODEL), jnp.bfloat16),
                        pltpu.SemaphoreType.DMA((ROW_SLOTS,)),
                        pltpu.SemaphoreType.DMA((ROW_SLOTS,)),
                        pltpu.SemaphoreType.DMA((W_SLOTS,))],
    )
    return pl.pallas_call(
        _expert_body,
        grid_spec=grid_spec,
        out_shape=jax.ShapeDtypeStruct(xs.shape, jnp.int32),
        compiler_params=pltpu.CompilerParams(dimension_semantics=("arbitrary",),
                                             vmem_limit_bytes=VMEM_LIMIT),
        name="experts",
    )(first_blk, n_blk, nused, xs, w_gate, w_up, w_down)


def _final_body(yg_ref, gate_ref, base_ref, g2_ref, b2_ref, *rest):
    o_ref = rest[-1]
    t = base_ref.shape[0]
    gate = gate_ref[...]
    lo = [None] * PACK_TILES
    hi = [None] * PACK_TILES
    for k in range(TOP_K):
        gk = gate[:, k:k + 1]
        for lt, w in enumerate(_load_row_tiles(yg_ref, (k,), t)):
            a, b = gk * _unpack_lo(w), gk * _unpack_hi(w)
            lo[lt] = a if k == 0 else lo[lt] + a
            hi[lt] = b if k == 0 else hi[lt] + b
    ff = jnp.concatenate(lo + hi, axis=1)
    o_ref[...] = _layernorm(base_ref[...] + ff, g2_ref[...], b2_ref[...])


def _final(yg, gates, base, g2, b2, chunk, prev_out):
    n = base.shape[0]
    steps = yg.shape[1] * SUBLANES // T_FIN
    off = chunk * steps
    full = lambda a: pl.BlockSpec(a.shape, lambda i: (0,) * a.ndim)
    in_specs = [pl.BlockSpec((TOP_K, T_FIN // SUBLANES, PACK_TILES, SUBLANES, LANES),
                             lambda i: (0, i, 0, 0, 0)),
                pl.BlockSpec((T_FIN, TOP_K), lambda i: (off + i, 0)),
                pl.BlockSpec((T_FIN, D_MODEL), lambda i: (off + i, 0)),
                full(g2), full(b2)]
    args = [yg, gates, base, g2, b2]
    aliases = {}
    if prev_out is not None:
        in_specs.append(pl.BlockSpec(memory_space=pl.ANY))
        aliases = {len(args): 0}
        args.append(prev_out)
    return pl.pallas_call(
        _final_body,
        grid=(steps,),
        in_specs=in_specs,
        out_specs=pl.BlockSpec((T_FIN, D_MODEL), lambda i: (off + i, 0)),
        out_shape=jax.ShapeDtypeStruct((n, D_MODEL), jnp.float32),
        input_output_aliases=aliases,
        compiler_params=pltpu.CompilerParams(dimension_semantics=("arbitrary",),
                                             vmem_limit_bytes=VMEM_LIMIT),
        name="final",
    )(*args)


def _rope_tables(pos):
    half = QK_ROPE // 2
    inv = ROPE_THETA ** (-jnp.arange(0, QK_ROPE, 2, dtype=jnp.float32) / QK_ROPE)
    ang = pos.astype(jnp.float32)[:, None] * inv[None, :]
    cos, sin = jnp.cos(ang), jnp.sin(ang)
    n = pos.shape[0]
    z = lambda w: jnp.zeros((n, w), jnp.float32)
    c = jnp.concatenate([jnp.ones((n, QK_NOPE), jnp.float32), cos, cos, z(LANES - QK_NOPE - QK_ROPE)], axis=1)
    s_up = jnp.concatenate([z(QK_NOPE + half), sin, z(LANES - QK_NOPE - QK_ROPE)], axis=1)
    s_dn = jnp.concatenate([z(QK_NOPE), -sin, z(half), z(LANES - QK_NOPE - QK_ROPE)], axis=1)
    return c, s_up, s_dn


def _pad_heads(w, per_head, keep):
    rows = w.shape[0]
    w = w.reshape(rows, ATTN_HEADS, per_head)[:, :, :keep]
    w = jnp.pad(w, ((0, 0), (0, 0), (0, HEAD_PAD - keep)))
    return w.reshape(rows, ATTN_HEADS * HEAD_PAD)


def _prep_weights(emb_ln_g, emb_ln_b, w_in, q_norm, w_uq, kv_norm, w_ukv, conv_w, attn_out_norm,
                  conv_out_norm, w_o, ln1_g, ln1_b, w_router, router_bias, ws_gate, ws_up, ws_down):
    o1 = Q_LORA
    o2 = o1 + KV_LORA
    o3 = o2 + QK_ROPE
    row = lambda a: a.reshape(1, -1)
    kpe_cols = jnp.pad(w_in[:, o2:o3], ((0, 0), (QK_NOPE, HEAD_PAD - QK_NOPE - QK_ROPE)))
    w_in_p = jnp.concatenate([w_in[:, :o2], kpe_cols, w_in[:, o3:]], axis=1)
    ukv = w_ukv.reshape(KV_LORA, ATTN_HEADS, QK_NOPE + V_HEAD)
    w_uk = jnp.pad(ukv[:, :, :QK_NOPE], ((0, 0), (0, 0), (0, HEAD_PAD - QK_NOPE)))
    w_uv = jnp.pad(ukv[:, :, QK_NOPE:], ((0, 0), (0, 0), (0, HEAD_PAD - V_HEAD)))
    woa = w_o[:D_ATTN]
    return {
        "emb_g": row(emb_ln_g), "emb_b": row(emb_ln_b),
        "w_in": _bf(w_in_p),
        "q_norm": row(q_norm), "w_uq_t": _bf(_pad_heads(w_uq, QK_NOPE + QK_ROPE, QK_NOPE + QK_ROPE).T),
        "kv_norm": row(kv_norm), "w_uk": _bf(w_uk.reshape(KV_LORA, -1)), "w_uv_t": _bf(w_uv.reshape(KV_LORA, -1).T),
        "conv_w": jnp.pad(conv_w, ((0, SUBLANES - CONV_W), (0, 0))), "conv_norm": row(conv_out_norm),
        "attn_norm": row(attn_out_norm), "wo_a": _bf(woa), "wo_c": _bf(w_o[D_ATTN:]),
        "ln1_g": row(ln1_g), "ln1_b": row(ln1_b),
        "ws_gate": _bf(ws_gate), "ws_up": _bf(ws_up), "ws_down": _bf(ws_down),
        "w_router_t": _bf(w_router.T), "router_bias": router_bias.reshape(-1, 1),
    }


def _slot_body(eidx_ref, pos_ref, start_ref, idx_ref, base_ref):
    t = eidx_ref.shape[2]
    shp = (N_GROUPS, GROUP_SIZE, t)
    eio = (lax.broadcasted_iota(jnp.int32, shp, 0) * GROUP_SIZE + lax.broadcasted_iota(jnp.int32, shp, 1))
    start = start_ref[...].reshape(N_GROUPS, GROUP_SIZE, 1)
    for k in range(TOP_K):
        hit = eio == eidx_ref[0, k:k + 1, :].reshape(1, 1, t)
        st = jnp.sum(jnp.sum(jnp.where(hit, start, 0.0), axis=0, keepdims=True), axis=1, keepdims=True)
        d = st.reshape(1, t).astype(jnp.int32) + pos_ref[0, k:k + 1, :]
        base_ref[k:k + 1, :] = (d // SUBLANES) * (PACK_TILES * SUBLANES) + d % SUBLANES

    j = lax.broadcasted_iota(jnp.int32, (TOP_K, LANES), 1)
    per_list = LANES // (PACK_TILES * SUBLANES)
    lane_tile = (j // SUBLANES) % PACK_TILES * SUBLANES
    for c in range(t // SC_ROWS):
        lane0 = c * SC_ROWS
        src = base_ref[:, lane0 // LANES * LANES:(lane0 // LANES + 1) * LANES]
        for g in range(SC_LISTS):
            tok = lane0 % LANES + (g * per_list + j // (PACK_TILES * SUBLANES)) * SUBLANES + j % SUBLANES
            idx_ref[c, g * TOP_K:(g + 1) * TOP_K, :] = jnp.take_along_axis(src, tok, axis=1) + lane_tile


def _slots(eidx, pos, start):
    nt, _, t = eidx.shape
    per_tile = t // SC_ROWS
    k_blk = pl.BlockSpec((1, TOP_K, t), lambda i: (i, 0, 0))
    return pl.pallas_call(
        _slot_body,
        grid=(nt,),
        in_specs=[k_blk, k_blk, pl.BlockSpec(start.shape, lambda i: (0, 0))],
        out_specs=pl.BlockSpec((per_tile, SC_LISTS * TOP_K, LANES), lambda i: (i, 0, 0)),
        out_shape=jax.ShapeDtypeStruct((nt * per_tile, SC_LISTS * TOP_K, LANES), jnp.int32),
        scratch_shapes=[pltpu.VMEM((TOP_K, t), jnp.int32)],
        compiler_params=pltpu.CompilerParams(dimension_semantics=("arbitrary",)),
        name="slots",
    )(eidx, pos, start)


def _slot_plan(eidx, pos, counts):
    counts = counts.reshape(-1).astype(jnp.int32)
    padded = (counts + BLK - 1) // BLK * BLK
    pad_end = jnp.cumsum(padded)
    pad_start = pad_end - padded
    idx = _slots(eidx, pos, pad_start.astype(jnp.float32).reshape(-1, 1))
    nb = eidx.size // BLK + N_EXPERTS
    nused = (pad_end[-1:] // BLK).astype(jnp.int32)
    return idx, (pad_start // BLK).astype(jnp.int32), (padded // BLK).astype(jnp.int32), nused, nb * BLK


def kernel(x, meta_tokens, emb_ln_g, emb_ln_b, w_in, q_norm, w_uq, kv_norm, w_ukv, conv_w, attn_out_norm, conv_out_norm, w_o, ln1_g, ln1_b, w_router, router_bias, w_gate, w_up, w_down, ws_gate, ws_up, ws_down, ln2_g, ln2_b):
    bsz, seq, _ = x.shape
    n = bsz * seq
    w = _prep_weights(emb_ln_g, emb_ln_b, w_in[0], q_norm[0], w_uq[0], kv_norm[0], w_ukv[0], conv_w[0],
                      attn_out_norm[0], conv_out_norm[0], w_o[0], ln1_g[0], ln1_b[0], w_router[0],
                      router_bias[0], ws_gate[0], ws_up[0], ws_down[0])

    meta_out = _inproj(meta_tokens[None], _rope_tables(jnp.arange(N_META)),
                       jnp.zeros((SUBLANES, D_CONV), jnp.float32), w, N_META)
    km = jnp.pad(meta_out[1][0], ((0, META_PAD - N_META), (0, 0)))
    vm = jnp.pad(meta_out[2][0], ((0, 0), (0, META_PAD - N_META)))
    vc_meta = meta_out[4][0]

    q, k, v, cn, _ = _inproj(x, _rope_tables(N_META + jnp.arange(seq)), vc_meta, w, T_IN)
    attn = _attention(q, k, v, km, vm)

    hp, base, eidx, gates, pos, counts = _mix_route(
        attn.reshape(n, D_ATTN), cn.reshape(n, D_CONV), x.reshape(n, D_MODEL), w)
    idx, first_blk, n_blk, nused, n_slots = _slot_plan(eidx, pos, counts)
    tiled = lambda a, lead: a.reshape(lead + (-1, PACK_TILES, SUBLANES, LANES))

    xs = _dispatch(hp.reshape(-1, LANES), idx, n_slots * PACK_TILES)
    ys = _experts(tiled(xs, ()), first_blk, n_blk, nused, w_gate[0], w_up[0], w_down[0])
    gates_t = gates.transpose(0, 2, 1).reshape(n, TOP_K)
    g2, b2 = ln2_g[0].reshape(1, -1), ln2_b[0].reshape(1, -1)
    ys_flat = ys.reshape(-1, LANES)
    per = idx.shape[0] // OUT_CHUNKS
    out = None
    for c in range(OUT_CHUNKS):
        yg = tiled(_combine(ys_flat, idx[c * per:(c + 1) * per], n // OUT_CHUNKS * PACK_TILES), (TOP_K,))
        out = _final(yg, gates_t, base, g2, b2, c, out)
    return out.reshape(bsz, seq, D_MODEL)
```

```python
import functools
import math

import jax
import jax.numpy as jnp
from jax import lax
from jax.experimental import pallas as pl
from jax.experimental.pallas import tpu as pltpu
from jax.experimental.pallas import tpu_sc as plsc

D_MODEL = 1024
N_META = 16
ATTN_HEADS = 8
QK_NOPE = 64
QK_ROPE = 32
V_HEAD = 64
Q_LORA = 384
KV_LORA = 256
ROPE_THETA = 10000.0
D_ATTN = ATTN_HEADS * V_HEAD
D_CONV = 512
CONV_W = 3
ATTN_SCALE = 1.0 / math.sqrt(QK_NOPE + QK_ROPE)
N_EXPERTS = 256
TOP_K = 8
N_GROUPS = 8
GROUP_SIZE = N_EXPERTS // N_GROUPS
TOPK_GROUPS = 4
D_EXPERT = 256
D_SHARED = 256
ROUTED_SCALE = 2.5
DEPTH = 1
ALPHA = (2.0 * DEPTH) ** 0.25

LANES = 128
SUBLANES = 8
HEAD_PAD = LANES
META_PAD = LANES
HALF = D_MODEL // 2
PACK_TILES = HALF // LANES
HI_MASK = -65536
SC_CORES = 2
SC_SUBCORES = 16

T_IN = 512
T_Q = 1024
SHIFT_MARGIN = 40.0
T_MIX = 512
BLK = 256
ROW_SLOTS = 4
W_SLOTS = 3
W_PARTS = 2
T_FIN = 256
OUT_CHUNKS = 4
SC_ROWS = 64
SC_PIECES = SC_ROWS * PACK_TILES
SC_LISTS = SC_PIECES // LANES

VMEM_LIMIT = 56 * 1024 * 1024

_NEG_INF = float("-inf")


def _bf(x):
    return x.astype(jnp.bfloat16)


def _dot(a, b):
    return jnp.dot(a, b, preferred_element_type=jnp.float32)


def _dot_t(a, b):
    return lax.dot_general(a, b, (((1,), (1,)), ((), ())), preferred_element_type=jnp.float32)


def _layernorm(x, g, b, eps=1e-5):
    mu = jnp.mean(x, axis=-1, keepdims=True)
    xc = x - mu
    var = jnp.mean(xc * xc, axis=-1, keepdims=True)
    return xc * lax.rsqrt(var + eps) * g + b


def _rms(x, g, n, eps=1e-6):
    ms = jnp.sum(x * x, axis=-1, keepdims=True) * (1.0 / n)
    return x * lax.rsqrt(ms + eps) * g


def _sigmoid(x):
    return 1.0 / (1.0 + jnp.exp(-x))


def _rope(x, c, s_up, s_dn):
    return x * c + pltpu.roll(x, QK_ROPE // 2, 1) * s_up + pltpu.roll(x, LANES - QK_ROPE // 2, 1) * s_dn


def _rope_t(x, c, s_up, s_dn):
    return x * c + pltpu.roll(x, QK_ROPE // 2, 0) * s_up + pltpu.roll(x, LANES - QK_ROPE // 2, 0) * s_dn


def _inproj_body(x_ref, g_ref, b_ref, win_ref, qn_ref, wuq_ref, kvn_ref, wuk_ref, wuv_ref, cw_ref, con_ref,
                 tc_ref, tu_ref, td_ref, tct_ref, tut_ref, tdt_ref, vcinit_ref,
                 q_ref, k_ref, v_ref, cn_ref, vct_ref, carry_ref):
    rows = x_ref.shape[1]

    @pl.when(pl.program_id(1) == 0)
    def _():
        carry_ref[...] = vcinit_ref[...]

    h0 = _layernorm(x_ref[0], g_ref[...], b_ref[...])
    z = _dot(_bf(h0), win_ref[...])
    o1 = Q_LORA
    o2 = o1 + KV_LORA
    o3 = o2 + HEAD_PAD
    o4 = o3 + D_CONV
    o5 = o4 + D_CONV
    c, s_up, s_dn = tc_ref[...], tu_ref[...], td_ref[...]

    ct, st_up, st_dn = tct_ref[...], tut_ref[...], tdt_ref[...]
    q_t = _dot_t(wuq_ref[...], _bf(_rms(z[:, :o1], qn_ref[...], Q_LORA)))
    for h in range(ATTN_HEADS):
        sl = slice(h * HEAD_PAD, (h + 1) * HEAD_PAD)
        q_ref[0, sl, :] = _bf(_rope_t(q_t[sl, :], ct, st_up, st_dn) * ATTN_SCALE)

    ckv = _bf(_rms(z[:, o1:o2], kvn_ref[...], KV_LORA))
    kk = _dot(ckv, wuk_ref[...])
    kpe = _rope(z[:, o2:o3], c, s_up, s_dn)
    for h in range(ATTN_HEADS):
        sl = slice(h * HEAD_PAD, (h + 1) * HEAD_PAD)
        k_ref[0, :, sl] = _bf(kk[:, sl] + kpe)
    v_t = _dot_t(wuv_ref[...], ckv)
    dim = lax.broadcasted_iota(jnp.int32, (HEAD_PAD, rows), 0)
    for h in range(ATTN_HEADS):
        sl = slice(h * HEAD_PAD, (h + 1) * HEAD_PAD)
        v_ref[0, sl, :] = _bf(jnp.where(dim < V_HEAD, v_t[sl, :], 1.0))

    vc = z[:, o4:o5] * z[:, o5:]
    carry = carry_ref[...]
    row = lax.broadcasted_iota(jnp.int32, (rows, D_CONV), 0)
    vc1 = jnp.where(row == 0, carry[SUBLANES - 1:SUBLANES, :], pltpu.roll(vc, 1, 0))
    vc2 = pltpu.roll(vc, 2, 0)
    vc2 = jnp.where(row == 0, carry[SUBLANES - 2:SUBLANES - 1, :], vc2)
    vc2 = jnp.where(row == 1, carry[SUBLANES - 1:SUBLANES, :], vc2)
    cw = cw_ref[...]
    conv = z[:, o3:o4] * (cw[0:1, :] * vc2 + cw[1:2, :] * vc1 + cw[2:3, :] * vc)
    cn_ref[0] = _bf(_rms(conv, con_ref[...], D_CONV))
    tail = vc[rows - SUBLANES:, :]
    carry_ref[...] = tail
    vct_ref[0] = tail


def _inproj(x, tabs, vcinit, w, rows):
    bsz, seq, _ = x.shape
    nj = seq // rows
    full = lambda a: pl.BlockSpec(a.shape, lambda b, j: (0,) * a.ndim)
    row_blk = lambda width: pl.BlockSpec((1, rows, width), lambda b, j: (b, j, 0))
    tab = pl.BlockSpec((rows, HEAD_PAD), lambda b, j: (j, 0))
    tab_t = pl.BlockSpec((HEAD_PAD, rows), lambda b, j: (0, j))
    col_blk = pl.BlockSpec((1, ATTN_HEADS * HEAD_PAD, rows), lambda b, j: (b, 0, j))
    params = [w["emb_g"], w["emb_b"], w["w_in"], w["q_norm"], w["w_uq_t"], w["kv_norm"], w["w_uk"], w["w_uv_t"],
              w["conv_w"], w["conv_norm"]]
    wide = ATTN_HEADS * HEAD_PAD
    out_shape = [
        jax.ShapeDtypeStruct((bsz, wide, seq), jnp.bfloat16),
        jax.ShapeDtypeStruct((bsz, seq, wide), jnp.bfloat16),
        jax.ShapeDtypeStruct((bsz, wide, seq), jnp.bfloat16),
        jax.ShapeDtypeStruct((bsz, seq, D_CONV), jnp.bfloat16),
        jax.ShapeDtypeStruct((bsz, nj * SUBLANES, D_CONV), jnp.float32),
    ]
    tabs_t = [t.T for t in tabs]
    return pl.pallas_call(
        _inproj_body,
        grid=(bsz, nj),
        in_specs=([row_blk(D_MODEL)] + [full(p) for p in params] + [tab, tab, tab, tab_t, tab_t, tab_t]
                  + [full(vcinit)]),
        out_specs=[col_blk, row_blk(wide), col_blk, row_blk(D_CONV),
                   pl.BlockSpec((1, SUBLANES, D_CONV), lambda b, j: (b, j, 0))],
        out_shape=out_shape,
        scratch_shapes=[pltpu.VMEM((SUBLANES, D_CONV), jnp.float32)],
        compiler_params=pltpu.CompilerParams(dimension_semantics=("arbitrary", "arbitrary"),
                                             vmem_limit_bytes=VMEM_LIMIT),
        name="inproj",
    )(x, *params, *tabs, *tabs_t, vcinit)


def _attn_body(q_ref, k_ref, v_ref, km_ref, vm_ref, o_ref):
    i = pl.program_id(2)
    tq = q_ref.shape[2]
    heads = [slice(hh * HEAD_PAD, (hh + 1) * HEAD_PAD) for hh in range(2)]
    qs = [q_ref[0, sl, :] for sl in heads]

    hq = tq // 2
    causal_a = lax.broadcasted_iota(jnp.int32, (hq, tq), 0) <= lax.broadcasted_iota(jnp.int32, (hq, tq), 1)
    causal_b = lax.broadcasted_iota(jnp.int32, (hq, hq), 0) <= lax.broadcasted_iota(jnp.int32, (hq, hq), 1)
    is_meta = lax.broadcasted_iota(jnp.int32, (META_PAD, tq), 0) < N_META
    r0 = pl.multiple_of(i * tq, tq)
    r1 = pl.multiple_of(i * tq + hq, hq)
    carry = []
    for q, sl in zip(qs, heads):
        sa = jnp.where(causal_a, _dot(k_ref[0, pl.ds(r0, hq), sl], q), _NEG_INF)
        sb = jnp.where(causal_b, _dot(k_ref[0, pl.ds(r1, hq), sl], q[:, hq:]), _NEG_INF)
        sm = jnp.where(is_meta, _dot(km_ref[:, sl], q), _NEG_INF)
        m = jnp.maximum(jnp.max(sa, axis=0, keepdims=True), jnp.max(sm, axis=0, keepdims=True))
        m = jnp.concatenate([m[:, :hq], jnp.maximum(m[:, hq:], jnp.max(sb, axis=0, keepdims=True))], axis=1)
        acc = (_dot(v_ref[0, sl, pl.ds(r0, hq)], _bf(jnp.exp(sa - m)))
               + _dot(vm_ref[sl, :], _bf(jnp.exp(sm - m))))
        acc_b = _dot(v_ref[0, sl, pl.ds(r1, hq)], _bf(jnp.exp(sb - m[:, hq:])))
        carry += [m, jnp.concatenate([acc[:, :hq], acc[:, hq:] + acc_b], axis=1)]

    def body(j, carry):
        rj = pl.multiple_of(j * tq, tq)
        ms, accs = (carry[0], carry[2]), (carry[1], carry[3])
        fast, tops = [], []
        for q, sl, m, acc in zip(qs, heads, ms, accs):
            s = _dot(k_ref[0, pl.ds(rj, tq), sl], q)
            tops.append(jnp.max(s, axis=0, keepdims=True))
            fast.append(acc + _dot(v_ref[0, sl, pl.ds(rj, tq)], _bf(jnp.exp(s - m))))
        excess = jnp.maximum(tops[0] - ms[0], tops[1] - ms[1])
        over = jnp.max(excess, axis=1, keepdims=True)[0, 0] > SHIFT_MARGIN

        def rescale(_):
            out = []
            for q, sl, m, acc, top in zip(qs, heads, ms, accs, tops):
                s = _dot(k_ref[0, pl.ds(rj, tq), sl], q)
                m_new = jnp.maximum(m, top)
                out += [m_new, jnp.exp(m - m_new) * acc + _dot(v_ref[0, sl, pl.ds(rj, tq)], _bf(jnp.exp(s - m_new)))]
            return tuple(out)

        return lax.cond(over, rescale, lambda _: (ms[0], fast[0], ms[1], fast[1]), 0)

    _, a0, _, a1 = lax.fori_loop(0, i, body, tuple(carry))
    o_t = jnp.concatenate([a[:V_HEAD, :] / a[V_HEAD:V_HEAD + 1, :] for a in (a0, a1)], axis=0)
    o_ref[0] = _bf(o_t.T)


def _attention(q_t, k, v_t, km, vm_t):
    bsz, seq, _ = k.shape
    pairs = ATTN_HEADS // 2
    pw = 2 * HEAD_PAD
    return pl.pallas_call(
        _attn_body,
        grid=(bsz, pairs, seq // T_Q),
        in_specs=[pl.BlockSpec((1, pw, T_Q), lambda b, p, i: (b, p, i)),
                  pl.BlockSpec((1, seq, pw), lambda b, p, i: (b, 0, p)),
                  pl.BlockSpec((1, pw, seq), lambda b, p, i: (b, p, 0)),
                  pl.BlockSpec((META_PAD, pw), lambda b, p, i: (0, p)),
                  pl.BlockSpec((pw, META_PAD), lambda b, p, i: (p, 0))],
        out_specs=pl.BlockSpec((1, T_Q, 2 * V_HEAD), lambda b, p, i: (b, i, p)),
        out_shape=jax.ShapeDtypeStruct((bsz, seq, D_ATTN), jnp.bfloat16),
        compiler_params=pltpu.CompilerParams(dimension_semantics=("arbitrary",) * 3,
                                             vmem_limit_bytes=VMEM_LIMIT),
        name="attention",
    )(q_t, k, v_t, km, vm_t)


def _mix_body(attn_ref, cn_ref, x_ref, eg_ref, eb_ref, an_ref, woa_ref, woc_ref, g1_ref, b1_ref,
              wsg_ref, wsu_ref, wsd_ref, wrt_ref, rb_ref, tri_ref,
              hp_ref, base_ref, eidx_ref, gate_ref, pos_ref, cnt_ref, run_ref):
    t = attn_ref.shape[0]

    @pl.when(pl.program_id(0) == 0)
    def _():
        run_ref[...] = jnp.zeros_like(run_ref)

    attn_n = _rms(attn_ref[...].astype(jnp.float32), an_ref[...], D_ATTN)
    mix = _dot(_bf(attn_n), woa_ref[...]) + _dot(cn_ref[...], woc_ref[...])
    h0 = _layernorm(x_ref[...], eg_ref[...], eb_ref[...])
    h1 = _layernorm(ALPHA * h0 + mix, g1_ref[...], b1_ref[...])
    _store_row_tiles(hp_ref, (), h1)
    h1b = _bf(h1)
    g = _dot(h1b, wsg_ref[...])
    u = _dot(h1b, wsu_ref[...])
    base_ref[...] = ALPHA * h1 + _dot(_bf(g * _sigmoid(g) * u), wsd_ref[...])

    shp = (N_GROUPS, GROUP_SIZE, t)
    sc = _sigmoid(_dot_t(wrt_ref[...], h1b)).reshape(shp)
    sel = sc + rb_ref[...].reshape(N_GROUPS, GROUP_SIZE, 1)
    gio = lax.broadcasted_iota(jnp.int32, shp, 0)
    rio = lax.broadcasted_iota(jnp.int32, shp, 1)
    eio = gio * GROUP_SIZE + rio

    m1 = jnp.max(sel, axis=1, keepdims=True)
    i1 = jnp.min(jnp.where(sel == m1, rio, GROUP_SIZE), axis=1, keepdims=True)
    m2 = jnp.max(jnp.where(rio == i1, _NEG_INF, sel), axis=1, keepdims=True)
    gs = m1 + m2
    gio1 = lax.broadcasted_iota(jnp.int32, (N_GROUPS, 1, t), 0)
    gmask = jnp.zeros((N_GROUPS, 1, t), jnp.bool_)
    for _ in range(TOPK_GROUPS):
        mg = jnp.max(gs, axis=0, keepdims=True)
        ig = jnp.min(jnp.where(gs == mg, gio1, N_GROUPS), axis=0, keepdims=True)
        hit = gio1 == ig
        gmask = jnp.logical_or(gmask, hit)
        gs = jnp.where(hit, _NEG_INF, gs)

    cand = jnp.where(gmask, sel, _NEG_INF)

    def red(x, op):
        return op(op(x, axis=0, keepdims=True), axis=1, keepdims=True)

    pm = jnp.zeros(shp, jnp.float32)
    e_rows, g_rows = [], []
    for _ in range(TOP_K):
        mk = red(cand, jnp.max)
        ek = red(jnp.where(cand == mk, eio, N_EXPERTS), jnp.min)
        hit = eio == ek
        g_rows.append(red(jnp.where(hit, sc, 0.0), jnp.sum))
        e_rows.append(ek)
        pm = jnp.where(hit, 1.0, pm)
        cand = jnp.where(hit, _NEG_INF, cand)
    gsum = g_rows[0]
    for gk in g_rows[1:]:
        gsum = gsum + gk

    pm = pm.reshape(N_EXPERTS, t)
    rank = _dot(_bf(pm), tri_ref[...])
    slot = (run_ref[...] + rank).reshape(shp)
    run_ref[...] = run_ref[...] + jnp.sum(pm, axis=1, keepdims=True)
    cnt_ref[...] = run_ref[...]
    for k in range(TOP_K):
        eidx_ref[0, k:k + 1, :] = e_rows[k].reshape(1, t)
        gate_ref[0, k:k + 1, :] = (g_rows[k] / gsum * ROUTED_SCALE).reshape(1, t)
        pk = red(jnp.where(eio == e_rows[k], slot, 0.0), jnp.sum)
        pos_ref[0, k:k + 1, :] = pk.reshape(1, t).astype(jnp.int32)


def _mix_route(attn, cn, x, w):
    n = attn.shape[0]
    nt = n // T_MIX
    tri = _bf(jnp.triu(jnp.ones((T_MIX, T_MIX), jnp.float32), 1))
    params = [w["emb_g"], w["emb_b"], w["attn_norm"], w["wo_a"], w["wo_c"], w["ln1_g"], w["ln1_b"], w["ws_gate"], w["ws_up"],
              w["ws_down"], w["w_router_t"], w["router_bias"], tri]
    full = lambda a: pl.BlockSpec(a.shape, lambda i: (0,) * a.ndim)
    row_blk = lambda width: pl.BlockSpec((T_MIX, width), lambda i: (i, 0))
    k_blk = pl.BlockSpec((1, TOP_K, T_MIX), lambda i: (i, 0, 0))
    out_shape = [
        jax.ShapeDtypeStruct((n // SUBLANES, PACK_TILES, SUBLANES, LANES), jnp.int32),
        jax.ShapeDtypeStruct((n, D_MODEL), jnp.float32),
        jax.ShapeDtypeStruct((nt, TOP_K, T_MIX), jnp.int32),
        jax.ShapeDtypeStruct((nt, TOP_K, T_MIX), jnp.float32),
        jax.ShapeDtypeStruct((nt, TOP_K, T_MIX), jnp.int32),
        jax.ShapeDtypeStruct((N_EXPERTS, 1), jnp.float32),
    ]
    return pl.pallas_call(
        _mix_body,
        grid=(nt,),
        in_specs=[row_blk(D_ATTN), row_blk(D_CONV), row_blk(D_MODEL)] + [full(p) for p in params],
        out_specs=[pl.BlockSpec((T_MIX // SUBLANES, PACK_TILES, SUBLANES, LANES), lambda i: (i, 0, 0, 0)),
                   row_blk(D_MODEL),
                   k_blk, k_blk, k_blk, pl.BlockSpec((N_EXPERTS, 1), lambda i: (0, 0))],
        out_shape=out_shape,
        scratch_shapes=[pltpu.VMEM((N_EXPERTS, 1), jnp.float32)],
        compiler_params=pltpu.CompilerParams(dimension_semantics=("arbitrary",),
                                             vmem_limit_bytes=VMEM_LIMIT),
        name="mix_route",
    )(attn, cn, x, *params)


def _pack_row_tiles(x):
    bits = lambda a: lax.bitcast_convert_type(_bf(a).astype(jnp.float32), jnp.int32)
    out = []
    for lt in range(PACK_TILES):
        lo = bits(x[:, lt * LANES:(lt + 1) * LANES])
        hi = bits(x[:, HALF + lt * LANES:HALF + (lt + 1) * LANES])
        out.append(lax.shift_right_logical(lo, 16) | (hi & HI_MASK))
    return out


def _unpack_lo(w):
    return lax.bitcast_convert_type(lax.shift_left(w, 16), jnp.float32)


def _unpack_hi(w):
    return lax.bitcast_convert_type(w & HI_MASK, jnp.float32)


def _store_row_tiles(ref, lead, x):
    rows = x.shape[0]
    for lt, w in enumerate(_pack_row_tiles(x)):
        ref[lead + (slice(None), lt)] = w.reshape(rows // SUBLANES, SUBLANES, LANES)


def _load_row_tiles(ref, lead, rows):
    return [ref[lead + (slice(None), lt)].reshape(rows, LANES) for lt in range(PACK_TILES)]


def _sc_mesh():
    return plsc.VectorSubcoreMesh(core_axis_name="c", subcore_axis_name="s")


def _worker_id():
    return lax.axis_index("s") * SC_CORES + lax.axis_index("c")


def _dispatch(hp, idx, n_rows):
    per_worker = hp.shape[0] // SC_PIECES // (SC_CORES * SC_SUBCORES)
    lists = idx.shape[1]

    @functools.partial(
        pl.kernel, mesh=_sc_mesh(),
        out_type=jax.ShapeDtypeStruct((n_rows, LANES), jnp.int32),
        scratch_types=[pltpu.VMEM((lists, LANES), jnp.int32),
                       pltpu.VMEM((SC_PIECES, LANES), jnp.int32),
                       pltpu.SemaphoreType.DMA],
        name="dispatch")
    def run(hp_hbm, idx_hbm, xs_hbm, idx_v, rows_v, sem):
        first = _worker_id() * per_worker

        @pl.loop(0, per_worker)
        def _(c):
            chunk = first + c
            pltpu.sync_copy(idx_hbm.at[chunk], idx_v)
            pltpu.sync_copy(hp_hbm.at[pl.ds(chunk * SC_PIECES, SC_PIECES)], rows_v)
            copies = [pltpu.async_copy(rows_v.at[pl.ds((j // TOP_K) * LANES, LANES)],
                                       xs_hbm.at[idx_v.at[j]], sem) for j in range(lists)]
            for cp in copies:
                cp.wait()

    return run(hp, idx)


def _combine(ys, idx, n_rows):
    per_worker = n_rows // SC_PIECES // (SC_CORES * SC_SUBCORES)
    lists = idx.shape[1]

    @functools.partial(
        pl.kernel, mesh=_sc_mesh(),
        out_type=jax.ShapeDtypeStruct((TOP_K, n_rows, LANES), jnp.int32),
        scratch_types=[pltpu.VMEM((lists, LANES), jnp.int32),
                       pltpu.VMEM((2, SC_PIECES, LANES), jnp.int32),
                       pltpu.SemaphoreType.DMA, pltpu.SemaphoreType.DMA],
        name="combine")
    def run(ys_hbm, idx_hbm, out_hbm, idx_v, rows_v, sem0, sem1):
        first = _worker_id() * per_worker
        sems = (sem0, sem1)

        def gather(k):
            buf = rows_v.at[k % 2]
            return [pltpu.async_copy(ys_hbm.at[idx_v.at[g * TOP_K + k]],
                                     buf.at[pl.ds(g * LANES, LANES)], sems[k % 2]) for g in range(SC_LISTS)]

        @pl.loop(0, per_worker)
        def _(c):
            chunk = first + c
            pltpu.sync_copy(idx_hbm.at[chunk], idx_v)
            pending = gather(0)
            for k in range(TOP_K):
                for cp in pending:
                    cp.wait()
                if k + 1 < TOP_K:
                    pending = gather(k + 1)
                pltpu.sync_copy(rows_v.at[k % 2], out_hbm.at[k, pl.ds(chunk * SC_PIECES, SC_PIECES)])

    return run(ys, idx)


def _expert_body(first_ref, count_ref, nused_ref, xs_hbm, wg_hbm, wu_hbm, wd_hbm, ys_hbm,
                 xbuf, ybuf, wgbuf, wubuf, wdbuf, wgu_ref, wdn_ref, sem_in, sem_out, sem_w):
    e = pl.program_id(0)
    nused = nused_ref[0]
    tiles = BLK // SUBLANES
    ahead = ROW_SLOTS - 1

    def fetch(b):
        slot = b % ROW_SLOTS
        return pltpu.make_async_copy(xs_hbm.at[pl.ds(b * tiles, tiles)], xbuf.at[slot], sem_in.at[slot])

    def writeback(b):
        slot = b % ROW_SLOTS
        return pltpu.make_async_copy(ybuf.at[slot], ys_hbm.at[pl.ds(b * tiles, tiles)], sem_out.at[slot])

    def weights(ex):
        slot = ex % W_SLOTS
        copies = []
        for src, dst in ((wg_hbm, wgbuf), (wu_hbm, wubuf), (wd_hbm, wdbuf)):
            part = src.shape[1] // W_PARTS
            for p in range(W_PARTS):
                rows = pl.ds(p * part, part)
                copies.append(pltpu.make_async_copy(src.at[ex, rows], dst.at[slot, rows], sem_w.at[slot]))
        return copies

    @pl.when(e == 0)
    def _():
        for b in range(ahead):
            @pl.when(b < nused)
            def _():
                fetch(b).start()
        for ex in range(W_SLOTS - 1):
            for cp in weights(ex):
                cp.start()

    for cp in weights(e):
        cp.wait()

    @pl.when(e + W_SLOTS - 1 < pl.num_programs(0))
    def _():
        for cp in weights(e + W_SLOTS - 1):
            cp.start()

    @pl.when(count_ref[e] > 0)
    def _():
        slot = e % W_SLOTS
        wgu_ref[:, :D_EXPERT] = _bf(wgbuf[slot])
        wgu_ref[:, D_EXPERT:] = _bf(wubuf[slot])
        wdn_ref[...] = _bf(wdbuf[slot])

    def block(t, carry):
        b = first_ref[e] + t
        slot = b % ROW_SLOTS
        fetch(b).wait()

        @pl.when(b + ahead < nused)
        def _():
            fetch(b + ahead).start()

        words = _load_row_tiles(xbuf, (slot,), BLK)
        x = jnp.concatenate([_bf(_unpack_lo(w)) for w in words] + [_bf(_unpack_hi(w)) for w in words], axis=1)
        gu = _dot(x, wgu_ref[...])
        g, u = gu[:, :D_EXPERT], gu[:, D_EXPERT:]
        y = _dot(_bf(g * _sigmoid(g) * u), wdn_ref[...])

        @pl.when(b >= ROW_SLOTS)
        def _():
            writeback(b - ROW_SLOTS).wait()

        _store_row_tiles(ybuf, (slot,), y)
        writeback(b).start()
        return carry

    lax.fori_loop(0, count_ref[e], block, 0)

    @pl.when(e == pl.num_programs(0) - 1)
    def _():
        for back in range(1, ROW_SLOTS + 1):
            @pl.when(nused >= back)
            def _():
                writeback(nused - back).wait()


def _experts(xs, first_blk, n_blk, nused, w_gate, w_up, w_down):
    blk = (BLK // SUBLANES, PACK_TILES, SUBLANES, LANES)
    grid_spec = pltpu.PrefetchScalarGridSpec(
        num_scalar_prefetch=3,
        grid=(N_EXPERTS,),
        in_specs=[pl.BlockSpec(memory_space=pl.ANY)] * 4,
        out_specs=pl.BlockSpec(memory_space=pl.ANY),
        scratch_shapes=[pltpu.VMEM((ROW_SLOTS,) + blk, jnp.int32),
                        pltpu.VMEM((ROW_SLOTS,) + blk, jnp.int32),
                        pltpu.VMEM((W_SLOTS, D_MODEL, D_EXPERT), jnp.float32),
                        pltpu.VMEM((W_SLOTS, D_MODEL, D_EXPERT), jnp.float32),
                        pltpu.VMEM((W_SLOTS, D_EXPERT, D_MODEL), jnp.float32),
                        pltpu.VMEM((D_MODEL, 2 * D_EXPERT), jnp.bfloat16),
                        pltpu.VMEM((D_EXPERT, D_MODEL), jnp.bfloat16),
                        pltpu.SemaphoreType.DMA((ROW_SLOTS,)),
                        pltpu.SemaphoreType.DMA((ROW_SLOTS,)),
                        pltpu.SemaphoreType.DMA((W_SLOTS,))],
    )
    return pl.pallas_call(
        _expert_body,
        grid_spec=grid_spec,
        out_shape=jax.ShapeDtypeStruct(xs.shape, jnp.int32),
        compiler_params=pltpu.CompilerParams(dimension_semantics=("arbitrary",),
                                             vmem_limit_bytes=VMEM_LIMIT),
        name="experts",
    )(first_blk, n_blk, nused, xs, w_gate, w_up, w_down)


def _final_body(yg_ref, gate_ref, base_ref, g2_ref, b2_ref, *rest):
    o_ref = rest[-1]
    t = base_ref.shape[0]
    gate = gate_ref[...]
    lo = [None] * PACK_TILES
    hi = [None] * PACK_TILES
    for k in range(TOP_K):
        gk = gate[:, k:k + 1]
        for lt, w in enumerate(_load_row_tiles(yg_ref, (k,), t)):
            a, b = gk * _unpack_lo(w), gk * _unpack_hi(w)
            lo[lt] = a if k == 0 else lo[lt] + a
            hi[lt] = b if k == 0 else hi[lt] + b
    ff = jnp.concatenate(lo + hi, axis=1)
    o_ref[...] = _layernorm(base_ref[...] + ff, g2_ref[...], b2_ref[...])


def _final(yg, gates, base, g2, b2, chunk, prev_out):
    n = base.shape[0]
    steps = yg.shape[1] * SUBLANES // T_FIN
    off = chunk * steps
    full = lambda a: pl.BlockSpec(a.shape, lambda i: (0,) * a.ndim)
    in_specs = [pl.BlockSpec((TOP_K, T_FIN // SUBLANES, PACK_TILES, SUBLANES, LANES),
                             lambda i: (0, i, 0, 0, 0)),
                pl.BlockSpec((T_FIN, TOP_K), lambda i: (off + i, 0)),
                pl.BlockSpec((T_FIN, D_MODEL), lambda i: (off + i, 0)),
                full(g2), full(b2)]
    args = [yg, gates, base, g2, b2]
    aliases = {}
    if prev_out is not None:
        in_specs.append(pl.BlockSpec(memory_space=pl.ANY))
        aliases = {len(args): 0}
        args.append(prev_out)
    return pl.pallas_call(
        _final_body,
        grid=(steps,),
        in_specs=in_specs,
        out_specs=pl.BlockSpec((T_FIN, D_MODEL), lambda i: (off + i, 0)),
        out_shape=jax.ShapeDtypeStruct((n, D_MODEL), jnp.float32),
        input_output_aliases=aliases,
        compiler_params=pltpu.CompilerParams(dimension_semantics=("arbitrary",),
                                             vmem_limit_bytes=VMEM_LIMIT),
        name="final",
    )(*args)


def _rope_tables(pos):
    half = QK_ROPE // 2
    inv = ROPE_THETA ** (-jnp.arange(0, QK_ROPE, 2, dtype=jnp.float32) / QK_ROPE)
    ang = pos.astype(jnp.float32)[:, None] * inv[None, :]
    cos, sin = jnp.cos(ang), jnp.sin(ang)
    n = pos.shape[0]
    z = lambda w: jnp.zeros((n, w), jnp.float32)
    c = jnp.concatenate([jnp.ones((n, QK_NOPE), jnp.float32), cos, cos, z(LANES - QK_NOPE - QK_ROPE)], axis=1)
    s_up = jnp.concatenate([z(QK_NOPE + half), sin, z(LANES - QK_NOPE - QK_ROPE)], axis=1)
    s_dn = jnp.concatenate([z(QK_NOPE), -sin, z(half), z(LANES - QK_NOPE - QK_ROPE)], axis=1)
    return c, s_up, s_dn


def _pad_heads(w, per_head, keep):
    rows = w.shape[0]
    w = w.reshape(rows, ATTN_HEADS, per_head)[:, :, :keep]
    w = jnp.pad(w, ((0, 0), (0, 0), (0, HEAD_PAD - keep)))
    return w.reshape(rows, ATTN_HEADS * HEAD_PAD)


def _prep_weights(emb_ln_g, emb_ln_b, w_in, q_norm, w_uq, kv_norm, w_ukv, conv_w, attn_out_norm,
                  conv_out_norm, w_o, ln1_g, ln1_b, w_router, router_bias, ws_gate, ws_up, ws_down):
    o1 = Q_LORA
    o2 = o1 + KV_LORA
    o3 = o2 + QK_ROPE
    row = lambda a: a.reshape(1, -1)
    kpe_cols = jnp.pad(w_in[:, o2:o3], ((0, 0), (QK_NOPE, HEAD_PAD - QK_NOPE - QK_ROPE)))
    w_in_p = jnp.concatenate([w_in[:, :o2], kpe_cols, w_in[:, o3:]], axis=1)
    ukv = w_ukv.reshape(KV_LORA, ATTN_HEADS, QK_NOPE + V_HEAD)
    w_uk = jnp.pad(ukv[:, :, :QK_NOPE], ((0, 0), (0, 0), (0, HEAD_PAD - QK_NOPE)))
    w_uv = jnp.pad(ukv[:, :, QK_NOPE:], ((0, 0), (0, 0), (0, HEAD_PAD - V_HEAD)))
    woa = w_o[:D_ATTN]
    return {
        "emb_g": row(emb_ln_g), "emb_b": row(emb_ln_b),
        "w_in": _bf(w_in_p),
        "q_norm": row(q_norm), "w_uq_t": _bf(_pad_heads(w_uq, QK_NOPE + QK_ROPE, QK_NOPE + QK_ROPE).T),
        "kv_norm": row(kv_norm), "w_uk": _bf(w_uk.reshape(KV_LORA, -1)), "w_uv_t": _bf(w_uv.reshape(KV_LORA, -1).T),
        "conv_w": jnp.pad(conv_w, ((0, SUBLANES - CONV_W), (0, 0))), "conv_norm": row(conv_out_norm),
        "attn_norm": row(attn_out_norm), "wo_a": _bf(woa), "wo_c": _bf(w_o[D_ATTN:]),
        "ln1_g": row(ln1_g), "ln1_b": row(ln1_b),
        "ws_gate": _bf(ws_gate), "ws_up": _bf(ws_up), "ws_down": _bf(ws_down),
        "w_router_t": _bf(w_router.T), "router_bias": router_bias.reshape(-1, 1),
    }


def _slot_body(eidx_ref, pos_ref, start_ref, idx_ref, base_ref):
    t = eidx_ref.shape[2]
    shp = (N_GROUPS, GROUP_SIZE, t)
    eio = (lax.broadcasted_iota(jnp.int32, shp, 0) * GROUP_SIZE + lax.broadcasted_iota(jnp.int32, shp, 1))
    start = start_ref[...].reshape(N_GROUPS, GROUP_SIZE, 1)
    for k in range(TOP_K):
        hit = eio == eidx_ref[0, k:k + 1, :].reshape(1, 1, t)
        st = jnp.sum(jnp.sum(jnp.where(hit, start, 0.0), axis=0, keepdims=True), axis=1, keepdims=True)
        d = st.reshape(1, t).astype(jnp.int32) + pos_ref[0, k:k + 1, :]
        base_ref[k:k + 1, :] = (d // SUBLANES) * (PACK_TILES * SUBLANES) + d % SUBLANES

    j = lax.broadcasted_iota(jnp.int32, (TOP_K, LANES), 1)
    per_list = LANES // (PACK_TILES * SUBLANES)
    lane_tile = (j // SUBLANES) % PACK_TILES * SUBLANES
    for c in range(t // SC_ROWS):
        lane0 = c * SC_ROWS
        src = base_ref[:, lane0 // LANES * LANES:(lane0 // LANES + 1) * LANES]
        for g in range(SC_LISTS):
            tok = lane0 % LANES + (g * per_list + j // (PACK_TILES * SUBLANES)) * SUBLANES + j % SUBLANES
            idx_ref[c, g * TOP_K:(g + 1) * TOP_K, :] = jnp.take_along_axis(src, tok, axis=1) + lane_tile


def _slots(eidx, pos, start):
    nt, _, t = eidx.shape
    per_tile = t // SC_ROWS
    k_blk = pl.BlockSpec((1, TOP_K, t), lambda i: (i, 0, 0))
    return pl.pallas_call(
        _slot_body,
        grid=(nt,),
        in_specs=[k_blk, k_blk, pl.BlockSpec(start.shape, lambda i: (0, 0))],
        out_specs=pl.BlockSpec((per_tile, SC_LISTS * TOP_K, LANES), lambda i: (i, 0, 0)),
        out_shape=jax.ShapeDtypeStruct((nt * per_tile, SC_LISTS * TOP_K, LANES), jnp.int32),
        scratch_shapes=[pltpu.VMEM((TOP_K, t), jnp.int32)],
        compiler_params=pltpu.CompilerParams(dimension_semantics=("arbitrary",)),
        name="slots",
    )(eidx, pos, start)


def _slot_plan(eidx, pos, counts):
    counts = counts.reshape(-1).astype(jnp.int32)
    padded = (counts + BLK - 1) // BLK * BLK
    pad_end = jnp.cumsum(padded)
    pad_start = pad_end - padded
    idx = _slots(eidx, pos, pad_start.astype(jnp.float32).reshape(-1, 1))
    nb = eidx.size // BLK + N_EXPERTS
    nused = (pad_end[-1:] // BLK).astype(jnp.int32)
    return idx, (pad_start // BLK).astype(jnp.int32), (padded // BLK).astype(jnp.int32), nused, nb * BLK


def kernel(x, meta_tokens, emb_ln_g, emb_ln_b, w_in, q_norm, w_uq, kv_norm, w_ukv, conv_w, attn_out_norm, conv_out_norm, w_o, ln1_g, ln1_b, w_router, router_bias, w_gate, w_up, w_down, ws_gate, ws_up, ws_down, ln2_g, ln2_b):
    bsz, seq, _ = x.shape
    n = bsz * seq
    w = _prep_weights(emb_ln_g, emb_ln_b, w_in[0], q_norm[0], w_uq[0], kv_norm[0], w_ukv[0], conv_w[0],
                      attn_out_norm[0], conv_out_norm[0], w_o[0], ln1_g[0], ln1_b[0], w_router[0],
                      router_bias[0], ws_gate[0], ws_up[0], ws_down[0])

    meta_out = _inproj(meta_tokens[None], _rope_tables(jnp.arange(N_META)),
                       jnp.zeros((SUBLANES, D_CONV), jnp.float32), w, N_META)
    km = jnp.pad(meta_out[1][0], ((0, META_PAD - N_META), (0, 0)))
    vm = jnp.pad(meta_out[2][0], ((0, 0), (0, META_PAD - N_META)))
    vc_meta = meta_out[4][0]

    q, k, v, cn, _ = _inproj(x, _rope_tables(N_META + jnp.arange(seq)), vc_meta, w, T_IN)
    attn = _attention(q, k, v, km, vm)

    hp, base, eidx, gates, pos, counts = _mix_route(
        attn.reshape(n, D_ATTN), cn.reshape(n, D_CONV), x.reshape(n, D_MODEL), w)
    idx, first_blk, n_blk, nused, n_slots = _slot_plan(eidx, pos, counts)
    tiled = lambda a, lead: a.reshape(lead + (-1, PACK_TILES, SUBLANES, LANES))

    xs = _dispatch(hp.reshape(-1, LANES), idx, n_slots * PACK_TILES)
    ys = _experts(tiled(xs, ()), first_blk, n_blk, nused, w_gate[0], w_up[0], w_down[0])
    gates_t = gates.transpose(0, 2, 1).reshape(n, TOP_K)
    g2, b2 = ln2_g[0].reshape(1, -1), ln2_b[0].reshape(1, -1)
    ys_flat = ys.reshape(-1, LANES)
    per = idx.shape[0] // OUT_CHUNKS
    out = None
    for c in range(OUT_CHUNKS):
        yg = tiled(_combine(ys_flat, idx[c * per:(c + 1) * per], n // OUT_CHUNKS * PACK_TILES), (TOP_K,))
        out = _final(yg, gates_t, base, g2, b2, c, out)
    return out.reshape(bsz, seq, D_MODEL)
```

```python
import functools
import math

import jax
import jax.numpy as jnp
from jax import lax
from jax.experimental import pallas as pl
from jax.experimental.pallas import tpu as pltpu
from jax.experimental.pallas import tpu_sc as plsc

D_MODEL = 1024
N_META = 16
ATTN_HEADS = 8
QK_NOPE = 64
QK_ROPE = 32
V_HEAD = 64
Q_LORA = 384
KV_LORA = 256
ROPE_THETA = 10000.0
D_ATTN = ATTN_HEADS * V_HEAD
D_CONV = 512
CONV_W = 3
ATTN_SCALE = 1.0 / math.sqrt(QK_NOPE + QK_ROPE)
N_EXPERTS = 256
TOP_K = 8
N_GROUPS = 8
GROUP_SIZE = N_EXPERTS // N_GROUPS
TOPK_GROUPS = 4
D_EXPERT = 256
ROUTED_SCALE = 2.5
DEPTH = 1
ALPHA = (2.0 * DEPTH) ** 0.25

LANES = 128
SUBLANES = 8
HEAD_PAD = LANES
META_PAD = LANES
HALF = D_MODEL // 2
PACK_TILES = HALF // LANES
BF16_BITS = 16
HI_MASK = -(1 << BF16_BITS)
SC_CORES = 2
SC_SUBCORES = 16

T_IN = 512
T_Q = 1024
DIAG_PARTS = 4
SHIFT_MARGIN = 40.0
T_MIX = 512
BLK = 256
ROW_SLOTS = 4
W_SLOTS = 3
W_PARTS = 2
T_FIN = 256
OUT_CHUNKS = 4
SC_ROWS = 64
SC_PIECES = SC_ROWS * PACK_TILES
SC_LISTS = SC_PIECES // LANES

VMEM_LIMIT = 56 * 1024 * 1024

_NEG_INF = float("-inf")


def _bf(x):
    return x.astype(jnp.bfloat16)


def _dot(a, b):
    return jnp.dot(a, b, preferred_element_type=jnp.float32)


def _dot_t(a, b):
    return lax.dot_general(a, b, (((1,), (1,)), ((), ())), preferred_element_type=jnp.float32)


def _layernorm(x, g, b, eps=1e-5):
    mu = jnp.mean(x, axis=-1, keepdims=True)
    xc = x - mu
    var = jnp.mean(xc * xc, axis=-1, keepdims=True)
    return xc * lax.rsqrt(var + eps) * g + b


def _rms(x, g, n, eps=1e-6):
    ms = jnp.sum(x * x, axis=-1, keepdims=True) * (1.0 / n)
    return x * lax.rsqrt(ms + eps) * g


def _sigmoid(x):
    return 1.0 / (1.0 + jnp.exp(-x))


def _rope(x, c, s_up, s_dn):
    return x * c + pltpu.roll(x, QK_ROPE // 2, 1) * s_up + pltpu.roll(x, LANES - QK_ROPE // 2, 1) * s_dn


def _rope_t(x, c, s_up, s_dn):
    return x * c + pltpu.roll(x, QK_ROPE // 2, 0) * s_up + pltpu.roll(x, LANES - QK_ROPE // 2, 0) * s_dn


def _inproj_body(x_ref, g_ref, b_ref, win_ref, qn_ref, wuq_ref, kvn_ref, wuk_ref, wuv_ref, cw_ref, con_ref,
                 tc_ref, tu_ref, td_ref, tct_ref, tut_ref, tdt_ref, vcinit_ref,
                 q_ref, k_ref, v_ref, cn_ref, vct_ref, carry_ref):
    rows = x_ref.shape[1]

    @pl.when(pl.program_id(1) == 0)
    def _():
        carry_ref[...] = vcinit_ref[...]

    h0 = _layernorm(x_ref[0], g_ref[...], b_ref[...])
    z = _dot(_bf(h0), win_ref[...])
    o1 = Q_LORA
    o2 = o1 + KV_LORA
    o3 = o2 + HEAD_PAD
    o4 = o3 + D_CONV
    o5 = o4 + D_CONV
    c, s_up, s_dn = tc_ref[...], tu_ref[...], td_ref[...]

    ct, st_up, st_dn = tct_ref[...], tut_ref[...], tdt_ref[...]
    q_t = _dot_t(wuq_ref[...], _bf(_rms(z[:, :o1], qn_ref[...], Q_LORA)))
    for h in range(ATTN_HEADS):
        sl = slice(h * HEAD_PAD, (h + 1) * HEAD_PAD)
        q_ref[0, sl, :] = _bf(_rope_t(q_t[sl, :], ct, st_up, st_dn) * ATTN_SCALE)

    ckv = _bf(_rms(z[:, o1:o2], kvn_ref[...], KV_LORA))
    kk = _dot(ckv, wuk_ref[...])
    kpe = _rope(z[:, o2:o3], c, s_up, s_dn)
    for h in range(ATTN_HEADS):
        sl = slice(h * HEAD_PAD, (h + 1) * HEAD_PAD)
        k_ref[0, :, sl] = _bf(kk[:, sl] + kpe)
    v_t = _dot_t(wuv_ref[...], ckv)
    dim = lax.broadcasted_iota(jnp.int32, (HEAD_PAD, rows), 0)
    for h in range(ATTN_HEADS):
        sl = slice(h * HEAD_PAD, (h + 1) * HEAD_PAD)
        v_ref[0, sl, :] = _bf(jnp.where(dim < V_HEAD, v_t[sl, :], 1.0))

    vc = z[:, o4:o5] * z[:, o5:]
    carry = carry_ref[...]
    row = lax.broadcasted_iota(jnp.int32, (rows, D_CONV), 0)
    vc1 = jnp.where(row == 0, carry[SUBLANES - 1:SUBLANES, :], pltpu.roll(vc, 1, 0))
    vc2 = pltpu.roll(vc, 2, 0)
    vc2 = jnp.where(row == 0, carry[SUBLANES - 2:SUBLANES - 1, :], vc2)
    vc2 = jnp.where(row == 1, carry[SUBLANES - 1:SUBLANES, :], vc2)
    cw = cw_ref[...]
    conv = z[:, o3:o4] * (cw[0:1, :] * vc2 + cw[1:2, :] * vc1 + cw[2:3, :] * vc)
    cn_ref[0] = _bf(_rms(conv, con_ref[...], D_CONV))
    tail = vc[rows - SUBLANES:, :]
    carry_ref[...] = tail
    vct_ref[0] = tail


def _inproj(x, tabs, vcinit, w, rows):
    bsz, seq, _ = x.shape
    nj = seq // rows
    full = lambda a: pl.BlockSpec(a.shape, lambda b, j: (0,) * a.ndim)
    row_blk = lambda width: pl.BlockSpec((1, rows, width), lambda b, j: (b, j, 0))
    tab = pl.BlockSpec((rows, HEAD_PAD), lambda b, j: (j, 0))
    tab_t = pl.BlockSpec((HEAD_PAD, rows), lambda b, j: (0, j))
    col_blk = pl.BlockSpec((1, ATTN_HEADS * HEAD_PAD, rows), lambda b, j: (b, 0, j))
    params = [w["emb_g"], w["emb_b"], w["w_in"], w["q_norm"], w["w_uq_t"], w["kv_norm"], w["w_uk"], w["w_uv_t"],
              w["conv_w"], w["conv_norm"]]
    wide = ATTN_HEADS * HEAD_PAD
    out_shape = [
        jax.ShapeDtypeStruct((bsz, wide, seq), jnp.bfloat16),
        jax.ShapeDtypeStruct((bsz, seq, wide), jnp.bfloat16),
        jax.ShapeDtypeStruct((bsz, wide, seq), jnp.bfloat16),
        jax.ShapeDtypeStruct((bsz, seq, D_CONV), jnp.bfloat16),
        jax.ShapeDtypeStruct((bsz, nj * SUBLANES, D_CONV), jnp.float32),
    ]
    tabs_t = [t.T for t in tabs]
    return pl.pallas_call(
        _inproj_body,
        grid=(bsz, nj),
        in_specs=([row_blk(D_MODEL)] + [full(p) for p in params] + [tab, tab, tab, tab_t, tab_t, tab_t]
                  + [full(vcinit)]),
        out_specs=[col_blk, row_blk(wide), col_blk, row_blk(D_CONV),
                   pl.BlockSpec((1, SUBLANES, D_CONV), lambda b, j: (b, j, 0))],
        out_shape=out_shape,
        scratch_shapes=[pltpu.VMEM((SUBLANES, D_CONV), jnp.float32)],
        compiler_params=pltpu.CompilerParams(dimension_semantics=("arbitrary", "arbitrary"),
                                             vmem_limit_bytes=VMEM_LIMIT),
        name="inproj",
    )(x, *params, *tabs, *tabs_t, vcinit)


def _attn_body(q_ref, k_ref, v_ref, km_ref, vm_ref, o_ref):
    i = pl.program_id(2)
    tq = q_ref.shape[2]
    heads = [slice(hh * HEAD_PAD, (hh + 1) * HEAD_PAD) for hh in range(2)]
    qs = [q_ref[0, sl, :] for sl in heads]

    part = tq // DIAG_PARTS
    is_meta = lax.broadcasted_iota(jnp.int32, (META_PAD, tq), 0) < N_META
    lead = lambda a, off, tail: tail if off == 0 else jnp.concatenate([a[:, :off], tail], axis=1)
    carry = []
    for q, sl in zip(qs, heads):
        sm = jnp.where(is_meta, _dot(km_ref[:, sl], q), _NEG_INF)
        m = jnp.max(sm, axis=0, keepdims=True)
        scores = []
        for c in range(DIAG_PARTS):
            off = c * part
            rows = pl.ds(pl.multiple_of(i * tq + off, part), part)
            causal = (lax.broadcasted_iota(jnp.int32, (part, tq - off), 0)
                      <= lax.broadcasted_iota(jnp.int32, (part, tq - off), 1))
            s = jnp.where(causal, _dot(k_ref[0, rows, sl], q[:, off:]), _NEG_INF)
            m = lead(m, off, jnp.maximum(m[:, off:], jnp.max(s, axis=0, keepdims=True)))
            scores.append((rows, off, s))
        acc = _dot(vm_ref[sl, :], _bf(jnp.exp(sm - m)))
        for rows, off, s in scores:
            acc = lead(acc, off, acc[:, off:] + _dot(v_ref[0, sl, rows], _bf(jnp.exp(s - m[:, off:]))))
        carry += [m, acc]

    def body(j, carry):
        rj = pl.multiple_of(j * tq, tq)
        ms, accs = (carry[0], carry[2]), (carry[1], carry[3])
        fast, tops = [], []
        for q, sl, m, acc in zip(qs, heads, ms, accs):
            s = _dot(k_ref[0, pl.ds(rj, tq), sl], q)
            tops.append(jnp.max(s, axis=0, keepdims=True))
            fast.append(acc + _dot(v_ref[0, sl, pl.ds(rj, tq)], _bf(jnp.exp(s - m))))
        excess = jnp.maximum(tops[0] - ms[0], tops[1] - ms[1])
        over = jnp.max(excess, axis=1, keepdims=True)[0, 0] > SHIFT_MARGIN

        def rescale(_):
            out = []
            for q, sl, m, acc, top in zip(qs, heads, ms, accs, tops):
                s = _dot(k_ref[0, pl.ds(rj, tq), sl], q)
                m_new = jnp.maximum(m, top)
                out += [m_new, jnp.exp(m - m_new) * acc + _dot(v_ref[0, sl, pl.ds(rj, tq)], _bf(jnp.exp(s - m_new)))]
            return tuple(out)

        return lax.cond(over, rescale, lambda _: (ms[0], fast[0], ms[1], fast[1]), 0)

    _, a0, _, a1 = lax.fori_loop(0, i, body, tuple(carry))
    o_t = jnp.concatenate([a[:V_HEAD, :] / a[V_HEAD:V_HEAD + 1, :] for a in (a0, a1)], axis=0)
    o_ref[0] = _bf(o_t.T)


def _attention(q_t, k, v_t, km, vm_t):
    bsz, seq, _ = k.shape
    pairs = ATTN_HEADS // 2
    pw = 2 * HEAD_PAD
    return pl.pallas_call(
        _attn_body,
        grid=(bsz, pairs, seq // T_Q),
        in_specs=[pl.BlockSpec((1, pw, T_Q), lambda b, p, i: (b, p, i)),
                  pl.BlockSpec((1, seq, pw), lambda b, p, i: (b, 0, p)),
                  pl.BlockSpec((1, pw, seq), lambda b, p, i: (b, p, 0)),
                  pl.BlockSpec((META_PAD, pw), lambda b, p, i: (0, p)),
                  pl.BlockSpec((pw, META_PAD), lambda b, p, i: (p, 0))],
        out_specs=pl.BlockSpec((1, T_Q, 2 * V_HEAD), lambda b, p, i: (b, i, p)),
        out_shape=jax.ShapeDtypeStruct((bsz, seq, D_ATTN), jnp.bfloat16),
        compiler_params=pltpu.CompilerParams(dimension_semantics=("arbitrary",) * 3,
                                             vmem_limit_bytes=VMEM_LIMIT),
        name="attention",
    )(q_t, k, v_t, km, vm_t)


def _mix_body(attn_ref, cn_ref, x_ref, eg_ref, eb_ref, an_ref, woa_ref, woc_ref, g1_ref, b1_ref,
              wsg_ref, wsu_ref, wsd_ref, wrt_ref, rb_ref, tri_ref,
              hp_ref, base_ref, eidx_ref, gate_ref, pos_ref, cnt_ref, run_ref):
    t = attn_ref.shape[0]

    @pl.when(pl.program_id(0) == 0)
    def _():
        run_ref[...] = jnp.zeros_like(run_ref)

    attn_n = _rms(attn_ref[...].astype(jnp.float32), an_ref[...], D_ATTN)
    mix = _dot(_bf(attn_n), woa_ref[...]) + _dot(cn_ref[...], woc_ref[...])
    h0 = _layernorm(x_ref[...], eg_ref[...], eb_ref[...])
    h1 = _layernorm(ALPHA * h0 + mix, g1_ref[...], b1_ref[...])
    _store_row_tiles(hp_ref, (), h1)
    h1b = _bf(h1)
    g = _dot(h1b, wsg_ref[...])
    u = _dot(h1b, wsu_ref[...])
    base_ref[...] = ALPHA * h1 + _dot(_bf(g * _sigmoid(g) * u), wsd_ref[...])

    shp = (N_GROUPS, GROUP_SIZE, t)
    sc = _sigmoid(_dot_t(wrt_ref[...], h1b)).reshape(shp)
    sel = sc + rb_ref[...].reshape(N_GROUPS, GROUP_SIZE, 1)
    gio = lax.broadcasted_iota(jnp.int32, shp, 0)
    rio = lax.broadcasted_iota(jnp.int32, shp, 1)
    eio = gio * GROUP_SIZE + rio

    m1 = jnp.max(sel, axis=1, keepdims=True)
    i1 = jnp.min(jnp.where(sel == m1, rio, GROUP_SIZE), axis=1, keepdims=True)
    m2 = jnp.max(jnp.where(rio == i1, _NEG_INF, sel), axis=1, keepdims=True)
    gs = m1 + m2
    gio1 = lax.broadcasted_iota(jnp.int32, (N_GROUPS, 1, t), 0)
    gmask = jnp.zeros((N_GROUPS, 1, t), jnp.bool_)
    for _ in range(TOPK_GROUPS):
        mg = jnp.max(gs, axis=0, keepdims=True)
        ig = jnp.min(jnp.where(gs == mg, gio1, N_GROUPS), axis=0, keepdims=True)
        hit = gio1 == ig
        gmask = jnp.logical_or(gmask, hit)
        gs = jnp.where(hit, _NEG_INF, gs)

    cand = jnp.where(gmask, sel, _NEG_INF)

    def red(x, op):
        return op(op(x, axis=0, keepdims=True), axis=1, keepdims=True)

    pm = jnp.zeros(shp, jnp.float32)
    e_rows, g_rows = [], []
    for _ in range(TOP_K):
        mk = red(cand, jnp.max)
        ek = red(jnp.where(cand == mk, eio, N_EXPERTS), jnp.min)
        hit = eio == ek
        g_rows.append(red(jnp.where(hit, sc, 0.0), jnp.sum))
        e_rows.append(ek)
        pm = jnp.where(hit, 1.0, pm)
        cand = jnp.where(hit, _NEG_INF, cand)
    gsum = g_rows[0]
    for gk in g_rows[1:]:
        gsum = gsum + gk

    pm = pm.reshape(N_EXPERTS, t)
    rank = _dot(_bf(pm), tri_ref[...])
    slot = (run_ref[...] + rank).reshape(shp)
    run_ref[...] = run_ref[...] + jnp.sum(pm, axis=1, keepdims=True)
    cnt_ref[...] = run_ref[...]
    for k in range(TOP_K):
        eidx_ref[0, k:k + 1, :] = e_rows[k].reshape(1, t)
        gate_ref[0, k:k + 1, :] = (g_rows[k] / gsum * ROUTED_SCALE).reshape(1, t)
        pk = red(jnp.where(eio == e_rows[k], slot, 0.0), jnp.sum)
        pos_ref[0, k:k + 1, :] = pk.reshape(1, t).astype(jnp.int32)


def _mix_route(attn, cn, x, w):
    n = attn.shape[0]
    nt = n // T_MIX
    tri = _bf(jnp.triu(jnp.ones((T_MIX, T_MIX), jnp.float32), 1))
    params = [w["emb_g"], w["emb_b"], w["attn_norm"], w["wo_a"], w["wo_c"], w["ln1_g"], w["ln1_b"], w["ws_gate"], w["ws_up"],
              w["ws_down"], w["w_router_t"], w["router_bias"], tri]
    full = lambda a: pl.BlockSpec(a.shape, lambda i: (0,) * a.ndim)
    row_blk = lambda width: pl.BlockSpec((T_MIX, width), lambda i: (i, 0))
    k_blk = pl.BlockSpec((1, TOP_K, T_MIX), lambda i: (i, 0, 0))
    out_shape = [
        jax.ShapeDtypeStruct((n // SUBLANES, PACK_TILES, SUBLANES, LANES), jnp.int32),
        jax.ShapeDtypeStruct((n, D_MODEL), jnp.float32),
        jax.ShapeDtypeStruct((nt, TOP_K, T_MIX), jnp.int32),
        jax.ShapeDtypeStruct((nt, TOP_K, T_MIX), jnp.float32),
        jax.ShapeDtypeStruct((nt, TOP_K, T_MIX), jnp.int32),
        jax.ShapeDtypeStruct((N_EXPERTS, 1), jnp.float32),
    ]
    return pl.pallas_call(
        _mix_body,
        grid=(nt,),
        in_specs=[row_blk(D_ATTN), row_blk(D_CONV), row_blk(D_MODEL)] + [full(p) for p in params],
        out_specs=[pl.BlockSpec((T_MIX // SUBLANES, PACK_TILES, SUBLANES, LANES), lambda i: (i, 0, 0, 0)),
                   row_blk(D_MODEL),
                   k_blk, k_blk, k_blk, pl.BlockSpec((N_EXPERTS, 1), lambda i: (0, 0))],
        out_shape=out_shape,
        scratch_shapes=[pltpu.VMEM((N_EXPERTS, 1), jnp.float32)],
        compiler_params=pltpu.CompilerParams(dimension_semantics=("arbitrary",),
                                             vmem_limit_bytes=VMEM_LIMIT),
        name="mix_route",
    )(attn, cn, x, *params)


def _pack_row_tiles(x):
    bits = lambda a: lax.bitcast_convert_type(_bf(a).astype(jnp.float32), jnp.int32)
    out = []
    for lt in range(PACK_TILES):
        lo = bits(x[:, lt * LANES:(lt + 1) * LANES])
        hi = bits(x[:, HALF + lt * LANES:HALF + (lt + 1) * LANES])
        out.append(lax.shift_right_logical(lo, BF16_BITS) | (hi & HI_MASK))
    return out


def _unpack_lo(w):
    return lax.bitcast_convert_type(lax.shift_left(w, BF16_BITS), jnp.float32)


def _unpack_hi(w):
    return lax.bitcast_convert_type(w & HI_MASK, jnp.float32)


def _store_row_tiles(ref, lead, x):
    rows = x.shape[0]
    for lt, w in enumerate(_pack_row_tiles(x)):
        ref[lead + (slice(None), lt)] = w.reshape(rows // SUBLANES, SUBLANES, LANES)


def _load_row_tiles(ref, lead, rows):
    return [ref[lead + (slice(None), lt)].reshape(rows, LANES) for lt in range(PACK_TILES)]


def _sc_mesh():
    return plsc.VectorSubcoreMesh(core_axis_name="c", subcore_axis_name="s")


def _worker_id():
    return lax.axis_index("s") * SC_CORES + lax.axis_index("c")


def _dispatch(hp, idx, n_rows):
    per_worker = hp.shape[0] // SC_PIECES // (SC_CORES * SC_SUBCORES)
    lists = idx.shape[1]

    @functools.partial(
        pl.kernel, mesh=_sc_mesh(),
        out_type=jax.ShapeDtypeStruct((n_rows, LANES), jnp.int32),
        scratch_types=[pltpu.VMEM((lists, LANES), jnp.int32),
                       pltpu.VMEM((SC_PIECES, LANES), jnp.int32),
                       pltpu.SemaphoreType.DMA],
        name="dispatch")
    def run(hp_hbm, idx_hbm, xs_hbm, idx_v, rows_v, sem):
        first = _worker_id() * per_worker

        @pl.loop(0, per_worker)
        def _(c):
            chunk = first + c
            pltpu.sync_copy(idx_hbm.at[chunk], idx_v)
            pltpu.sync_copy(hp_hbm.at[pl.ds(chunk * SC_PIECES, SC_PIECES)], rows_v)
            copies = [pltpu.async_copy(rows_v.at[pl.ds((j // TOP_K) * LANES, LANES)],
                                       xs_hbm.at[idx_v.at[j]], sem) for j in range(lists)]
            for cp in copies:
                cp.wait()

    return run(hp, idx)


def _combine(ys, idx, n_rows):
    per_worker = n_rows // SC_PIECES // (SC_CORES * SC_SUBCORES)
    lists = idx.shape[1]

    @functools.partial(
        pl.kernel, mesh=_sc_mesh(),
        out_type=jax.ShapeDtypeStruct((TOP_K, n_rows, LANES), jnp.int32),
        scratch_types=[pltpu.VMEM((lists, LANES), jnp.int32),
                       pltpu.VMEM((2, SC_PIECES, LANES), jnp.int32),
                       pltpu.SemaphoreType.DMA, pltpu.SemaphoreType.DMA],
        name="combine")
    def run(ys_hbm, idx_hbm, out_hbm, idx_v, rows_v, sem0, sem1):
        first = _worker_id() * per_worker
        sems = (sem0, sem1)

        def gather(k):
            buf = rows_v.at[k % 2]
            return [pltpu.async_copy(ys_hbm.at[idx_v.at[g * TOP_K + k]],
                                     buf.at[pl.ds(g * LANES, LANES)], sems[k % 2]) for g in range(SC_LISTS)]

        @pl.loop(0, per_worker)
        def _(c):
            chunk = first + c
            pltpu.sync_copy(idx_hbm.at[chunk], idx_v)
            pending = gather(0)
            for k in range(TOP_K):
                for cp in pending:
                    cp.wait()
                if k + 1 < TOP_K:
                    pending = gather(k + 1)
                pltpu.sync_copy(rows_v.at[k % 2], out_hbm.at[k, pl.ds(chunk * SC_PIECES, SC_PIECES)])

    return run(ys, idx)


def _expert_body(first_ref, count_ref, nused_ref, xs_hbm, wg_hbm, wu_hbm, wd_hbm, ys_hbm,
                 xbuf, ybuf, wgbuf, wubuf, wdbuf, wgu_ref, wdn_ref, sem_in, sem_out, sem_w):
    e = pl.program_id(0)
    nused = nused_ref[0]
    tiles = BLK // SUBLANES
    ahead = ROW_SLOTS - 1

    def fetch(b):
        slot = b % ROW_SLOTS
        return pltpu.make_async_copy(xs_hbm.at[pl.ds(b * tiles, tiles)], xbuf.at[slot], sem_in.at[slot])

    def writeback(b):
        slot = b % ROW_SLOTS
        return pltpu.make_async_copy(ybuf.at[slot], ys_hbm.at[pl.ds(b * tiles, tiles)], sem_out.at[slot])

    def weights(ex):
        slot = ex % W_SLOTS
        copies = []
        for src, dst in ((wg_hbm, wgbuf), (wu_hbm, wubuf), (wd_hbm, wdbuf)):
            part = src.shape[1] // W_PARTS
            for p in range(W_PARTS):
                rows = pl.ds(p * part, part)
                copies.append(pltpu.make_async_copy(src.at[ex, rows], dst.at[slot, rows], sem_w.at[slot]))
        return copies

    @pl.when(e == 0)
    def _():
        for b in range(ahead):
            @pl.when(b < nused)
            def _():
                fetch(b).start()
        for ex in range(W_SLOTS - 1):
            for cp in weights(ex):
                cp.start()

    for cp in weights(e):
        cp.wait()

    @pl.when(e + W_SLOTS - 1 < pl.num_programs(0))
    def _():
        for cp in weights(e + W_SLOTS - 1):
            cp.start()

    @pl.when(count_ref[e] > 0)
    def _():
        slot = e % W_SLOTS
        wgu_ref[:, :D_EXPERT] = _bf(wgbuf[slot])
        wgu_ref[:, D_EXPERT:] = _bf(wubuf[slot])
        wdn_ref[...] = _bf(wdbuf[slot])

    def block(t, carry):
        b = first_ref[e] + t
        slot = b % ROW_SLOTS
        fetch(b).wait()

        @pl.when(b + ahead < nused)
        def _():
            fetch(b + ahead).start()

        words = _load_row_tiles(xbuf, (slot,), BLK)
        x = jnp.concatenate([_bf(_unpack_lo(w)) for w in words] + [_bf(_unpack_hi(w)) for w in words], axis=1)
        gu = _dot(x, wgu_ref[...])
        g, u = gu[:, :D_EXPERT], gu[:, D_EXPERT:]
        y = _dot(_bf(g * _sigmoid(g) * u), wdn_ref[...])

        @pl.when(b >= ROW_SLOTS)
        def _():
            writeback(b - ROW_SLOTS).wait()

        _store_row_tiles(ybuf, (slot,), y)
        writeback(b).start()
        return carry

    lax.fori_loop(0, count_ref[e], block, 0)

    @pl.when(e == pl.num_programs(0) - 1)
    def _():
        for back in range(1, ROW_SLOTS + 1):
            @pl.when(nused >= back)
            def _():
                writeback(nused - back).wait()


def _experts(xs, first_blk, n_blk, nused, w_gate, w_up, w_down):
    blk = (BLK // SUBLANES, PACK_TILES, SUBLANES, LANES)
    grid_spec = pltpu.PrefetchScalarGridSpec(
        num_scalar_prefetch=3,
        grid=(N_EXPERTS,),
        in_specs=[pl.BlockSpec(memory_space=pl.ANY)] * 4,
        out_specs=pl.BlockSpec(memory_space=pl.ANY),
        scratch_shapes=[pltpu.VMEM((ROW_SLOTS,) + blk, jnp.int32),
                        pltpu.VMEM((ROW_SLOTS,) + blk, jnp.int32),
                        pltpu.VMEM((W_SLOTS, D_MODEL, D_EXPERT), jnp.float32),
                        pltpu.VMEM((W_SLOTS, D_MODEL, D_EXPERT), jnp.float32),
                        pltpu.VMEM((W_SLOTS, D_EXPERT, D_MODEL), jnp.float32),
                        pltpu.VMEM((D_MODEL, 2 * D_EXPERT), jnp.bfloat16),
                        pltpu.VMEM((D_EXPERT, D_MODEL), jnp.bfloat16),
                        pltpu.SemaphoreType.DMA((ROW_SLOTS,)),
                        pltpu.SemaphoreType.DMA((ROW_SLOTS,)),
                        pltpu.SemaphoreType.DMA((W_SLOTS,))],
    )
    return pl.pallas_call(
        _expert_body,
        grid_spec=grid_spec,
        out_shape=jax.ShapeDtypeStruct(xs.shape, jnp.int32),
        compiler_params=pltpu.CompilerParams(dimension_semantics=("arbitrary",),
                                             vmem_limit_bytes=VMEM_LIMIT),
        name="experts",
    )(first_blk, n_blk, nused, xs, w_gate, w_up, w_down)


def _final_body(yg_ref, gate_ref, base_ref, g2_ref, b2_ref, *rest):
    o_ref = rest[-1]
    t = base_ref.shape[0]
    gate = gate_ref[...]
    lo = [None] * PACK_TILES
    hi = [None] * PACK_TILES
    for k in range(TOP_K):
        gk = gate[:, k:k + 1]
        for lt, w in enumerate(_load_row_tiles(yg_ref, (k,), t)):
            a, b = gk * _unpack_lo(w), gk * _unpack_hi(w)
            lo[lt] = a if k == 0 else lo[lt] + a
            hi[lt] = b if k == 0 else hi[lt] + b
    ff = jnp.concatenate(lo + hi, axis=1)
    o_ref[...] = _layernorm(base_ref[...] + ff, g2_ref[...], b2_ref[...])


def _final(yg, gates, base, g2, b2, chunk, prev_out):
    n = base.shape[0]
    steps = yg.shape[1] * SUBLANES // T_FIN
    off = chunk * steps
    full = lambda a: pl.BlockSpec(a.shape, lambda i: (0,) * a.ndim)
    in_specs = [pl.BlockSpec((TOP_K, T_FIN // SUBLANES, PACK_TILES, SUBLANES, LANES),
                             lambda i: (0, i, 0, 0, 0)),
                pl.BlockSpec((T_FIN, TOP_K), lambda i: (off + i, 0)),
                pl.BlockSpec((T_FIN, D_MODEL), lambda i: (off + i, 0)),
                full(g2), full(b2)]
    args = [yg, gates, base, g2, b2]
    aliases = {}
    if prev_out is not None:
        in_specs.append(pl.BlockSpec(memory_space=pl.ANY))
        aliases = {len(args): 0}
        args.append(prev_out)
    return pl.pallas_call(
        _final_body,
        grid=(steps,),
        in_specs=in_specs,
        out_specs=pl.BlockSpec((T_FIN, D_MODEL), lambda i: (off + i, 0)),
        out_shape=jax.ShapeDtypeStruct((n, D_MODEL), jnp.float32),
        input_output_aliases=aliases,
        compiler_params=pltpu.CompilerParams(dimension_semantics=("arbitrary",),
                                             vmem_limit_bytes=VMEM_LIMIT),
        name="final",
    )(*args)


def _rope_tables(pos):
    half = QK_ROPE // 2
    inv = ROPE_THETA ** (-jnp.arange(0, QK_ROPE, 2, dtype=jnp.float32) / QK_ROPE)
    ang = pos.astype(jnp.float32)[:, None] * inv[None, :]
    cos, sin = jnp.cos(ang), jnp.sin(ang)
    n = pos.shape[0]
    z = lambda w: jnp.zeros((n, w), jnp.float32)
    c = jnp.concatenate([jnp.ones((n, QK_NOPE), jnp.float32), cos, cos, z(LANES - QK_NOPE - QK_ROPE)], axis=1)
    s_up = jnp.concatenate([z(QK_NOPE + half), sin, z(LANES - QK_NOPE - QK_ROPE)], axis=1)
    s_dn = jnp.concatenate([z(QK_NOPE), -sin, z(half), z(LANES - QK_NOPE - QK_ROPE)], axis=1)
    return c, s_up, s_dn


def _pad_heads(w, per_head, keep):
    rows = w.shape[0]
    w = w.reshape(rows, ATTN_HEADS, per_head)[:, :, :keep]
    w = jnp.pad(w, ((0, 0), (0, 0), (0, HEAD_PAD - keep)))
    return w.reshape(rows, ATTN_HEADS * HEAD_PAD)


def _prep_weights(emb_ln_g, emb_ln_b, w_in, q_norm, w_uq, kv_norm, w_ukv, conv_w, attn_out_norm,
                  conv_out_norm, w_o, ln1_g, ln1_b, w_router, router_bias, ws_gate, ws_up, ws_down):
    o1 = Q_LORA
    o2 = o1 + KV_LORA
    o3 = o2 + QK_ROPE
    row = lambda a: a.reshape(1, -1)
    kpe_cols = jnp.pad(w_in[:, o2:o3], ((0, 0), (QK_NOPE, HEAD_PAD - QK_NOPE - QK_ROPE)))
    w_in_p = jnp.concatenate([w_in[:, :o2], kpe_cols, w_in[:, o3:]], axis=1)
    ukv = w_ukv.reshape(KV_LORA, ATTN_HEADS, QK_NOPE + V_HEAD)
    w_uk = jnp.pad(ukv[:, :, :QK_NOPE], ((0, 0), (0, 0), (0, HEAD_PAD - QK_NOPE)))
    w_uv = jnp.pad(ukv[:, :, QK_NOPE:], ((0, 0), (0, 0), (0, HEAD_PAD - V_HEAD)))
    woa = w_o[:D_ATTN]
    return {
        "emb_g": row(emb_ln_g), "emb_b": row(emb_ln_b),
        "w_in": _bf(w_in_p),
        "q_norm": row(q_norm), "w_uq_t": _bf(_pad_heads(w_uq, QK_NOPE + QK_ROPE, QK_NOPE + QK_ROPE).T),
        "kv_norm": row(kv_norm), "w_uk": _bf(w_uk.reshape(KV_LORA, -1)), "w_uv_t": _bf(w_uv.reshape(KV_LORA, -1).T),
        "conv_w": jnp.pad(conv_w, ((0, SUBLANES - CONV_W), (0, 0))), "conv_norm": row(conv_out_norm),
        "attn_norm": row(attn_out_norm), "wo_a": _bf(woa), "wo_c": _bf(w_o[D_ATTN:]),
        "ln1_g": row(ln1_g), "ln1_b": row(ln1_b),
        "ws_gate": _bf(ws_gate), "ws_up": _bf(ws_up), "ws_down": _bf(ws_down),
        "w_router_t": _bf(w_router.T), "router_bias": router_bias.reshape(-1, 1),
    }


def _slot_body(eidx_ref, pos_ref, start_ref, idx_ref, base_ref):
    t = eidx_ref.shape[2]
    shp = (N_GROUPS, GROUP_SIZE, t)
    eio = (lax.broadcasted_iota(jnp.int32, shp, 0) * GROUP_SIZE + lax.broadcasted_iota(jnp.int32, shp, 1))
    start = start_ref[...].reshape(N_GROUPS, GROUP_SIZE, 1)
    for k in range(TOP_K):
        hit = eio == eidx_ref[0, k:k + 1, :].reshape(1, 1, t)
        st = jnp.sum(jnp.sum(jnp.where(hit, start, 0.0), axis=0, keepdims=True), axis=1, keepdims=True)
        d = st.reshape(1, t).astype(jnp.int32) + pos_ref[0, k:k + 1, :]
        base_ref[k:k + 1, :] = (d // SUBLANES) * (PACK_TILES * SUBLANES) + d % SUBLANES

    j = lax.broadcasted_iota(jnp.int32, (TOP_K, LANES), 1)
    per_list = LANES // (PACK_TILES * SUBLANES)
    lane_tile = (j // SUBLANES) % PACK_TILES * SUBLANES
    for c in range(t // SC_ROWS):
        lane0 = c * SC_ROWS
        src = base_ref[:, lane0 // LANES * LANES:(lane0 // LANES + 1) * LANES]
        for g in range(SC_LISTS):
            tok = lane0 % LANES + (g * per_list + j // (PACK_TILES * SUBLANES)) * SUBLANES + j % SUBLANES
            idx_ref[c, g * TOP_K:(g + 1) * TOP_K, :] = jnp.take_along_axis(src, tok, axis=1) + lane_tile


def _slots(eidx, pos, start):
    nt, _, t = eidx.shape
    per_tile = t // SC_ROWS
    k_blk = pl.BlockSpec((1, TOP_K, t), lambda i: (i, 0, 0))
    return pl.pallas_call(
        _slot_body,
        grid=(nt,),
        in_specs=[k_blk, k_blk, pl.BlockSpec(start.shape, lambda i: (0, 0))],
        out_specs=pl.BlockSpec((per_tile, SC_LISTS * TOP_K, LANES), lambda i: (i, 0, 0)),
        out_shape=jax.ShapeDtypeStruct((nt * per_tile, SC_LISTS * TOP_K, LANES), jnp.int32),
        scratch_shapes=[pltpu.VMEM((TOP_K, t), jnp.int32)],
        compiler_params=pltpu.CompilerParams(dimension_semantics=("arbitrary",)),
        name="slots",
    )(eidx, pos, start)


def _slot_plan(eidx, pos, counts):
    counts = counts.reshape(-1).astype(jnp.int32)
    padded = (counts + BLK - 1) // BLK * BLK
    pad_end = jnp.cumsum(padded)
    pad_start = pad_end - padded
    idx = _slots(eidx, pos, pad_start.astype(jnp.float32).reshape(-1, 1))
    nb = eidx.size // BLK + N_EXPERTS
    nused = (pad_end[-1:] // BLK).astype(jnp.int32)
    return idx, (pad_start // BLK).astype(jnp.int32), (padded // BLK).astype(jnp.int32), nused, nb * BLK


def kernel(x, meta_tokens, emb_ln_g, emb_ln_b, w_in, q_norm, w_uq, kv_norm, w_ukv, conv_w, attn_out_norm, conv_out_norm, w_o, ln1_g, ln1_b, w_router, router_bias, w_gate, w_up, w_down, ws_gate, ws_up, ws_down, ln2_g, ln2_b):
    bsz, seq, _ = x.shape
    n = bsz * seq
    w = _prep_weights(emb_ln_g, emb_ln_b, w_in[0], q_norm[0], w_uq[0], kv_norm[0], w_ukv[0], conv_w[0],
                      attn_out_norm[0], conv_out_norm[0], w_o[0], ln1_g[0], ln1_b[0], w_router[0],
                      router_bias[0], ws_gate[0], ws_up[0], ws_down[0])

    meta_out = _inproj(meta_tokens[None], _rope_tables(jnp.arange(N_META)),
                       jnp.zeros((SUBLANES, D_CONV), jnp.float32), w, N_META)
    km = jnp.pad(meta_out[1][0], ((0, META_PAD - N_META), (0, 0)))
    vm = jnp.pad(meta_out[2][0], ((0, 0), (0, META_PAD - N_META)))
    vc_meta = meta_out[4][0]

    q, k, v, cn, _ = _inproj(x, _rope_tables(N_META + jnp.arange(seq)), vc_meta, w, T_IN)
    attn = _attention(q, k, v, km, vm)

    hp, base, eidx, gates, pos, counts = _mix_route(
        attn.reshape(n, D_ATTN), cn.reshape(n, D_CONV), x.reshape(n, D_MODEL), w)
    idx, first_blk, n_blk, nused, n_slots = _slot_plan(eidx, pos, counts)
    tiled = lambda a, lead: a.reshape(lead + (-1, PACK_TILES, SUBLANES, LANES))

    xs = _dispatch(hp.reshape(-1, LANES), idx, n_slots * PACK_TILES)
    ys = _experts(tiled(xs, ()), first_blk, n_blk, nused, w_gate[0], w_up[0], w_down[0])
    gates_t = gates.transpose(0, 2, 1).reshape(n, TOP_K)
    g2, b2 = ln2_g[0].reshape(1, -1), ln2_b[0].reshape(1, -1)
    ys_flat = ys.reshape(-1, LANES)
    per = idx.shape[0] // OUT_CHUNKS
    out = None
    for c in range(OUT_CHUNKS):
        yg = tiled(_combine(ys_flat, idx[c * per:(c + 1) * per], n // OUT_CHUNKS * PACK_TILES), (TOP_K,))
        out = _final(yg, gates_t, base, g2, b2, c, out)
    return out.reshape(bsz, seq, D_MODEL)
```

```python
import functools
import math

import jax
import jax.numpy as jnp
from jax import lax
from jax.experimental import pallas as pl
from jax.experimental.pallas import tpu as pltpu
from jax.experimental.pallas import tpu_sc as plsc

D_MODEL = 1024
N_META = 16
ATTN_HEADS = 8
QK_NOPE = 64
QK_ROPE = 32
V_HEAD = 64
Q_LORA = 384
KV_LORA = 256
ROPE_THETA = 10000.0
D_ATTN = ATTN_HEADS * V_HEAD
D_CONV = 512
CONV_W = 3
ATTN_SCALE = 1.0 / math.sqrt(QK_NOPE + QK_ROPE)
N_EXPERTS = 256
TOP_K = 8
N_GROUPS = 8
GROUP_SIZE = N_EXPERTS // N_GROUPS
TOPK_GROUPS = 4
D_EXPERT = 256
ROUTED_SCALE = 2.5
DEPTH = 1
ALPHA = (2.0 * DEPTH) ** 0.25

LANES = 128
SUBLANES = 8
HEAD_PAD = LANES
META_PAD = LANES
HALF = D_MODEL // 2
PACK_TILES = HALF // LANES
BF16_BITS = 16
HI_MASK = -(1 << BF16_BITS)
SC_CORES = 2
SC_SUBCORES = 16

T_IN = 512
T_Q = 1024
DIAG_PARTS = 4
SHIFT_MARGIN = 40.0
T_MIX = 512
BLK = 256
BLOCK_GROUP = 3
ROW_SLOTS = 8
W_SLOTS = 3
W_PARTS = 2
T_FIN = 256
OUT_CHUNKS = 4
SC_ROWS = 64
SC_PIECES = SC_ROWS * PACK_TILES
SC_LISTS = SC_PIECES // LANES

VMEM_LIMIT = 56 * 1024 * 1024

_NEG_INF = float("-inf")


def _bf(x):
    return x.astype(jnp.bfloat16)


def _dot(a, b):
    return jnp.dot(a, b, preferred_element_type=jnp.float32)


def _dot_t(a, b):
    return lax.dot_general(a, b, (((1,), (1,)), ((), ())), preferred_element_type=jnp.float32)


def _layernorm(x, g, b, eps=1e-5):
    mu = jnp.mean(x, axis=-1, keepdims=True)
    xc = x - mu
    var = jnp.mean(xc * xc, axis=-1, keepdims=True)
    return xc * lax.rsqrt(var + eps) * g + b


def _rms(x, g, n, eps=1e-6):
    ms = jnp.sum(x * x, axis=-1, keepdims=True) * (1.0 / n)
    return x * lax.rsqrt(ms + eps) * g


def _sigmoid(x):
    return 1.0 / (1.0 + jnp.exp(-x))


def _rope(x, c, s_up, s_dn):
    return x * c + pltpu.roll(x, QK_ROPE // 2, 1) * s_up + pltpu.roll(x, LANES - QK_ROPE // 2, 1) * s_dn


def _rope_t(x, c, s_up, s_dn):
    return x * c + pltpu.roll(x, QK_ROPE // 2, 0) * s_up + pltpu.roll(x, LANES - QK_ROPE // 2, 0) * s_dn


def _inproj_body(x_ref, g_ref, b_ref, win_ref, qn_ref, wuq_ref, kvn_ref, wuk_ref, wuv_ref, cw_ref, con_ref,
                 tc_ref, tu_ref, td_ref, tct_ref, tut_ref, tdt_ref, vcinit_ref,
                 q_ref, k_ref, v_ref, cn_ref, vct_ref, carry_ref):
    rows = x_ref.shape[1]

    @pl.when(pl.program_id(1) == 0)
    def _():
        carry_ref[...] = vcinit_ref[...]

    h0 = _layernorm(x_ref[0], g_ref[...], b_ref[...])
    z = _dot(_bf(h0), win_ref[...])
    o1 = Q_LORA
    o2 = o1 + KV_LORA
    o3 = o2 + HEAD_PAD
    o4 = o3 + D_CONV
    o5 = o4 + D_CONV
    c, s_up, s_dn = tc_ref[...], tu_ref[...], td_ref[...]

    ct, st_up, st_dn = tct_ref[...], tut_ref[...], tdt_ref[...]
    q_t = _dot_t(wuq_ref[...], _bf(_rms(z[:, :o1], qn_ref[...], Q_LORA)))
    for h in range(ATTN_HEADS):
        sl = slice(h * HEAD_PAD, (h + 1) * HEAD_PAD)
        q_ref[0, sl, :] = _bf(_rope_t(q_t[sl, :], ct, st_up, st_dn) * ATTN_SCALE)

    ckv = _bf(_rms(z[:, o1:o2], kvn_ref[...], KV_LORA))
    kk = _dot(ckv, wuk_ref[...])
    kpe = _rope(z[:, o2:o3], c, s_up, s_dn)
    for h in range(ATTN_HEADS):
        sl = slice(h * HEAD_PAD, (h + 1) * HEAD_PAD)
        k_ref[0, :, sl] = _bf(kk[:, sl] + kpe)
    v_t = _dot_t(wuv_ref[...], ckv)
    dim = lax.broadcasted_iota(jnp.int32, (HEAD_PAD, rows), 0)
    for h in range(ATTN_HEADS):
        sl = slice(h * HEAD_PAD, (h + 1) * HEAD_PAD)
        v_ref[0, sl, :] = _bf(jnp.where(dim < V_HEAD, v_t[sl, :], 1.0))

    vc = z[:, o4:o5] * z[:, o5:]
    carry = carry_ref[...]
    row = lax.broadcasted_iota(jnp.int32, (rows, D_CONV), 0)
    vc1 = jnp.where(row == 0, carry[SUBLANES - 1:SUBLANES, :], pltpu.roll(vc, 1, 0))
    vc2 = pltpu.roll(vc, 2, 0)
    vc2 = jnp.where(row == 0, carry[SUBLANES - 2:SUBLANES - 1, :], vc2)
    vc2 = jnp.where(row == 1, carry[SUBLANES - 1:SUBLANES, :], vc2)
    cw = cw_ref[...]
    conv = z[:, o3:o4] * (cw[0:1, :] * vc2 + cw[1:2, :] * vc1 + cw[2:3, :] * vc)
    cn_ref[0] = _bf(_rms(conv, con_ref[...], D_CONV))
    tail = vc[rows - SUBLANES:, :]
    carry_ref[...] = tail
    vct_ref[0] = tail


def _inproj(x, tabs, vcinit, w, rows):
    bsz, seq, _ = x.shape
    nj = seq // rows
    full = lambda a: pl.BlockSpec(a.shape, lambda b, j: (0,) * a.ndim)
    row_blk = lambda width: pl.BlockSpec((1, rows, width), lambda b, j: (b, j, 0))
    tab = pl.BlockSpec((rows, HEAD_PAD), lambda b, j: (j, 0))
    tab_t = pl.BlockSpec((HEAD_PAD, rows), lambda b, j: (0, j))
    col_blk = pl.BlockSpec((1, ATTN_HEADS * HEAD_PAD, rows), lambda b, j: (b, 0, j))
    params = [w["emb_g"], w["emb_b"], w["w_in"], w["q_norm"], w["w_uq_t"], w["kv_norm"], w["w_uk"], w["w_uv_t"],
              w["conv_w"], w["conv_norm"]]
    wide = ATTN_HEADS * HEAD_PAD
    out_shape = [
        jax.ShapeDtypeStruct((bsz, wide, seq), jnp.bfloat16),
        jax.ShapeDtypeStruct((bsz, seq, wide), jnp.bfloat16),
        jax.ShapeDtypeStruct((bsz, wide, seq), jnp.bfloat16),
        jax.ShapeDtypeStruct((bsz, seq, D_CONV), jnp.bfloat16),
        jax.ShapeDtypeStruct((bsz, nj * SUBLANES, D_CONV), jnp.float32),
    ]
    tabs_t = [t.T for t in tabs]
    return pl.pallas_call(
        _inproj_body,
        grid=(bsz, nj),
        in_specs=([row_blk(D_MODEL)] + [full(p) for p in params] + [tab, tab, tab, tab_t, tab_t, tab_t]
                  + [full(vcinit)]),
        out_specs=[col_blk, row_blk(wide), col_blk, row_blk(D_CONV),
                   pl.BlockSpec((1, SUBLANES, D_CONV), lambda b, j: (b, j, 0))],
        out_shape=out_shape,
        scratch_shapes=[pltpu.VMEM((SUBLANES, D_CONV), jnp.float32)],
        compiler_params=pltpu.CompilerParams(dimension_semantics=("arbitrary", "arbitrary"),
                                             vmem_limit_bytes=VMEM_LIMIT),
        name="inproj",
    )(x, *params, *tabs, *tabs_t, vcinit)


def _attn_body(q_ref, k_ref, v_ref, km_ref, vm_ref, o_ref):
    i = pl.program_id(2)
    tq = q_ref.shape[2]
    heads = [slice(hh * HEAD_PAD, (hh + 1) * HEAD_PAD) for hh in range(2)]
    qs = [q_ref[0, sl, :] for sl in heads]

    part = tq // DIAG_PARTS
    is_meta = lax.broadcasted_iota(jnp.int32, (META_PAD, tq), 0) < N_META
    lead = lambda a, off, tail: tail if off == 0 else jnp.concatenate([a[:, :off], tail], axis=1)
    carry = []
    for q, sl in zip(qs, heads):
        sm = jnp.where(is_meta, _dot(km_ref[:, sl], q), _NEG_INF)
        m = jnp.max(sm, axis=0, keepdims=True)
        scores = []
        for c in range(DIAG_PARTS):
            off = c * part
            rows = pl.ds(pl.multiple_of(i * tq + off, part), part)
            causal = (lax.broadcasted_iota(jnp.int32, (part, tq - off), 0)
                      <= lax.broadcasted_iota(jnp.int32, (part, tq - off), 1))
            s = jnp.where(causal, _dot(k_ref[0, rows, sl], q[:, off:]), _NEG_INF)
            m = lead(m, off, jnp.maximum(m[:, off:], jnp.max(s, axis=0, keepdims=True)))
            scores.append((rows, off, s))
        acc = _dot(vm_ref[sl, :], _bf(jnp.exp(sm - m)))
        for rows, off, s in scores:
            acc = lead(acc, off, acc[:, off:] + _dot(v_ref[0, sl, rows], _bf(jnp.exp(s - m[:, off:]))))
        carry += [m, acc]

    def body(j, carry):
        rj = pl.multiple_of(j * tq, tq)
        ms, accs = (carry[0], carry[2]), (carry[1], carry[3])
        fast, tops = [], []
        for q, sl, m, acc in zip(qs, heads, ms, accs):
            s = _dot(k_ref[0, pl.ds(rj, tq), sl], q)
            tops.append(jnp.max(s, axis=0, keepdims=True))
            fast.append(acc + _dot(v_ref[0, sl, pl.ds(rj, tq)], _bf(jnp.exp(s - m))))
        excess = jnp.maximum(tops[0] - ms[0], tops[1] - ms[1])
        over = jnp.max(excess, axis=1, keepdims=True)[0, 0] > SHIFT_MARGIN

        def rescale(_):
            out = []
            for q, sl, m, acc, top in zip(qs, heads, ms, accs, tops):
                s = _dot(k_ref[0, pl.ds(rj, tq), sl], q)
                m_new = jnp.maximum(m, top)
                out += [m_new, jnp.exp(m - m_new) * acc + _dot(v_ref[0, sl, pl.ds(rj, tq)], _bf(jnp.exp(s - m_new)))]
            return tuple(out)

        return lax.cond(over, rescale, lambda _: (ms[0], fast[0], ms[1], fast[1]), 0)

    _, a0, _, a1 = lax.fori_loop(0, i, body, tuple(carry))
    o_t = jnp.concatenate([a[:V_HEAD, :] / a[V_HEAD:V_HEAD + 1, :] for a in (a0, a1)], axis=0)
    o_ref[0] = _bf(o_t.T)


def _attention(q_t, k, v_t, km, vm_t):
    bsz, seq, _ = k.shape
    pairs = ATTN_HEADS // 2
    pw = 2 * HEAD_PAD
    return pl.pallas_call(
        _attn_body,
        grid=(bsz, pairs, seq // T_Q),
        in_specs=[pl.BlockSpec((1, pw, T_Q), lambda b, p, i: (b, p, i)),
                  pl.BlockSpec((1, seq, pw), lambda b, p, i: (b, 0, p)),
                  pl.BlockSpec((1, pw, seq), lambda b, p, i: (b, p, 0)),
                  pl.BlockSpec((META_PAD, pw), lambda b, p, i: (0, p)),
                  pl.BlockSpec((pw, META_PAD), lambda b, p, i: (p, 0))],
        out_specs=pl.BlockSpec((1, T_Q, 2 * V_HEAD), lambda b, p, i: (b, i, p)),
        out_shape=jax.ShapeDtypeStruct((bsz, seq, D_ATTN), jnp.bfloat16),
        compiler_params=pltpu.CompilerParams(dimension_semantics=("arbitrary",) * 3,
                                             vmem_limit_bytes=VMEM_LIMIT),
        name="attention",
    )(q_t, k, v_t, km, vm_t)


def _mix_body(attn_ref, cn_ref, x_ref, eg_ref, eb_ref, an_ref, woa_ref, woc_ref, g1_ref, b1_ref,
              wsg_ref, wsu_ref, wsd_ref, wrt_ref, rb_ref, tri_ref,
              hp_ref, base_ref, eidx_ref, gate_ref, pos_ref, cnt_ref, run_ref):
    t = attn_ref.shape[0]

    @pl.when(pl.program_id(0) == 0)
    def _():
        run_ref[...] = jnp.zeros_like(run_ref)

    attn_n = _rms(attn_ref[...].astype(jnp.float32), an_ref[...], D_ATTN)
    mix = _dot(_bf(attn_n), woa_ref[...]) + _dot(cn_ref[...], woc_ref[...])
    h0 = _layernorm(x_ref[...], eg_ref[...], eb_ref[...])
    h1 = _layernorm(ALPHA * h0 + mix, g1_ref[...], b1_ref[...])
    _store_row_tiles(hp_ref, (), h1)
    h1b = _bf(h1)
    g = _dot(h1b, wsg_ref[...])
    u = _dot(h1b, wsu_ref[...])
    base_ref[...] = ALPHA * h1 + _dot(_bf(g * _sigmoid(g) * u), wsd_ref[...])

    shp = (N_GROUPS, GROUP_SIZE, t)
    sc = _sigmoid(_dot_t(wrt_ref[...], h1b)).reshape(shp)
    sel = sc + rb_ref[...].reshape(N_GROUPS, GROUP_SIZE, 1)
    gio = lax.broadcasted_iota(jnp.int32, shp, 0)
    rio = lax.broadcasted_iota(jnp.int32, shp, 1)
    eio = gio * GROUP_SIZE + rio

    m1 = jnp.max(sel, axis=1, keepdims=True)
    i1 = jnp.min(jnp.where(sel == m1, rio, GROUP_SIZE), axis=1, keepdims=True)
    m2 = jnp.max(jnp.where(rio == i1, _NEG_INF, sel), axis=1, keepdims=True)
    gs = m1 + m2
    gio1 = lax.broadcasted_iota(jnp.int32, (N_GROUPS, 1, t), 0)
    gmask = jnp.zeros((N_GROUPS, 1, t), jnp.bool_)
    for _ in range(TOPK_GROUPS):
        mg = jnp.max(gs, axis=0, keepdims=True)
        ig = jnp.min(jnp.where(gs == mg, gio1, N_GROUPS), axis=0, keepdims=True)
        hit = gio1 == ig
        gmask = jnp.logical_or(gmask, hit)
        gs = jnp.where(hit, _NEG_INF, gs)

    cand = jnp.where(gmask, sel, _NEG_INF)

    def red(x, op):
        return op(op(x, axis=0, keepdims=True), axis=1, keepdims=True)

    pm = jnp.zeros(shp, jnp.float32)
    e_rows, g_rows = [], []
    for _ in range(TOP_K):
        mk = red(cand, jnp.max)
        ek = red(jnp.where(cand == mk, eio, N_EXPERTS), jnp.min)
        hit = eio == ek
        g_rows.append(red(jnp.where(hit, sc, 0.0), jnp.sum))
        e_rows.append(ek)
        pm = jnp.where(hit, 1.0, pm)
        cand = jnp.where(hit, _NEG_INF, cand)
    gsum = g_rows[0]
    for gk in g_rows[1:]:
        gsum = gsum + gk

    pm = pm.reshape(N_EXPERTS, t)
    rank = _dot(_bf(pm), tri_ref[...])
    slot = (run_ref[...] + rank).reshape(shp)
    run_ref[...] = run_ref[...] + jnp.sum(pm, axis=1, keepdims=True)
    cnt_ref[...] = run_ref[...]
    for k in range(TOP_K):
        eidx_ref[0, k:k + 1, :] = e_rows[k].reshape(1, t)
        gate_ref[0, k:k + 1, :] = (g_rows[k] / gsum * ROUTED_SCALE).reshape(1, t)
        pk = red(jnp.where(eio == e_rows[k], slot, 0.0), jnp.sum)
        pos_ref[0, k:k + 1, :] = pk.reshape(1, t).astype(jnp.int32)


def _mix_route(attn, cn, x, w):
    n = attn.shape[0]
    nt = n // T_MIX
    tri = _bf(jnp.triu(jnp.ones((T_MIX, T_MIX), jnp.float32), 1))
    params = [w["emb_g"], w["emb_b"], w["attn_norm"], w["wo_a"], w["wo_c"], w["ln1_g"], w["ln1_b"], w["ws_gate"], w["ws_up"],
              w["ws_down"], w["w_router_t"], w["router_bias"], tri]
    full = lambda a: pl.BlockSpec(a.shape, lambda i: (0,) * a.ndim)
    row_blk = lambda width: pl.BlockSpec((T_MIX, width), lambda i: (i, 0))
    k_blk = pl.BlockSpec((1, TOP_K, T_MIX), lambda i: (i, 0, 0))
    out_shape = [
        jax.ShapeDtypeStruct((n // SUBLANES, PACK_TILES, SUBLANES, LANES), jnp.int32),
        jax.ShapeDtypeStruct((n, D_MODEL), jnp.float32),
        jax.ShapeDtypeStruct((nt, TOP_K, T_MIX), jnp.int32),
        jax.ShapeDtypeStruct((nt, TOP_K, T_MIX), jnp.float32),
        jax.ShapeDtypeStruct((nt, TOP_K, T_MIX), jnp.int32),
        jax.ShapeDtypeStruct((N_EXPERTS, 1), jnp.float32),
    ]
    return pl.pallas_call(
        _mix_body,
        grid=(nt,),
        in_specs=[row_blk(D_ATTN), row_blk(D_CONV), row_blk(D_MODEL)] + [full(p) for p in params],
        out_specs=[pl.BlockSpec((T_MIX // SUBLANES, PACK_TILES, SUBLANES, LANES), lambda i: (i, 0, 0, 0)),
                   row_blk(D_MODEL),
                   k_blk, k_blk, k_blk, pl.BlockSpec((N_EXPERTS, 1), lambda i: (0, 0))],
        out_shape=out_shape,
        scratch_shapes=[pltpu.VMEM((N_EXPERTS, 1), jnp.float32)],
        compiler_params=pltpu.CompilerParams(dimension_semantics=("arbitrary",),
                                             vmem_limit_bytes=VMEM_LIMIT),
        name="mix_route",
    )(attn, cn, x, *params)


def _pack_row_tiles(x):
    bits = lambda a: lax.bitcast_convert_type(_bf(a).astype(jnp.float32), jnp.int32)
    out = []
    for lt in range(PACK_TILES):
        lo = bits(x[:, lt * LANES:(lt + 1) * LANES])
        hi = bits(x[:, HALF + lt * LANES:HALF + (lt + 1) * LANES])
        out.append(lax.shift_right_logical(lo, BF16_BITS) | (hi & HI_MASK))
    return out


def _unpack_lo(w):
    return lax.bitcast_convert_type(lax.shift_left(w, BF16_BITS), jnp.float32)


def _unpack_hi(w):
    return lax.bitcast_convert_type(w & HI_MASK, jnp.float32)


def _store_row_tiles(ref, lead, x):
    rows = x.shape[0]
    for lt, w in enumerate(_pack_row_tiles(x)):
        ref[lead + (slice(None), lt)] = w.reshape(rows // SUBLANES, SUBLANES, LANES)


def _load_row_tiles(ref, lead, rows):
    return [ref[lead + (slice(None), lt)].reshape(rows, LANES) for lt in range(PACK_TILES)]


def _sc_mesh():
    return plsc.VectorSubcoreMesh(core_axis_name="c", subcore_axis_name="s")


def _worker_id():
    return lax.axis_index("s") * SC_CORES + lax.axis_index("c")


def _dispatch(hp, idx, n_rows):
    per_worker = hp.shape[0] // SC_PIECES // (SC_CORES * SC_SUBCORES)
    lists = idx.shape[1]

    @functools.partial(
        pl.kernel, mesh=_sc_mesh(),
        out_type=jax.ShapeDtypeStruct((n_rows, LANES), jnp.int32),
        scratch_types=[pltpu.VMEM((lists, LANES), jnp.int32),
                       pltpu.VMEM((SC_PIECES, LANES), jnp.int32),
                       pltpu.SemaphoreType.DMA],
        name="dispatch")
    def run(hp_hbm, idx_hbm, xs_hbm, idx_v, rows_v, sem):
        first = _worker_id() * per_worker

        @pl.loop(0, per_worker)
        def _(c):
            chunk = first + c
            pltpu.sync_copy(idx_hbm.at[chunk], idx_v)
            pltpu.sync_copy(hp_hbm.at[pl.ds(chunk * SC_PIECES, SC_PIECES)], rows_v)
            copies = [pltpu.async_copy(rows_v.at[pl.ds((j // TOP_K) * LANES, LANES)],
                                       xs_hbm.at[idx_v.at[j]], sem) for j in range(lists)]
            for cp in copies:
                cp.wait()

    return run(hp, idx)


def _combine(ys, idx, n_rows):
    per_worker = n_rows // SC_PIECES // (SC_CORES * SC_SUBCORES)
    lists = idx.shape[1]

    @functools.partial(
        pl.kernel, mesh=_sc_mesh(),
        out_type=jax.ShapeDtypeStruct((TOP_K, n_rows, LANES), jnp.int32),
        scratch_types=[pltpu.VMEM((lists, LANES), jnp.int32),
                       pltpu.VMEM((2, SC_PIECES, LANES), jnp.int32),
                       pltpu.SemaphoreType.DMA, pltpu.SemaphoreType.DMA],
        name="combine")
    def run(ys_hbm, idx_hbm, out_hbm, idx_v, rows_v, sem0, sem1):
        first = _worker_id() * per_worker
        sems = (sem0, sem1)

        def gather(k):
            buf = rows_v.at[k % 2]
            return [pltpu.async_copy(ys_hbm.at[idx_v.at[g * TOP_K + k]],
                                     buf.at[pl.ds(g * LANES, LANES)], sems[k % 2]) for g in range(SC_LISTS)]

        @pl.loop(0, per_worker)
        def _(c):
            chunk = first + c
            pltpu.sync_copy(idx_hbm.at[chunk], idx_v)
            pending = gather(0)
            for k in range(TOP_K):
                for cp in pending:
                    cp.wait()
                if k + 1 < TOP_K:
                    pending = gather(k + 1)
                pltpu.sync_copy(rows_v.at[k % 2], out_hbm.at[k, pl.ds(chunk * SC_PIECES, SC_PIECES)])

    return run(ys, idx)


def _expert_body(first_ref, count_ref, nused_ref, xs_hbm, wg_hbm, wu_hbm, wd_hbm, ys_hbm,
                 xbuf, ybuf, wgbuf, wubuf, wdbuf, wgu_ref, wdn_ref, sem_in, sem_out, sem_w):
    e = pl.program_id(0)
    nused = nused_ref[0]
    tiles = BLK // SUBLANES
    ahead = ROW_SLOTS - BLOCK_GROUP

    def fetch(b):
        slot = b % ROW_SLOTS
        return pltpu.make_async_copy(xs_hbm.at[pl.ds(b * tiles, tiles)], xbuf.at[slot], sem_in.at[slot])

    def writeback(b):
        slot = b % ROW_SLOTS
        return pltpu.make_async_copy(ybuf.at[slot], ys_hbm.at[pl.ds(b * tiles, tiles)], sem_out.at[slot])

    def weights(ex):
        slot = ex % W_SLOTS
        copies = []
        for src, dst in ((wg_hbm, wgbuf), (wu_hbm, wubuf), (wd_hbm, wdbuf)):
            part = src.shape[1] // W_PARTS
            for p in range(W_PARTS):
                rows = pl.ds(p * part, part)
                copies.append(pltpu.make_async_copy(src.at[ex, rows], dst.at[slot, rows], sem_w.at[slot]))
        return copies

    @pl.when(e == 0)
    def _():
        for b in range(ahead):
            @pl.when(b < nused)
            def _():
                fetch(b).start()
        for ex in range(W_SLOTS - 1):
            for cp in weights(ex):
                cp.start()

    for cp in weights(e):
        cp.wait()

    @pl.when(e + W_SLOTS - 1 < pl.num_programs(0))
    def _():
        for cp in weights(e + W_SLOTS - 1):
            cp.start()

    @pl.when(count_ref[e] > 0)
    def _():
        slot = e % W_SLOTS
        wgu_ref[:, :D_EXPERT] = _bf(wgbuf[slot])
        wgu_ref[:, D_EXPERT:] = _bf(wubuf[slot])
        wdn_ref[...] = _bf(wdbuf[slot])

    def run(b0, nblk):
        blocks = [b0 + c for c in range(nblk)]
        for b in blocks:
            fetch(b).wait()
        for b in blocks:
            @pl.when(b + ahead < nused)
            def _():
                fetch(b + ahead).start()

        words = [jnp.concatenate(ws, axis=0) for ws in
                 zip(*[_load_row_tiles(xbuf, (b % ROW_SLOTS,), BLK) for b in blocks])]
        x = jnp.concatenate([_bf(_unpack_lo(w)) for w in words] + [_bf(_unpack_hi(w)) for w in words], axis=1)
        gu = _dot(x, wgu_ref[...])
        g, u = gu[:, :D_EXPERT], gu[:, D_EXPERT:]
        y = _dot(_bf(g * _sigmoid(g) * u), wdn_ref[...])

        for c, b in enumerate(blocks):
            @pl.when(b >= ROW_SLOTS)
            def _():
                writeback(b - ROW_SLOTS).wait()

            _store_row_tiles(ybuf, (b % ROW_SLOTS,), y[c * BLK:(c + 1) * BLK])
            writeback(b).start()

    start, left = first_ref[e], count_ref[e]
    for size in range(BLOCK_GROUP, 0, -1):
        def group(t, carry, start=start, size=size):
            run(start + size * t, size)
            return carry

        lax.fori_loop(0, left // size, group, 0)
        start, left = start + left // size * size, left % size

    @pl.when(e == pl.num_programs(0) - 1)
    def _():
        for back in range(1, ROW_SLOTS + 1):
            @pl.when(nused >= back)
            def _():
                writeback(nused - back).wait()


def _experts(xs, first_blk, n_blk, nused, w_gate, w_up, w_down):
    blk = (BLK // SUBLANES, PACK_TILES, SUBLANES, LANES)
    grid_spec = pltpu.PrefetchScalarGridSpec(
        num_scalar_prefetch=3,
        grid=(N_EXPERTS,),
        in_specs=[pl.BlockSpec(memory_space=pl.ANY)] * 4,
        out_specs=pl.BlockSpec(memory_space=pl.ANY),
        scratch_shapes=[pltpu.VMEM((ROW_SLOTS,) + blk, jnp.int32),
                        pltpu.VMEM((ROW_SLOTS,) + blk, jnp.int32),
                        pltpu.VMEM((W_SLOTS, D_MODEL, D_EXPERT), jnp.float32),
                        pltpu.VMEM((W_SLOTS, D_MODEL, D_EXPERT), jnp.float32),
                        pltpu.VMEM((W_SLOTS, D_EXPERT, D_MODEL), jnp.float32),
                        pltpu.VMEM((D_MODEL, 2 * D_EXPERT), jnp.bfloat16),
                        pltpu.VMEM((D_EXPERT, D_MODEL), jnp.bfloat16),
                        pltpu.SemaphoreType.DMA((ROW_SLOTS,)),
                        pltpu.SemaphoreType.DMA((ROW_SLOTS,)),
                        pltpu.SemaphoreType.DMA((W_SLOTS,))],
    )
    return pl.pallas_call(
        _expert_body,
        grid_spec=grid_spec,
        out_shape=jax.ShapeDtypeStruct(xs.shape, jnp.int32),
        compiler_params=pltpu.CompilerParams(dimension_semantics=("arbitrary",),
                                             vmem_limit_bytes=VMEM_LIMIT),
        name="experts",
    )(first_blk, n_blk, nused, xs, w_gate, w_up, w_down)


def _final_body(yg_ref, gate_ref, base_ref, g2_ref, b2_ref, *rest):
    o_ref = rest[-1]
    t = base_ref.shape[0]
    gate = gate_ref[...]
    lo = [None] * PACK_TILES
    hi = [None] * PACK_TILES
    for k in range(TOP_K):
        gk = gate[:, k:k + 1]
        for lt, w in enumerate(_load_row_tiles(yg_ref, (k,), t)):
            a, b = gk * _unpack_lo(w), gk * _unpack_hi(w)
            lo[lt] = a if k == 0 else lo[lt] + a
            hi[lt] = b if k == 0 else hi[lt] + b
    ff = jnp.concatenate(lo + hi, axis=1)
    o_ref[...] = _layernorm(base_ref[...] + ff, g2_ref[...], b2_ref[...])


def _final(yg, gates, base, g2, b2, chunk, prev_out):
    n = base.shape[0]
    steps = yg.shape[1] * SUBLANES // T_FIN
    off = chunk * steps
    full = lambda a: pl.BlockSpec(a.shape, lambda i: (0,) * a.ndim)
    in_specs = [pl.BlockSpec((TOP_K, T_FIN // SUBLANES, PACK_TILES, SUBLANES, LANES),
                             lambda i: (0, i, 0, 0, 0)),
                pl.BlockSpec((T_FIN, TOP_K), lambda i: (off + i, 0)),
                pl.BlockSpec((T_FIN, D_MODEL), lambda i: (off + i, 0)),
                full(g2), full(b2)]
    args = [yg, gates, base, g2, b2]
    aliases = {}
    if prev_out is not None:
        in_specs.append(pl.BlockSpec(memory_space=pl.ANY))
        aliases = {len(args): 0}
        args.append(prev_out)
    return pl.pallas_call(
        _final_body,
        grid=(steps,),
        in_specs=in_specs,
        out_specs=pl.BlockSpec((T_FIN, D_MODEL), lambda i: (off + i, 0)),
        out_shape=jax.ShapeDtypeStruct((n, D_MODEL), jnp.float32),
        input_output_aliases=aliases,
        compiler_params=pltpu.CompilerParams(dimension_semantics=("arbitrary",),
                                             vmem_limit_bytes=VMEM_LIMIT),
        name="final",
    )(*args)


def _rope_tables(pos):
    half = QK_ROPE // 2
    inv = ROPE_THETA ** (-jnp.arange(0, QK_ROPE, 2, dtype=jnp.float32) / QK_ROPE)
    ang = pos.astype(jnp.float32)[:, None] * inv[None, :]
    cos, sin = jnp.cos(ang), jnp.sin(ang)
    n = pos.shape[0]
    z = lambda w: jnp.zeros((n, w), jnp.float32)
    c = jnp.concatenate([jnp.ones((n, QK_NOPE), jnp.float32), cos, cos, z(LANES - QK_NOPE - QK_ROPE)], axis=1)
    s_up = jnp.concatenate([z(QK_NOPE + half), sin, z(LANES - QK_NOPE - QK_ROPE)], axis=1)
    s_dn = jnp.concatenate([z(QK_NOPE), -sin, z(half), z(LANES - QK_NOPE - QK_ROPE)], axis=1)
    return c, s_up, s_dn


def _pad_heads(w, per_head, keep):
    rows = w.shape[0]
    w = w.reshape(rows, ATTN_HEADS, per_head)[:, :, :keep]
    w = jnp.pad(w, ((0, 0), (0, 0), (0, HEAD_PAD - keep)))
    return w.reshape(rows, ATTN_HEADS * HEAD_PAD)


def _prep_weights(emb_ln_g, emb_ln_b, w_in, q_norm, w_uq, kv_norm, w_ukv, conv_w, attn_out_norm,
                  conv_out_norm, w_o, ln1_g, ln1_b, w_router, router_bias, ws_gate, ws_up, ws_down):
    o1 = Q_LORA
    o2 = o1 + KV_LORA
    o3 = o2 + QK_ROPE
    row = lambda a: a.reshape(1, -1)
    kpe_cols = jnp.pad(w_in[:, o2:o3], ((0, 0), (QK_NOPE, HEAD_PAD - QK_NOPE - QK_ROPE)))
    w_in_p = jnp.concatenate([w_in[:, :o2], kpe_cols, w_in[:, o3:]], axis=1)
    ukv = w_ukv.reshape(KV_LORA, ATTN_HEADS, QK_NOPE + V_HEAD)
    w_uk = jnp.pad(ukv[:, :, :QK_NOPE], ((0, 0), (0, 0), (0, HEAD_PAD - QK_NOPE)))
    w_uv = jnp.pad(ukv[:, :, QK_NOPE:], ((0, 0), (0, 0), (0, HEAD_PAD - V_HEAD)))
    woa = w_o[:D_ATTN]
    return {
        "emb_g": row(emb_ln_g), "emb_b": row(emb_ln_b),
        "w_in": _bf(w_in_p),
        "q_norm": row(q_norm), "w_uq_t": _bf(_pad_heads(w_uq, QK_NOPE + QK_ROPE, QK_NOPE + QK_ROPE).T),
        "kv_norm": row(kv_norm), "w_uk": _bf(w_uk.reshape(KV_LORA, -1)), "w_uv_t": _bf(w_uv.reshape(KV_LORA, -1).T),
        "conv_w": jnp.pad(conv_w, ((0, SUBLANES - CONV_W), (0, 0))), "conv_norm": row(conv_out_norm),
        "attn_norm": row(attn_out_norm), "wo_a": _bf(woa), "wo_c": _bf(w_o[D_ATTN:]),
        "ln1_g": row(ln1_g), "ln1_b": row(ln1_b),
        "ws_gate": _bf(ws_gate), "ws_up": _bf(ws_up), "ws_down": _bf(ws_down),
        "w_router_t": _bf(w_router.T), "router_bias": router_bias.reshape(-1, 1),
    }


def _slot_body(eidx_ref, pos_ref, start_ref, idx_ref, base_ref):
    t = eidx_ref.shape[2]
    shp = (N_GROUPS, GROUP_SIZE, t)
    eio = (lax.broadcasted_iota(jnp.int32, shp, 0) * GROUP_SIZE + lax.broadcasted_iota(jnp.int32, shp, 1))
    start = start_ref[...].reshape(N_GROUPS, GROUP_SIZE, 1)
    for k in range(TOP_K):
        hit = eio == eidx_ref[0, k:k + 1, :].reshape(1, 1, t)
        st = jnp.sum(jnp.sum(jnp.where(hit, start, 0.0), axis=0, keepdims=True), axis=1, keepdims=True)
        d = st.reshape(1, t).astype(jnp.int32) + pos_ref[0, k:k + 1, :]
        base_ref[k:k + 1, :] = (d // SUBLANES) * (PACK_TILES * SUBLANES) + d % SUBLANES

    j = lax.broadcasted_iota(jnp.int32, (TOP_K, LANES), 1)
    per_list = LANES // (PACK_TILES * SUBLANES)
    lane_tile = (j // SUBLANES) % PACK_TILES * SUBLANES
    for c in range(t // SC_ROWS):
        lane0 = c * SC_ROWS
        src = base_ref[:, lane0 // LANES * LANES:(lane0 // LANES + 1) * LANES]
        for g in range(SC_LISTS):
            tok = lane0 % LANES + (g * per_list + j // (PACK_TILES * SUBLANES)) * SUBLANES + j % SUBLANES
            idx_ref[c, g * TOP_K:(g + 1) * TOP_K, :] = jnp.take_along_axis(src, tok, axis=1) + lane_tile


def _slots(eidx, pos, start):
    nt, _, t = eidx.shape
    per_tile = t // SC_ROWS
    k_blk = pl.BlockSpec((1, TOP_K, t), lambda i: (i, 0, 0))
    return pl.pallas_call(
        _slot_body,
        grid=(nt,),
        in_specs=[k_blk, k_blk, pl.BlockSpec(start.shape, lambda i: (0, 0))],
        out_specs=pl.BlockSpec((per_tile, SC_LISTS * TOP_K, LANES), lambda i: (i, 0, 0)),
        out_shape=jax.ShapeDtypeStruct((nt * per_tile, SC_LISTS * TOP_K, LANES), jnp.int32),
        scratch_shapes=[pltpu.VMEM((TOP_K, t), jnp.int32)],
        compiler_params=pltpu.CompilerParams(dimension_semantics=("arbitrary",)),
        name="slots",
    )(eidx, pos, start)


def _slot_plan(eidx, pos, counts):
    counts = counts.reshape(-1).astype(jnp.int32)
    padded = (counts + BLK - 1) // BLK * BLK
    pad_end = jnp.cumsum(padded)
    pad_start = pad_end - padded
    idx = _slots(eidx, pos, pad_start.astype(jnp.float32).reshape(-1, 1))
    nb = eidx.size // BLK + N_EXPERTS
    nused = (pad_end[-1:] // BLK).astype(jnp.int32)
    return idx, (pad_start // BLK).astype(jnp.int32), (padded // BLK).astype(jnp.int32), nused, nb * BLK


def kernel(x, meta_tokens, emb_ln_g, emb_ln_b, w_in, q_norm, w_uq, kv_norm, w_ukv, conv_w, attn_out_norm, conv_out_norm, w_o, ln1_g, ln1_b, w_router, router_bias, w_gate, w_up, w_down, ws_gate, ws_up, ws_down, ln2_g, ln2_b):
    bsz, seq, _ = x.shape
    n = bsz * seq
    w = _prep_weights(emb_ln_g, emb_ln_b, w_in[0], q_norm[0], w_uq[0], kv_norm[0], w_ukv[0], conv_w[0],
                      attn_out_norm[0], conv_out_norm[0], w_o[0], ln1_g[0], ln1_b[0], w_router[0],
                      router_bias[0], ws_gate[0], ws_up[0], ws_down[0])

    meta_out = _inproj(meta_tokens[None], _rope_tables(jnp.arange(N_META)),
                       jnp.zeros((SUBLANES, D_CONV), jnp.float32), w, N_META)
    km = jnp.pad(meta_out[1][0], ((0, META_PAD - N_META), (0, 0)))
    vm = jnp.pad(meta_out[2][0], ((0, 0), (0, META_PAD - N_META)))
    vc_meta = meta_out[4][0]

    q, k, v, cn, _ = _inproj(x, _rope_tables(N_META + jnp.arange(seq)), vc_meta, w, T_IN)
    attn = _attention(q, k, v, km, vm)

    hp, base, eidx, gates, pos, counts = _mix_route(
        attn.reshape(n, D_ATTN), cn.reshape(n, D_CONV), x.reshape(n, D_MODEL), w)
    idx, first_blk, n_blk, nused, n_slots = _slot_plan(eidx, pos, counts)
    tiled = lambda a, lead: a.reshape(lead + (-1, PACK_TILES, SUBLANES, LANES))

    xs = _dispatch(hp.reshape(-1, LANES), idx, n_slots * PACK_TILES)
    ys = _experts(tiled(xs, ()), first_blk, n_blk, nused, w_gate[0], w_up[0], w_down[0])
    gates_t = gates.transpose(0, 2, 1).reshape(n, TOP_K)
    g2, b2 = ln2_g[0].reshape(1, -1), ln2_b[0].reshape(1, -1)
    ys_flat = ys.reshape(-1, LANES)
    per = idx.shape[0] // OUT_CHUNKS
    out = None
    for c in range(OUT_CHUNKS):
        yg = tiled(_combine(ys_flat, idx[c * per:(c + 1) * per], n // OUT_CHUNKS * PACK_TILES), (TOP_K,))
        out = _final(yg, gates_t, base, g2, b2, c, out)
    return out.reshape(bsz, seq, D_MODEL)
```

```python
import functools
import math

import jax
import jax.numpy as jnp
import numpy as np
from jax import lax
from jax.experimental import pallas as pl
from jax.experimental.pallas import tpu as pltpu
from jax.experimental.pallas import tpu_sc as plsc

D_MODEL = 1024
N_META = 16
ATTN_HEADS = 8
QK_NOPE = 64
QK_ROPE = 32
V_HEAD = 64
Q_LORA = 384
KV_LORA = 256
ROPE_THETA = 10000.0
D_ATTN = ATTN_HEADS * V_HEAD
D_CONV = 512
CONV_W = 3
ATTN_SCALE = 1.0 / math.sqrt(QK_NOPE + QK_ROPE)
N_EXPERTS = 256
TOP_K = 8
N_GROUPS = 8
GROUP_SIZE = N_EXPERTS // N_GROUPS
TOPK_GROUPS = 4
D_EXPERT = 256
ROUTED_SCALE = 2.5
DEPTH = 1
ALPHA = (2.0 * DEPTH) ** 0.25

LANES = 128
SUBLANES = 8
HEAD_PAD = LANES
META_PAD = LANES
HALF = D_MODEL // 2
PACK_TILES = HALF // LANES
BF16_BITS = 16
HI_MASK = -(1 << BF16_BITS)
SC_CORES = 2
SC_SUBCORES = 16

T_IN = 512
T_Q = 1024
DIAG_PARTS = 4
SHIFT_MARGIN = 40.0
T_MIX = 512
BLK = 256
BLOCK_GROUP = 3
ROW_SLOTS = 8
W_SLOTS = 3
W_PARTS = 2
T_FIN = 256
OUT_CHUNKS = 4
SC_ROWS = 64
SC_PIECES = SC_ROWS * PACK_TILES
SC_LISTS = SC_PIECES // LANES

VMEM_LIMIT = 56 * 1024 * 1024

_NEG_INF = float("-inf")


def _bf(x):
    return x.astype(jnp.bfloat16)


def _dot(a, b):
    return jnp.dot(a, b, preferred_element_type=jnp.float32)


def _dot_t(a, b):
    return lax.dot_general(a, b, (((1,), (1,)), ((), ())), preferred_element_type=jnp.float32)


def _layernorm(x, g, b, eps=1e-5):
    mu = jnp.mean(x, axis=-1, keepdims=True)
    xc = x - mu
    var = jnp.mean(xc * xc, axis=-1, keepdims=True)
    return xc * lax.rsqrt(var + eps) * g + b


def _rms(x, g, n, eps=1e-6):
    ms = jnp.sum(x * x, axis=-1, keepdims=True) * (1.0 / n)
    return x * lax.rsqrt(ms + eps) * g


def _sigmoid(x):
    return 1.0 / (1.0 + jnp.exp(-x))


def _rope(x, c, s_up, s_dn):
    return x * c + pltpu.roll(x, QK_ROPE // 2, 1) * s_up + pltpu.roll(x, LANES - QK_ROPE // 2, 1) * s_dn


def _rope_t(x, c, s_up, s_dn):
    return x * c + pltpu.roll(x, QK_ROPE // 2, 0) * s_up + pltpu.roll(x, LANES - QK_ROPE // 2, 0) * s_dn


def _inproj_body(x_ref, g_ref, b_ref, win_ref, qn_ref, wuq_ref, kvn_ref, wuk_ref, wuv_ref, cw_ref, con_ref,
                 tc_ref, tu_ref, td_ref, tct_ref, tut_ref, tdt_ref, vcinit_ref,
                 q_ref, k_ref, v_ref, cn_ref, vct_ref, carry_ref):
    rows = x_ref.shape[1]

    @pl.when(pl.program_id(1) == 0)
    def _():
        carry_ref[...] = vcinit_ref[...]

    h0 = _layernorm(x_ref[0], g_ref[...], b_ref[...])
    z = _dot(_bf(h0), win_ref[...])
    o1 = Q_LORA
    o2 = o1 + KV_LORA
    o3 = o2 + HEAD_PAD
    o4 = o3 + D_CONV
    o5 = o4 + D_CONV
    c, s_up, s_dn = tc_ref[...], tu_ref[...], td_ref[...]

    ct, st_up, st_dn = tct_ref[...], tut_ref[...], tdt_ref[...]
    q_t = _dot_t(wuq_ref[...], _bf(_rms(z[:, :o1], qn_ref[...], Q_LORA)))
    for h in range(ATTN_HEADS):
        sl = slice(h * HEAD_PAD, (h + 1) * HEAD_PAD)
        q_ref[0, sl, :] = _bf(_rope_t(q_t[sl, :], ct, st_up, st_dn) * ATTN_SCALE)

    ckv = _bf(_rms(z[:, o1:o2], kvn_ref[...], KV_LORA))
    kk = _dot(ckv, wuk_ref[...])
    kpe = _rope(z[:, o2:o3], c, s_up, s_dn)
    for h in range(ATTN_HEADS):
        sl = slice(h * HEAD_PAD, (h + 1) * HEAD_PAD)
        k_ref[0, :, sl] = _bf(kk[:, sl] + kpe)
    v_t = _dot_t(wuv_ref[...], ckv)
    dim = lax.broadcasted_iota(jnp.int32, (HEAD_PAD, rows), 0)
    for h in range(ATTN_HEADS):
        sl = slice(h * HEAD_PAD, (h + 1) * HEAD_PAD)
        v_ref[0, sl, :] = _bf(jnp.where(dim < V_HEAD, v_t[sl, :], 1.0))

    vc = z[:, o4:o5] * z[:, o5:]
    carry = carry_ref[...]
    row = lax.broadcasted_iota(jnp.int32, (rows, D_CONV), 0)
    vc1 = jnp.where(row == 0, carry[SUBLANES - 1:SUBLANES, :], pltpu.roll(vc, 1, 0))
    vc2 = pltpu.roll(vc, 2, 0)
    vc2 = jnp.where(row == 0, carry[SUBLANES - 2:SUBLANES - 1, :], vc2)
    vc2 = jnp.where(row == 1, carry[SUBLANES - 1:SUBLANES, :], vc2)
    cw = cw_ref[...]
    conv = z[:, o3:o4] * (cw[0:1, :] * vc2 + cw[1:2, :] * vc1 + cw[2:3, :] * vc)
    cn_ref[0] = _bf(_rms(conv, con_ref[...], D_CONV))
    tail = vc[rows - SUBLANES:, :]
    carry_ref[...] = tail
    vct_ref[0] = tail


def _inproj(x, tabs, vcinit, w, rows):
    bsz, seq, _ = x.shape
    nj = seq // rows
    full = lambda a: pl.BlockSpec(a.shape, lambda b, j: (0,) * a.ndim)
    row_blk = lambda width: pl.BlockSpec((1, rows, width), lambda b, j: (b, j, 0))
    tab = pl.BlockSpec((rows, HEAD_PAD), lambda b, j: (j, 0))
    tab_t = pl.BlockSpec((HEAD_PAD, rows), lambda b, j: (0, j))
    col_blk = pl.BlockSpec((1, ATTN_HEADS * HEAD_PAD, rows), lambda b, j: (b, 0, j))
    params = [w["emb_g"], w["emb_b"], w["w_in"], w["q_norm"], w["w_uq_t"], w["kv_norm"], w["w_uk"], w["w_uv_t"],
              w["conv_w"], w["conv_norm"]]
    wide = ATTN_HEADS * HEAD_PAD
    out_shape = [
        jax.ShapeDtypeStruct((bsz, wide, seq), jnp.bfloat16),
        jax.ShapeDtypeStruct((bsz, seq, wide), jnp.bfloat16),
        jax.ShapeDtypeStruct((bsz, wide, seq), jnp.bfloat16),
        jax.ShapeDtypeStruct((bsz, seq, D_CONV), jnp.bfloat16),
        jax.ShapeDtypeStruct((bsz, nj * SUBLANES, D_CONV), jnp.float32),
    ]
    tabs_t = [t.T for t in tabs]
    return pl.pallas_call(
        _inproj_body,
        grid=(bsz, nj),
        in_specs=([row_blk(D_MODEL)] + [full(p) for p in params] + [tab, tab, tab, tab_t, tab_t, tab_t]
                  + [full(vcinit)]),
        out_specs=[col_blk, row_blk(wide), col_blk, row_blk(D_CONV),
                   pl.BlockSpec((1, SUBLANES, D_CONV), lambda b, j: (b, j, 0))],
        out_shape=out_shape,
        scratch_shapes=[pltpu.VMEM((SUBLANES, D_CONV), jnp.float32)],
        compiler_params=pltpu.CompilerParams(dimension_semantics=("arbitrary", "arbitrary"),
                                             vmem_limit_bytes=VMEM_LIMIT),
        name="inproj",
    )(x, *params, *tabs, *tabs_t, vcinit)


def _attn_body(q_ref, k_ref, v_ref, km_ref, vm_ref, o_ref):
    i = pl.program_id(2)
    tq = q_ref.shape[2]
    heads = [slice(hh * HEAD_PAD, (hh + 1) * HEAD_PAD) for hh in range(2)]
    qs = [q_ref[0, sl, :] for sl in heads]

    part = tq // DIAG_PARTS
    is_meta = lax.broadcasted_iota(jnp.int32, (META_PAD, tq), 0) < N_META
    lead = lambda a, off, tail: tail if off == 0 else jnp.concatenate([a[:, :off], tail], axis=1)
    carry = []
    for q, sl in zip(qs, heads):
        sm = jnp.where(is_meta, _dot(km_ref[:, sl], q), _NEG_INF)
        m = jnp.max(sm, axis=0, keepdims=True)
        scores = []
        for c in range(DIAG_PARTS):
            off = c * part
            rows = pl.ds(pl.multiple_of(i * tq + off, part), part)
            causal = (lax.broadcasted_iota(jnp.int32, (part, tq - off), 0)
                      <= lax.broadcasted_iota(jnp.int32, (part, tq - off), 1))
            s = jnp.where(causal, _dot(k_ref[0, rows, sl], q[:, off:]), _NEG_INF)
            m = lead(m, off, jnp.maximum(m[:, off:], jnp.max(s, axis=0, keepdims=True)))
            scores.append((rows, off, s))
        acc = _dot(vm_ref[sl, :], _bf(jnp.exp(sm - m)))
        for rows, off, s in scores:
            acc = lead(acc, off, acc[:, off:] + _dot(v_ref[0, sl, rows], _bf(jnp.exp(s - m[:, off:]))))
        carry += [m, acc]

    def body(j, carry):
        rj = pl.multiple_of(j * tq, tq)
        ms, accs = (carry[0], carry[2]), (carry[1], carry[3])
        fast, tops = [], []
        for q, sl, m, acc in zip(qs, heads, ms, accs):
            s = _dot(k_ref[0, pl.ds(rj, tq), sl], q)
            tops.append(jnp.max(s, axis=0, keepdims=True))
            fast.append(acc + _dot(v_ref[0, sl, pl.ds(rj, tq)], _bf(jnp.exp(s - m))))
        excess = jnp.maximum(tops[0] - ms[0], tops[1] - ms[1])
        over = jnp.max(excess, axis=1, keepdims=True)[0, 0] > SHIFT_MARGIN

        def rescale(_):
            out = []
            for q, sl, m, acc, top in zip(qs, heads, ms, accs, tops):
                s = _dot(k_ref[0, pl.ds(rj, tq), sl], q)
                m_new = jnp.maximum(m, top)
                out += [m_new, jnp.exp(m - m_new) * acc + _dot(v_ref[0, sl, pl.ds(rj, tq)], _bf(jnp.exp(s - m_new)))]
            return tuple(out)

        return lax.cond(over, rescale, lambda _: (ms[0], fast[0], ms[1], fast[1]), 0)

    _, a0, _, a1 = lax.fori_loop(0, i, body, tuple(carry))
    o_t = jnp.concatenate([a[:V_HEAD, :] / a[V_HEAD:V_HEAD + 1, :] for a in (a0, a1)], axis=0)
    o_ref[0] = _bf(o_t.T)


def _attention(q_t, k, v_t, km, vm_t):
    bsz, seq, _ = k.shape
    pairs = ATTN_HEADS // 2
    pw = 2 * HEAD_PAD
    return pl.pallas_call(
        _attn_body,
        grid=(bsz, pairs, seq // T_Q),
        in_specs=[pl.BlockSpec((1, pw, T_Q), lambda b, p, i: (b, p, i)),
                  pl.BlockSpec((1, seq, pw), lambda b, p, i: (b, 0, p)),
                  pl.BlockSpec((1, pw, seq), lambda b, p, i: (b, p, 0)),
                  pl.BlockSpec((META_PAD, pw), lambda b, p, i: (0, p)),
                  pl.BlockSpec((pw, META_PAD), lambda b, p, i: (p, 0))],
        out_specs=pl.BlockSpec((1, T_Q, 2 * V_HEAD), lambda b, p, i: (b, i, p)),
        out_shape=jax.ShapeDtypeStruct((bsz, seq, D_ATTN), jnp.bfloat16),
        compiler_params=pltpu.CompilerParams(dimension_semantics=("arbitrary",) * 3,
                                             vmem_limit_bytes=VMEM_LIMIT),
        name="attention",
    )(q_t, k, v_t, km, vm_t)


def _mix_body(attn_ref, cn_ref, x_ref, eg_ref, eb_ref, an_ref, woa_ref, woc_ref, g1_ref, b1_ref,
              wsg_ref, wsu_ref, wsd_ref, wrt_ref, rb_ref, tri_ref,
              hp_ref, base_ref, eidx_ref, gate_ref, pos_ref, cnt_ref, run_ref):
    t = attn_ref.shape[0]

    @pl.when(pl.program_id(0) == 0)
    def _():
        run_ref[...] = jnp.zeros_like(run_ref)

    attn_n = _rms(attn_ref[...].astype(jnp.float32), an_ref[...], D_ATTN)
    mix = _dot(_bf(attn_n), woa_ref[...]) + _dot(cn_ref[...], woc_ref[...])
    h0 = _layernorm(x_ref[...], eg_ref[...], eb_ref[...])
    h1 = _layernorm(ALPHA * h0 + mix, g1_ref[...], b1_ref[...])
    _store_row_tiles(hp_ref, (), h1)
    h1b = _bf(h1)
    g = _dot(h1b, wsg_ref[...])
    u = _dot(h1b, wsu_ref[...])
    base_ref[...] = ALPHA * h1 + _dot(_bf(g * _sigmoid(g) * u), wsd_ref[...])

    shp = (N_GROUPS, GROUP_SIZE, t)
    sc = _sigmoid(_dot_t(wrt_ref[...], h1b)).reshape(shp)
    sel = sc + rb_ref[...].reshape(N_GROUPS, GROUP_SIZE, 1)
    gio = lax.broadcasted_iota(jnp.int32, shp, 0)
    rio = lax.broadcasted_iota(jnp.int32, shp, 1)
    eio = gio * GROUP_SIZE + rio

    m1 = jnp.max(sel, axis=1, keepdims=True)
    i1 = jnp.min(jnp.where(sel == m1, rio, GROUP_SIZE), axis=1, keepdims=True)
    m2 = jnp.max(jnp.where(rio == i1, _NEG_INF, sel), axis=1, keepdims=True)
    gs = m1 + m2
    gio1 = lax.broadcasted_iota(jnp.int32, (N_GROUPS, 1, t), 0)
    gmask = jnp.zeros((N_GROUPS, 1, t), jnp.bool_)
    for _ in range(TOPK_GROUPS):
        mg = jnp.max(gs, axis=0, keepdims=True)
        ig = jnp.min(jnp.where(gs == mg, gio1, N_GROUPS), axis=0, keepdims=True)
        hit = gio1 == ig
        gmask = jnp.logical_or(gmask, hit)
        gs = jnp.where(hit, _NEG_INF, gs)

    cand = jnp.where(gmask, sel, _NEG_INF)

    def red(x, op):
        return op(op(x, axis=0, keepdims=True), axis=1, keepdims=True)

    pm = jnp.zeros(shp, jnp.float32)
    e_rows, g_rows = [], []
    for _ in range(TOP_K):
        mk = red(cand, jnp.max)
        ek = red(jnp.where(cand == mk, eio, N_EXPERTS), jnp.min)
        hit = eio == ek
        g_rows.append(red(jnp.where(hit, sc, 0.0), jnp.sum))
        e_rows.append(ek)
        pm = jnp.where(hit, 1.0, pm)
        cand = jnp.where(hit, _NEG_INF, cand)
    gsum = g_rows[0]
    for gk in g_rows[1:]:
        gsum = gsum + gk

    pm = pm.reshape(N_EXPERTS, t)
    rank = _dot(_bf(pm), tri_ref[...])
    slot = (run_ref[...] + rank).reshape(shp)
    run_ref[...] = run_ref[...] + jnp.sum(pm, axis=1, keepdims=True)
    cnt_ref[...] = run_ref[...]
    for k in range(TOP_K):
        eidx_ref[0, k:k + 1, :] = e_rows[k].reshape(1, t)
        gate_ref[0, k:k + 1, :] = (g_rows[k] / gsum * ROUTED_SCALE).reshape(1, t)
        pk = red(jnp.where(eio == e_rows[k], slot, 0.0), jnp.sum)
        pos_ref[0, k:k + 1, :] = pk.reshape(1, t).astype(jnp.int32)


def _mix_route(attn, cn, x, w):
    n = attn.shape[0]
    nt = n // T_MIX
    tri = _bf(jnp.triu(jnp.ones((T_MIX, T_MIX), jnp.float32), 1))
    params = [w["emb_g"], w["emb_b"], w["attn_norm"], w["wo_a"], w["wo_c"], w["ln1_g"], w["ln1_b"], w["ws_gate"], w["ws_up"],
              w["ws_down"], w["w_router_t"], w["router_bias"], tri]
    full = lambda a: pl.BlockSpec(a.shape, lambda i: (0,) * a.ndim)
    row_blk = lambda width: pl.BlockSpec((T_MIX, width), lambda i: (i, 0))
    k_blk = pl.BlockSpec((1, TOP_K, T_MIX), lambda i: (i, 0, 0))
    out_shape = [
        jax.ShapeDtypeStruct((n // SUBLANES, PACK_TILES, SUBLANES, LANES), jnp.int32),
        jax.ShapeDtypeStruct((n, D_MODEL), jnp.float32),
        jax.ShapeDtypeStruct((nt, TOP_K, T_MIX), jnp.int32),
        jax.ShapeDtypeStruct((nt, TOP_K, T_MIX), jnp.float32),
        jax.ShapeDtypeStruct((nt, TOP_K, T_MIX), jnp.int32),
        jax.ShapeDtypeStruct((N_EXPERTS, 1), jnp.float32),
    ]
    return pl.pallas_call(
        _mix_body,
        grid=(nt,),
        in_specs=[row_blk(D_ATTN), row_blk(D_CONV), row_blk(D_MODEL)] + [full(p) for p in params],
        out_specs=[pl.BlockSpec((T_MIX // SUBLANES, PACK_TILES, SUBLANES, LANES), lambda i: (i, 0, 0, 0)),
                   row_blk(D_MODEL),
                   k_blk, k_blk, k_blk, pl.BlockSpec((N_EXPERTS, 1), lambda i: (0, 0))],
        out_shape=out_shape,
        scratch_shapes=[pltpu.VMEM((N_EXPERTS, 1), jnp.float32)],
        compiler_params=pltpu.CompilerParams(dimension_semantics=("arbitrary",),
                                             vmem_limit_bytes=VMEM_LIMIT),
        name="mix_route",
    )(attn, cn, x, *params)


def _pack_row_tiles(x):
    bits = lambda a: lax.bitcast_convert_type(_bf(a).astype(jnp.float32), jnp.int32)
    out = []
    for lt in range(PACK_TILES):
        lo = bits(x[:, lt * LANES:(lt + 1) * LANES])
        hi = bits(x[:, HALF + lt * LANES:HALF + (lt + 1) * LANES])
        out.append(lax.shift_right_logical(lo, BF16_BITS) | (hi & HI_MASK))
    return out


def _unpack_lo(w):
    return lax.bitcast_convert_type(lax.shift_left(w, BF16_BITS), jnp.float32)


def _unpack_hi(w):
    return lax.bitcast_convert_type(w & HI_MASK, jnp.float32)


def _store_row_tiles(ref, lead, x):
    rows = x.shape[0]
    for lt, w in enumerate(_pack_row_tiles(x)):
        ref[lead + (slice(None), lt)] = w.reshape(rows // SUBLANES, SUBLANES, LANES)


def _load_row_tiles(ref, lead, rows):
    return [ref[lead + (slice(None), lt)].reshape(rows, LANES) for lt in range(PACK_TILES)]


def _sc_mesh():
    return plsc.VectorSubcoreMesh(core_axis_name="c", subcore_axis_name="s")


def _worker_id():
    return lax.axis_index("s") * SC_CORES + lax.axis_index("c")


def _dispatch(hp, idx, n_rows):
    per_worker = hp.shape[0] // SC_PIECES // (SC_CORES * SC_SUBCORES)
    lists = idx.shape[1]

    @functools.partial(
        pl.kernel, mesh=_sc_mesh(),
        out_type=jax.ShapeDtypeStruct((n_rows, LANES), jnp.int32),
        scratch_types=[pltpu.VMEM((lists, LANES), jnp.int32),
                       pltpu.VMEM((SC_PIECES, LANES), jnp.int32),
                       pltpu.SemaphoreType.DMA],
        name="dispatch")
    def run(hp_hbm, idx_hbm, xs_hbm, idx_v, rows_v, sem):
        first = _worker_id() * per_worker

        @pl.loop(0, per_worker)
        def _(c):
            chunk = first + c
            pltpu.sync_copy(idx_hbm.at[chunk], idx_v)
            pltpu.sync_copy(hp_hbm.at[pl.ds(chunk * SC_PIECES, SC_PIECES)], rows_v)
            copies = [pltpu.async_copy(rows_v.at[pl.ds((j // TOP_K) * LANES, LANES)],
                                       xs_hbm.at[idx_v.at[j]], sem) for j in range(lists)]
            for cp in copies:
                cp.wait()

    return run(hp, idx)


def _combine(ys, idx, n_rows, first_chunk):
    per_worker = n_rows // SC_PIECES // (SC_CORES * SC_SUBCORES)
    lists = idx.shape[1]

    @functools.partial(
        pl.kernel, mesh=_sc_mesh(),
        out_type=jax.ShapeDtypeStruct((TOP_K, n_rows, LANES), jnp.int32),
        scratch_types=[pltpu.VMEM((lists, LANES), jnp.int32),
                       pltpu.VMEM((2, SC_PIECES, LANES), jnp.int32),
                       pltpu.SemaphoreType.DMA, pltpu.SemaphoreType.DMA],
        name="combine")
    def run(ys_hbm, idx_hbm, out_hbm, idx_v, rows_v, sem0, sem1):
        first = _worker_id() * per_worker
        sems = (sem0, sem1)

        def gather(k):
            buf = rows_v.at[k % 2]
            return [pltpu.async_copy(ys_hbm.at[idx_v.at[g * TOP_K + k]],
                                     buf.at[pl.ds(g * LANES, LANES)], sems[k % 2]) for g in range(SC_LISTS)]

        @pl.loop(0, per_worker)
        def _(c):
            chunk = first + c
            pltpu.sync_copy(idx_hbm.at[first_chunk + chunk], idx_v)
            pending = gather(0)
            for k in range(TOP_K):
                for cp in pending:
                    cp.wait()
                if k + 1 < TOP_K:
                    pending = gather(k + 1)
                pltpu.sync_copy(rows_v.at[k % 2], out_hbm.at[k, pl.ds(chunk * SC_PIECES, SC_PIECES)])

    return run(ys, idx)


def _expert_body(first_ref, count_ref, nused_ref, xs_hbm, wg_hbm, wu_hbm, wd_hbm, ys_hbm,
                 xbuf, ybuf, wgbuf, wubuf, wdbuf, wgu_ref, wdn_ref, sem_in, sem_out, sem_w):
    e = pl.program_id(0)
    nused = nused_ref[0]
    tiles = BLK // SUBLANES
    ahead = ROW_SLOTS - BLOCK_GROUP

    def fetch(b):
        slot = b % ROW_SLOTS
        return pltpu.make_async_copy(xs_hbm.at[pl.ds(b * tiles, tiles)], xbuf.at[slot], sem_in.at[slot])

    def writeback(b):
        slot = b % ROW_SLOTS
        return pltpu.make_async_copy(ybuf.at[slot], ys_hbm.at[pl.ds(b * tiles, tiles)], sem_out.at[slot])

    def weights(ex):
        slot = ex % W_SLOTS
        copies = []
        for src, dst in ((wg_hbm, wgbuf), (wu_hbm, wubuf), (wd_hbm, wdbuf)):
            part = src.shape[1] // W_PARTS
            for p in range(W_PARTS):
                rows = pl.ds(p * part, part)
                copies.append(pltpu.make_async_copy(src.at[ex, rows], dst.at[slot, rows], sem_w.at[slot]))
        return copies

    @pl.when(e == 0)
    def _():
        for b in range(ahead):
            @pl.when(b < nused)
            def _():
                fetch(b).start()
        for ex in range(W_SLOTS - 1):
            for cp in weights(ex):
                cp.start()

    for cp in weights(e):
        cp.wait()

    @pl.when(e + W_SLOTS - 1 < pl.num_programs(0))
    def _():
        for cp in weights(e + W_SLOTS - 1):
            cp.start()

    @pl.when(count_ref[e] > 0)
    def _():
        slot = e % W_SLOTS
        wgu_ref[:, :D_EXPERT] = _bf(wgbuf[slot])
        wgu_ref[:, D_EXPERT:] = _bf(wubuf[slot])
        wdn_ref[...] = _bf(wdbuf[slot])

    def run(b0, nblk):
        blocks = [b0 + c for c in range(nblk)]
        for b in blocks:
            fetch(b).wait()
        for b in blocks:
            @pl.when(b + ahead < nused)
            def _():
                fetch(b + ahead).start()

        words = [jnp.concatenate(ws, axis=0) for ws in
                 zip(*[_load_row_tiles(xbuf, (b % ROW_SLOTS,), BLK) for b in blocks])]
        x = jnp.concatenate([_bf(_unpack_lo(w)) for w in words] + [_bf(_unpack_hi(w)) for w in words], axis=1)
        gu = _dot(x, wgu_ref[...])
        g, u = gu[:, :D_EXPERT], gu[:, D_EXPERT:]
        y = _dot(_bf(g * _sigmoid(g) * u), wdn_ref[...])

        for c, b in enumerate(blocks):
            @pl.when(b >= ROW_SLOTS)
            def _():
                writeback(b - ROW_SLOTS).wait()

            _store_row_tiles(ybuf, (b % ROW_SLOTS,), y[c * BLK:(c + 1) * BLK])
            writeback(b).start()

    start, left = first_ref[e], count_ref[e]
    for size in range(BLOCK_GROUP, 0, -1):
        def group(t, carry, start=start, size=size):
            run(start + size * t, size)
            return carry

        lax.fori_loop(0, left // size, group, 0)
        start, left = start + left // size * size, left % size

    @pl.when(e == pl.num_programs(0) - 1)
    def _():
        for back in range(1, ROW_SLOTS + 1):
            @pl.when(nused >= back)
            def _():
                writeback(nused - back).wait()


def _experts(xs, first_blk, n_blk, nused, w_gate, w_up, w_down):
    blk = (BLK // SUBLANES, PACK_TILES, SUBLANES, LANES)
    grid_spec = pltpu.PrefetchScalarGridSpec(
        num_scalar_prefetch=3,
        grid=(N_EXPERTS,),
        in_specs=[pl.BlockSpec(memory_space=pl.ANY)] * 4,
        out_specs=pl.BlockSpec(memory_space=pl.ANY),
        scratch_shapes=[pltpu.VMEM((ROW_SLOTS,) + blk, jnp.int32),
                        pltpu.VMEM((ROW_SLOTS,) + blk, jnp.int32),
                        pltpu.VMEM((W_SLOTS, D_MODEL, D_EXPERT), jnp.float32),
                        pltpu.VMEM((W_SLOTS, D_MODEL, D_EXPERT), jnp.float32),
                        pltpu.VMEM((W_SLOTS, D_EXPERT, D_MODEL), jnp.float32),
                        pltpu.VMEM((D_MODEL, 2 * D_EXPERT), jnp.bfloat16),
                        pltpu.VMEM((D_EXPERT, D_MODEL), jnp.bfloat16),
                        pltpu.SemaphoreType.DMA((ROW_SLOTS,)),
                        pltpu.SemaphoreType.DMA((ROW_SLOTS,)),
                        pltpu.SemaphoreType.DMA((W_SLOTS,))],
    )
    return pl.pallas_call(
        _expert_body,
        grid_spec=grid_spec,
        out_shape=jax.ShapeDtypeStruct(xs.shape, jnp.int32),
        compiler_params=pltpu.CompilerParams(dimension_semantics=("arbitrary",),
                                             vmem_limit_bytes=VMEM_LIMIT),
        name="experts",
    )(first_blk, n_blk, nused, xs, w_gate, w_up, w_down)


def _final_body(yg_ref, gate_ref, base_ref, g2_ref, b2_ref, *rest):
    o_ref = rest[-1]
    t = base_ref.shape[0]
    gate = gate_ref[...]
    lo = [None] * PACK_TILES
    hi = [None] * PACK_TILES
    for k in range(TOP_K):
        gk = gate[:, k:k + 1]
        for lt, w in enumerate(_load_row_tiles(yg_ref, (k,), t)):
            a, b = gk * _unpack_lo(w), gk * _unpack_hi(w)
            lo[lt] = a if k == 0 else lo[lt] + a
            hi[lt] = b if k == 0 else hi[lt] + b
    ff = jnp.concatenate(lo + hi, axis=1)
    o_ref[...] = _layernorm(base_ref[...] + ff, g2_ref[...], b2_ref[...])


def _final(yg, gates, base, g2, b2, chunk, prev_out):
    n = base.shape[0]
    steps = yg.shape[1] * SUBLANES // T_FIN
    off = chunk * steps
    full = lambda a: pl.BlockSpec(a.shape, lambda i: (0,) * a.ndim)
    in_specs = [pl.BlockSpec((TOP_K, T_FIN // SUBLANES, PACK_TILES, SUBLANES, LANES),
                             lambda i: (0, i, 0, 0, 0)),
                pl.BlockSpec((T_FIN, TOP_K), lambda i: (off + i, 0)),
                pl.BlockSpec((T_FIN, D_MODEL), lambda i: (off + i, 0)),
                full(g2), full(b2)]
    args = [yg, gates, base, g2, b2]
    aliases = {}
    if prev_out is not None:
        in_specs.append(pl.BlockSpec(memory_space=pl.ANY))
        aliases = {len(args): 0}
        args.append(prev_out)
    return pl.pallas_call(
        _final_body,
        grid=(steps,),
        in_specs=in_specs,
        out_specs=pl.BlockSpec((T_FIN, D_MODEL), lambda i: (off + i, 0)),
        out_shape=jax.ShapeDtypeStruct((n, D_MODEL), jnp.float32),
        input_output_aliases=aliases,
        compiler_params=pltpu.CompilerParams(dimension_semantics=("arbitrary",),
                                             vmem_limit_bytes=VMEM_LIMIT),
        name="final",
    )(*args)


def _rope_tables(pos):
    half = QK_ROPE // 2
    inv = np.float32(ROPE_THETA) ** (-np.arange(0, QK_ROPE, 2, dtype=np.float32) / np.float32(QK_ROPE))
    ang = pos.astype(np.float32)[:, None] * inv[None, :]
    cos, sin = np.cos(ang).astype(np.float32), np.sin(ang).astype(np.float32)
    n = pos.shape[0]
    z = lambda w: np.zeros((n, w), np.float32)
    c = np.concatenate([np.ones((n, QK_NOPE), np.float32), cos, cos, z(LANES - QK_NOPE - QK_ROPE)], axis=1)
    s_up = np.concatenate([z(QK_NOPE + half), sin, z(LANES - QK_NOPE - QK_ROPE)], axis=1)
    s_dn = np.concatenate([z(QK_NOPE), -sin, z(half), z(LANES - QK_NOPE - QK_ROPE)], axis=1)
    return c, s_up, s_dn


def _pad_heads(w, per_head, keep):
    rows = w.shape[0]
    w = w.reshape(rows, ATTN_HEADS, per_head)[:, :, :keep]
    w = jnp.pad(w, ((0, 0), (0, 0), (0, HEAD_PAD - keep)))
    return w.reshape(rows, ATTN_HEADS * HEAD_PAD)


def _prep_weights(emb_ln_g, emb_ln_b, w_in, q_norm, w_uq, kv_norm, w_ukv, conv_w, attn_out_norm,
                  conv_out_norm, w_o, ln1_g, ln1_b, w_router, router_bias, ws_gate, ws_up, ws_down):
    o1 = Q_LORA
    o2 = o1 + KV_LORA
    o3 = o2 + QK_ROPE
    row = lambda a: a.reshape(1, -1)
    kpe_cols = jnp.pad(w_in[:, o2:o3], ((0, 0), (QK_NOPE, HEAD_PAD - QK_NOPE - QK_ROPE)))
    w_in_p = jnp.concatenate([w_in[:, :o2], kpe_cols, w_in[:, o3:]], axis=1)
    ukv = w_ukv.reshape(KV_LORA, ATTN_HEADS, QK_NOPE + V_HEAD)
    w_uk = jnp.pad(ukv[:, :, :QK_NOPE], ((0, 0), (0, 0), (0, HEAD_PAD - QK_NOPE)))
    w_uv = jnp.pad(ukv[:, :, QK_NOPE:], ((0, 0), (0, 0), (0, HEAD_PAD - V_HEAD)))
    woa = w_o[:D_ATTN]
    return {
        "emb_g": row(emb_ln_g), "emb_b": row(emb_ln_b),
        "w_in": _bf(w_in_p),
        "q_norm": row(q_norm), "w_uq_t": _bf(_pad_heads(w_uq, QK_NOPE + QK_ROPE, QK_NOPE + QK_ROPE).T),
        "kv_norm": row(kv_norm), "w_uk": _bf(w_uk.reshape(KV_LORA, -1)), "w_uv_t": _bf(w_uv.reshape(KV_LORA, -1).T),
        "conv_w": jnp.pad(conv_w, ((0, SUBLANES - CONV_W), (0, 0))), "conv_norm": row(conv_out_norm),
        "attn_norm": row(attn_out_norm), "wo_a": _bf(woa), "wo_c": _bf(w_o[D_ATTN:]),
        "ln1_g": row(ln1_g), "ln1_b": row(ln1_b),
        "ws_gate": _bf(ws_gate), "ws_up": _bf(ws_up), "ws_down": _bf(ws_down),
        "w_router_t": _bf(w_router.T), "router_bias": router_bias.reshape(-1, 1),
    }


def _slot_body(eidx_ref, pos_ref, start_ref, idx_ref, base_ref):
    t = eidx_ref.shape[2]
    shp = (N_GROUPS, GROUP_SIZE, t)
    eio = (lax.broadcasted_iota(jnp.int32, shp, 0) * GROUP_SIZE + lax.broadcasted_iota(jnp.int32, shp, 1))
    start = start_ref[...].reshape(N_GROUPS, GROUP_SIZE, 1)
    for k in range(TOP_K):
        hit = eio == eidx_ref[0, k:k + 1, :].reshape(1, 1, t)
        st = jnp.sum(jnp.sum(jnp.where(hit, start, 0.0), axis=0, keepdims=True), axis=1, keepdims=True)
        d = st.reshape(1, t).astype(jnp.int32) + pos_ref[0, k:k + 1, :]
        base_ref[k:k + 1, :] = (d // SUBLANES) * (PACK_TILES * SUBLANES) + d % SUBLANES

    j = lax.broadcasted_iota(jnp.int32, (TOP_K, LANES), 1)
    per_list = LANES // (PACK_TILES * SUBLANES)
    lane_tile = (j // SUBLANES) % PACK_TILES * SUBLANES
    for c in range(t // SC_ROWS):
        lane0 = c * SC_ROWS
        src = base_ref[:, lane0 // LANES * LANES:(lane0 // LANES + 1) * LANES]
        for g in range(SC_LISTS):
            tok = lane0 % LANES + (g * per_list + j // (PACK_TILES * SUBLANES)) * SUBLANES + j % SUBLANES
            idx_ref[c, g * TOP_K:(g + 1) * TOP_K, :] = jnp.take_along_axis(src, tok, axis=1) + lane_tile


def _slots(eidx, pos, start):
    nt, _, t = eidx.shape
    per_tile = t // SC_ROWS
    k_blk = pl.BlockSpec((1, TOP_K, t), lambda i: (i, 0, 0))
    return pl.pallas_call(
        _slot_body,
        grid=(nt,),
        in_specs=[k_blk, k_blk, pl.BlockSpec(start.shape, lambda i: (0, 0))],
        out_specs=pl.BlockSpec((per_tile, SC_LISTS * TOP_K, LANES), lambda i: (i, 0, 0)),
        out_shape=jax.ShapeDtypeStruct((nt * per_tile, SC_LISTS * TOP_K, LANES), jnp.int32),
        scratch_shapes=[pltpu.VMEM((TOP_K, t), jnp.int32)],
        compiler_params=pltpu.CompilerParams(dimension_semantics=("arbitrary",)),
        name="slots",
    )(eidx, pos, start)


def _slot_plan(eidx, pos, counts):
    counts = counts.reshape(-1).astype(jnp.int32)
    padded = (counts + BLK - 1) // BLK * BLK
    pad_end = jnp.cumsum(padded)
    pad_start = pad_end - padded
    idx = _slots(eidx, pos, pad_start.astype(jnp.float32).reshape(-1, 1))
    nb = eidx.size // BLK + N_EXPERTS
    nused = (pad_end[-1:] // BLK).astype(jnp.int32)
    return idx, (pad_start // BLK).astype(jnp.int32), (padded // BLK).astype(jnp.int32), nused, nb * BLK


def kernel(x, meta_tokens, emb_ln_g, emb_ln_b, w_in, q_norm, w_uq, kv_norm, w_ukv, conv_w, attn_out_norm, conv_out_norm, w_o, ln1_g, ln1_b, w_router, router_bias, w_gate, w_up, w_down, ws_gate, ws_up, ws_down, ln2_g, ln2_b):
    bsz, seq, _ = x.shape
    n = bsz * seq
    w = _prep_weights(emb_ln_g, emb_ln_b, w_in[0], q_norm[0], w_uq[0], kv_norm[0], w_ukv[0], conv_w[0],
                      attn_out_norm[0], conv_out_norm[0], w_o[0], ln1_g[0], ln1_b[0], w_router[0],
                      router_bias[0], ws_gate[0], ws_up[0], ws_down[0])

    meta_out = _inproj(meta_tokens[None], _rope_tables(np.arange(N_META)),
                       jnp.zeros((SUBLANES, D_CONV), jnp.float32), w, N_META)
    km = jnp.pad(meta_out[1][0], ((0, META_PAD - N_META), (0, 0)))
    vm = jnp.pad(meta_out[2][0], ((0, 0), (0, META_PAD - N_META)))
    vc_meta = meta_out[4][0]

    q, k, v, cn, _ = _inproj(x, _rope_tables(N_META + np.arange(seq)), vc_meta, w, T_IN)
    attn = _attention(q, k, v, km, vm)

    hp, base, eidx, gates, pos, counts = _mix_route(
        attn.reshape(n, D_ATTN), cn.reshape(n, D_CONV), x.reshape(n, D_MODEL), w)
    idx, first_blk, n_blk, nused, n_slots = _slot_plan(eidx, pos, counts)
    tiled = lambda a, lead: a.reshape(lead + (-1, PACK_TILES, SUBLANES, LANES))

    xs = _dispatch(hp.reshape(-1, LANES), idx, n_slots * PACK_TILES)
    ys = _experts(tiled(xs, ()), first_blk, n_blk, nused, w_gate[0], w_up[0], w_down[0])
    gates_t = gates.transpose(0, 2, 1).reshape(n, TOP_K)
    g2, b2 = ln2_g[0].reshape(1, -1), ln2_b[0].reshape(1, -1)
    ys_flat = ys.reshape(-1, LANES)
    per = idx.shape[0] // OUT_CHUNKS
    out = None
    for c in range(OUT_CHUNKS):
        yg = tiled(_combine(ys_flat, idx, n // OUT_CHUNKS * PACK_TILES, c * per), (TOP_K,))
        out = _final(yg, gates_t, base, g2, b2, c, out)
    return out.reshape(bsz, seq, D_MODEL)
```

```python
import functools
import math

import jax
import jax.numpy as jnp
import numpy as np
from jax import lax
from jax.experimental import pallas as pl
from jax.experimental.pallas import tpu as pltpu
from jax.experimental.pallas import tpu_sc as plsc

D_MODEL = 1024
N_META = 16
ATTN_HEADS = 8
QK_NOPE = 64
QK_ROPE = 32
V_HEAD = 64
Q_LORA = 384
KV_LORA = 256
ROPE_THETA = 10000.0
D_ATTN = ATTN_HEADS * V_HEAD
D_CONV = 512
CONV_W = 3
ATTN_SCALE = 1.0 / math.sqrt(QK_NOPE + QK_ROPE)
N_EXPERTS = 256
TOP_K = 8
N_GROUPS = 8
GROUP_SIZE = N_EXPERTS // N_GROUPS
TOPK_GROUPS = 4
D_EXPERT = 256
ROUTED_SCALE = 2.5
DEPTH = 1
ALPHA = (2.0 * DEPTH) ** 0.25

LANES = 128
SUBLANES = 8
HEAD_PAD = LANES
META_PAD = LANES
HALF = D_MODEL // 2
PACK_TILES = HALF // LANES
BF16_BITS = 16
HI_MASK = -(1 << BF16_BITS)
SC_CORES = 2
SC_SUBCORES = 16

T_IN = 512
T_Q = 1024
DIAG_PARTS = 4
SHIFT_MARGIN = 40.0
T_MIX = 512
BLK = 256
BLOCK_GROUP = 3
ROW_SLOTS = 8
W_SLOTS = 3
W_PARTS = 2
T_FIN = 256
OUT_CHUNKS = 4
SC_ROWS = 64
SC_PIECES = SC_ROWS * PACK_TILES
SC_LISTS = SC_PIECES // LANES

VMEM_LIMIT = 56 * 1024 * 1024

_NEG_INF = float("-inf")


def _bf(x):
    return x.astype(jnp.bfloat16)


def _dot(a, b):
    return jnp.dot(a, b, preferred_element_type=jnp.float32)


def _dot_t(a, b):
    return lax.dot_general(a, b, (((1,), (1,)), ((), ())), preferred_element_type=jnp.float32)


def _layernorm(x, g, b, eps=1e-5):
    mu = jnp.mean(x, axis=-1, keepdims=True)
    xc = x - mu
    var = jnp.mean(xc * xc, axis=-1, keepdims=True)
    return xc * lax.rsqrt(var + eps) * g + b


def _rms(x, g, n, eps=1e-6):
    ms = jnp.sum(x * x, axis=-1, keepdims=True) * (1.0 / n)
    return x * lax.rsqrt(ms + eps) * g


def _sigmoid(x):
    return 1.0 / (1.0 + jnp.exp(-x))


def _rope(x, c, s_up, s_dn):
    return x * c + pltpu.roll(x, QK_ROPE // 2, 1) * s_up + pltpu.roll(x, LANES - QK_ROPE // 2, 1) * s_dn


def _rope_t(x, c, s_up, s_dn):
    return x * c + pltpu.roll(x, QK_ROPE // 2, 0) * s_up + pltpu.roll(x, LANES - QK_ROPE // 2, 0) * s_dn


def _inproj_body(x_ref, g_ref, b_ref, win_ref, qn_ref, wuq_ref, kvn_ref, wuk_ref, wuv_ref, cw_ref, con_ref,
                 tc_ref, tu_ref, td_ref, tct_ref, tut_ref, tdt_ref, vcinit_ref,
                 q_ref, k_ref, v_ref, cn_ref, vct_ref, carry_ref):
    rows = x_ref.shape[1]

    @pl.when(pl.program_id(1) == 0)
    def _():
        carry_ref[...] = vcinit_ref[...]

    h0 = _layernorm(x_ref[0], g_ref[...], b_ref[...])
    z = _dot(_bf(h0), win_ref[...])
    o1 = Q_LORA
    o2 = o1 + KV_LORA
    o3 = o2 + HEAD_PAD
    o4 = o3 + D_CONV
    o5 = o4 + D_CONV
    c, s_up, s_dn = tc_ref[...], tu_ref[...], td_ref[...]

    ct, st_up, st_dn = tct_ref[...], tut_ref[...], tdt_ref[...]
    q_t = _dot_t(wuq_ref[...], _bf(_rms(z[:, :o1], qn_ref[...], Q_LORA)))
    for h in range(ATTN_HEADS):
        sl = slice(h * HEAD_PAD, (h + 1) * HEAD_PAD)
        q_ref[0, sl, :] = _bf(_rope_t(q_t[sl, :], ct, st_up, st_dn) * ATTN_SCALE)

    ckv = _bf(_rms(z[:, o1:o2], kvn_ref[...], KV_LORA))
    kk = _dot(ckv, wuk_ref[...])
    kpe = _rope(z[:, o2:o3], c, s_up, s_dn)
    for h in range(ATTN_HEADS):
        sl = slice(h * HEAD_PAD, (h + 1) * HEAD_PAD)
        k_ref[0, :, sl] = _bf(kk[:, sl] + kpe)
    v_t = _dot_t(wuv_ref[...], ckv)
    dim = lax.broadcasted_iota(jnp.int32, (HEAD_PAD, rows), 0)
    for h in range(ATTN_HEADS):
        sl = slice(h * HEAD_PAD, (h + 1) * HEAD_PAD)
        v_ref[0, sl, :] = _bf(jnp.where(dim < V_HEAD, v_t[sl, :], 1.0))

    vc = z[:, o4:o5] * z[:, o5:]
    carry = carry_ref[...]
    row = lax.broadcasted_iota(jnp.int32, (rows, D_CONV), 0)
    vc1 = jnp.where(row == 0, carry[SUBLANES - 1:SUBLANES, :], pltpu.roll(vc, 1, 0))
    vc2 = pltpu.roll(vc, 2, 0)
    vc2 = jnp.where(row == 0, carry[SUBLANES - 2:SUBLANES - 1, :], vc2)
    vc2 = jnp.where(row == 1, carry[SUBLANES - 1:SUBLANES, :], vc2)
    cw = cw_ref[...]
    conv = z[:, o3:o4] * (cw[0:1, :] * vc2 + cw[1:2, :] * vc1 + cw[2:3, :] * vc)
    cn_ref[0] = _bf(_rms(conv, con_ref[...], D_CONV))
    tail = vc[rows - SUBLANES:, :]
    carry_ref[...] = tail
    vct_ref[0] = tail


def _inproj(x, tabs, vcinit, w, rows):
    bsz, seq, _ = x.shape
    nj = seq // rows
    full = lambda a: pl.BlockSpec(a.shape, lambda b, j: (0,) * a.ndim)
    row_blk = lambda width: pl.BlockSpec((1, rows, width), lambda b, j: (b, j, 0))
    tab = pl.BlockSpec((rows, HEAD_PAD), lambda b, j: (j, 0))
    tab_t = pl.BlockSpec((HEAD_PAD, rows), lambda b, j: (0, j))
    col_blk = pl.BlockSpec((1, ATTN_HEADS * HEAD_PAD, rows), lambda b, j: (b, 0, j))
    params = [w["emb_g"], w["emb_b"], w["w_in"], w["q_norm"], w["w_uq_t"], w["kv_norm"], w["w_uk"], w["w_uv_t"],
              w["conv_w"], w["conv_norm"]]
    wide = ATTN_HEADS * HEAD_PAD
    out_shape = [
        jax.ShapeDtypeStruct((bsz, wide, seq), jnp.bfloat16),
        jax.ShapeDtypeStruct((bsz, seq, wide), jnp.bfloat16),
        jax.ShapeDtypeStruct((bsz, wide, seq), jnp.bfloat16),
        jax.ShapeDtypeStruct((bsz, seq, D_CONV), jnp.bfloat16),
        jax.ShapeDtypeStruct((bsz, nj * SUBLANES, D_CONV), jnp.float32),
    ]
    tabs_t = [t.T for t in tabs]
    return pl.pallas_call(
        _inproj_body,
        grid=(bsz, nj),
        in_specs=([row_blk(D_MODEL)] + [full(p) for p in params] + [tab, tab, tab, tab_t, tab_t, tab_t]
                  + [full(vcinit)]),
        out_specs=[col_blk, row_blk(wide), col_blk, row_blk(D_CONV),
                   pl.BlockSpec((1, SUBLANES, D_CONV), lambda b, j: (b, j, 0))],
        out_shape=out_shape,
        scratch_shapes=[pltpu.VMEM((SUBLANES, D_CONV), jnp.float32)],
        compiler_params=pltpu.CompilerParams(dimension_semantics=("arbitrary", "arbitrary"),
                                             vmem_limit_bytes=VMEM_LIMIT),
        name="inproj",
    )(x, *params, *tabs, *tabs_t, vcinit)


def _attn_body(q_ref, k_ref, v_ref, km_ref, vm_ref, o_ref):
    i = pl.program_id(2)
    tq = q_ref.shape[2]
    heads = [slice(hh * HEAD_PAD, (hh + 1) * HEAD_PAD) for hh in range(2)]
    qs = [q_ref[0, sl, :] for sl in heads]

    part = tq // DIAG_PARTS
    is_meta = lax.broadcasted_iota(jnp.int32, (META_PAD, tq), 0) < N_META
    lead = lambda a, off, tail: tail if off == 0 else jnp.concatenate([a[:, :off], tail], axis=1)
    carry = []
    for q, sl in zip(qs, heads):
        sm = jnp.where(is_meta, _dot(km_ref[:, sl], q), _NEG_INF)
        m = jnp.max(sm, axis=0, keepdims=True)
        scores = []
        for c in range(DIAG_PARTS):
            off = c * part
            rows = pl.ds(pl.multiple_of(i * tq + off, part), part)
            causal = (lax.broadcasted_iota(jnp.int32, (part, tq - off), 0)
                      <= lax.broadcasted_iota(jnp.int32, (part, tq - off), 1))
            s = jnp.where(causal, _dot(k_ref[0, rows, sl], q[:, off:]), _NEG_INF)
            m = lead(m, off, jnp.maximum(m[:, off:], jnp.max(s, axis=0, keepdims=True)))
            scores.append((rows, off, s))
        acc = _dot(vm_ref[sl, :], _bf(jnp.exp(sm - m)))
        for rows, off, s in scores:
            acc = lead(acc, off, acc[:, off:] + _dot(v_ref[0, sl, rows], _bf(jnp.exp(s - m[:, off:]))))
        carry += [m, acc]

    def body(j, carry):
        rj = pl.multiple_of(j * tq, tq)
        ms, accs = (carry[0], carry[2]), (carry[1], carry[3])
        fast, tops = [], []
        for q, sl, m, acc in zip(qs, heads, ms, accs):
            s = _dot(k_ref[0, pl.ds(rj, tq), sl], q)
            tops.append(jnp.max(s, axis=0, keepdims=True))
            fast.append(acc + _dot(v_ref[0, sl, pl.ds(rj, tq)], _bf(jnp.exp(s - m))))
        excess = jnp.maximum(tops[0] - ms[0], tops[1] - ms[1])
        over = jnp.max(excess, axis=1, keepdims=True)[0, 0] > SHIFT_MARGIN

        def rescale(_):
            out = []
            for q, sl, m, acc, top in zip(qs, heads, ms, accs, tops):
                s = _dot(k_ref[0, pl.ds(rj, tq), sl], q)
                m_new = jnp.maximum(m, top)
                out += [m_new, jnp.exp(m - m_new) * acc + _dot(v_ref[0, sl, pl.ds(rj, tq)], _bf(jnp.exp(s - m_new)))]
            return tuple(out)

        return lax.cond(over, rescale, lambda _: (ms[0], fast[0], ms[1], fast[1]), 0)

    _, a0, _, a1 = lax.fori_loop(0, i, body, tuple(carry))
    o_t = jnp.concatenate([a[:V_HEAD, :] / a[V_HEAD:V_HEAD + 1, :] for a in (a0, a1)], axis=0)
    o_ref[0] = _bf(o_t.T)


def _attention(q_t, k, v_t, km, vm_t):
    bsz, seq, _ = k.shape
    pairs = ATTN_HEADS // 2
    pw = 2 * HEAD_PAD
    return pl.pallas_call(
        _attn_body,
        grid=(bsz, pairs, seq // T_Q),
        in_specs=[pl.BlockSpec((1, pw, T_Q), lambda b, p, i: (b, p, i)),
                  pl.BlockSpec((1, seq, pw), lambda b, p, i: (b, 0, p)),
                  pl.BlockSpec((1, pw, seq), lambda b, p, i: (b, p, 0)),
                  pl.BlockSpec((META_PAD, pw), lambda b, p, i: (0, p)),
                  pl.BlockSpec((pw, META_PAD), lambda b, p, i: (p, 0))],
        out_specs=pl.BlockSpec((1, T_Q, 2 * V_HEAD), lambda b, p, i: (b, i, p)),
        out_shape=jax.ShapeDtypeStruct((bsz, seq, D_ATTN), jnp.bfloat16),
        compiler_params=pltpu.CompilerParams(dimension_semantics=("arbitrary",) * 3,
                                             vmem_limit_bytes=VMEM_LIMIT),
        name="attention",
    )(q_t, k, v_t, km, vm_t)


def _mix_body(attn_ref, cn_ref, x_ref, eg_ref, eb_ref, an_ref, woa_ref, woc_ref, g1_ref, b1_ref,
              wsg_ref, wsu_ref, wsd_ref, wrt_ref, rb_ref, tri_ref,
              hp_ref, base_ref, eidx_ref, gate_ref, pos_ref, cnt_ref, run_ref):
    t = attn_ref.shape[0]

    @pl.when(pl.program_id(0) == 0)
    def _():
        run_ref[...] = jnp.zeros_like(run_ref)

    attn_n = _rms(attn_ref[...].astype(jnp.float32), an_ref[...], D_ATTN)
    mix = _dot(_bf(attn_n), woa_ref[...]) + _dot(cn_ref[...], woc_ref[...])
    h0 = _layernorm(x_ref[...], eg_ref[...], eb_ref[...])
    h1 = _layernorm(ALPHA * h0 + mix, g1_ref[...], b1_ref[...])
    _store_row_tiles(hp_ref, (), h1)
    h1b = _bf(h1)
    g = _dot(h1b, wsg_ref[...])
    u = _dot(h1b, wsu_ref[...])
    base_ref[...] = ALPHA * h1 + _dot(_bf(g * _sigmoid(g) * u), wsd_ref[...])

    shp = (N_GROUPS, GROUP_SIZE, t)
    sc = _sigmoid(_dot_t(wrt_ref[...], h1b)).reshape(shp)
    sel = sc + rb_ref[...].reshape(N_GROUPS, GROUP_SIZE, 1)
    gio = lax.broadcasted_iota(jnp.int32, shp, 0)
    rio = lax.broadcasted_iota(jnp.int32, shp, 1)
    eio = gio * GROUP_SIZE + rio

    m1 = jnp.max(sel, axis=1, keepdims=True)
    i1 = jnp.min(jnp.where(sel == m1, rio, GROUP_SIZE), axis=1, keepdims=True)
    m2 = jnp.max(jnp.where(rio == i1, _NEG_INF, sel), axis=1, keepdims=True)
    gs = m1 + m2
    gio1 = lax.broadcasted_iota(jnp.int32, (N_GROUPS, 1, t), 0)
    gmask = jnp.zeros((N_GROUPS, 1, t), jnp.bool_)
    for _ in range(TOPK_GROUPS):
        mg = jnp.max(gs, axis=0, keepdims=True)
        ig = jnp.min(jnp.where(gs == mg, gio1, N_GROUPS), axis=0, keepdims=True)
        hit = gio1 == ig
        gmask = jnp.logical_or(gmask, hit)
        gs = jnp.where(hit, _NEG_INF, gs)

    cand = jnp.where(gmask, sel, _NEG_INF)

    def red(x, op):
        return op(op(x, axis=0, keepdims=True), axis=1, keepdims=True)

    pm = jnp.zeros(shp, jnp.float32)
    e_rows, g_rows = [], []
    for _ in range(TOP_K):
        mk = red(cand, jnp.max)
        ek = red(jnp.where(cand == mk, eio, N_EXPERTS), jnp.min)
        hit = eio == ek
        g_rows.append(red(jnp.where(hit, sc, 0.0), jnp.sum))
        e_rows.append(ek)
        pm = jnp.where(hit, 1.0, pm)
        cand = jnp.where(hit, _NEG_INF, cand)
    gsum = g_rows[0]
    for gk in g_rows[1:]:
        gsum = gsum + gk

    pm = pm.reshape(N_EXPERTS, t)
    rank = _dot(_bf(pm), tri_ref[...])
    slot = (run_ref[...] + rank).reshape(shp)
    run_ref[...] = run_ref[...] + jnp.sum(pm, axis=1, keepdims=True)
    cnt_ref[...] = run_ref[...]
    for k in range(TOP_K):
        eidx_ref[0, k:k + 1, :] = e_rows[k].reshape(1, t)
        gate_ref[0, k:k + 1, :] = (g_rows[k] / gsum * ROUTED_SCALE).reshape(1, t)
        pk = red(jnp.where(eio == e_rows[k], slot, 0.0), jnp.sum)
        pos_ref[0, k:k + 1, :] = pk.reshape(1, t).astype(jnp.int32)


def _mix_route(attn, cn, x, w):
    n = attn.shape[0]
    nt = n // T_MIX
    tri = _bf(jnp.triu(jnp.ones((T_MIX, T_MIX), jnp.float32), 1))
    params = [w["emb_g"], w["emb_b"], w["attn_norm"], w["wo_a"], w["wo_c"], w["ln1_g"], w["ln1_b"], w["ws_gate"], w["ws_up"],
              w["ws_down"], w["w_router_t"], w["router_bias"], tri]
    full = lambda a: pl.BlockSpec(a.shape, lambda i: (0,) * a.ndim)
    row_blk = lambda width: pl.BlockSpec((T_MIX, width), lambda i: (i, 0))
    k_blk = pl.BlockSpec((1, TOP_K, T_MIX), lambda i: (i, 0, 0))
    out_shape = [
        jax.ShapeDtypeStruct((n // SUBLANES, PACK_TILES, SUBLANES, LANES), jnp.int32),
        jax.ShapeDtypeStruct((n, D_MODEL), jnp.float32),
        jax.ShapeDtypeStruct((nt, TOP_K, T_MIX), jnp.int32),
        jax.ShapeDtypeStruct((nt, TOP_K, T_MIX), jnp.float32),
        jax.ShapeDtypeStruct((nt, TOP_K, T_MIX), jnp.int32),
        jax.ShapeDtypeStruct((N_EXPERTS, 1), jnp.float32),
    ]
    return pl.pallas_call(
        _mix_body,
        grid=(nt,),
        in_specs=[row_blk(D_ATTN), row_blk(D_CONV), row_blk(D_MODEL)] + [full(p) for p in params],
        out_specs=[pl.BlockSpec((T_MIX // SUBLANES, PACK_TILES, SUBLANES, LANES), lambda i: (i, 0, 0, 0)),
                   row_blk(D_MODEL),
                   k_blk, k_blk, k_blk, pl.BlockSpec((N_EXPERTS, 1), lambda i: (0, 0))],
        out_shape=out_shape,
        scratch_shapes=[pltpu.VMEM((N_EXPERTS, 1), jnp.float32)],
        compiler_params=pltpu.CompilerParams(dimension_semantics=("arbitrary",),
                                             vmem_limit_bytes=VMEM_LIMIT),
        name="mix_route",
    )(attn, cn, x, *params)


def _pack_row_tiles(x):
    bits = lambda a: lax.bitcast_convert_type(_bf(a).astype(jnp.float32), jnp.int32)
    out = []
    for lt in range(PACK_TILES):
        lo = bits(x[:, lt * LANES:(lt + 1) * LANES])
        hi = bits(x[:, HALF + lt * LANES:HALF + (lt + 1) * LANES])
        out.append(lax.shift_right_logical(lo, BF16_BITS) | (hi & HI_MASK))
    return out


def _unpack_lo(w):
    return lax.bitcast_convert_type(lax.shift_left(w, BF16_BITS), jnp.float32)


def _unpack_hi(w):
    return lax.bitcast_convert_type(w & HI_MASK, jnp.float32)


def _store_row_tiles(ref, lead, x):
    rows = x.shape[0]
    for lt, w in enumerate(_pack_row_tiles(x)):
        ref[lead + (slice(None), lt)] = w.reshape(rows // SUBLANES, SUBLANES, LANES)


def _load_row_tiles(ref, lead, rows):
    return [ref[lead + (slice(None), lt)].reshape(rows, LANES) for lt in range(PACK_TILES)]


def _sc_mesh():
    return plsc.VectorSubcoreMesh(core_axis_name="c", subcore_axis_name="s")


def _worker_id():
    return lax.axis_index("s") * SC_CORES + lax.axis_index("c")


def _dispatch(hp, idx, n_rows):
    per_worker = hp.shape[0] // SC_PIECES // (SC_CORES * SC_SUBCORES)
    lists = idx.shape[1]

    @functools.partial(
        pl.kernel, mesh=_sc_mesh(),
        out_type=jax.ShapeDtypeStruct((n_rows, LANES), jnp.int32),
        scratch_types=[pltpu.VMEM((lists, LANES), jnp.int32),
                       pltpu.VMEM((SC_PIECES, LANES), jnp.int32),
                       pltpu.SemaphoreType.DMA],
        name="dispatch")
    def run(hp_hbm, idx_hbm, xs_hbm, idx_v, rows_v, sem):
        first = _worker_id() * per_worker

        @pl.loop(0, per_worker)
        def _(c):
            chunk = first + c
            pltpu.sync_copy(idx_hbm.at[chunk], idx_v)
            pltpu.sync_copy(hp_hbm.at[pl.ds(chunk * SC_PIECES, SC_PIECES)], rows_v)
            copies = [pltpu.async_copy(rows_v.at[pl.ds((j // TOP_K) * LANES, LANES)],
                                       xs_hbm.at[idx_v.at[j]], sem) for j in range(lists)]
            for cp in copies:
                cp.wait()

    return run(hp, idx)


def _combine(ys, idx, n_rows, first_chunk):
    per_worker = n_rows // SC_PIECES // (SC_CORES * SC_SUBCORES)
    lists = idx.shape[1]

    @functools.partial(
        pl.kernel, mesh=_sc_mesh(),
        out_type=jax.ShapeDtypeStruct((TOP_K, n_rows, LANES), jnp.int32),
        scratch_types=[pltpu.VMEM((lists, LANES), jnp.int32),
                       pltpu.VMEM((2, SC_PIECES, LANES), jnp.int32),
                       pltpu.SemaphoreType.DMA, pltpu.SemaphoreType.DMA],
        name="combine")
    def run(ys_hbm, idx_hbm, out_hbm, idx_v, rows_v, sem0, sem1):
        first = _worker_id() * per_worker
        sems = (sem0, sem1)

        def gather(k):
            buf = rows_v.at[k % 2]
            return [pltpu.async_copy(ys_hbm.at[idx_v.at[g * TOP_K + k]],
                                     buf.at[pl.ds(g * LANES, LANES)], sems[k % 2]) for g in range(SC_LISTS)]

        @pl.loop(0, per_worker)
        def _(c):
            chunk = first + c
            pltpu.sync_copy(idx_hbm.at[first_chunk + chunk], idx_v)
            pending = gather(0)
            for k in range(TOP_K):
                for cp in pending:
                    cp.wait()
                if k + 1 < TOP_K:
                    pending = gather(k + 1)
                pltpu.sync_copy(rows_v.at[k % 2], out_hbm.at[k, pl.ds(chunk * SC_PIECES, SC_PIECES)])

    return run(ys, idx)


def _expert_body(first_ref, count_ref, nused_ref, xs_hbm, wg_hbm, wu_hbm, wd_hbm, ys_hbm,
                 xbuf, ybuf, wgbuf, wubuf, wdbuf, wgu_ref, wdn_ref, sem_in, sem_out, sem_w):
    e = pl.program_id(0)
    nused = nused_ref[0]
    tiles = BLK // SUBLANES
    ahead = ROW_SLOTS - BLOCK_GROUP

    def fetch(b):
        slot = b % ROW_SLOTS
        return pltpu.make_async_copy(xs_hbm.at[pl.ds(b * tiles, tiles)], xbuf.at[slot], sem_in.at[slot])

    def writeback(b):
        slot = b % ROW_SLOTS
        return pltpu.make_async_copy(ybuf.at[slot], ys_hbm.at[pl.ds(b * tiles, tiles)], sem_out.at[slot])

    def weights(ex):
        slot = ex % W_SLOTS
        copies = []
        for src, dst in ((wg_hbm, wgbuf), (wu_hbm, wubuf), (wd_hbm, wdbuf)):
            part = src.shape[1] // W_PARTS
            for p in range(W_PARTS):
                rows = pl.ds(p * part, part)
                copies.append(pltpu.make_async_copy(src.at[ex, rows], dst.at[slot, rows], sem_w.at[slot]))
        return copies

    @pl.when(e == 0)
    def _():
        for b in range(ahead):
            @pl.when(b < nused)
            def _():
                fetch(b).start()
        for ex in range(W_SLOTS - 1):
            for cp in weights(ex):
                cp.start()

    for cp in weights(e):
        cp.wait()

    @pl.when(e + W_SLOTS - 1 < pl.num_programs(0))
    def _():
        for cp in weights(e + W_SLOTS - 1):
            cp.start()

    @pl.when(count_ref[e] > 0)
    def _():
        slot = e % W_SLOTS
        wgu_ref[:, :D_EXPERT] = _bf(wgbuf[slot])
        wgu_ref[:, D_EXPERT:] = _bf(wubuf[slot])
        wdn_ref[...] = _bf(wdbuf[slot])

    def run(b0, nblk):
        blocks = [b0 + c for c in range(nblk)]
        for b in blocks:
            fetch(b).wait()
        for b in blocks:
            @pl.when(b + ahead < nused)
            def _():
                fetch(b + ahead).start()

        words = [jnp.concatenate(ws, axis=0) for ws in
                 zip(*[_load_row_tiles(xbuf, (b % ROW_SLOTS,), BLK) for b in blocks])]
        x = jnp.concatenate([_bf(_unpack_lo(w)) for w in words] + [_bf(_unpack_hi(w)) for w in words], axis=1)
        gu = _dot(x, wgu_ref[...])
        g, u = gu[:, :D_EXPERT], gu[:, D_EXPERT:]
        y = _dot(_bf(g * _sigmoid(g) * u), wdn_ref[...])

        for c, b in enumerate(blocks):
            @pl.when(b >= ROW_SLOTS)
            def _():
                writeback(b - ROW_SLOTS).wait()

            _store_row_tiles(ybuf, (b % ROW_SLOTS,), y[c * BLK:(c + 1) * BLK])
            writeback(b).start()

    start, left = first_ref[e], count_ref[e]
    for size in range(BLOCK_GROUP, 0, -1):
        def group(t, carry, start=start, size=size):
            run(start + size * t, size)
            return carry

        lax.fori_loop(0, left // size, group, 0)
        start, left = start + left // size * size, left % size

    @pl.when(e == pl.num_programs(0) - 1)
    def _():
        for back in range(1, ROW_SLOTS + 1):
            @pl.when(nused >= back)
            def _():
                writeback(nused - back).wait()


def _experts(xs, first_blk, n_blk, nused, w_gate, w_up, w_down):
    blk = (BLK // SUBLANES, PACK_TILES, SUBLANES, LANES)
    grid_spec = pltpu.PrefetchScalarGridSpec(
        num_scalar_prefetch=3,
        grid=(N_EXPERTS,),
        in_specs=[pl.BlockSpec(memory_space=pl.ANY)] * 4,
        out_specs=pl.BlockSpec(memory_space=pl.ANY),
        scratch_shapes=[pltpu.VMEM((ROW_SLOTS,) + blk, jnp.int32),
                        pltpu.VMEM((ROW_SLOTS,) + blk, jnp.int32),
                        pltpu.VMEM((W_SLOTS, D_MODEL, D_EXPERT), jnp.float32),
                        pltpu.VMEM((W_SLOTS, D_MODEL, D_EXPERT), jnp.float32),
                        pltpu.VMEM((W_SLOTS, D_EXPERT, D_MODEL), jnp.float32),
                        pltpu.VMEM((D_MODEL, 2 * D_EXPERT), jnp.bfloat16),
                        pltpu.VMEM((D_EXPERT, D_MODEL), jnp.bfloat16),
                        pltpu.SemaphoreType.DMA((ROW_SLOTS,)),
                        pltpu.SemaphoreType.DMA((ROW_SLOTS,)),
                        pltpu.SemaphoreType.DMA((W_SLOTS,))],
    )
    return pl.pallas_call(
        _expert_body,
        grid_spec=grid_spec,
        out_shape=jax.ShapeDtypeStruct(xs.shape, jnp.int32),
        compiler_params=pltpu.CompilerParams(dimension_semantics=("arbitrary",),
                                             vmem_limit_bytes=VMEM_LIMIT),
        name="experts",
    )(first_blk, n_blk, nused, xs, w_gate, w_up, w_down)


def _final_body(yg_ref, gate_ref, base_ref, g2_ref, b2_ref, *rest):
    o_ref = rest[-1]
    t = base_ref.shape[0]
    gate = gate_ref[...]
    lo = [None] * PACK_TILES
    hi = [None] * PACK_TILES
    for k in range(TOP_K):
        gk = gate[:, k:k + 1]
        for lt, w in enumerate(_load_row_tiles(yg_ref, (k,), t)):
            a, b = gk * _unpack_lo(w), gk * _unpack_hi(w)
            lo[lt] = a if k == 0 else lo[lt] + a
            hi[lt] = b if k == 0 else hi[lt] + b
    ff = jnp.concatenate(lo + hi, axis=1)
    o_ref[...] = _layernorm(base_ref[...] + ff, g2_ref[...], b2_ref[...])


def _final(yg, gates, base, g2, b2, chunk, prev_out):
    n = base.shape[0]
    steps = yg.shape[1] * SUBLANES // T_FIN
    off = chunk * steps
    full = lambda a: pl.BlockSpec(a.shape, lambda i: (0,) * a.ndim)
    in_specs = [pl.BlockSpec((TOP_K, T_FIN // SUBLANES, PACK_TILES, SUBLANES, LANES),
                             lambda i: (0, i, 0, 0, 0)),
                pl.BlockSpec((T_FIN, TOP_K), lambda i: (off + i, 0)),
                pl.BlockSpec((T_FIN, D_MODEL), lambda i: (off + i, 0)),
                full(g2), full(b2)]
    args = [yg, gates, base, g2, b2]
    aliases = {}
    if prev_out is not None:
        in_specs.append(pl.BlockSpec(memory_space=pl.ANY))
        aliases = {len(args): 0}
        args.append(prev_out)
    return pl.pallas_call(
        _final_body,
        grid=(steps,),
        in_specs=in_specs,
        out_specs=pl.BlockSpec((T_FIN, D_MODEL), lambda i: (off + i, 0)),
        out_shape=jax.ShapeDtypeStruct((n, D_MODEL), jnp.float32),
        input_output_aliases=aliases,
        compiler_params=pltpu.CompilerParams(dimension_semantics=("arbitrary",),
                                             vmem_limit_bytes=VMEM_LIMIT),
        name="final",
    )(*args)


def _rope_tables(pos):
    half = QK_ROPE // 2
    inv = np.float32(ROPE_THETA) ** (-np.arange(0, QK_ROPE, 2, dtype=np.float32) / np.float32(QK_ROPE))
    ang = pos.astype(np.float32)[:, None] * inv[None, :]
    cos, sin = np.cos(ang).astype(np.float32), np.sin(ang).astype(np.float32)
    n = pos.shape[0]
    z = lambda w: np.zeros((n, w), np.float32)
    c = np.concatenate([np.ones((n, QK_NOPE), np.float32), cos, cos, z(LANES - QK_NOPE - QK_ROPE)], axis=1)
    s_up = np.concatenate([z(QK_NOPE + half), sin, z(LANES - QK_NOPE - QK_ROPE)], axis=1)
    s_dn = np.concatenate([z(QK_NOPE), -sin, z(half), z(LANES - QK_NOPE - QK_ROPE)], axis=1)
    return c, s_up, s_dn


def _pad_heads(w, per_head, keep):
    rows = w.shape[0]
    w = w.reshape(rows, ATTN_HEADS, per_head)[:, :, :keep]
    w = jnp.pad(w, ((0, 0), (0, 0), (0, HEAD_PAD - keep)))
    return w.reshape(rows, ATTN_HEADS * HEAD_PAD)


def _prep_weights(emb_ln_g, emb_ln_b, w_in, q_norm, w_uq, kv_norm, w_ukv, conv_w, attn_out_norm,
                  conv_out_norm, w_o, ln1_g, ln1_b, w_router, router_bias, ws_gate, ws_up, ws_down):
    o1 = Q_LORA
    o2 = o1 + KV_LORA
    o3 = o2 + QK_ROPE
    row = lambda a: a.reshape(1, -1)
    kpe_cols = jnp.pad(w_in[:, o2:o3], ((0, 0), (QK_NOPE, HEAD_PAD - QK_NOPE - QK_ROPE)))
    w_in_p = jnp.concatenate([w_in[:, :o2], kpe_cols, w_in[:, o3:]], axis=1)
    ukv = w_ukv.reshape(KV_LORA, ATTN_HEADS, QK_NOPE + V_HEAD)
    w_uk = jnp.pad(ukv[:, :, :QK_NOPE], ((0, 0), (0, 0), (0, HEAD_PAD - QK_NOPE)))
    w_uv = jnp.pad(ukv[:, :, QK_NOPE:], ((0, 0), (0, 0), (0, HEAD_PAD - V_HEAD)))
    woa = w_o[:D_ATTN]
    return {
        "emb_g": row(emb_ln_g), "emb_b": row(emb_ln_b),
        "w_in": _bf(w_in_p),
        "q_norm": row(q_norm), "w_uq_t": _bf(_pad_heads(w_uq, QK_NOPE + QK_ROPE, QK_NOPE + QK_ROPE).T),
        "kv_norm": row(kv_norm), "w_uk": _bf(w_uk.reshape(KV_LORA, -1)), "w_uv_t": _bf(w_uv.reshape(KV_LORA, -1).T),
        "conv_w": jnp.pad(conv_w, ((0, SUBLANES - CONV_W), (0, 0))), "conv_norm": row(conv_out_norm),
        "attn_norm": row(attn_out_norm), "wo_a": _bf(woa), "wo_c": _bf(w_o[D_ATTN:]),
        "ln1_g": row(ln1_g), "ln1_b": row(ln1_b),
        "ws_gate": _bf(ws_gate), "ws_up": _bf(ws_up), "ws_down": _bf(ws_down),
        "w_router_t": _bf(w_router.T), "router_bias": router_bias.reshape(-1, 1),
    }


def _slot_body(eidx_ref, pos_ref, start_ref, idx_ref, base_ref):
    t = eidx_ref.shape[2]
    halves = [jnp.broadcast_to(start_ref[:, h * LANES:(h + 1) * LANES], (TOP_K, LANES))
              for h in range(N_EXPERTS // LANES)]
    for c in range(t // LANES):
        lanes = slice(c * LANES, (c + 1) * LANES)
        e = eidx_ref[0, :, lanes]
        st = jnp.take_along_axis(halves[0], e % LANES, axis=1)
        for h in range(1, len(halves)):
            st = jnp.where(e // LANES == h, jnp.take_along_axis(halves[h], e % LANES, axis=1), st)
        d = st + pos_ref[0, :, lanes]
        base_ref[:, lanes] = (d // SUBLANES) * (PACK_TILES * SUBLANES) + d % SUBLANES

    j = lax.broadcasted_iota(jnp.int32, (TOP_K, LANES), 1)
    per_list = LANES // (PACK_TILES * SUBLANES)
    lane_tile = (j // SUBLANES) % PACK_TILES * SUBLANES
    for c in range(t // SC_ROWS):
        lane0 = c * SC_ROWS
        src = base_ref[:, lane0 // LANES * LANES:(lane0 // LANES + 1) * LANES]
        for g in range(SC_LISTS):
            tok = lane0 % LANES + (g * per_list + j // (PACK_TILES * SUBLANES)) * SUBLANES + j % SUBLANES
            idx_ref[c, g * TOP_K:(g + 1) * TOP_K, :] = jnp.take_along_axis(src, tok, axis=1) + lane_tile


def _slots(eidx, pos, start):
    nt, _, t = eidx.shape
    per_tile = t // SC_ROWS
    k_blk = pl.BlockSpec((1, TOP_K, t), lambda i: (i, 0, 0))
    return pl.pallas_call(
        _slot_body,
        grid=(nt,),
        in_specs=[k_blk, k_blk, pl.BlockSpec(start.shape, lambda i: (0, 0))],
        out_specs=pl.BlockSpec((per_tile, SC_LISTS * TOP_K, LANES), lambda i: (i, 0, 0)),
        out_shape=jax.ShapeDtypeStruct((nt * per_tile, SC_LISTS * TOP_K, LANES), jnp.int32),
        scratch_shapes=[pltpu.VMEM((TOP_K, t), jnp.int32)],
        compiler_params=pltpu.CompilerParams(dimension_semantics=("arbitrary",)),
        name="slots",
    )(eidx, pos, start)


def _slot_plan(eidx, pos, counts):
    counts = counts.reshape(-1).astype(jnp.int32)
    padded = (counts + BLK - 1) // BLK * BLK
    pad_end = jnp.cumsum(padded)
    pad_start = pad_end - padded
    idx = _slots(eidx, pos, pad_start.astype(jnp.int32).reshape(1, -1))
    nb = eidx.size // BLK + N_EXPERTS
    nused = (pad_end[-1:] // BLK).astype(jnp.int32)
    return idx, (pad_start // BLK).astype(jnp.int32), (padded // BLK).astype(jnp.int32), nused, nb * BLK


def kernel(x, meta_tokens, emb_ln_g, emb_ln_b, w_in, q_norm, w_uq, kv_norm, w_ukv, conv_w, attn_out_norm, conv_out_norm, w_o, ln1_g, ln1_b, w_router, router_bias, w_gate, w_up, w_down, ws_gate, ws_up, ws_down, ln2_g, ln2_b):
    bsz, seq, _ = x.shape
    n = bsz * seq
    w = _prep_weights(emb_ln_g, emb_ln_b, w_in[0], q_norm[0], w_uq[0], kv_norm[0], w_ukv[0], conv_w[0],
                      attn_out_norm[0], conv_out_norm[0], w_o[0], ln1_g[0], ln1_b[0], w_router[0],
                      router_bias[0], ws_gate[0], ws_up[0], ws_down[0])

    meta_out = _inproj(meta_tokens[None], _rope_tables(np.arange(N_META)),
                       jnp.zeros((SUBLANES, D_CONV), jnp.float32), w, N_META)
    km = jnp.pad(meta_out[1][0], ((0, META_PAD - N_META), (0, 0)))
    vm = jnp.pad(meta_out[2][0], ((0, 0), (0, META_PAD - N_META)))
    vc_meta = meta_out[4][0]

    q, k, v, cn, _ = _inproj(x, _rope_tables(N_META + np.arange(seq)), vc_meta, w, T_IN)
    attn = _attention(q, k, v, km, vm)

    hp, base, eidx, gates, pos, counts = _mix_route(
        attn.reshape(n, D_ATTN), cn.reshape(n, D_CONV), x.reshape(n, D_MODEL), w)
    idx, first_blk, n_blk, nused, n_slots = _slot_plan(eidx, pos, counts)
    tiled = lambda a, lead: a.reshape(lead + (-1, PACK_TILES, SUBLANES, LANES))

    xs = _dispatch(hp.reshape(-1, LANES), idx, n_slots * PACK_TILES)
    ys = _experts(tiled(xs, ()), first_blk, n_blk, nused, w_gate[0], w_up[0], w_down[0])
    gates_t = gates.transpose(0, 2, 1).reshape(n, TOP_K)
    g2, b2 = ln2_g[0].reshape(1, -1), ln2_b[0].reshape(1, -1)
    ys_flat = ys.reshape(-1, LANES)
    per = idx.shape[0] // OUT_CHUNKS
    out = None
    for c in range(OUT_CHUNKS):
        yg = tiled(_combine(ys_flat, idx, n // OUT_CHUNKS * PACK_TILES, c * per), (TOP_K,))
        out = _final(yg, gates_t, base, g2, b2, c, out)
    return out.reshape(bsz, seq, D_MODEL)
```

```python
import functools
import math

import jax
import jax.numpy as jnp
import numpy as np
from jax import lax
from jax.experimental import pallas as pl
from jax.experimental.pallas import tpu as pltpu
from jax.experimental.pallas import tpu_sc as plsc

D_MODEL = 1024
N_META = 16
ATTN_HEADS = 8
QK_NOPE = 64
QK_ROPE = 32
V_HEAD = 64
Q_LORA = 384
KV_LORA = 256
ROPE_THETA = 10000.0
D_ATTN = ATTN_HEADS * V_HEAD
D_CONV = 512
CONV_W = 3
ATTN_SCALE = 1.0 / math.sqrt(QK_NOPE + QK_ROPE)
N_EXPERTS = 256
TOP_K = 8
N_GROUPS = 8
GROUP_SIZE = N_EXPERTS // N_GROUPS
TOPK_GROUPS = 4
D_EXPERT = 256
ROUTED_SCALE = 2.5
DEPTH = 1
ALPHA = (2.0 * DEPTH) ** 0.25

LANES = 128
SUBLANES = 8
HEAD_PAD = LANES
META_PAD = LANES
HALF = D_MODEL // 2
PACK_TILES = HALF // LANES
BF16_BITS = 16
HI_MASK = -(1 << BF16_BITS)
SC_CORES = 2
SC_SUBCORES = 16

T_IN = 512
T_Q = 1024
HEADS_PER_STEP = 4
DIAG_PARTS = 4
SHIFT_MARGIN = 40.0
T_MIX = 512
BLK = 256
BLOCK_GROUP = 3
ROW_SLOTS = 8
W_SLOTS = 3
W_PARTS = 2
T_FIN = 256
OUT_CHUNKS = 4
SC_ROWS = 64
SC_PIECES = SC_ROWS * PACK_TILES
SC_LISTS = SC_PIECES // LANES

VMEM_LIMIT = 56 * 1024 * 1024

_NEG_INF = float("-inf")


def _bf(x):
    return x.astype(jnp.bfloat16)


def _dot(a, b):
    return jnp.dot(a, b, preferred_element_type=jnp.float32)


def _dot_t(a, b):
    return lax.dot_general(a, b, (((1,), (1,)), ((), ())), preferred_element_type=jnp.float32)


def _layernorm(x, g, b, eps=1e-5):
    mu = jnp.mean(x, axis=-1, keepdims=True)
    xc = x - mu
    var = jnp.mean(xc * xc, axis=-1, keepdims=True)
    return xc * lax.rsqrt(var + eps) * g + b


def _rms(x, g, n, eps=1e-6):
    ms = jnp.sum(x * x, axis=-1, keepdims=True) * (1.0 / n)
    return x * lax.rsqrt(ms + eps) * g


def _sigmoid(x):
    return 1.0 / (1.0 + jnp.exp(-x))


def _rope(x, c, s_up, s_dn):
    return x * c + pltpu.roll(x, QK_ROPE // 2, 1) * s_up + pltpu.roll(x, LANES - QK_ROPE // 2, 1) * s_dn


def _rope_t(x, c, s_up, s_dn):
    return x * c + pltpu.roll(x, QK_ROPE // 2, 0) * s_up + pltpu.roll(x, LANES - QK_ROPE // 2, 0) * s_dn


def _inproj_body(x_ref, g_ref, b_ref, win_ref, qn_ref, wuq_ref, kvn_ref, wuk_ref, wuv_ref, cw_ref, con_ref,
                 tc_ref, tu_ref, td_ref, tct_ref, tut_ref, tdt_ref, vcinit_ref,
                 q_ref, k_ref, v_ref, cn_ref, vct_ref, carry_ref):
    rows = x_ref.shape[1]

    @pl.when(pl.program_id(1) == 0)
    def _():
        carry_ref[...] = vcinit_ref[...]

    h0 = _layernorm(x_ref[0], g_ref[...], b_ref[...])
    z = _dot(_bf(h0), win_ref[...])
    o1 = Q_LORA
    o2 = o1 + KV_LORA
    o3 = o2 + HEAD_PAD
    o4 = o3 + D_CONV
    o5 = o4 + D_CONV
    c, s_up, s_dn = tc_ref[...], tu_ref[...], td_ref[...]

    ct, st_up, st_dn = tct_ref[...], tut_ref[...], tdt_ref[...]
    q_t = _dot_t(wuq_ref[...], _bf(_rms(z[:, :o1], qn_ref[...], Q_LORA)))
    for h in range(ATTN_HEADS):
        sl = slice(h * HEAD_PAD, (h + 1) * HEAD_PAD)
        q_ref[0, sl, :] = _bf(_rope_t(q_t[sl, :], ct, st_up, st_dn) * ATTN_SCALE)

    ckv = _bf(_rms(z[:, o1:o2], kvn_ref[...], KV_LORA))
    kk = _dot(ckv, wuk_ref[...])
    kpe = _rope(z[:, o2:o3], c, s_up, s_dn)
    for h in range(ATTN_HEADS):
        sl = slice(h * HEAD_PAD, (h + 1) * HEAD_PAD)
        k_ref[0, :, sl] = _bf(kk[:, sl] + kpe)
    v_t = _dot_t(wuv_ref[...], ckv)
    dim = lax.broadcasted_iota(jnp.int32, (HEAD_PAD, rows), 0)
    for h in range(ATTN_HEADS):
        sl = slice(h * HEAD_PAD, (h + 1) * HEAD_PAD)
        v_ref[0, sl, :] = _bf(jnp.where(dim < V_HEAD, v_t[sl, :], 1.0))

    vc = z[:, o4:o5] * z[:, o5:]
    carry = carry_ref[...]
    row = lax.broadcasted_iota(jnp.int32, (rows, D_CONV), 0)
    vc1 = jnp.where(row == 0, carry[SUBLANES - 1:SUBLANES, :], pltpu.roll(vc, 1, 0))
    vc2 = pltpu.roll(vc, 2, 0)
    vc2 = jnp.where(row == 0, carry[SUBLANES - 2:SUBLANES - 1, :], vc2)
    vc2 = jnp.where(row == 1, carry[SUBLANES - 1:SUBLANES, :], vc2)
    cw = cw_ref[...]
    conv = z[:, o3:o4] * (cw[0:1, :] * vc2 + cw[1:2, :] * vc1 + cw[2:3, :] * vc)
    cn_ref[0] = _bf(_rms(conv, con_ref[...], D_CONV))
    tail = vc[rows - SUBLANES:, :]
    carry_ref[...] = tail
    vct_ref[0] = tail


def _inproj(x, tabs, vcinit, w, rows):
    bsz, seq, _ = x.shape
    nj = seq // rows
    full = lambda a: pl.BlockSpec(a.shape, lambda b, j: (0,) * a.ndim)
    row_blk = lambda width: pl.BlockSpec((1, rows, width), lambda b, j: (b, j, 0))
    tab = pl.BlockSpec((rows, HEAD_PAD), lambda b, j: (j, 0))
    tab_t = pl.BlockSpec((HEAD_PAD, rows), lambda b, j: (0, j))
    col_blk = pl.BlockSpec((1, ATTN_HEADS * HEAD_PAD, rows), lambda b, j: (b, 0, j))
    params = [w["emb_g"], w["emb_b"], w["w_in"], w["q_norm"], w["w_uq_t"], w["kv_norm"], w["w_uk"], w["w_uv_t"],
              w["conv_w"], w["conv_norm"]]
    wide = ATTN_HEADS * HEAD_PAD
    out_shape = [
        jax.ShapeDtypeStruct((bsz, wide, seq), jnp.bfloat16),
        jax.ShapeDtypeStruct((bsz, seq, wide), jnp.bfloat16),
        jax.ShapeDtypeStruct((bsz, wide, seq), jnp.bfloat16),
        jax.ShapeDtypeStruct((bsz, seq, D_CONV), jnp.bfloat16),
        jax.ShapeDtypeStruct((bsz, nj * SUBLANES, D_CONV), jnp.float32),
    ]
    tabs_t = [t.T for t in tabs]
    return pl.pallas_call(
        _inproj_body,
        grid=(bsz, nj),
        in_specs=([row_blk(D_MODEL)] + [full(p) for p in params] + [tab, tab, tab, tab_t, tab_t, tab_t]
                  + [full(vcinit)]),
        out_specs=[col_blk, row_blk(wide), col_blk, row_blk(D_CONV),
                   pl.BlockSpec((1, SUBLANES, D_CONV), lambda b, j: (b, j, 0))],
        out_shape=out_shape,
        scratch_shapes=[pltpu.VMEM((SUBLANES, D_CONV), jnp.float32)],
        compiler_params=pltpu.CompilerParams(dimension_semantics=("arbitrary", "arbitrary"),
                                             vmem_limit_bytes=VMEM_LIMIT),
        name="inproj",
    )(x, *params, *tabs, *tabs_t, vcinit)


def _attn_body(q_ref, k_ref, v_ref, km_ref, vm_ref, o_ref):
    i = pl.program_id(2)
    tq = q_ref.shape[2]
    heads = [slice(hh * HEAD_PAD, (hh + 1) * HEAD_PAD) for hh in range(HEADS_PER_STEP)]
    qs = [q_ref[0, sl, :] for sl in heads]

    part = tq // DIAG_PARTS
    is_meta = lax.broadcasted_iota(jnp.int32, (META_PAD, tq), 0) < N_META
    lead = lambda a, off, tail: tail if off == 0 else jnp.concatenate([a[:, :off], tail], axis=1)
    carry = []
    for q, sl in zip(qs, heads):
        sm = jnp.where(is_meta, _dot(km_ref[:, sl], q), _NEG_INF)
        m = jnp.max(sm, axis=0, keepdims=True)
        scores = []
        for c in range(DIAG_PARTS):
            off = c * part
            rows = pl.ds(pl.multiple_of(i * tq + off, part), part)
            causal = (lax.broadcasted_iota(jnp.int32, (part, tq - off), 0)
                      <= lax.broadcasted_iota(jnp.int32, (part, tq - off), 1))
            s = jnp.where(causal, _dot(k_ref[0, rows, sl], q[:, off:]), _NEG_INF)
            m = lead(m, off, jnp.maximum(m[:, off:], jnp.max(s, axis=0, keepdims=True)))
            scores.append((rows, off, s))
        acc = _dot(vm_ref[sl, :], _bf(jnp.exp(sm - m)))
        for rows, off, s in scores:
            acc = lead(acc, off, acc[:, off:] + _dot(v_ref[0, sl, rows], _bf(jnp.exp(s - m[:, off:]))))
        carry += [m, acc]

    def body(j, carry):
        rj = pl.multiple_of(j * tq, tq)
        ms, accs = carry[0::2], carry[1::2]
        fast, tops = [], []
        for q, sl, m, acc in zip(qs, heads, ms, accs):
            s = _dot(k_ref[0, pl.ds(rj, tq), sl], q)
            tops.append(jnp.max(s, axis=0, keepdims=True))
            fast.append(acc + _dot(v_ref[0, sl, pl.ds(rj, tq)], _bf(jnp.exp(s - m))))
        excess = functools.reduce(jnp.maximum, [top - m for top, m in zip(tops, ms)])
        over = jnp.max(excess, axis=1, keepdims=True)[0, 0] > SHIFT_MARGIN

        def rescale(_):
            out = []
            for q, sl, m, acc, top in zip(qs, heads, ms, accs, tops):
                s = _dot(k_ref[0, pl.ds(rj, tq), sl], q)
                m_new = jnp.maximum(m, top)
                out += [m_new, jnp.exp(m - m_new) * acc + _dot(v_ref[0, sl, pl.ds(rj, tq)], _bf(jnp.exp(s - m_new)))]
            return tuple(out)

        return lax.cond(over, rescale, lambda _: tuple(v for pair in zip(ms, fast) for v in pair), 0)

    accs = lax.fori_loop(0, i, body, tuple(carry))[1::2]
    o_t = jnp.concatenate([a[:V_HEAD, :] / a[V_HEAD:V_HEAD + 1, :] for a in accs], axis=0)
    o_ref[0] = _bf(o_t.T)


def _attention(q_t, k, v_t, km, vm_t):
    bsz, seq, _ = k.shape
    pairs = ATTN_HEADS // HEADS_PER_STEP
    pw = HEADS_PER_STEP * HEAD_PAD
    return pl.pallas_call(
        _attn_body,
        grid=(bsz, pairs, seq // T_Q),
        in_specs=[pl.BlockSpec((1, pw, T_Q), lambda b, p, i: (b, p, i)),
                  pl.BlockSpec((1, seq, pw), lambda b, p, i: (b, 0, p)),
                  pl.BlockSpec((1, pw, seq), lambda b, p, i: (b, p, 0)),
                  pl.BlockSpec((META_PAD, pw), lambda b, p, i: (0, p)),
                  pl.BlockSpec((pw, META_PAD), lambda b, p, i: (p, 0))],
        out_specs=pl.BlockSpec((1, T_Q, HEADS_PER_STEP * V_HEAD), lambda b, p, i: (b, i, p)),
        out_shape=jax.ShapeDtypeStruct((bsz, seq, D_ATTN), jnp.bfloat16),
        compiler_params=pltpu.CompilerParams(dimension_semantics=("arbitrary",) * 3,
                                             vmem_limit_bytes=VMEM_LIMIT),
        name="attention",
    )(q_t, k, v_t, km, vm_t)


def _mix_body(attn_ref, cn_ref, x_ref, eg_ref, eb_ref, an_ref, woa_ref, woc_ref, g1_ref, b1_ref,
              wsg_ref, wsu_ref, wsd_ref, wrt_ref, rb_ref, tri_ref,
              hp_ref, base_ref, eidx_ref, gate_ref, pos_ref, cnt_ref, run_ref):
    t = attn_ref.shape[0]

    @pl.when(pl.program_id(0) == 0)
    def _():
        run_ref[...] = jnp.zeros_like(run_ref)

    attn_n = _rms(attn_ref[...].astype(jnp.float32), an_ref[...], D_ATTN)
    mix = _dot(_bf(attn_n), woa_ref[...]) + _dot(cn_ref[...], woc_ref[...])
    h0 = _layernorm(x_ref[...], eg_ref[...], eb_ref[...])
    h1 = _layernorm(ALPHA * h0 + mix, g1_ref[...], b1_ref[...])
    _store_row_tiles(hp_ref, (), h1)
    h1b = _bf(h1)
    g = _dot(h1b, wsg_ref[...])
    u = _dot(h1b, wsu_ref[...])
    base_ref[...] = ALPHA * h1 + _dot(_bf(g * _sigmoid(g) * u), wsd_ref[...])

    shp = (N_GROUPS, GROUP_SIZE, t)
    sc = _sigmoid(_dot_t(wrt_ref[...], h1b)).reshape(shp)
    sel = sc + rb_ref[...].reshape(N_GROUPS, GROUP_SIZE, 1)
    gio = lax.broadcasted_iota(jnp.int32, shp, 0)
    rio = lax.broadcasted_iota(jnp.int32, shp, 1)
    eio = gio * GROUP_SIZE + rio

    m1 = jnp.max(sel, axis=1, keepdims=True)
    i1 = jnp.min(jnp.where(sel == m1, rio, GROUP_SIZE), axis=1, keepdims=True)
    m2 = jnp.max(jnp.where(rio == i1, _NEG_INF, sel), axis=1, keepdims=True)
    gs = m1 + m2
    gio1 = lax.broadcasted_iota(jnp.int32, (N_GROUPS, 1, t), 0)
    gmask = jnp.zeros((N_GROUPS, 1, t), jnp.bool_)
    for _ in range(TOPK_GROUPS):
        mg = jnp.max(gs, axis=0, keepdims=True)
        ig = jnp.min(jnp.where(gs == mg, gio1, N_GROUPS), axis=0, keepdims=True)
        hit = gio1 == ig
        gmask = jnp.logical_or(gmask, hit)
        gs = jnp.where(hit, _NEG_INF, gs)

    cand = jnp.where(gmask, sel, _NEG_INF)

    def red(x, op):
        return op(op(x, axis=0, keepdims=True), axis=1, keepdims=True)

    pm = jnp.zeros(shp, jnp.float32)
    e_rows, g_rows = [], []
    for _ in range(TOP_K):
        mk = red(cand, jnp.max)
        ek = red(jnp.where(cand == mk, eio, N_EXPERTS), jnp.min)
        hit = eio == ek
        g_rows.append(red(jnp.where(hit, sc, 0.0), jnp.sum))
        e_rows.append(ek)
        pm = jnp.where(hit, 1.0, pm)
        cand = jnp.where(hit, _NEG_INF, cand)
    gsum = g_rows[0]
    for gk in g_rows[1:]:
        gsum = gsum + gk

    pm = pm.reshape(N_EXPERTS, t)
    rank = _dot(_bf(pm), tri_ref[...])
    slot = (run_ref[...] + rank).reshape(shp)
    run_ref[...] = run_ref[...] + jnp.sum(pm, axis=1, keepdims=True)
    cnt_ref[...] = run_ref[...]
    for k in range(TOP_K):
        eidx_ref[0, k:k + 1, :] = e_rows[k].reshape(1, t)
        gate_ref[0, k:k + 1, :] = (g_rows[k] / gsum * ROUTED_SCALE).reshape(1, t)
        pk = red(jnp.where(eio == e_rows[k], slot, 0.0), jnp.sum)
        pos_ref[0, k:k + 1, :] = pk.reshape(1, t).astype(jnp.int32)


def _mix_route(attn, cn, x, w):
    n = attn.shape[0]
    nt = n // T_MIX
    tri = _bf(jnp.triu(jnp.ones((T_MIX, T_MIX), jnp.float32), 1))
    params = [w["emb_g"], w["emb_b"], w["attn_norm"], w["wo_a"], w["wo_c"], w["ln1_g"], w["ln1_b"], w["ws_gate"], w["ws_up"],
              w["ws_down"], w["w_router_t"], w["router_bias"], tri]
    full = lambda a: pl.BlockSpec(a.shape, lambda i: (0,) * a.ndim)
    row_blk = lambda width: pl.BlockSpec((T_MIX, width), lambda i: (i, 0))
    k_blk = pl.BlockSpec((1, TOP_K, T_MIX), lambda i: (i, 0, 0))
    out_shape = [
        jax.ShapeDtypeStruct((n // SUBLANES, PACK_TILES, SUBLANES, LANES), jnp.int32),
        jax.ShapeDtypeStruct((n, D_MODEL), jnp.float32),
        jax.ShapeDtypeStruct((nt, TOP_K, T_MIX), jnp.int32),
        jax.ShapeDtypeStruct((nt, TOP_K, T_MIX), jnp.float32),
        jax.ShapeDtypeStruct((nt, TOP_K, T_MIX), jnp.int32),
        jax.ShapeDtypeStruct((N_EXPERTS, 1), jnp.float32),
    ]
    return pl.pallas_call(
        _mix_body,
        grid=(nt,),
        in_specs=[row_blk(D_ATTN), row_blk(D_CONV), row_blk(D_MODEL)] + [full(p) for p in params],
        out_specs=[pl.BlockSpec((T_MIX // SUBLANES, PACK_TILES, SUBLANES, LANES), lambda i: (i, 0, 0, 0)),
                   row_blk(D_MODEL),
                   k_blk, k_blk, k_blk, pl.BlockSpec((N_EXPERTS, 1), lambda i: (0, 0))],
        out_shape=out_shape,
        scratch_shapes=[pltpu.VMEM((N_EXPERTS, 1), jnp.float32)],
        compiler_params=pltpu.CompilerParams(dimension_semantics=("arbitrary",),
                                             vmem_limit_bytes=VMEM_LIMIT),
        name="mix_route",
    )(attn, cn, x, *params)


def _pack_row_tiles(x):
    bits = lambda a: lax.bitcast_convert_type(_bf(a).astype(jnp.float32), jnp.int32)
    out = []
    for lt in range(PACK_TILES):
        lo = bits(x[:, lt * LANES:(lt + 1) * LANES])
        hi = bits(x[:, HALF + lt * LANES:HALF + (lt + 1) * LANES])
        out.append(lax.shift_right_logical(lo, BF16_BITS) | (hi & HI_MASK))
    return out


def _unpack_lo(w):
    return lax.bitcast_convert_type(lax.shift_left(w, BF16_BITS), jnp.float32)


def _unpack_hi(w):
    return lax.bitcast_convert_type(w & HI_MASK, jnp.float32)


def _store_row_tiles(ref, lead, x):
    rows = x.shape[0]
    for lt, w in enumerate(_pack_row_tiles(x)):
        ref[lead + (slice(None), lt)] = w.reshape(rows // SUBLANES, SUBLANES, LANES)


def _load_row_tiles(ref, lead, rows):
    return [ref[lead + (slice(None), lt)].reshape(rows, LANES) for lt in range(PACK_TILES)]


def _sc_mesh():
    return plsc.VectorSubcoreMesh(core_axis_name="c", subcore_axis_name="s")


def _worker_id():
    return lax.axis_index("s") * SC_CORES + lax.axis_index("c")


def _dispatch(hp, idx, n_rows):
    per_worker = hp.shape[0] // SC_PIECES // (SC_CORES * SC_SUBCORES)
    lists = idx.shape[1]

    @functools.partial(
        pl.kernel, mesh=_sc_mesh(),
        out_type=jax.ShapeDtypeStruct((n_rows, LANES), jnp.int32),
        scratch_types=[pltpu.VMEM((lists, LANES), jnp.int32),
                       pltpu.VMEM((SC_PIECES, LANES), jnp.int32),
                       pltpu.SemaphoreType.DMA],
        name="dispatch")
    def run(hp_hbm, idx_hbm, xs_hbm, idx_v, rows_v, sem):
        first = _worker_id() * per_worker

        @pl.loop(0, per_worker)
        def _(c):
            chunk = first + c
            pltpu.sync_copy(idx_hbm.at[chunk], idx_v)
            pltpu.sync_copy(hp_hbm.at[pl.ds(chunk * SC_PIECES, SC_PIECES)], rows_v)
            copies = [pltpu.async_copy(rows_v.at[pl.ds((j // TOP_K) * LANES, LANES)],
                                       xs_hbm.at[idx_v.at[j]], sem) for j in range(lists)]
            for cp in copies:
                cp.wait()

    return run(hp, idx)


def _combine(ys, idx, n_rows, first_chunk):
    per_worker = n_rows // SC_PIECES // (SC_CORES * SC_SUBCORES)
    lists = idx.shape[1]

    @functools.partial(
        pl.kernel, mesh=_sc_mesh(),
        out_type=jax.ShapeDtypeStruct((TOP_K, n_rows, LANES), jnp.int32),
        scratch_types=[pltpu.VMEM((lists, LANES), jnp.int32),
                       pltpu.VMEM((2, SC_PIECES, LANES), jnp.int32),
                       pltpu.SemaphoreType.DMA, pltpu.SemaphoreType.DMA],
        name="combine")
    def run(ys_hbm, idx_hbm, out_hbm, idx_v, rows_v, sem0, sem1):
        first = _worker_id() * per_worker
        sems = (sem0, sem1)

        def gather(k):
            buf = rows_v.at[k % 2]
            return [pltpu.async_copy(ys_hbm.at[idx_v.at[g * TOP_K + k]],
                                     buf.at[pl.ds(g * LANES, LANES)], sems[k % 2]) for g in range(SC_LISTS)]

        @pl.loop(0, per_worker)
        def _(c):
            chunk = first + c
            pltpu.sync_copy(idx_hbm.at[first_chunk + chunk], idx_v)
            pending = gather(0)
            for k in range(TOP_K):
                for cp in pending:
                    cp.wait()
                if k + 1 < TOP_K:
                    pending = gather(k + 1)
                pltpu.sync_copy(rows_v.at[k % 2], out_hbm.at[k, pl.ds(chunk * SC_PIECES, SC_PIECES)])

    return run(ys, idx)


def _expert_body(first_ref, count_ref, nused_ref, xs_hbm, wg_hbm, wu_hbm, wd_hbm, ys_hbm,
                 xbuf, ybuf, wgbuf, wubuf, wdbuf, wgu_ref, wdn_ref, sem_in, sem_out, sem_w):
    e = pl.program_id(0)
    nused = nused_ref[0]
    tiles = BLK // SUBLANES
    ahead = ROW_SLOTS - BLOCK_GROUP

    def fetch(b):
        slot = b % ROW_SLOTS
        return pltpu.make_async_copy(xs_hbm.at[pl.ds(b * tiles, tiles)], xbuf.at[slot], sem_in.at[slot])

    def writeback(b):
        slot = b % ROW_SLOTS
        return pltpu.make_async_copy(ybuf.at[slot], ys_hbm.at[pl.ds(b * tiles, tiles)], sem_out.at[slot])

    def weights(ex):
        slot = ex % W_SLOTS
        copies = []
        for src, dst in ((wg_hbm, wgbuf), (wu_hbm, wubuf), (wd_hbm, wdbuf)):
            part = src.shape[1] // W_PARTS
            for p in range(W_PARTS):
                rows = pl.ds(p * part, part)
                copies.append(pltpu.make_async_copy(src.at[ex, rows], dst.at[slot, rows], sem_w.at[slot]))
        return copies

    @pl.when(e == 0)
    def _():
        for b in range(ahead):
            @pl.when(b < nused)
            def _():
                fetch(b).start()
        for ex in range(W_SLOTS - 1):
            for cp in weights(ex):
                cp.start()

    for cp in weights(e):
        cp.wait()

    @pl.when(e + W_SLOTS - 1 < pl.num_programs(0))
    def _():
        for cp in weights(e + W_SLOTS - 1):
            cp.start()

    @pl.when(count_ref[e] > 0)
    def _():
        slot = e % W_SLOTS
        wgu_ref[:, :D_EXPERT] = _bf(wgbuf[slot])
        wgu_ref[:, D_EXPERT:] = _bf(wubuf[slot])
        wdn_ref[...] = _bf(wdbuf[slot])

    def run(b0, nblk):
        blocks = [b0 + c for c in range(nblk)]
        for b in blocks:
            fetch(b).wait()
        for b in blocks:
            @pl.when(b + ahead < nused)
            def _():
                fetch(b + ahead).start()

        words = [jnp.concatenate(ws, axis=0) for ws in
                 zip(*[_load_row_tiles(xbuf, (b % ROW_SLOTS,), BLK) for b in blocks])]
        x = jnp.concatenate([_bf(_unpack_lo(w)) for w in words] + [_bf(_unpack_hi(w)) for w in words], axis=1)
        gu = _dot(x, wgu_ref[...])
        g, u = gu[:, :D_EXPERT], gu[:, D_EXPERT:]
        y = _dot(_bf(g * _sigmoid(g) * u), wdn_ref[...])

        for c, b in enumerate(blocks):
            @pl.when(b >= ROW_SLOTS)
            def _():
                writeback(b - ROW_SLOTS).wait()

            _store_row_tiles(ybuf, (b % ROW_SLOTS,), y[c * BLK:(c + 1) * BLK])
            writeback(b).start()

    start, left = first_ref[e], count_ref[e]
    for size in range(BLOCK_GROUP, 0, -1):
        def group(t, carry, start=start, size=size):
            run(start + size * t, size)
            return carry

        lax.fori_loop(0, left // size, group, 0)
        start, left = start + left // size * size, left % size

    @pl.when(e == pl.num_programs(0) - 1)
    def _():
        for back in range(1, ROW_SLOTS + 1):
            @pl.when(nused >= back)
            def _():
                writeback(nused - back).wait()


def _experts(xs, first_blk, n_blk, nused, w_gate, w_up, w_down):
    blk = (BLK // SUBLANES, PACK_TILES, SUBLANES, LANES)
    grid_spec = pltpu.PrefetchScalarGridSpec(
        num_scalar_prefetch=3,
        grid=(N_EXPERTS,),
        in_specs=[pl.BlockSpec(memory_space=pl.ANY)] * 4,
        out_specs=pl.BlockSpec(memory_space=pl.ANY),
        scratch_shapes=[pltpu.VMEM((ROW_SLOTS,) + blk, jnp.int32),
                        pltpu.VMEM((ROW_SLOTS,) + blk, jnp.int32),
                        pltpu.VMEM((W_SLOTS, D_MODEL, D_EXPERT), jnp.float32),
                        pltpu.VMEM((W_SLOTS, D_MODEL, D_EXPERT), jnp.float32),
                        pltpu.VMEM((W_SLOTS, D_EXPERT, D_MODEL), jnp.float32),
                        pltpu.VMEM((D_MODEL, 2 * D_EXPERT), jnp.bfloat16),
                        pltpu.VMEM((D_EXPERT, D_MODEL), jnp.bfloat16),
                        pltpu.SemaphoreType.DMA((ROW_SLOTS,)),
                        pltpu.SemaphoreType.DMA((ROW_SLOTS,)),
                        pltpu.SemaphoreType.DMA((W_SLOTS,))],
    )
    return pl.pallas_call(
        _expert_body,
        grid_spec=grid_spec,
        out_shape=jax.ShapeDtypeStruct(xs.shape, jnp.int32),
        compiler_params=pltpu.CompilerParams(dimension_semantics=("arbitrary",),
                                             vmem_limit_bytes=VMEM_LIMIT),
        name="experts",
    )(first_blk, n_blk, nused, xs, w_gate, w_up, w_down)


def _final_body(yg_ref, gate_ref, base_ref, g2_ref, b2_ref, *rest):
    o_ref = rest[-1]
    t = base_ref.shape[0]
    gate = gate_ref[...]
    lo = [None] * PACK_TILES
    hi = [None] * PACK_TILES
    for k in range(TOP_K):
        gk = gate[:, k:k + 1]
        for lt, w in enumerate(_load_row_tiles(yg_ref, (k,), t)):
            a, b = gk * _unpack_lo(w), gk * _unpack_hi(w)
            lo[lt] = a if k == 0 else lo[lt] + a
            hi[lt] = b if k == 0 else hi[lt] + b
    ff = jnp.concatenate(lo + hi, axis=1)
    o_ref[...] = _layernorm(base_ref[...] + ff, g2_ref[...], b2_ref[...])


def _final(yg, gates, base, g2, b2, chunk, prev_out):
    n = base.shape[0]
    steps = yg.shape[1] * SUBLANES // T_FIN
    off = chunk * steps
    full = lambda a: pl.BlockSpec(a.shape, lambda i: (0,) * a.ndim)
    in_specs = [pl.BlockSpec((TOP_K, T_FIN // SUBLANES, PACK_TILES, SUBLANES, LANES),
                             lambda i: (0, i, 0, 0, 0)),
                pl.BlockSpec((T_FIN, TOP_K), lambda i: (off + i, 0)),
                pl.BlockSpec((T_FIN, D_MODEL), lambda i: (off + i, 0)),
                full(g2), full(b2)]
    args = [yg, gates, base, g2, b2]
    aliases = {}
    if prev_out is not None:
        in_specs.append(pl.BlockSpec(memory_space=pl.ANY))
        aliases = {len(args): 0}
        args.append(prev_out)
    return pl.pallas_call(
        _final_body,
        grid=(steps,),
        in_specs=in_specs,
        out_specs=pl.BlockSpec((T_FIN, D_MODEL), lambda i: (off + i, 0)),
        out_shape=jax.ShapeDtypeStruct((n, D_MODEL), jnp.float32),
        input_output_aliases=aliases,
        compiler_params=pltpu.CompilerParams(dimension_semantics=("arbitrary",),
                                             vmem_limit_bytes=VMEM_LIMIT),
        name="final",
    )(*args)


def _rope_tables(pos):
    half = QK_ROPE // 2
    inv = np.float32(ROPE_THETA) ** (-np.arange(0, QK_ROPE, 2, dtype=np.float32) / np.float32(QK_ROPE))
    ang = pos.astype(np.float32)[:, None] * inv[None, :]
    cos, sin = np.cos(ang).astype(np.float32), np.sin(ang).astype(np.float32)
    n = pos.shape[0]
    z = lambda w: np.zeros((n, w), np.float32)
    c = np.concatenate([np.ones((n, QK_NOPE), np.float32), cos, cos, z(LANES - QK_NOPE - QK_ROPE)], axis=1)
    s_up = np.concatenate([z(QK_NOPE + half), sin, z(LANES - QK_NOPE - QK_ROPE)], axis=1)
    s_dn = np.concatenate([z(QK_NOPE), -sin, z(half), z(LANES - QK_NOPE - QK_ROPE)], axis=1)
    return c, s_up, s_dn


def _pad_heads(w, per_head, keep):
    rows = w.shape[0]
    w = w.reshape(rows, ATTN_HEADS, per_head)[:, :, :keep]
    w = jnp.pad(w, ((0, 0), (0, 0), (0, HEAD_PAD - keep)))
    return w.reshape(rows, ATTN_HEADS * HEAD_PAD)


def _prep_weights(emb_ln_g, emb_ln_b, w_in, q_norm, w_uq, kv_norm, w_ukv, conv_w, attn_out_norm,
                  conv_out_norm, w_o, ln1_g, ln1_b, w_router, router_bias, ws_gate, ws_up, ws_down):
    o1 = Q_LORA
    o2 = o1 + KV_LORA
    o3 = o2 + QK_ROPE
    row = lambda a: a.reshape(1, -1)
    kpe_cols = jnp.pad(w_in[:, o2:o3], ((0, 0), (QK_NOPE, HEAD_PAD - QK_NOPE - QK_ROPE)))
    w_in_p = jnp.concatenate([w_in[:, :o2], kpe_cols, w_in[:, o3:]], axis=1)
    ukv = w_ukv.reshape(KV_LORA, ATTN_HEADS, QK_NOPE + V_HEAD)
    w_uk = jnp.pad(ukv[:, :, :QK_NOPE], ((0, 0), (0, 0), (0, HEAD_PAD - QK_NOPE)))
    w_uv = jnp.pad(ukv[:, :, QK_NOPE:], ((0, 0), (0, 0), (0, HEAD_PAD - V_HEAD)))
    woa = w_o[:D_ATTN]
    return {
        "emb_g": row(emb_ln_g), "emb_b": row(emb_ln_b),
        "w_in": _bf(w_in_p),
        "q_norm": row(q_norm), "w_uq_t": _bf(_pad_heads(w_uq, QK_NOPE + QK_ROPE, QK_NOPE + QK_ROPE).T),
        "kv_norm": row(kv_norm), "w_uk": _bf(w_uk.reshape(KV_LORA, -1)), "w_uv_t": _bf(w_uv.reshape(KV_LORA, -1).T),
        "conv_w": jnp.pad(conv_w, ((0, SUBLANES - CONV_W), (0, 0))), "conv_norm": row(conv_out_norm),
        "attn_norm": row(attn_out_norm), "wo_a": _bf(woa), "wo_c": _bf(w_o[D_ATTN:]),
        "ln1_g": row(ln1_g), "ln1_b": row(ln1_b),
        "ws_gate": _bf(ws_gate), "ws_up": _bf(ws_up), "ws_down": _bf(ws_down),
        "w_router_t": _bf(w_router.T), "router_bias": router_bias.reshape(-1, 1),
    }


def _slot_body(eidx_ref, pos_ref, start_ref, idx_ref, base_ref):
    t = eidx_ref.shape[2]
    halves = [jnp.broadcast_to(start_ref[:, h * LANES:(h + 1) * LANES], (TOP_K, LANES))
              for h in range(N_EXPERTS // LANES)]
    for c in range(t // LANES):
        lanes = slice(c * LANES, (c + 1) * LANES)
        e = eidx_ref[0, :, lanes]
        st = jnp.take_along_axis(halves[0], e % LANES, axis=1)
        for h in range(1, len(halves)):
            st = jnp.where(e // LANES == h, jnp.take_along_axis(halves[h], e % LANES, axis=1), st)
        d = st + pos_ref[0, :, lanes]
        base_ref[:, lanes] = (d // SUBLANES) * (PACK_TILES * SUBLANES) + d % SUBLANES

    j = lax.broadcasted_iota(jnp.int32, (TOP_K, LANES), 1)
    per_list = LANES // (PACK_TILES * SUBLANES)
    lane_tile = (j // SUBLANES) % PACK_TILES * SUBLANES
    for c in range(t // SC_ROWS):
        lane0 = c * SC_ROWS
        src = base_ref[:, lane0 // LANES * LANES:(lane0 // LANES + 1) * LANES]
        for g in range(SC_LISTS):
            tok = lane0 % LANES + (g * per_list + j // (PACK_TILES * SUBLANES)) * SUBLANES + j % SUBLANES
            idx_ref[c, g * TOP_K:(g + 1) * TOP_K, :] = jnp.take_along_axis(src, tok, axis=1) + lane_tile


def _slots(eidx, pos, start):
    nt, _, t = eidx.shape
    per_tile = t // SC_ROWS
    k_blk = pl.BlockSpec((1, TOP_K, t), lambda i: (i, 0, 0))
    return pl.pallas_call(
        _slot_body,
        grid=(nt,),
        in_specs=[k_blk, k_blk, pl.BlockSpec(start.shape, lambda i: (0, 0))],
        out_specs=pl.BlockSpec((per_tile, SC_LISTS * TOP_K, LANES), lambda i: (i, 0, 0)),
        out_shape=jax.ShapeDtypeStruct((nt * per_tile, SC_LISTS * TOP_K, LANES), jnp.int32),
        scratch_shapes=[pltpu.VMEM((TOP_K, t), jnp.int32)],
        compiler_params=pltpu.CompilerParams(dimension_semantics=("arbitrary",)),
        name="slots",
    )(eidx, pos, start)


def _slot_plan(eidx, pos, counts):
    counts = counts.reshape(-1).astype(jnp.int32)
    padded = (counts + BLK - 1) // BLK * BLK
    pad_end = jnp.cumsum(padded)
    pad_start = pad_end - padded
    idx = _slots(eidx, pos, pad_start.astype(jnp.int32).reshape(1, -1))
    nb = eidx.size // BLK + N_EXPERTS
    nused = (pad_end[-1:] // BLK).astype(jnp.int32)
    return idx, (pad_start // BLK).astype(jnp.int32), (padded // BLK).astype(jnp.int32), nused, nb * BLK


def kernel(x, meta_tokens, emb_ln_g, emb_ln_b, w_in, q_norm, w_uq, kv_norm, w_ukv, conv_w, attn_out_norm, conv_out_norm, w_o, ln1_g, ln1_b, w_router, router_bias, w_gate, w_up, w_down, ws_gate, ws_up, ws_down, ln2_g, ln2_b):
    bsz, seq, _ = x.shape
    n = bsz * seq
    w = _prep_weights(emb_ln_g, emb_ln_b, w_in[0], q_norm[0], w_uq[0], kv_norm[0], w_ukv[0], conv_w[0],
                      attn_out_norm[0], conv_out_norm[0], w_o[0], ln1_g[0], ln1_b[0], w_router[0],
                      router_bias[0], ws_gate[0], ws_up[0], ws_down[0])

    meta_out = _inproj(meta_tokens[None], _rope_tables(np.arange(N_META)),
                       jnp.zeros((SUBLANES, D_CONV), jnp.float32), w, N_META)
    km = jnp.pad(meta_out[1][0], ((0, META_PAD - N_META), (0, 0)))
    vm = jnp.pad(meta_out[2][0], ((0, 0), (0, META_PAD - N_META)))
    vc_meta = meta_out[4][0]

    q, k, v, cn, _ = _inproj(x, _rope_tables(N_META + np.arange(seq)), vc_meta, w, T_IN)
    attn = _attention(q, k, v, km, vm)

    hp, base, eidx, gates, pos, counts = _mix_route(
        attn.reshape(n, D_ATTN), cn.reshape(n, D_CONV), x.reshape(n, D_MODEL), w)
    idx, first_blk, n_blk, nused, n_slots = _slot_plan(eidx, pos, counts)
    tiled = lambda a, lead: a.reshape(lead + (-1, PACK_TILES, SUBLANES, LANES))

    xs = _dispatch(hp.reshape(-1, LANES), idx, n_slots * PACK_TILES)
    ys = _experts(tiled(xs, ()), first_blk, n_blk, nused, w_gate[0], w_up[0], w_down[0])
    gates_t = gates.transpose(0, 2, 1).reshape(n, TOP_K)
    g2, b2 = ln2_g[0].reshape(1, -1), ln2_b[0].reshape(1, -1)
    ys_flat = ys.reshape(-1, LANES)
    per = idx.shape[0] // OUT_CHUNKS
    out = None
    for c in range(OUT_CHUNKS):
        yg = tiled(_combine(ys_flat, idx, n // OUT_CHUNKS * PACK_TILES, c * per), (TOP_K,))
        out = _final(yg, gates_t, base, g2, b2, c, out)
    return out.reshape(bsz, seq, D_MODEL)
```

```python
import functools
import math

import jax
import jax.numpy as jnp
import numpy as np
from jax import lax
from jax.experimental import pallas as pl
from jax.experimental.pallas import tpu as pltpu
from jax.experimental.pallas import tpu_sc as plsc

D_MODEL = 1024
N_META = 16
ATTN_HEADS = 8
QK_NOPE = 64
QK_ROPE = 32
V_HEAD = 64
Q_LORA = 384
KV_LORA = 256
ROPE_THETA = 10000.0
D_ATTN = ATTN_HEADS * V_HEAD
D_CONV = 512
CONV_W = 3
ATTN_SCALE = 1.0 / math.sqrt(QK_NOPE + QK_ROPE)
N_EXPERTS = 256
TOP_K = 8
N_GROUPS = 8
GROUP_SIZE = N_EXPERTS // N_GROUPS
TOPK_GROUPS = 4
D_EXPERT = 256
ROUTED_SCALE = 2.5
DEPTH = 1
ALPHA = (2.0 * DEPTH) ** 0.25

LANES = 128
SUBLANES = 8
HEAD_PAD = LANES
META_PAD = LANES
HALF = D_MODEL // 2
PACK_TILES = HALF // LANES
BF16_BITS = 16
HI_MASK = -(1 << BF16_BITS)
SC_CORES = 2
SC_SUBCORES = 16

T_IN = 512
T_Q = 1024
HEADS_PER_STEP = 4
DIAG_PARTS = 4
SHIFT_MARGIN = 40.0
T_MIX = 1024
BLK = 256
BLOCK_GROUP = 3
ROW_SLOTS = 8
W_SLOTS = 3
W_PARTS = 2
T_FIN = 512
OUT_CHUNKS = 4
SC_ROWS = 64
SC_PIECES = SC_ROWS * PACK_TILES
SC_LISTS = SC_PIECES // LANES

VMEM_LIMIT = 56 * 1024 * 1024

_NEG_INF = float("-inf")


def _bf(x):
    return x.astype(jnp.bfloat16)


def _dot(a, b):
    return jnp.dot(a, b, preferred_element_type=jnp.float32)


def _dot_t(a, b):
    return lax.dot_general(a, b, (((1,), (1,)), ((), ())), preferred_element_type=jnp.float32)


def _layernorm(x, g, b, eps=1e-5):
    mu = jnp.mean(x, axis=-1, keepdims=True)
    xc = x - mu
    var = jnp.mean(xc * xc, axis=-1, keepdims=True)
    return xc * lax.rsqrt(var + eps) * g + b


def _rms(x, g, n, eps=1e-6):
    ms = jnp.sum(x * x, axis=-1, keepdims=True) * (1.0 / n)
    return x * lax.rsqrt(ms + eps) * g


def _sigmoid(x):
    return 1.0 / (1.0 + jnp.exp(-x))


def _rope(x, c, s_up, s_dn):
    return x * c + pltpu.roll(x, QK_ROPE // 2, 1) * s_up + pltpu.roll(x, LANES - QK_ROPE // 2, 1) * s_dn


def _rope_t(x, c, s_up, s_dn):
    return x * c + pltpu.roll(x, QK_ROPE // 2, 0) * s_up + pltpu.roll(x, LANES - QK_ROPE // 2, 0) * s_dn


def _inproj_body(x_ref, g_ref, b_ref, win_ref, qn_ref, wuq_ref, kvn_ref, wuk_ref, wuv_ref, cw_ref, con_ref,
                 tc_ref, tu_ref, td_ref, tct_ref, tut_ref, tdt_ref, vcinit_ref,
                 q_ref, k_ref, v_ref, cn_ref, vct_ref, carry_ref):
    rows = x_ref.shape[1]

    @pl.when(pl.program_id(1) == 0)
    def _():
        carry_ref[...] = vcinit_ref[...]

    h0 = _layernorm(x_ref[0], g_ref[...], b_ref[...])
    z = _dot(_bf(h0), win_ref[...])
    o1 = Q_LORA
    o2 = o1 + KV_LORA
    o3 = o2 + HEAD_PAD
    o4 = o3 + D_CONV
    o5 = o4 + D_CONV
    c, s_up, s_dn = tc_ref[...], tu_ref[...], td_ref[...]

    ct, st_up, st_dn = tct_ref[...], tut_ref[...], tdt_ref[...]
    q_t = _dot_t(wuq_ref[...], _bf(_rms(z[:, :o1], qn_ref[...], Q_LORA)))
    for h in range(ATTN_HEADS):
        sl = slice(h * HEAD_PAD, (h + 1) * HEAD_PAD)
        q_ref[0, sl, :] = _bf(_rope_t(q_t[sl, :], ct, st_up, st_dn) * ATTN_SCALE)

    ckv = _bf(_rms(z[:, o1:o2], kvn_ref[...], KV_LORA))
    kk = _dot(ckv, wuk_ref[...])
    kpe = _rope(z[:, o2:o3], c, s_up, s_dn)
    for h in range(ATTN_HEADS):
        sl = slice(h * HEAD_PAD, (h + 1) * HEAD_PAD)
        k_ref[0, :, sl] = _bf(kk[:, sl] + kpe)
    v_t = _dot_t(wuv_ref[...], ckv)
    dim = lax.broadcasted_iota(jnp.int32, (HEAD_PAD, rows), 0)
    for h in range(ATTN_HEADS):
        sl = slice(h * HEAD_PAD, (h + 1) * HEAD_PAD)
        v_ref[0, sl, :] = _bf(jnp.where(dim < V_HEAD, v_t[sl, :], 1.0))

    vc = z[:, o4:o5] * z[:, o5:]
    carry = carry_ref[...]
    row = lax.broadcasted_iota(jnp.int32, (rows, D_CONV), 0)
    vc1 = jnp.where(row == 0, carry[SUBLANES - 1:SUBLANES, :], pltpu.roll(vc, 1, 0))
    vc2 = pltpu.roll(vc, 2, 0)
    vc2 = jnp.where(row == 0, carry[SUBLANES - 2:SUBLANES - 1, :], vc2)
    vc2 = jnp.where(row == 1, carry[SUBLANES - 1:SUBLANES, :], vc2)
    cw = cw_ref[...]
    conv = z[:, o3:o4] * (cw[0:1, :] * vc2 + cw[1:2, :] * vc1 + cw[2:3, :] * vc)
    cn_ref[0] = _bf(_rms(conv, con_ref[...], D_CONV))
    tail = vc[rows - SUBLANES:, :]
    carry_ref[...] = tail
    vct_ref[0] = tail


def _inproj(x, tabs, vcinit, w, rows):
    bsz, seq, _ = x.shape
    nj = seq // rows
    full = lambda a: pl.BlockSpec(a.shape, lambda b, j: (0,) * a.ndim)
    row_blk = lambda width: pl.BlockSpec((1, rows, width), lambda b, j: (b, j, 0))
    tab = pl.BlockSpec((rows, HEAD_PAD), lambda b, j: (j, 0))
    tab_t = pl.BlockSpec((HEAD_PAD, rows), lambda b, j: (0, j))
    col_blk = pl.BlockSpec((1, ATTN_HEADS * HEAD_PAD, rows), lambda b, j: (b, 0, j))
    params = [w["emb_g"], w["emb_b"], w["w_in"], w["q_norm"], w["w_uq_t"], w["kv_norm"], w["w_uk"], w["w_uv_t"],
              w["conv_w"], w["conv_norm"]]
    wide = ATTN_HEADS * HEAD_PAD
    out_shape = [
        jax.ShapeDtypeStruct((bsz, wide, seq), jnp.bfloat16),
        jax.ShapeDtypeStruct((bsz, seq, wide), jnp.bfloat16),
        jax.ShapeDtypeStruct((bsz, wide, seq), jnp.bfloat16),
        jax.ShapeDtypeStruct((bsz, seq, D_CONV), jnp.bfloat16),
        jax.ShapeDtypeStruct((bsz, nj * SUBLANES, D_CONV), jnp.float32),
    ]
    tabs_t = [t.T for t in tabs]
    return pl.pallas_call(
        _inproj_body,
        grid=(bsz, nj),
        in_specs=([row_blk(D_MODEL)] + [full(p) for p in params] + [tab, tab, tab, tab_t, tab_t, tab_t]
                  + [full(vcinit)]),
        out_specs=[col_blk, row_blk(wide), col_blk, row_blk(D_CONV),
                   pl.BlockSpec((1, SUBLANES, D_CONV), lambda b, j: (b, j, 0))],
        out_shape=out_shape,
        scratch_shapes=[pltpu.VMEM((SUBLANES, D_CONV), jnp.float32)],
        compiler_params=pltpu.CompilerParams(dimension_semantics=("arbitrary", "arbitrary"),
                                             vmem_limit_bytes=VMEM_LIMIT),
        name="inproj",
    )(x, *params, *tabs, *tabs_t, vcinit)


def _attn_body(q_ref, k_ref, v_ref, km_ref, vm_ref, o_ref):
    i = pl.program_id(2)
    tq = q_ref.shape[2]
    heads = [slice(hh * HEAD_PAD, (hh + 1) * HEAD_PAD) for hh in range(HEADS_PER_STEP)]
    qs = [q_ref[0, sl, :] for sl in heads]

    part = tq // DIAG_PARTS
    is_meta = lax.broadcasted_iota(jnp.int32, (META_PAD, tq), 0) < N_META
    lead = lambda a, off, tail: tail if off == 0 else jnp.concatenate([a[:, :off], tail], axis=1)
    carry = []
    for q, sl in zip(qs, heads):
        sm = jnp.where(is_meta, _dot(km_ref[:, sl], q), _NEG_INF)
        m = jnp.max(sm, axis=0, keepdims=True)
        scores = []
        for c in range(DIAG_PARTS):
            off = c * part
            rows = pl.ds(pl.multiple_of(i * tq + off, part), part)
            causal = (lax.broadcasted_iota(jnp.int32, (part, tq - off), 0)
                      <= lax.broadcasted_iota(jnp.int32, (part, tq - off), 1))
            s = jnp.where(causal, _dot(k_ref[0, rows, sl], q[:, off:]), _NEG_INF)
            m = lead(m, off, jnp.maximum(m[:, off:], jnp.max(s, axis=0, keepdims=True)))
            scores.append((rows, off, s))
        acc = _dot(vm_ref[sl, :], _bf(jnp.exp(sm - m)))
        for rows, off, s in scores:
            acc = lead(acc, off, acc[:, off:] + _dot(v_ref[0, sl, rows], _bf(jnp.exp(s - m[:, off:]))))
        carry += [m, acc]

    def body(j, carry):
        rj = pl.multiple_of(j * tq, tq)
        ms, accs = carry[0::2], carry[1::2]
        fast, tops = [], []
        for q, sl, m, acc in zip(qs, heads, ms, accs):
            s = _dot(k_ref[0, pl.ds(rj, tq), sl], q)
            tops.append(jnp.max(s, axis=0, keepdims=True))
            fast.append(acc + _dot(v_ref[0, sl, pl.ds(rj, tq)], _bf(jnp.exp(s - m))))
        excess = functools.reduce(jnp.maximum, [top - m for top, m in zip(tops, ms)])
        over = jnp.max(excess, axis=1, keepdims=True)[0, 0] > SHIFT_MARGIN

        def rescale(_):
            out = []
            for q, sl, m, acc, top in zip(qs, heads, ms, accs, tops):
                s = _dot(k_ref[0, pl.ds(rj, tq), sl], q)
                m_new = jnp.maximum(m, top)
                out += [m_new, jnp.exp(m - m_new) * acc + _dot(v_ref[0, sl, pl.ds(rj, tq)], _bf(jnp.exp(s - m_new)))]
            return tuple(out)

        return lax.cond(over, rescale, lambda _: tuple(v for pair in zip(ms, fast) for v in pair), 0)

    accs = lax.fori_loop(0, i, body, tuple(carry))[1::2]
    o_t = jnp.concatenate([a[:V_HEAD, :] / a[V_HEAD:V_HEAD + 1, :] for a in accs], axis=0)
    o_ref[0] = _bf(o_t.T)


def _attention(q_t, k, v_t, km, vm_t):
    bsz, seq, _ = k.shape
    pairs = ATTN_HEADS // HEADS_PER_STEP
    pw = HEADS_PER_STEP * HEAD_PAD
    return pl.pallas_call(
        _attn_body,
        grid=(bsz, pairs, seq // T_Q),
        in_specs=[pl.BlockSpec((1, pw, T_Q), lambda b, p, i: (b, p, i)),
                  pl.BlockSpec((1, seq, pw), lambda b, p, i: (b, 0, p)),
                  pl.BlockSpec((1, pw, seq), lambda b, p, i: (b, p, 0)),
                  pl.BlockSpec((META_PAD, pw), lambda b, p, i: (0, p)),
                  pl.BlockSpec((pw, META_PAD), lambda b, p, i: (p, 0))],
        out_specs=pl.BlockSpec((1, T_Q, HEADS_PER_STEP * V_HEAD), lambda b, p, i: (b, i, p)),
        out_shape=jax.ShapeDtypeStruct((bsz, seq, D_ATTN), jnp.bfloat16),
        compiler_params=pltpu.CompilerParams(dimension_semantics=("arbitrary",) * 3,
                                             vmem_limit_bytes=VMEM_LIMIT),
        name="attention",
    )(q_t, k, v_t, km, vm_t)


def _mix_body(attn_ref, cn_ref, x_ref, eg_ref, eb_ref, an_ref, woa_ref, woc_ref, g1_ref, b1_ref,
              wsg_ref, wsu_ref, wsd_ref, wrt_ref, rb_ref, tri_ref,
              hp_ref, base_ref, eidx_ref, gate_ref, pos_ref, cnt_ref, run_ref):
    t = attn_ref.shape[0]

    @pl.when(pl.program_id(0) == 0)
    def _():
        run_ref[...] = jnp.zeros_like(run_ref)

    attn_n = _rms(attn_ref[...].astype(jnp.float32), an_ref[...], D_ATTN)
    mix = _dot(_bf(attn_n), woa_ref[...]) + _dot(cn_ref[...], woc_ref[...])
    h0 = _layernorm(x_ref[...], eg_ref[...], eb_ref[...])
    h1 = _layernorm(ALPHA * h0 + mix, g1_ref[...], b1_ref[...])
    _store_row_tiles(hp_ref, (), h1)
    h1b = _bf(h1)
    g = _dot(h1b, wsg_ref[...])
    u = _dot(h1b, wsu_ref[...])
    base_ref[...] = ALPHA * h1 + _dot(_bf(g * _sigmoid(g) * u), wsd_ref[...])

    shp = (N_GROUPS, GROUP_SIZE, t)
    sc = _sigmoid(_dot_t(wrt_ref[...], h1b)).reshape(shp)
    sel = sc + rb_ref[...].reshape(N_GROUPS, GROUP_SIZE, 1)
    gio = lax.broadcasted_iota(jnp.int32, shp, 0)
    rio = lax.broadcasted_iota(jnp.int32, shp, 1)
    eio = gio * GROUP_SIZE + rio

    m1 = jnp.max(sel, axis=1, keepdims=True)
    i1 = jnp.min(jnp.where(sel == m1, rio, GROUP_SIZE), axis=1, keepdims=True)
    m2 = jnp.max(jnp.where(rio == i1, _NEG_INF, sel), axis=1, keepdims=True)
    gs = m1 + m2
    gio1 = lax.broadcasted_iota(jnp.int32, (N_GROUPS, 1, t), 0)
    gmask = jnp.zeros((N_GROUPS, 1, t), jnp.bool_)
    for _ in range(TOPK_GROUPS):
        mg = jnp.max(gs, axis=0, keepdims=True)
        ig = jnp.min(jnp.where(gs == mg, gio1, N_GROUPS), axis=0, keepdims=True)
        hit = gio1 == ig
        gmask = jnp.logical_or(gmask, hit)
        gs = jnp.where(hit, _NEG_INF, gs)

    cand = jnp.where(gmask, sel, _NEG_INF)

    def red(x, op):
        return op(op(x, axis=0, keepdims=True), axis=1, keepdims=True)

    pm = jnp.zeros(shp, jnp.float32)
    e_rows, g_rows = [], []
    for _ in range(TOP_K):
        mk = red(cand, jnp.max)
        ek = red(jnp.where(cand == mk, eio, N_EXPERTS), jnp.min)
        hit = eio == ek
        g_rows.append(red(jnp.where(hit, sc, 0.0), jnp.sum))
        e_rows.append(ek)
        pm = jnp.where(hit, 1.0, pm)
        cand = jnp.where(hit, _NEG_INF, cand)
    gsum = g_rows[0]
    for gk in g_rows[1:]:
        gsum = gsum + gk

    pm = pm.reshape(N_EXPERTS, t)
    rank = _dot(_bf(pm), tri_ref[...])
    slot = (run_ref[...] + rank).reshape(shp)
    run_ref[...] = run_ref[...] + jnp.sum(pm, axis=1, keepdims=True)
    cnt_ref[...] = run_ref[...]
    for k in range(TOP_K):
        eidx_ref[0, k:k + 1, :] = e_rows[k].reshape(1, t)
        gate_ref[0, k:k + 1, :] = (g_rows[k] / gsum * ROUTED_SCALE).reshape(1, t)
        pk = red(jnp.where(eio == e_rows[k], slot, 0.0), jnp.sum)
        pos_ref[0, k:k + 1, :] = pk.reshape(1, t).astype(jnp.int32)


def _mix_route(attn, cn, x, w):
    n = attn.shape[0]
    nt = n // T_MIX
    tri = _bf(jnp.triu(jnp.ones((T_MIX, T_MIX), jnp.float32), 1))
    params = [w["emb_g"], w["emb_b"], w["attn_norm"], w["wo_a"], w["wo_c"], w["ln1_g"], w["ln1_b"], w["ws_gate"], w["ws_up"],
              w["ws_down"], w["w_router_t"], w["router_bias"], tri]
    full = lambda a: pl.BlockSpec(a.shape, lambda i: (0,) * a.ndim)
    row_blk = lambda width: pl.BlockSpec((T_MIX, width), lambda i: (i, 0))
    k_blk = pl.BlockSpec((1, TOP_K, T_MIX), lambda i: (i, 0, 0))
    out_shape = [
        jax.ShapeDtypeStruct((n // SUBLANES, PACK_TILES, SUBLANES, LANES), jnp.int32),
        jax.ShapeDtypeStruct((n, D_MODEL), jnp.float32),
        jax.ShapeDtypeStruct((nt, TOP_K, T_MIX), jnp.int32),
        jax.ShapeDtypeStruct((nt, TOP_K, T_MIX), jnp.float32),
        jax.ShapeDtypeStruct((nt, TOP_K, T_MIX), jnp.int32),
        jax.ShapeDtypeStruct((N_EXPERTS, 1), jnp.float32),
    ]
    return pl.pallas_call(
        _mix_body,
        grid=(nt,),
        in_specs=[row_blk(D_ATTN), row_blk(D_CONV), row_blk(D_MODEL)] + [full(p) for p in params],
        out_specs=[pl.BlockSpec((T_MIX // SUBLANES, PACK_TILES, SUBLANES, LANES), lambda i: (i, 0, 0, 0)),
                   row_blk(D_MODEL),
                   k_blk, k_blk, k_blk, pl.BlockSpec((N_EXPERTS, 1), lambda i: (0, 0))],
        out_shape=out_shape,
        scratch_shapes=[pltpu.VMEM((N_EXPERTS, 1), jnp.float32)],
        compiler_params=pltpu.CompilerParams(dimension_semantics=("arbitrary",),
                                             vmem_limit_bytes=VMEM_LIMIT),
        name="mix_route",
    )(attn, cn, x, *params)


def _pack_row_tiles(x):
    bits = lambda a: lax.bitcast_convert_type(_bf(a).astype(jnp.float32), jnp.int32)
    out = []
    for lt in range(PACK_TILES):
        lo = bits(x[:, lt * LANES:(lt + 1) * LANES])
        hi = bits(x[:, HALF + lt * LANES:HALF + (lt + 1) * LANES])
        out.append(lax.shift_right_logical(lo, BF16_BITS) | (hi & HI_MASK))
    return out


def _unpack_lo(w):
    return lax.bitcast_convert_type(lax.shift_left(w, BF16_BITS), jnp.float32)


def _unpack_hi(w):
    return lax.bitcast_convert_type(w & HI_MASK, jnp.float32)


def _store_row_tiles(ref, lead, x):
    rows = x.shape[0]
    for lt, w in enumerate(_pack_row_tiles(x)):
        ref[lead + (slice(None), lt)] = w.reshape(rows // SUBLANES, SUBLANES, LANES)


def _load_row_tiles(ref, lead, rows):
    return [ref[lead + (slice(None), lt)].reshape(rows, LANES) for lt in range(PACK_TILES)]


def _sc_mesh():
    return plsc.VectorSubcoreMesh(core_axis_name="c", subcore_axis_name="s")


def _worker_id():
    return lax.axis_index("s") * SC_CORES + lax.axis_index("c")


def _dispatch(hp, idx, n_rows):
    per_worker = hp.shape[0] // SC_PIECES // (SC_CORES * SC_SUBCORES)
    lists = idx.shape[1]

    @functools.partial(
        pl.kernel, mesh=_sc_mesh(),
        out_type=jax.ShapeDtypeStruct((n_rows, LANES), jnp.int32),
        scratch_types=[pltpu.VMEM((2, lists, LANES), jnp.int32),
                       pltpu.VMEM((2, SC_PIECES, LANES), jnp.int32),
                       pltpu.SemaphoreType.DMA, pltpu.SemaphoreType.DMA],
        name="dispatch")
    def run(hp_hbm, idx_hbm, xs_hbm, idx_v, rows_v, sem0, sem1):
        first = _worker_id() * per_worker
        sems = (sem0, sem1)

        def load(chunk, s):
            pltpu.sync_copy(idx_hbm.at[chunk], idx_v.at[s])
            pltpu.sync_copy(hp_hbm.at[pl.ds(chunk * SC_PIECES, SC_PIECES)], rows_v.at[s])

        def scatter(s):
            return [pltpu.async_copy(rows_v.at[s, pl.ds((j // TOP_K) * LANES, LANES)],
                                     xs_hbm.at[idx_v.at[s, j]], sems[s]) for j in range(lists)]

        load(first, 0)

        @pl.loop(0, per_worker, step=2)
        def _(c):
            a = scatter(0)
            load(first + c + 1, 1)
            b = scatter(1)
            for cp in a:
                cp.wait()

            @pl.when(c + 2 < per_worker)
            def _():
                load(first + c + 2, 0)

            for cp in b:
                cp.wait()

    return run(hp, idx)


def _combine(ys, idx, n_rows, first_chunk):
    per_worker = n_rows // SC_PIECES // (SC_CORES * SC_SUBCORES)
    lists = idx.shape[1]

    @functools.partial(
        pl.kernel, mesh=_sc_mesh(),
        out_type=jax.ShapeDtypeStruct((TOP_K, n_rows, LANES), jnp.int32),
        scratch_types=[pltpu.VMEM((lists, LANES), jnp.int32),
                       pltpu.VMEM((2, SC_PIECES, LANES), jnp.int32),
                       pltpu.SemaphoreType.DMA, pltpu.SemaphoreType.DMA],
        name="combine")
    def run(ys_hbm, idx_hbm, out_hbm, idx_v, rows_v, sem0, sem1):
        first = _worker_id() * per_worker
        sems = (sem0, sem1)

        def gather(k):
            buf = rows_v.at[k % 2]
            return [pltpu.async_copy(ys_hbm.at[idx_v.at[g * TOP_K + k]],
                                     buf.at[pl.ds(g * LANES, LANES)], sems[k % 2]) for g in range(SC_LISTS)]

        @pl.loop(0, per_worker)
        def _(c):
            chunk = first + c
            pltpu.sync_copy(idx_hbm.at[first_chunk + chunk], idx_v)
            pending = gather(0)
            for k in range(TOP_K):
                for cp in pending:
                    cp.wait()
                if k + 1 < TOP_K:
                    pending = gather(k + 1)
                pltpu.sync_copy(rows_v.at[k % 2], out_hbm.at[k, pl.ds(chunk * SC_PIECES, SC_PIECES)])

    return run(ys, idx)


def _expert_body(first_ref, count_ref, nused_ref, xs_hbm, wg_hbm, wu_hbm, wd_hbm, ys_hbm,
                 xbuf, ybuf, wgbuf, wubuf, wdbuf, wgu_ref, wdn_ref, sem_in, sem_out, sem_w):
    e = pl.program_id(0)
    nused = nused_ref[0]
    tiles = BLK // SUBLANES
    ahead = ROW_SLOTS - BLOCK_GROUP

    def fetch(b):
        slot = b % ROW_SLOTS
        return pltpu.make_async_copy(xs_hbm.at[pl.ds(b * tiles, tiles)], xbuf.at[slot], sem_in.at[slot])

    def writeback(b):
        slot = b % ROW_SLOTS
        return pltpu.make_async_copy(ybuf.at[slot], ys_hbm.at[pl.ds(b * tiles, tiles)], sem_out.at[slot])

    def weights(ex):
        slot = ex % W_SLOTS
        copies = []
        for src, dst in ((wg_hbm, wgbuf), (wu_hbm, wubuf), (wd_hbm, wdbuf)):
            part = src.shape[1] // W_PARTS
            for p in range(W_PARTS):
                rows = pl.ds(p * part, part)
                copies.append(pltpu.make_async_copy(src.at[ex, rows], dst.at[slot, rows], sem_w.at[slot]))
        return copies

    @pl.when(e == 0)
    def _():
        for b in range(ahead):
            @pl.when(b < nused)
            def _():
                fetch(b).start()
        for ex in range(W_SLOTS - 1):
            for cp in weights(ex):
                cp.start()

    for cp in weights(e):
        cp.wait()

    @pl.when(e + W_SLOTS - 1 < pl.num_programs(0))
    def _():
        for cp in weights(e + W_SLOTS - 1):
            cp.start()

    @pl.when(count_ref[e] > 0)
    def _():
        slot = e % W_SLOTS
        wgu_ref[:, :D_EXPERT] = _bf(wgbuf[slot])
        wgu_ref[:, D_EXPERT:] = _bf(wubuf[slot])
        wdn_ref[...] = _bf(wdbuf[slot])

    def run(b0, nblk):
        blocks = [b0 + c for c in range(nblk)]
        for b in blocks:
            fetch(b).wait()
        for b in blocks:
            @pl.when(b + ahead < nused)
            def _():
                fetch(b + ahead).start()

        words = [jnp.concatenate(ws, axis=0) for ws in
                 zip(*[_load_row_tiles(xbuf, (b % ROW_SLOTS,), BLK) for b in blocks])]
        x = jnp.concatenate([_bf(_unpack_lo(w)) for w in words] + [_bf(_unpack_hi(w)) for w in words], axis=1)
        gu = _dot(x, wgu_ref[...])
        g, u = gu[:, :D_EXPERT], gu[:, D_EXPERT:]
        y = _dot(_bf(g * _sigmoid(g) * u), wdn_ref[...])

        for c, b in enumerate(blocks):
            @pl.when(b >= ROW_SLOTS)
            def _():
                writeback(b - ROW_SLOTS).wait()

            _store_row_tiles(ybuf, (b % ROW_SLOTS,), y[c * BLK:(c + 1) * BLK])
            writeback(b).start()

    start, left = first_ref[e], count_ref[e]
    for size in range(BLOCK_GROUP, 0, -1):
        def group(t, carry, start=start, size=size):
            run(start + size * t, size)
            return carry

        lax.fori_loop(0, left // size, group, 0)
        start, left = start + left // size * size, left % size

    @pl.when(e == pl.num_programs(0) - 1)
    def _():
        for back in range(1, ROW_SLOTS + 1):
            @pl.when(nused >= back)
            def _():
                writeback(nused - back).wait()


def _experts(xs, first_blk, n_blk, nused, w_gate, w_up, w_down):
    blk = (BLK // SUBLANES, PACK_TILES, SUBLANES, LANES)
    grid_spec = pltpu.PrefetchScalarGridSpec(
        num_scalar_prefetch=3,
        grid=(N_EXPERTS,),
        in_specs=[pl.BlockSpec(memory_space=pl.ANY)] * 4,
        out_specs=pl.BlockSpec(memory_space=pl.ANY),
        scratch_shapes=[pltpu.VMEM((ROW_SLOTS,) + blk, jnp.int32),
                        pltpu.VMEM((ROW_SLOTS,) + blk, jnp.int32),
                        pltpu.VMEM((W_SLOTS, D_MODEL, D_EXPERT), jnp.float32),
                        pltpu.VMEM((W_SLOTS, D_MODEL, D_EXPERT), jnp.float32),
                        pltpu.VMEM((W_SLOTS, D_EXPERT, D_MODEL), jnp.float32),
                        pltpu.VMEM((D_MODEL, 2 * D_EXPERT), jnp.bfloat16),
                        pltpu.VMEM((D_EXPERT, D_MODEL), jnp.bfloat16),
                        pltpu.SemaphoreType.DMA((ROW_SLOTS,)),
                        pltpu.SemaphoreType.DMA((ROW_SLOTS,)),
                        pltpu.SemaphoreType.DMA((W_SLOTS,))],
    )
    return pl.pallas_call(
        _expert_body,
        grid_spec=grid_spec,
        out_shape=jax.ShapeDtypeStruct(xs.shape, jnp.int32),
        compiler_params=pltpu.CompilerParams(dimension_semantics=("arbitrary",),
                                             vmem_limit_bytes=VMEM_LIMIT),
        name="experts",
    )(first_blk, n_blk, nused, xs, w_gate, w_up, w_down)


def _final_body(yg_ref, gate_ref, base_ref, g2_ref, b2_ref, *rest):
    o_ref = rest[-1]
    t = base_ref.shape[0]
    gate = gate_ref[...]
    lo = [None] * PACK_TILES
    hi = [None] * PACK_TILES
    for k in range(TOP_K):
        gk = gate[:, k:k + 1]
        for lt, w in enumerate(_load_row_tiles(yg_ref, (k,), t)):
            a, b = gk * _unpack_lo(w), gk * _unpack_hi(w)
            lo[lt] = a if k == 0 else lo[lt] + a
            hi[lt] = b if k == 0 else hi[lt] + b
    ff = jnp.concatenate(lo + hi, axis=1)
    o_ref[...] = _layernorm(base_ref[...] + ff, g2_ref[...], b2_ref[...])


def _final(yg, gates, base, g2, b2, chunk, prev_out):
    n = base.shape[0]
    steps = yg.shape[1] * SUBLANES // T_FIN
    off = chunk * steps
    full = lambda a: pl.BlockSpec(a.shape, lambda i: (0,) * a.ndim)
    in_specs = [pl.BlockSpec((TOP_K, T_FIN // SUBLANES, PACK_TILES, SUBLANES, LANES),
                             lambda i: (0, i, 0, 0, 0)),
                pl.BlockSpec((T_FIN, TOP_K), lambda i: (off + i, 0)),
                pl.BlockSpec((T_FIN, D_MODEL), lambda i: (off + i, 0)),
                full(g2), full(b2)]
    args = [yg, gates, base, g2, b2]
    aliases = {}
    if prev_out is not None:
        in_specs.append(pl.BlockSpec(memory_space=pl.ANY))
        aliases = {len(args): 0}
        args.append(prev_out)
    return pl.pallas_call(
        _final_body,
        grid=(steps,),
        in_specs=in_specs,
        out_specs=pl.BlockSpec((T_FIN, D_MODEL), lambda i: (off + i, 0)),
        out_shape=jax.ShapeDtypeStruct((n, D_MODEL), jnp.float32),
        input_output_aliases=aliases,
        compiler_params=pltpu.CompilerParams(dimension_semantics=("arbitrary",),
                                             vmem_limit_bytes=VMEM_LIMIT),
        name="final",
    )(*args)


def _rope_tables(pos):
    half = QK_ROPE // 2
    inv = np.float32(ROPE_THETA) ** (-np.arange(0, QK_ROPE, 2, dtype=np.float32) / np.float32(QK_ROPE))
    ang = pos.astype(np.float32)[:, None] * inv[None, :]
    cos, sin = np.cos(ang).astype(np.float32), np.sin(ang).astype(np.float32)
    n = pos.shape[0]
    z = lambda w: np.zeros((n, w), np.float32)
    c = np.concatenate([np.ones((n, QK_NOPE), np.float32), cos, cos, z(LANES - QK_NOPE - QK_ROPE)], axis=1)
    s_up = np.concatenate([z(QK_NOPE + half), sin, z(LANES - QK_NOPE - QK_ROPE)], axis=1)
    s_dn = np.concatenate([z(QK_NOPE), -sin, z(half), z(LANES - QK_NOPE - QK_ROPE)], axis=1)
    return c, s_up, s_dn


def _pad_heads(w, per_head, keep):
    rows = w.shape[0]
    w = w.reshape(rows, ATTN_HEADS, per_head)[:, :, :keep]
    w = jnp.pad(w, ((0, 0), (0, 0), (0, HEAD_PAD - keep)))
    return w.reshape(rows, ATTN_HEADS * HEAD_PAD)


def _prep_weights(emb_ln_g, emb_ln_b, w_in, q_norm, w_uq, kv_norm, w_ukv, conv_w, attn_out_norm,
                  conv_out_norm, w_o, ln1_g, ln1_b, w_router, router_bias, ws_gate, ws_up, ws_down):
    o1 = Q_LORA
    o2 = o1 + KV_LORA
    o3 = o2 + QK_ROPE
    row = lambda a: a.reshape(1, -1)
    kpe_cols = jnp.pad(w_in[:, o2:o3], ((0, 0), (QK_NOPE, HEAD_PAD - QK_NOPE - QK_ROPE)))
    w_in_p = jnp.concatenate([w_in[:, :o2], kpe_cols, w_in[:, o3:]], axis=1)
    ukv = w_ukv.reshape(KV_LORA, ATTN_HEADS, QK_NOPE + V_HEAD)
    w_uk = jnp.pad(ukv[:, :, :QK_NOPE], ((0, 0), (0, 0), (0, HEAD_PAD - QK_NOPE)))
    w_uv = jnp.pad(ukv[:, :, QK_NOPE:], ((0, 0), (0, 0), (0, HEAD_PAD - V_HEAD)))
    woa = w_o[:D_ATTN]
    return {
        "emb_g": row(emb_ln_g), "emb_b": row(emb_ln_b),
        "w_in": _bf(w_in_p),
        "q_norm": row(q_norm), "w_uq_t": _bf(_pad_heads(w_uq, QK_NOPE + QK_ROPE, QK_NOPE + QK_ROPE).T),
        "kv_norm": row(kv_norm), "w_uk": _bf(w_uk.reshape(KV_LORA, -1)), "w_uv_t": _bf(w_uv.reshape(KV_LORA, -1).T),
        "conv_w": jnp.pad(conv_w, ((0, SUBLANES - CONV_W), (0, 0))), "conv_norm": row(conv_out_norm),
        "attn_norm": row(attn_out_norm), "wo_a": _bf(woa), "wo_c": _bf(w_o[D_ATTN:]),
        "ln1_g": row(ln1_g), "ln1_b": row(ln1_b),
        "ws_gate": _bf(ws_gate), "ws_up": _bf(ws_up), "ws_down": _bf(ws_down),
        "w_router_t": _bf(w_router.T), "router_bias": router_bias.reshape(-1, 1),
    }


def _slot_body(eidx_ref, pos_ref, start_ref, idx_ref, base_ref):
    t = eidx_ref.shape[2]
    halves = [jnp.broadcast_to(start_ref[:, h * LANES:(h + 1) * LANES], (TOP_K, LANES))
              for h in range(N_EXPERTS // LANES)]
    for c in range(t // LANES):
        lanes = slice(c * LANES, (c + 1) * LANES)
        e = eidx_ref[0, :, lanes]
        st = jnp.take_along_axis(halves[0], e % LANES, axis=1)
        for h in range(1, len(halves)):
            st = jnp.where(e // LANES == h, jnp.take_along_axis(halves[h], e % LANES, axis=1), st)
        d = st + pos_ref[0, :, lanes]
        base_ref[:, lanes] = (d // SUBLANES) * (PACK_TILES * SUBLANES) + d % SUBLANES

    j = lax.broadcasted_iota(jnp.int32, (TOP_K, LANES), 1)
    per_list = LANES // (PACK_TILES * SUBLANES)
    lane_tile = (j // SUBLANES) % PACK_TILES * SUBLANES
    for c in range(t // SC_ROWS):
        lane0 = c * SC_ROWS
        src = base_ref[:, lane0 // LANES * LANES:(lane0 // LANES + 1) * LANES]
        for g in range(SC_LISTS):
            tok = lane0 % LANES + (g * per_list + j // (PACK_TILES * SUBLANES)) * SUBLANES + j % SUBLANES
            idx_ref[c, g * TOP_K:(g + 1) * TOP_K, :] = jnp.take_along_axis(src, tok, axis=1) + lane_tile


def _slots(eidx, pos, start):
    nt, _, t = eidx.shape
    per_tile = t // SC_ROWS
    k_blk = pl.BlockSpec((1, TOP_K, t), lambda i: (i, 0, 0))
    return pl.pallas_call(
        _slot_body,
        grid=(nt,),
        in_specs=[k_blk, k_blk, pl.BlockSpec(start.shape, lambda i: (0, 0))],
        out_specs=pl.BlockSpec((per_tile, SC_LISTS * TOP_K, LANES), lambda i: (i, 0, 0)),
        out_shape=jax.ShapeDtypeStruct((nt * per_tile, SC_LISTS * TOP_K, LANES), jnp.int32),
        scratch_shapes=[pltpu.VMEM((TOP_K, t), jnp.int32)],
        compiler_params=pltpu.CompilerParams(dimension_semantics=("arbitrary",)),
        name="slots",
    )(eidx, pos, start)


def _slot_plan(eidx, pos, counts):
    counts = counts.reshape(-1).astype(jnp.int32)
    padded = (counts + BLK - 1) // BLK * BLK
    pad_end = jnp.cumsum(padded)
    pad_start = pad_end - padded
    idx = _slots(eidx, pos, pad_start.astype(jnp.int32).reshape(1, -1))
    nb = eidx.size // BLK + N_EXPERTS
    nused = (pad_end[-1:] // BLK).astype(jnp.int32)
    return idx, (pad_start // BLK).astype(jnp.int32), (padded // BLK).astype(jnp.int32), nused, nb * BLK


def kernel(x, meta_tokens, emb_ln_g, emb_ln_b, w_in, q_norm, w_uq, kv_norm, w_ukv, conv_w, attn_out_norm, conv_out_norm, w_o, ln1_g, ln1_b, w_router, router_bias, w_gate, w_up, w_down, ws_gate, ws_up, ws_down, ln2_g, ln2_b):
    bsz, seq, _ = x.shape
    n = bsz * seq
    w = _prep_weights(emb_ln_g, emb_ln_b, w_in[0], q_norm[0], w_uq[0], kv_norm[0], w_ukv[0], conv_w[0],
                      attn_out_norm[0], conv_out_norm[0], w_o[0], ln1_g[0], ln1_b[0], w_router[0],
                      router_bias[0], ws_gate[0], ws_up[0], ws_down[0])

    meta_out = _inproj(meta_tokens[None], _rope_tables(np.arange(N_META)),
                       jnp.zeros((SUBLANES, D_CONV), jnp.float32), w, N_META)
    km = jnp.pad(meta_out[1][0], ((0, META_PAD - N_META), (0, 0)))
    vm = jnp.pad(meta_out[2][0], ((0, 0), (0, META_PAD - N_META)))
    vc_meta = meta_out[4][0]

    q, k, v, cn, _ = _inproj(x, _rope_tables(N_META + np.arange(seq)), vc_meta, w, T_IN)
    attn = _attention(q, k, v, km, vm)

    hp, base, eidx, gates, pos, counts = _mix_route(
        attn.reshape(n, D_ATTN), cn.reshape(n, D_CONV), x.reshape(n, D_MODEL), w)
    idx, first_blk, n_blk, nused, n_slots = _slot_plan(eidx, pos, counts)
    tiled = lambda a, lead: a.reshape(lead + (-1, PACK_TILES, SUBLANES, LANES))

    xs = _dispatch(hp.reshape(-1, LANES), idx, n_slots * PACK_TILES)
    ys = _experts(tiled(xs, ()), first_blk, n_blk, nused, w_gate[0], w_up[0], w_down[0])
    gates_t = gates.transpose(0, 2, 1).reshape(n, TOP_K)
    g2, b2 = ln2_g[0].reshape(1, -1), ln2_b[0].reshape(1, -1)
    ys_flat = ys.reshape(-1, LANES)
    per = idx.shape[0] // OUT_CHUNKS
    out = None
    for c in range(OUT_CHUNKS):
        yg = tiled(_combine(ys_flat, idx, n // OUT_CHUNKS * PACK_TILES, c * per), (TOP_K,))
        out = _final(yg, gates_t, base, g2, b2, c, out)
    return out.reshape(bsz, seq, D_MODEL)
```

```python
import functools
import math

import jax
import jax.numpy as jnp
import numpy as np
from jax import lax
from jax.experimental import pallas as pl
from jax.experimental.pallas import tpu as pltpu
from jax.experimental.pallas import tpu_sc as plsc

D_MODEL = 1024
N_META = 16
ATTN_HEADS = 8
QK_NOPE = 64
QK_ROPE = 32
V_HEAD = 64
Q_LORA = 384
KV_LORA = 256
ROPE_THETA = 10000.0
D_ATTN = ATTN_HEADS * V_HEAD
D_CONV = 512
CONV_W = 3
ATTN_SCALE = 1.0 / math.sqrt(QK_NOPE + QK_ROPE)
N_EXPERTS = 256
TOP_K = 8
N_GROUPS = 8
GROUP_SIZE = N_EXPERTS // N_GROUPS
TOPK_GROUPS = 4
D_EXPERT = 256
ROUTED_SCALE = 2.5
DEPTH = 1
ALPHA = (2.0 * DEPTH) ** 0.25

LANES = 128
SUBLANES = 8
HEAD_PAD = LANES
META_PAD = LANES
HALF = D_MODEL // 2
PACK_TILES = HALF // LANES
BF16_BITS = 16
HI_MASK = -(1 << BF16_BITS)
SC_CORES = 2
SC_SUBCORES = 16

T_IN = 512
T_Q = 1024
HEADS_PER_STEP = 4
DIAG_PARTS = 4
SHIFT_MARGIN = 40.0
T_MIX = 1024
BLK = 256
BLOCK_GROUP = 3
ROW_SLOTS = 8
W_SLOTS = 3
W_PARTS = 2
T_FIN = 512
OUT_CHUNKS = 4
SC_ROWS = 64
SC_PIECES = SC_ROWS * PACK_TILES
SC_LISTS = SC_PIECES // LANES

VMEM_LIMIT = 56 * 1024 * 1024

_NEG_INF = float("-inf")


def _bf(x):
    return x.astype(jnp.bfloat16)


def _dot(a, b):
    return jnp.dot(a, b, preferred_element_type=jnp.float32)


def _dot_t(a, b):
    return lax.dot_general(a, b, (((1,), (1,)), ((), ())), preferred_element_type=jnp.float32)


def _layernorm(x, g, b, eps=1e-5):
    mu = jnp.mean(x, axis=-1, keepdims=True)
    xc = x - mu
    var = jnp.mean(xc * xc, axis=-1, keepdims=True)
    return xc * lax.rsqrt(var + eps) * g + b


def _rms(x, g, n, eps=1e-6):
    ms = jnp.sum(x * x, axis=-1, keepdims=True) * (1.0 / n)
    return x * lax.rsqrt(ms + eps) * g


def _sigmoid(x):
    return 1.0 / (1.0 + jnp.exp(-x))


def _rope(x, c, s_up, s_dn):
    return x * c + pltpu.roll(x, QK_ROPE // 2, 1) * s_up + pltpu.roll(x, LANES - QK_ROPE // 2, 1) * s_dn


def _rope_t(x, c, s_up, s_dn):
    return x * c + pltpu.roll(x, QK_ROPE // 2, 0) * s_up + pltpu.roll(x, LANES - QK_ROPE // 2, 0) * s_dn


def _inproj_body(x_ref, g_ref, b_ref, win_ref, qn_ref, wuq_ref, kvn_ref, wuk_ref, wuv_ref, cw_ref, con_ref,
                 tc_ref, tu_ref, td_ref, tct_ref, tut_ref, tdt_ref, vcinit_ref,
                 q_ref, k_ref, v_ref, cn_ref, vct_ref, carry_ref):
    rows = x_ref.shape[1]

    @pl.when(pl.program_id(1) == 0)
    def _():
        carry_ref[...] = vcinit_ref[...]

    h0 = _layernorm(x_ref[0], g_ref[...], b_ref[...])
    z = _dot(_bf(h0), win_ref[...])
    o1 = Q_LORA
    o2 = o1 + KV_LORA
    o3 = o2 + HEAD_PAD
    o4 = o3 + D_CONV
    o5 = o4 + D_CONV
    c, s_up, s_dn = tc_ref[...], tu_ref[...], td_ref[...]

    ct, st_up, st_dn = tct_ref[...], tut_ref[...], tdt_ref[...]
    q_t = _dot_t(wuq_ref[...], _bf(_rms(z[:, :o1], qn_ref[...], Q_LORA)))
    for h in range(ATTN_HEADS):
        sl = slice(h * HEAD_PAD, (h + 1) * HEAD_PAD)
        q_ref[0, sl, :] = _bf(_rope_t(q_t[sl, :], ct, st_up, st_dn) * ATTN_SCALE)

    ckv = _bf(_rms(z[:, o1:o2], kvn_ref[...], KV_LORA))
    kk = _dot(ckv, wuk_ref[...])
    kpe = _rope(z[:, o2:o3], c, s_up, s_dn)
    for h in range(ATTN_HEADS):
        sl = slice(h * HEAD_PAD, (h + 1) * HEAD_PAD)
        k_ref[0, :, sl] = _bf(kk[:, sl] + kpe)
    v_t = _dot_t(wuv_ref[...], ckv)
    dim = lax.broadcasted_iota(jnp.int32, (HEAD_PAD, rows), 0)
    for h in range(ATTN_HEADS):
        sl = slice(h * HEAD_PAD, (h + 1) * HEAD_PAD)
        v_ref[0, sl, :] = _bf(jnp.where(dim < V_HEAD, v_t[sl, :], 1.0))

    vc = z[:, o4:o5] * z[:, o5:]
    carry = carry_ref[...]
    row = lax.broadcasted_iota(jnp.int32, (rows, D_CONV), 0)
    vc1 = jnp.where(row == 0, carry[SUBLANES - 1:SUBLANES, :], pltpu.roll(vc, 1, 0))
    vc2 = pltpu.roll(vc, 2, 0)
    vc2 = jnp.where(row == 0, carry[SUBLANES - 2:SUBLANES - 1, :], vc2)
    vc2 = jnp.where(row == 1, carry[SUBLANES - 1:SUBLANES, :], vc2)
    cw = cw_ref[...]
    conv = z[:, o3:o4] * (cw[0:1, :] * vc2 + cw[1:2, :] * vc1 + cw[2:3, :] * vc)
    cn_ref[0] = _bf(_rms(conv, con_ref[...], D_CONV))
    tail = vc[rows - SUBLANES:, :]
    carry_ref[...] = tail
    vct_ref[0] = tail


def _inproj(x, tabs, vcinit, w, rows):
    bsz, seq, _ = x.shape
    nj = seq // rows
    full = lambda a: pl.BlockSpec(a.shape, lambda b, j: (0,) * a.ndim)
    row_blk = lambda width: pl.BlockSpec((1, rows, width), lambda b, j: (b, j, 0))
    tab = pl.BlockSpec((rows, HEAD_PAD), lambda b, j: (j, 0))
    tab_t = pl.BlockSpec((HEAD_PAD, rows), lambda b, j: (0, j))
    col_blk = pl.BlockSpec((1, ATTN_HEADS * HEAD_PAD, rows), lambda b, j: (b, 0, j))
    params = [w["emb_g"], w["emb_b"], w["w_in"], w["q_norm"], w["w_uq_t"], w["kv_norm"], w["w_uk"], w["w_uv_t"],
              w["conv_w"], w["conv_norm"]]
    wide = ATTN_HEADS * HEAD_PAD
    out_shape = [
        jax.ShapeDtypeStruct((bsz, wide, seq), jnp.bfloat16),
        jax.ShapeDtypeStruct((bsz, seq, wide), jnp.bfloat16),
        jax.ShapeDtypeStruct((bsz, wide, seq), jnp.bfloat16),
        jax.ShapeDtypeStruct((bsz, seq, D_CONV), jnp.bfloat16),
        jax.ShapeDtypeStruct((bsz, nj * SUBLANES, D_CONV), jnp.float32),
    ]
    tabs_t = [t.T for t in tabs]
    return pl.pallas_call(
        _inproj_body,
        grid=(bsz, nj),
        in_specs=([row_blk(D_MODEL)] + [full(p) for p in params] + [tab, tab, tab, tab_t, tab_t, tab_t]
                  + [full(vcinit)]),
        out_specs=[col_blk, row_blk(wide), col_blk, row_blk(D_CONV),
                   pl.BlockSpec((1, SUBLANES, D_CONV), lambda b, j: (b, j, 0))],
        out_shape=out_shape,
        scratch_shapes=[pltpu.VMEM((SUBLANES, D_CONV), jnp.float32)],
        compiler_params=pltpu.CompilerParams(dimension_semantics=("arbitrary", "arbitrary"),
                                             vmem_limit_bytes=VMEM_LIMIT),
        name="inproj",
    )(x, *params, *tabs, *tabs_t, vcinit)


def _attn_body(q_ref, k_ref, v_ref, km_ref, vm_ref, o_ref):
    i = pl.program_id(2)
    tq = q_ref.shape[2]
    heads = [slice(hh * HEAD_PAD, (hh + 1) * HEAD_PAD) for hh in range(HEADS_PER_STEP)]
    qs = [q_ref[0, sl, :] for sl in heads]

    part = tq // DIAG_PARTS
    is_meta = lax.broadcasted_iota(jnp.int32, (META_PAD, tq), 0) < N_META
    lead = lambda a, off, tail: tail if off == 0 else jnp.concatenate([a[:, :off], tail], axis=1)
    carry = []
    for q, sl in zip(qs, heads):
        sm = jnp.where(is_meta, _dot(km_ref[:, sl], q), _NEG_INF)
        m = jnp.max(sm, axis=0, keepdims=True)
        scores = []
        for c in range(DIAG_PARTS):
            off = c * part
            rows = pl.ds(pl.multiple_of(i * tq + off, part), part)
            causal = (lax.broadcasted_iota(jnp.int32, (part, tq - off), 0)
                      <= lax.broadcasted_iota(jnp.int32, (part, tq - off), 1))
            s = jnp.where(causal, _dot(k_ref[0, rows, sl], q[:, off:]), _NEG_INF)
            m = lead(m, off, jnp.maximum(m[:, off:], jnp.max(s, axis=0, keepdims=True)))
            scores.append((rows, off, s))
        acc = _dot(vm_ref[sl, :], _bf(jnp.exp(sm - m)))
        for rows, off, s in scores:
            acc = lead(acc, off, acc[:, off:] + _dot(v_ref[0, sl, rows], _bf(jnp.exp(s - m[:, off:]))))
        carry += [m, acc]

    def body(j, carry):
        rj = pl.multiple_of(j * tq, tq)
        ms, accs = carry[0::2], carry[1::2]
        fast, tops = [], []
        for q, sl, m, acc in zip(qs, heads, ms, accs):
            s = _dot(k_ref[0, pl.ds(rj, tq), sl], q)
            tops.append(jnp.max(s, axis=0, keepdims=True))
            fast.append(acc + _dot(v_ref[0, sl, pl.ds(rj, tq)], _bf(jnp.exp(s - m))))
        excess = functools.reduce(jnp.maximum, [top - m for top, m in zip(tops, ms)])
        over = jnp.max(excess, axis=1, keepdims=True)[0, 0] > SHIFT_MARGIN

        def rescale(_):
            out = []
            for q, sl, m, acc, top in zip(qs, heads, ms, accs, tops):
                s = _dot(k_ref[0, pl.ds(rj, tq), sl], q)
                m_new = jnp.maximum(m, top)
                out += [m_new, jnp.exp(m - m_new) * acc + _dot(v_ref[0, sl, pl.ds(rj, tq)], _bf(jnp.exp(s - m_new)))]
            return tuple(out)

        return lax.cond(over, rescale, lambda _: tuple(v for pair in zip(ms, fast) for v in pair), 0)

    accs = lax.fori_loop(0, i, body, tuple(carry))[1::2]
    o_t = jnp.concatenate([a[:V_HEAD, :] / a[V_HEAD:V_HEAD + 1, :] for a in accs], axis=0)
    o_ref[0] = _bf(o_t.T)


def _attention(q_t, k, v_t, km, vm_t):
    bsz, seq, _ = k.shape
    pairs = ATTN_HEADS // HEADS_PER_STEP
    pw = HEADS_PER_STEP * HEAD_PAD
    return pl.pallas_call(
        _attn_body,
        grid=(bsz, pairs, seq // T_Q),
        in_specs=[pl.BlockSpec((1, pw, T_Q), lambda b, p, i: (b, p, i)),
                  pl.BlockSpec((1, seq, pw), lambda b, p, i: (b, 0, p)),
                  pl.BlockSpec((1, pw, seq), lambda b, p, i: (b, p, 0)),
                  pl.BlockSpec((META_PAD, pw), lambda b, p, i: (0, p)),
                  pl.BlockSpec((pw, META_PAD), lambda b, p, i: (p, 0))],
        out_specs=pl.BlockSpec((1, T_Q, HEADS_PER_STEP * V_HEAD), lambda b, p, i: (b, i, p)),
        out_shape=jax.ShapeDtypeStruct((bsz, seq, D_ATTN), jnp.bfloat16),
        compiler_params=pltpu.CompilerParams(dimension_semantics=("arbitrary",) * 3,
                                             vmem_limit_bytes=VMEM_LIMIT),
        name="attention",
    )(q_t, k, v_t, km, vm_t)


def _mix_body(attn_ref, cn_ref, x_ref, eg_ref, eb_ref, an_ref, woa_ref, woc_ref, g1_ref, b1_ref,
              wsg_ref, wsu_ref, wsd_ref, wrt_ref, rb_ref, tri_ref,
              hp_ref, base_ref, eidx_ref, gate_ref, pos_ref, cnt_ref, run_ref):
    t = attn_ref.shape[0]

    @pl.when(pl.program_id(0) == 0)
    def _():
        run_ref[...] = jnp.zeros_like(run_ref)

    attn_n = _rms(attn_ref[...].astype(jnp.float32), an_ref[...], D_ATTN)
    mix = _dot(_bf(attn_n), woa_ref[...]) + _dot(cn_ref[...], woc_ref[...])
    h0 = _layernorm(x_ref[...], eg_ref[...], eb_ref[...])
    h1 = _layernorm(ALPHA * h0 + mix, g1_ref[...], b1_ref[...])
    _store_row_tiles(hp_ref, (), h1)
    h1b = _bf(h1)
    g = _dot(h1b, wsg_ref[...])
    u = _dot(h1b, wsu_ref[...])
    base_ref[...] = ALPHA * h1 + _dot(_bf(g * _sigmoid(g) * u), wsd_ref[...])

    shp = (N_GROUPS, GROUP_SIZE, t)
    sc = _sigmoid(_dot_t(wrt_ref[...], h1b)).reshape(shp)
    sel = sc + rb_ref[...].reshape(N_GROUPS, GROUP_SIZE, 1)
    gio = lax.broadcasted_iota(jnp.int32, shp, 0)
    rio = lax.broadcasted_iota(jnp.int32, shp, 1)
    eio = gio * GROUP_SIZE + rio

    m1 = jnp.max(sel, axis=1, keepdims=True)
    i1 = jnp.min(jnp.where(sel == m1, rio, GROUP_SIZE), axis=1, keepdims=True)
    m2 = jnp.max(jnp.where(rio == i1, _NEG_INF, sel), axis=1, keepdims=True)
    gs = m1 + m2
    gio1 = lax.broadcasted_iota(jnp.int32, (N_GROUPS, 1, t), 0)
    gmask = jnp.zeros((N_GROUPS, 1, t), jnp.bool_)
    for _ in range(TOPK_GROUPS):
        mg = jnp.max(gs, axis=0, keepdims=True)
        ig = jnp.min(jnp.where(gs == mg, gio1, N_GROUPS), axis=0, keepdims=True)
        hit = gio1 == ig
        gmask = jnp.logical_or(gmask, hit)
        gs = jnp.where(hit, _NEG_INF, gs)

    cand = jnp.where(gmask, sel, _NEG_INF)

    def red(x, op):
        return op(op(x, axis=0, keepdims=True), axis=1, keepdims=True)

    pm = jnp.zeros(shp, jnp.float32)
    e_rows, g_rows = [], []
    for _ in range(TOP_K):
        mk = red(cand, jnp.max)
        ek = red(jnp.where(cand == mk, eio, N_EXPERTS), jnp.min)
        hit = eio == ek
        g_rows.append(red(jnp.where(hit, sc, 0.0), jnp.sum))
        e_rows.append(ek)
        pm = jnp.where(hit, 1.0, pm)
        cand = jnp.where(hit, _NEG_INF, cand)
    gsum = g_rows[0]
    for gk in g_rows[1:]:
        gsum = gsum + gk

    pm = pm.reshape(N_EXPERTS, t)
    rank = _dot(_bf(pm), tri_ref[...])
    slot = (run_ref[...] + rank).reshape(shp)
    run_ref[...] = run_ref[...] + jnp.sum(pm, axis=1, keepdims=True)
    cnt_ref[...] = run_ref[...]
    for k in range(TOP_K):
        eidx_ref[0, k:k + 1, :] = e_rows[k].reshape(1, t)
        gate_ref[0, k:k + 1, :] = (g_rows[k] / gsum * ROUTED_SCALE).reshape(1, t)
        pk = red(jnp.where(eio == e_rows[k], slot, 0.0), jnp.sum)
        pos_ref[0, k:k + 1, :] = pk.reshape(1, t).astype(jnp.int32)


def _mix_route(attn, cn, x, w):
    n = attn.shape[0]
    nt = n // T_MIX
    tri = _bf(jnp.triu(jnp.ones((T_MIX, T_MIX), jnp.float32), 1))
    params = [w["emb_g"], w["emb_b"], w["attn_norm"], w["wo_a"], w["wo_c"], w["ln1_g"], w["ln1_b"], w["ws_gate"], w["ws_up"],
              w["ws_down"], w["w_router_t"], w["router_bias"], tri]
    full = lambda a: pl.BlockSpec(a.shape, lambda i: (0,) * a.ndim)
    row_blk = lambda width: pl.BlockSpec((T_MIX, width), lambda i: (i, 0))
    k_blk = pl.BlockSpec((1, TOP_K, T_MIX), lambda i: (i, 0, 0))
    out_shape = [
        jax.ShapeDtypeStruct((n // SUBLANES, PACK_TILES, SUBLANES, LANES), jnp.int32),
        jax.ShapeDtypeStruct((n, D_MODEL), jnp.float32),
        jax.ShapeDtypeStruct((nt, TOP_K, T_MIX), jnp.int32),
        jax.ShapeDtypeStruct((nt, TOP_K, T_MIX), jnp.float32),
        jax.ShapeDtypeStruct((nt, TOP_K, T_MIX), jnp.int32),
        jax.ShapeDtypeStruct((N_EXPERTS, 1), jnp.float32),
    ]
    return pl.pallas_call(
        _mix_body,
        grid=(nt,),
        in_specs=[row_blk(D_ATTN), row_blk(D_CONV), row_blk(D_MODEL)] + [full(p) for p in params],
        out_specs=[pl.BlockSpec((T_MIX // SUBLANES, PACK_TILES, SUBLANES, LANES), lambda i: (i, 0, 0, 0)),
                   row_blk(D_MODEL),
                   k_blk, k_blk, k_blk, pl.BlockSpec((N_EXPERTS, 1), lambda i: (0, 0))],
        out_shape=out_shape,
        scratch_shapes=[pltpu.VMEM((N_EXPERTS, 1), jnp.float32)],
        compiler_params=pltpu.CompilerParams(dimension_semantics=("arbitrary",),
                                             vmem_limit_bytes=VMEM_LIMIT),
        name="mix_route",
    )(attn, cn, x, *params)


def _pack_row_tiles(x):
    bits = lambda a: lax.bitcast_convert_type(_bf(a).astype(jnp.float32), jnp.int32)
    out = []
    for lt in range(PACK_TILES):
        lo = bits(x[:, lt * LANES:(lt + 1) * LANES])
        hi = bits(x[:, HALF + lt * LANES:HALF + (lt + 1) * LANES])
        out.append(lax.shift_right_logical(lo, BF16_BITS) | (hi & HI_MASK))
    return out


def _unpack_lo(w):
    return lax.bitcast_convert_type(lax.shift_left(w, BF16_BITS), jnp.float32)


def _unpack_hi(w):
    return lax.bitcast_convert_type(w & HI_MASK, jnp.float32)


def _store_row_tiles(ref, lead, x):
    rows = x.shape[0]
    for lt, w in enumerate(_pack_row_tiles(x)):
        ref[lead + (slice(None), lt)] = w.reshape(rows // SUBLANES, SUBLANES, LANES)


def _load_row_tiles(ref, lead, rows):
    return [ref[lead + (slice(None), lt)].reshape(rows, LANES) for lt in range(PACK_TILES)]


def _sc_mesh():
    return plsc.VectorSubcoreMesh(core_axis_name="c", subcore_axis_name="s")


def _worker_id():
    return lax.axis_index("s") * SC_CORES + lax.axis_index("c")


def _dispatch(hp, idx, n_rows):
    per_worker = hp.shape[0] // SC_PIECES // (SC_CORES * SC_SUBCORES)
    lists = idx.shape[1]

    @functools.partial(
        pl.kernel, mesh=_sc_mesh(),
        out_type=jax.ShapeDtypeStruct((n_rows, LANES), jnp.int32),
        scratch_types=[pltpu.VMEM((2, lists, LANES), jnp.int32),
                       pltpu.VMEM((2, SC_PIECES, LANES), jnp.int32),
                       pltpu.SemaphoreType.DMA, pltpu.SemaphoreType.DMA],
        name="dispatch")
    def run(hp_hbm, idx_hbm, xs_hbm, idx_v, rows_v, sem0, sem1):
        first = _worker_id() * per_worker
        sems = (sem0, sem1)

        def load(chunk, s):
            pltpu.sync_copy(idx_hbm.at[chunk], idx_v.at[s])
            pltpu.sync_copy(hp_hbm.at[pl.ds(chunk * SC_PIECES, SC_PIECES)], rows_v.at[s])

        def scatter(s):
            return [pltpu.async_copy(rows_v.at[s, pl.ds((j // TOP_K) * LANES, LANES)],
                                     xs_hbm.at[idx_v.at[s, j]], sems[s]) for j in range(lists)]

        load(first, 0)

        @pl.loop(0, per_worker, step=2)
        def _(c):
            a = scatter(0)
            load(first + c + 1, 1)
            b = scatter(1)
            for cp in a:
                cp.wait()

            @pl.when(c + 2 < per_worker)
            def _():
                load(first + c + 2, 0)

            for cp in b:
                cp.wait()

    return run(hp, idx)


def _combine(ys, idx, n_rows, first_chunk):
    per_worker = n_rows // SC_PIECES // (SC_CORES * SC_SUBCORES)
    lists = idx.shape[1]

    @functools.partial(
        pl.kernel, mesh=_sc_mesh(),
        out_type=jax.ShapeDtypeStruct((TOP_K, n_rows, LANES), jnp.int32),
        scratch_types=[pltpu.VMEM((lists, LANES), jnp.int32),
                       pltpu.VMEM((2, SC_PIECES, LANES), jnp.int32),
                       pltpu.SemaphoreType.DMA, pltpu.SemaphoreType.DMA],
        name="combine")
    def run(ys_hbm, idx_hbm, out_hbm, idx_v, rows_v, sem0, sem1):
        first = _worker_id() * per_worker
        sems = (sem0, sem1)

        def gather(k):
            buf = rows_v.at[k % 2]
            return [pltpu.async_copy(ys_hbm.at[idx_v.at[g * TOP_K + k]],
                                     buf.at[pl.ds(g * LANES, LANES)], sems[k % 2]) for g in range(SC_LISTS)]

        @pl.loop(0, per_worker)
        def _(c):
            chunk = first + c
            pltpu.sync_copy(idx_hbm.at[first_chunk + chunk], idx_v)
            pending = gather(0)
            for k in range(TOP_K):
                for cp in pending:
                    cp.wait()
                if k + 1 < TOP_K:
                    pending = gather(k + 1)
                pltpu.sync_copy(rows_v.at[k % 2], out_hbm.at[k, pl.ds(chunk * SC_PIECES, SC_PIECES)])

    return run(ys, idx)


def _expert_body(first_ref, count_ref, nused_ref, xs_hbm, wg_hbm, wu_hbm, wd_hbm, ys_hbm,
                 xbuf, ybuf, wgbuf, wubuf, wdbuf, wgu_ref, wdn_ref, sem_in, sem_out, sem_w):
    e = pl.program_id(0)
    nused = nused_ref[0]
    tiles = BLK // SUBLANES
    ahead = ROW_SLOTS - BLOCK_GROUP

    def fetch(b):
        slot = b % ROW_SLOTS
        return pltpu.make_async_copy(xs_hbm.at[pl.ds(b * tiles, tiles)], xbuf.at[slot], sem_in.at[slot])

    def writeback(b):
        slot = b % ROW_SLOTS
        return pltpu.make_async_copy(ybuf.at[slot], ys_hbm.at[pl.ds(b * tiles, tiles)], sem_out.at[slot])

    def weights(ex):
        slot = ex % W_SLOTS
        copies = []
        for src, dst in ((wg_hbm, wgbuf), (wu_hbm, wubuf), (wd_hbm, wdbuf)):
            part = src.shape[1] // W_PARTS
            for p in range(W_PARTS):
                rows = pl.ds(p * part, part)
                copies.append(pltpu.make_async_copy(src.at[ex, rows], dst.at[slot, rows], sem_w.at[slot]))
        return copies

    @pl.when(e == 0)
    def _():
        for b in range(ahead):
            @pl.when(b < nused)
            def _():
                fetch(b).start()
        for ex in range(W_SLOTS - 1):
            for cp in weights(ex):
                cp.start()

    for cp in weights(e):
        cp.wait()

    @pl.when(e + W_SLOTS - 1 < pl.num_programs(0))
    def _():
        for cp in weights(e + W_SLOTS - 1):
            cp.start()

    @pl.when(count_ref[e] > 0)
    def _():
        slot = e % W_SLOTS
        wgu_ref[:, :D_EXPERT] = _bf(wgbuf[slot])
        wgu_ref[:, D_EXPERT:] = _bf(wubuf[slot])
        wdn_ref[...] = _bf(wdbuf[slot])

    def run(b0, nblk):
        blocks = [b0 + c for c in range(nblk)]
        for b in blocks:
            fetch(b).wait()
        for b in blocks:
            @pl.when(b + ahead < nused)
            def _():
                fetch(b + ahead).start()

        words = [jnp.concatenate(ws, axis=0) for ws in
                 zip(*[_load_row_tiles(xbuf, (b % ROW_SLOTS,), BLK) for b in blocks])]
        x = jnp.concatenate([_bf(_unpack_lo(w)) for w in words] + [_bf(_unpack_hi(w)) for w in words], axis=1)
        gu = _dot(x, wgu_ref[...])
        g, u = gu[:, :D_EXPERT], gu[:, D_EXPERT:]
        y = _dot(_bf(g * _sigmoid(g) * u), wdn_ref[...])

        for c, b in enumerate(blocks):
            @pl.when(b >= ROW_SLOTS)
            def _():
                writeback(b - ROW_SLOTS).wait()

            _store_row_tiles(ybuf, (b % ROW_SLOTS,), y[c * BLK:(c + 1) * BLK])
            writeback(b).start()

    start, left = first_ref[e], count_ref[e]
    for size in range(BLOCK_GROUP, 0, -1):
        def group(t, carry, start=start, size=size):
            run(start + size * t, size)
            return carry

        lax.fori_loop(0, left // size, group, 0)
        start, left = start + left // size * size, left % size

    @pl.when(e == pl.num_programs(0) - 1)
    def _():
        for back in range(1, ROW_SLOTS + 1):
            @pl.when(nused >= back)
            def _():
                writeback(nused - back).wait()


def _experts(xs, first_blk, n_blk, nused, w_gate, w_up, w_down):
    blk = (BLK // SUBLANES, PACK_TILES, SUBLANES, LANES)
    grid_spec = pltpu.PrefetchScalarGridSpec(
        num_scalar_prefetch=3,
        grid=(N_EXPERTS,),
        in_specs=[pl.BlockSpec(memory_space=pl.ANY)] * 4,
        out_specs=pl.BlockSpec(memory_space=pl.ANY),
        scratch_shapes=[pltpu.VMEM((ROW_SLOTS,) + blk, jnp.int32),
                        pltpu.VMEM((ROW_SLOTS,) + blk, jnp.int32),
                        pltpu.VMEM((W_SLOTS, D_MODEL, D_EXPERT), jnp.float32),
                        pltpu.VMEM((W_SLOTS, D_MODEL, D_EXPERT), jnp.float32),
                        pltpu.VMEM((W_SLOTS, D_EXPERT, D_MODEL), jnp.float32),
                        pltpu.VMEM((D_MODEL, 2 * D_EXPERT), jnp.bfloat16),
                        pltpu.VMEM((D_EXPERT, D_MODEL), jnp.bfloat16),
                        pltpu.SemaphoreType.DMA((ROW_SLOTS,)),
                        pltpu.SemaphoreType.DMA((ROW_SLOTS,)),
                        pltpu.SemaphoreType.DMA((W_SLOTS,))],
    )
    return pl.pallas_call(
        _expert_body,
        grid_spec=grid_spec,
        out_shape=jax.ShapeDtypeStruct(xs.shape, jnp.int32),
        compiler_params=pltpu.CompilerParams(dimension_semantics=("arbitrary",),
                                             vmem_limit_bytes=VMEM_LIMIT),
        name="experts",
    )(first_blk, n_blk, nused, xs, w_gate, w_up, w_down)


def _final_body(yg_ref, gate_ref, base_ref, g2_ref, b2_ref, *rest):
    o_ref = rest[-1]
    t = base_ref.shape[0]
    gate = gate_ref[...]
    lo = [None] * PACK_TILES
    hi = [None] * PACK_TILES
    for k in range(TOP_K):
        gk = gate[:, k:k + 1]
        for lt, w in enumerate(_load_row_tiles(yg_ref, (k,), t)):
            a, b = gk * _unpack_lo(w), gk * _unpack_hi(w)
            lo[lt] = a if k == 0 else lo[lt] + a
            hi[lt] = b if k == 0 else hi[lt] + b
    ff = jnp.concatenate(lo + hi, axis=1)
    o_ref[...] = _layernorm(base_ref[...] + ff, g2_ref[...], b2_ref[...])


def _final(yg, gates, base, g2, b2, chunk, prev_out):
    n = base.shape[0]
    steps = yg.shape[1] * SUBLANES // T_FIN
    off = chunk * steps
    full = lambda a: pl.BlockSpec(a.shape, lambda i: (0,) * a.ndim)
    in_specs = [pl.BlockSpec((TOP_K, T_FIN // SUBLANES, PACK_TILES, SUBLANES, LANES),
                             lambda i: (0, i, 0, 0, 0)),
                pl.BlockSpec((T_FIN, TOP_K), lambda i: (off + i, 0)),
                pl.BlockSpec((T_FIN, D_MODEL), lambda i: (off + i, 0)),
                full(g2), full(b2)]
    args = [yg, gates, base, g2, b2]
    aliases = {}
    if prev_out is not None:
        in_specs.append(pl.BlockSpec(memory_space=pl.ANY))
        aliases = {len(args): 0}
        args.append(prev_out)
    return pl.pallas_call(
        _final_body,
        grid=(steps,),
        in_specs=in_specs,
        out_specs=pl.BlockSpec((T_FIN, D_MODEL), lambda i: (off + i, 0)),
        out_shape=jax.ShapeDtypeStruct((n, D_MODEL), jnp.float32),
        input_output_aliases=aliases,
        compiler_params=pltpu.CompilerParams(dimension_semantics=("arbitrary",),
                                             vmem_limit_bytes=VMEM_LIMIT),
        name="final",
    )(*args)


def _rope_tables(pos):
    half = QK_ROPE // 2
    inv = np.float32(ROPE_THETA) ** (-np.arange(0, QK_ROPE, 2, dtype=np.float32) / np.float32(QK_ROPE))
    ang = pos.astype(np.float32)[:, None] * inv[None, :]
    cos, sin = np.cos(ang).astype(np.float32), np.sin(ang).astype(np.float32)
    n = pos.shape[0]
    z = lambda w: np.zeros((n, w), np.float32)
    c = np.concatenate([np.ones((n, QK_NOPE), np.float32), cos, cos, z(LANES - QK_NOPE - QK_ROPE)], axis=1)
    s_up = np.concatenate([z(QK_NOPE + half), sin, z(LANES - QK_NOPE - QK_ROPE)], axis=1)
    s_dn = np.concatenate([z(QK_NOPE), -sin, z(half), z(LANES - QK_NOPE - QK_ROPE)], axis=1)
    return c, s_up, s_dn


def _pad_heads(w, per_head, keep):
    rows = w.shape[0]
    w = w.reshape(rows, ATTN_HEADS, per_head)[:, :, :keep]
    w = jnp.pad(w, ((0, 0), (0, 0), (0, HEAD_PAD - keep)))
    return w.reshape(rows, ATTN_HEADS * HEAD_PAD)


def _prep_weights(emb_ln_g, emb_ln_b, w_in, q_norm, w_uq, kv_norm, w_ukv, conv_w, attn_out_norm,
                  conv_out_norm, w_o, ln1_g, ln1_b, w_router, router_bias, ws_gate, ws_up, ws_down):
    o1 = Q_LORA
    o2 = o1 + KV_LORA
    o3 = o2 + QK_ROPE
    row = lambda a: a.reshape(1, -1)
    kpe_cols = jnp.pad(w_in[:, o2:o3], ((0, 0), (QK_NOPE, HEAD_PAD - QK_NOPE - QK_ROPE)))
    w_in_p = jnp.concatenate([w_in[:, :o2], kpe_cols, w_in[:, o3:]], axis=1)
    ukv = w_ukv.reshape(KV_LORA, ATTN_HEADS, QK_NOPE + V_HEAD)
    w_uk = jnp.pad(ukv[:, :, :QK_NOPE], ((0, 0), (0, 0), (0, HEAD_PAD - QK_NOPE)))
    w_uv = jnp.pad(ukv[:, :, QK_NOPE:], ((0, 0), (0, 0), (0, HEAD_PAD - V_HEAD)))
    woa = w_o[:D_ATTN]
    return {
        "emb_g": row(emb_ln_g), "emb_b": row(emb_ln_b),
        "w_in": _bf(w_in_p),
        "q_norm": row(q_norm), "w_uq_t": _bf(_pad_heads(w_uq, QK_NOPE + QK_ROPE, QK_NOPE + QK_ROPE).T),
        "kv_norm": row(kv_norm), "w_uk": _bf(w_uk.reshape(KV_LORA, -1)), "w_uv_t": _bf(w_uv.reshape(KV_LORA, -1).T),
        "conv_w": jnp.pad(conv_w, ((0, SUBLANES - CONV_W), (0, 0))), "conv_norm": row(conv_out_norm),
        "attn_norm": row(attn_out_norm), "wo_a": _bf(woa), "wo_c": _bf(w_o[D_ATTN:]),
        "ln1_g": row(ln1_g), "ln1_b": row(ln1_b),
        "ws_gate": _bf(ws_gate), "ws_up": _bf(ws_up), "ws_down": _bf(ws_down),
        "w_router_t": _bf(w_router.T), "router_bias": router_bias.reshape(-1, 1),
    }


def _slot_body(eidx_ref, pos_ref, start_ref, idx_ref, base_ref):
    t = eidx_ref.shape[2]
    halves = [jnp.broadcast_to(start_ref[:, h * LANES:(h + 1) * LANES], (TOP_K, LANES))
              for h in range(N_EXPERTS // LANES)]
    for c in range(t // LANES):
        lanes = slice(c * LANES, (c + 1) * LANES)
        e = eidx_ref[0, :, lanes]
        st = jnp.take_along_axis(halves[0], e % LANES, axis=1)
        for h in range(1, len(halves)):
            st = jnp.where(e // LANES == h, jnp.take_along_axis(halves[h], e % LANES, axis=1), st)
        d = st + pos_ref[0, :, lanes]
        base_ref[:, lanes] = (d // SUBLANES) * (PACK_TILES * SUBLANES) + d % SUBLANES

    j = lax.broadcasted_iota(jnp.int32, (TOP_K, LANES), 1)
    per_list = LANES // (PACK_TILES * SUBLANES)
    lane_tile = (j // SUBLANES) % PACK_TILES * SUBLANES
    for c in range(t // SC_ROWS):
        lane0 = c * SC_ROWS
        src = base_ref[:, lane0 // LANES * LANES:(lane0 // LANES + 1) * LANES]
        for g in range(SC_LISTS):
            tok = lane0 % LANES + (g * per_list + j // (PACK_TILES * SUBLANES)) * SUBLANES + j % SUBLANES
            idx_ref[c, g * TOP_K:(g + 1) * TOP_K, :] = jnp.take_along_axis(src, tok, axis=1) + lane_tile


def _slots(eidx, pos, start):
    nt, _, t = eidx.shape
    per_tile = t // SC_ROWS
    k_blk = pl.BlockSpec((1, TOP_K, t), lambda i: (i, 0, 0))
    return pl.pallas_call(
        _slot_body,
        grid=(nt,),
        in_specs=[k_blk, k_blk, pl.BlockSpec(start.shape, lambda i: (0, 0))],
        out_specs=pl.BlockSpec((per_tile, SC_LISTS * TOP_K, LANES), lambda i: (i, 0, 0)),
        out_shape=jax.ShapeDtypeStruct((nt * per_tile, SC_LISTS * TOP_K, LANES), jnp.int32),
        scratch_shapes=[pltpu.VMEM((TOP_K, t), jnp.int32)],
        compiler_params=pltpu.CompilerParams(dimension_semantics=("arbitrary",)),
        name="slots",
    )(eidx, pos, start)


def _slot_plan(eidx, pos, counts):
    counts = counts.reshape(-1).astype(jnp.int32)
    padded = (counts + BLK - 1) // BLK * BLK
    pad_end = jnp.cumsum(padded)
    pad_start = pad_end - padded
    idx = _slots(eidx, pos, pad_start.astype(jnp.int32).reshape(1, -1))
    nb = eidx.size // BLK + N_EXPERTS
    nused = (pad_end[-1:] // BLK).astype(jnp.int32)
    return idx, (pad_start // BLK).astype(jnp.int32), (padded // BLK).astype(jnp.int32), nused, nb * BLK


def kernel(x, meta_tokens, emb_ln_g, emb_ln_b, w_in, q_norm, w_uq, kv_norm, w_ukv, conv_w, attn_out_norm, conv_out_norm, w_o, ln1_g, ln1_b, w_router, router_bias, w_gate, w_up, w_down, ws_gate, ws_up, ws_down, ln2_g, ln2_b):
    bsz, seq, _ = x.shape
    n = bsz * seq
    workers = SC_CORES * SC_SUBCORES
    assert x.shape[2] == D_MODEL and seq % T_Q == 0 and seq % T_IN == 0 and T_Q % DIAG_PARTS == 0
    assert n % T_MIX == 0 and T_MIX % SC_ROWS == 0 and n % (OUT_CHUNKS * T_FIN) == 0
    assert n % (2 * SC_ROWS * workers) == 0 and n % (OUT_CHUNKS * SC_ROWS * workers) == 0
    w = _prep_weights(emb_ln_g, emb_ln_b, w_in[0], q_norm[0], w_uq[0], kv_norm[0], w_ukv[0], conv_w[0],
                      attn_out_norm[0], conv_out_norm[0], w_o[0], ln1_g[0], ln1_b[0], w_router[0],
                      router_bias[0], ws_gate[0], ws_up[0], ws_down[0])

    meta_out = _inproj(meta_tokens[None], _rope_tables(np.arange(N_META)),
                       jnp.zeros((SUBLANES, D_CONV), jnp.float32), w, N_META)
    km = jnp.pad(meta_out[1][0], ((0, META_PAD - N_META), (0, 0)))
    vm = jnp.pad(meta_out[2][0], ((0, 0), (0, META_PAD - N_META)))
    vc_meta = meta_out[4][0]

    q, k, v, cn, _ = _inproj(x, _rope_tables(N_META + np.arange(seq)), vc_meta, w, T_IN)
    attn = _attention(q, k, v, km, vm)

    hp, base, eidx, gates, pos, counts = _mix_route(
        attn.reshape(n, D_ATTN), cn.reshape(n, D_CONV), x.reshape(n, D_MODEL), w)
    idx, first_blk, n_blk, nused, n_slots = _slot_plan(eidx, pos, counts)
    tiled = lambda a, lead: a.reshape(lead + (-1, PACK_TILES, SUBLANES, LANES))

    xs = _dispatch(hp.reshape(-1, LANES), idx, n_slots * PACK_TILES)
    ys = _experts(tiled(xs, ()), first_blk, n_blk, nused, w_gate[0], w_up[0], w_down[0])
    gates_t = gates.transpose(0, 2, 1).reshape(n, TOP_K)
    g2, b2 = ln2_g[0].reshape(1, -1), ln2_b[0].reshape(1, -1)
    ys_flat = ys.reshape(-1, LANES)
    per = idx.shape[0] // OUT_CHUNKS
    out = None
    for c in range(OUT_CHUNKS):
        yg = tiled(_combine(ys_flat, idx, n // OUT_CHUNKS * PACK_TILES, c * per), (TOP_K,))
        out = _final(yg, gates_t, base, g2, b2, c, out)
    return out.reshape(bsz, seq, D_MODEL)
```
